```python
import math
import jax
import jax.numpy as jnp
from jax import lax
import numpy as np

D_MODEL = 2048
BATCH = 2
SEQ = 8192
DEPTH = 2

GRID_W = 64
CTX_LEN = 256
NORM_EPS = 1e-6
N_MOD = 6

SHORT_CONV = 3
CONV_W = 512

SSD_HEADS = 8
SSD_HEAD_DIM = 64
SSD_INNER = SSD_HEADS * SSD_HEAD_DIM
SSD_STATE = 128
SSD_GROUPS = 2
SSD_CHUNK = 128
SSD_XBC = SSD_INNER + 2 * SSD_GROUPS * SSD_STATE

DA_HEADS = 8
DA_HEAD_DIM = 64
DA_V_DIM = 2 * DA_HEAD_DIM
DA_QK = DA_HEADS * 2 * DA_HEAD_DIM
DA_WIDTH = DA_HEADS * DA_V_DIM
DA_SCALE = DA_HEAD_DIM ** -0.5
Q_BLOCK = 128
ROPE_THETA = 10000.0

MIX_WIDTH = CONV_W + SSD_INNER + DA_WIDTH

COL_CONV = 0
COL_Z = COL_CONV + 3 * CONV_W
COL_Q = COL_Z + SSD_INNER
COL_XBC = COL_Q + DA_QK
COL_DT = COL_XBC + SSD_XBC
COL_K = COL_DT + 2 * SSD_HEADS
COL_V = COL_K + DA_QK
IN_COLS = COL_V + DA_WIDTH
CTX_STATE_COL = COL_XBC

FFN_DENSE = 5632
N_EXPERTS = 8
TOP_K = 2
FFN_EXPERT = 2816

kernel_name = 'hybrid_parallel_group_flow_block'


def _rms(x, g):
    xf = x.astype(jnp.float32)
    y = xf * lax.rsqrt(jnp.mean(xf * xf, axis=-1, keepdims=True) + NORM_EPS)
    return (y * g.astype(jnp.float32)).astype(x.dtype)


def _modulate(x, g, shift, scale):
    return _rms(x, g) * (1 + scale) + shift


def _cols(p, lo, width, base):
    return p[..., lo - base: lo - base + width]


def _short_conv(u, w, b=None):
    k = w.shape[0]
    n = u.shape[1]
    pad = k // 2
    up = jnp.pad(u, ((0, 0), (pad, pad), (0, 0)))
    y = up[:, 0:n] * w[0]
    for j in range(1, k):
        y = y + up[:, j:j + n] * w[j]
    if b is not None:
        y = y + b
    return y


def _axial_rope_tables(n_tokens):
    rows = n_tokens // GRID_W
    row = jnp.repeat(jnp.arange(rows, dtype=jnp.float32), GRID_W)
    col = jnp.broadcast_to(jnp.arange(GRID_W, dtype=jnp.float32), (rows, GRID_W)).reshape(-1)
    n_freq = DA_HEAD_DIM // 4
    inv = ROPE_THETA ** (-jnp.arange(n_freq, dtype=jnp.float32) / n_freq)
    ang_r = row[:, None] * inv
    ang_c = col[:, None] * inv
    ex = lambda a: a[:, None, None, :]
    return (ex(jnp.cos(ang_r)), ex(jnp.sin(ang_r)), ex(jnp.cos(ang_c)), ex(jnp.sin(ang_c)))


def _rope2d(t, tabs):
    cr, sr, cc, sc = tabs
    f = DA_HEAD_DIM // 4
    h = DA_HEAD_DIM // 2
    tf = t.astype(jnp.float32)

    def rot(seg, cos, sin):
        a1, a2 = seg[..., :f], seg[..., f:]
        return jnp.concatenate([a1 * cos - a2 * sin, a2 * cos + a1 * sin], axis=-1)

    return jnp.concatenate([rot(tf[..., :h], cr, sr), rot(tf[..., h:], cc, sc)], axis=-1).astype(t.dtype)


def _conv_mixer(p, w):
    bg = _cols(p, COL_CONV, CONV_W, 0)
    cg = _cols(p, COL_CONV + CONV_W, CONV_W, 0)
    hv = _cols(p, COL_CONV + 2 * CONV_W, CONV_W, 0)
    return bg * _short_conv(cg * hv, w)


def _segsum(a):
    t = a.shape[-1]
    xr = jnp.broadcast_to(a[..., :, None], a.shape + (t,))
    xr = jnp.where(jnp.tril(jnp.ones((t, t), dtype=bool), -1), xr, 0.0)
    s = jnp.cumsum(xr, axis=-2)
    return jnp.where(jnp.tril(jnp.ones((t, t), dtype=bool), 0), s, -jnp.inf)


def _ssd_states(xdt, da, bm, init):
    b, s, h, p = xdt.shape
    n = bm.shape[-1]
    c = s // SSD_CHUNK
    x = xdt.reshape(b, c, SSD_CHUNK, h, p)
    bc = bm.reshape(b, c, SSD_CHUNK, h, n)
    a = jnp.moveaxis(da.reshape(b, c, SSD_CHUNK, h), -1, 1)
    a_cum = jnp.cumsum(a, axis=-1)
    decay = jnp.exp(a_cum[..., -1:] - a_cum)
    states = jnp.einsum('bclhn,bhcl,bclhp->bchpn', bc, decay, x)
    states = jnp.concatenate([init[:, None], states], axis=1)
    chunk_decay = jnp.exp(_segsum(jnp.pad(a_cum[..., -1], ((0, 0), (0, 0), (1, 0)))))
    new = jnp.einsum('bhzc,bchpn->bzhpn', chunk_decay, states)
    return x, bc, a, a_cum, new[:, :-1], new[:, -1]


def _ssd_scan(xdt, da, bm, cm, init):
    b, s, h, p = xdt.shape
    x, bc, a, a_cum, prev, final = _ssd_states(xdt, da, bm, init)
    cc = cm.reshape(bc.shape)
    decay_in = jnp.exp(_segsum(a))
    g = jnp.einsum('bclhn,bcshn->bhcls', cc, bc) * decay_in
    y_diag = jnp.einsum('bhcls,bcshp->bclhp', g, x)
    y_off = jnp.einsum('bclhn,bchpn->bclhp', cc, prev) * jnp.exp(a_cum).transpose(0, 2, 3, 1)[..., None]
    return (y_diag + y_off).reshape(b, s, h, p), final


def _ssd_inputs(p, base, conv_w, conv_b, dt_bias):
    b, n = p.shape[:2]
    xbc = jax.nn.silu(_short_conv(_cols(p, COL_XBC, SSD_XBC, base), conv_w, conv_b)).astype(jnp.float32)
    gs = SSD_GROUPS * SSD_STATE
    rep = SSD_HEADS // SSD_GROUPS
    xh = xbc[..., :SSD_INNER].reshape(b, n, SSD_HEADS, SSD_HEAD_DIM)
    bm = jnp.repeat(xbc[..., SSD_INNER:SSD_INNER + gs].reshape(b, n, SSD_GROUPS, SSD_STATE), rep, axis=2)
    cm = jnp.repeat(xbc[..., SSD_INNER + gs:].reshape(b, n, SSD_GROUPS, SSD_STATE), rep, axis=2)
    dt_raw = _cols(p, COL_DT, 2 * SSD_HEADS, base).astype(jnp.float32).reshape(b, n, 2, SSD_HEADS)
    dt = jax.nn.softplus(dt_raw + dt_bias.astype(jnp.float32))
    return xh, bm, cm, dt


def _ssd_gate_norm(y, z, g):
    b, n = y.shape[:2]
    u = y.reshape(b, n, SSD_INNER) * jax.nn.silu(z.astype(jnp.float32))
    u = u.reshape(b, n, SSD_GROUPS, SSD_INNER // SSD_GROUPS)
    u = u * lax.rsqrt(jnp.mean(u * u, axis=-1, keepdims=True) + NORM_EPS)
    return (u.reshape(b, n, SSD_INNER) * g.astype(jnp.float32)).astype(z.dtype)


def _ssd_mixer(pl, pc, base_c, conv_w, conv_b, a_log, dt_bias, d_skip, norm_g, ctx_out):
    xl, bl, cl, dtl = _ssd_inputs(pl, 0, conv_w, conv_b, dt_bias)
    xc, bc, cc, dtc = _ssd_inputs(pc, base_c, conv_w, conv_b, dt_bias)
    a = -jnp.exp(a_log.astype(jnp.float32))
    d = d_skip.astype(jnp.float32)[:, None]
    yl = d * xl
    yc = d * xc
    init = jnp.zeros((xc.shape[0], SSD_HEADS, SSD_HEAD_DIM, SSD_STATE), jnp.float32)
    for di in range(2):
        fl = (lambda t: jnp.flip(t, axis=1)) if di == 1 else (lambda t: t)
        dt_c = dtc[:, :, di]
        dt_l = dtl[:, :, di]
        xdt_c, da_c, b_c = fl(xc * dt_c[..., None]), fl(dt_c * a[di]), fl(bc)
        if ctx_out:
            y_d, h_c = _ssd_scan(xdt_c, da_c, b_c, fl(cc), init)
            yc = yc + fl(y_d)
        else:
            h_c = _ssd_states(xdt_c, da_c, b_c, init)[-1]
        y_d, _ = _ssd_scan(fl(xl * dt_l[..., None]), fl(dt_l * a[di]), fl(bl), fl(cl), h_c)
        yl = yl + fl(y_d)
    out_l = _ssd_gate_norm(yl, _cols(pl, COL_Z, SSD_INNER, 0), norm_g)
    if not ctx_out:
        return out_l, None
    return out_l, _ssd_gate_norm(yc, _cols(pc, COL_Z, SSD_INNER, 0), norm_g)


def _diff_mix(q, k, v, lam):
    s = jnp.einsum('bhmqd,bhmkd->bhmqk', q, k).astype(jnp.float32) * DA_SCALE
    p = jax.nn.softmax(s, axis=-1)
    w = p[:, :, 0] - lam * p[:, :, 1]
    return jnp.einsum('bhqk,bhkv->bhqv', w.astype(v.dtype), v)


def _head_out(o, g, lam_init):
    b, n = o.shape[:2]
    return (_rms(o, g) * (1.0 - lam_init)).reshape(b, n, DA_WIDTH)


def _diff_attention(pl, pc, base_c, da_lambda, da_subln, lam_init, rope, ctx_out):
    b, n_lat = pl.shape[:2]
    n_ctx = pc.shape[1]
    qk_shape = lambda n: (b, n, DA_HEADS, 2, DA_HEAD_DIM)
    q_l = _rope2d(_cols(pl, COL_Q, DA_QK, 0).reshape(qk_shape(n_lat)), rope)
    k_l = _rope2d(_cols(pl, COL_K, DA_QK, 0).reshape(qk_shape(n_lat)), rope)
    v_l = _cols(pl, COL_V, DA_WIDTH, 0).reshape(b, n_lat, DA_HEADS, DA_V_DIM)
    k_c = _cols(pc, COL_K, DA_QK, base_c).reshape(qk_shape(n_ctx))
    v_c = _cols(pc, COL_V, DA_WIDTH, base_c).reshape(b, n_ctx, DA_HEADS, DA_V_DIM)
    lv = da_lambda.astype(jnp.float32)
    lam = jnp.exp(jnp.sum(lv[0] * lv[1])) - jnp.exp(jnp.sum(lv[2] * lv[3])) + lam_init
    k_all = jnp.concatenate([k_c, k_l], axis=1).transpose(0, 2, 3, 1, 4)
    v_all = jnp.concatenate([v_c, v_l], axis=1).transpose(0, 2, 1, 3)
    nb = n_lat // Q_BLOCK
    qb = q_l.reshape(b, nb, Q_BLOCK, DA_HEADS, 2, DA_HEAD_DIM).transpose(1, 0, 3, 4, 2, 5)
    o = lax.map(lambda q: _diff_mix(q, k_all, v_all, lam), qb)
    o_l = o.transpose(1, 0, 3, 2, 4).reshape(b, n_lat, DA_HEADS, DA_V_DIM)
    y_l = _head_out(o_l, da_subln, lam_init)
    if not ctx_out:
        return y_l, None
    q_c = _cols(pc, COL_Q, DA_QK, 0).reshape(qk_shape(n_ctx))
    o_c = _diff_mix(q_c.transpose(0, 2, 3, 1, 4), k_c.transpose(0, 2, 3, 1, 4),
                    v_c.transpose(0, 2, 1, 3), lam).transpose(0, 2, 1, 3)
    return y_l, _head_out(o_c, da_subln, lam_init)


def _mixer(hl, hc, w_in, conv_w, ssd_conv_w, ssd_conv_b, ssd_a_log, ssd_dt_bias, ssd_d, ssd_norm,
           da_lambda, da_subln, w_out, lam_init, rope, ctx_out):
    base_c = 0 if ctx_out else CTX_STATE_COL
    pl = hl @ w_in
    pc = hc @ w_in[:, base_c:]
    ya_l = _conv_mixer(pl, conv_w)
    yb_l, yb_c = _ssd_mixer(pl, pc, base_c, ssd_conv_w, ssd_conv_b, ssd_a_log, ssd_dt_bias,
                            ssd_d, ssd_norm, ctx_out)
    yc_l, yc_c = _diff_attention(pl, pc, base_c, da_lambda, da_subln, lam_init, rope, ctx_out)
    out_l = jnp.concatenate([ya_l, yb_l, yc_l], axis=-1) @ w_out
    if not ctx_out:
        return out_l, None
    ya_c = _conv_mixer(pc, conv_w)
    out_c = jnp.concatenate([ya_c, yb_c, yc_c], axis=-1) @ w_out
    return out_l, out_c


def _swiglu(h, wg, wu, wd):
    return (jax.nn.silu(h @ wg) * (h @ wu)) @ wd


def _moe(h, w_router, b_router, w_gate, w_up, w_down):
    logits = (h @ w_router + b_router).astype(jnp.float32)
    top_v, top_i = lax.top_k(logits, TOP_K)
    probs = jax.nn.softmax(top_v, axis=-1)
    combine = jnp.sum(jax.nn.one_hot(top_i, N_EXPERTS, dtype=jnp.float32) * probs[..., None], axis=-2)
    y = jnp.zeros_like(h)
    for e in range(N_EXPERTS):
        y = y + combine[..., e:e + 1].astype(h.dtype) * _swiglu(h, w_gate[e], w_up[e], w_down[e])
    return y


def setup_inputs(seed: int = 0) -> dict:
    key = jax.random.key(seed)
    ks = iter(jax.random.split(key, 32))
    f32 = jnp.float32
    nrm = lambda shape, scale: jax.random.normal(next(ks), shape, f32) * scale
    D = D_MODEL
    L = DEPTH
    nd = (DEPTH + 1) // 2
    nm = DEPTH // 2
    x = nrm((BATCH, SEQ, D), 1.0)
    c = nrm((BATCH, D), 1.0)
    ctx = nrm((BATCH, CTX_LEN, D), 1.0)
    c_ctx = nrm((D,), 1.0)
    w_mod = nrm((L, D, N_MOD * D), 0.3 * D ** -0.5)
    b_mod = nrm((L, N_MOD * D), 0.02)
    g_mix = 1.0 + nrm((L, D), 0.02)
    g_ffn = 1.0 + nrm((L, D), 0.02)
    w_in = nrm((L, D, IN_COLS), D ** -0.5)
    conv_w = nrm((L, SHORT_CONV, CONV_W), SHORT_CONV ** -0.5)
    ssd_conv_w = nrm((L, SHORT_CONV, SSD_XBC), SHORT_CONV ** -0.5)
    ssd_conv_b = nrm((L, SSD_XBC), 0.02)
    ssd_a_log = jnp.log(jax.random.uniform(next(ks), (L, 2, SSD_HEADS), f32, 1.0, 16.0))
    dt0 = jnp.exp(jax.random.uniform(next(ks), (L, 2, SSD_HEADS), f32, math.log(1e-3), math.log(1e-1)))
    ssd_dt_bias = dt0 + jnp.log(-jnp.expm1(-dt0))
    ssd_d = 1.0 + nrm((L, SSD_HEADS), 0.1)
    ssd_norm = 1.0 + nrm((L, SSD_INNER), 0.02)
    da_lambda = nrm((L, 4, DA_HEAD_DIM), 0.1)
    da_subln = 1.0 + nrm((L, DA_V_DIM), 0.02)
    w_out = nrm((L, MIX_WIDTH, D), MIX_WIDTH ** -0.5)
    ffn_w_gate = nrm((nd, D, FFN_DENSE), D ** -0.5)
    ffn_w_up = nrm((nd, D, FFN_DENSE), D ** -0.5)
    ffn_w_down = nrm((nd, FFN_DENSE, D), FFN_DENSE ** -0.5)
    moe_w_router = nrm((nm, D, N_EXPERTS), D ** -0.5)
    moe_b_router = nrm((nm, N_EXPERTS), 0.01)
    moe_w_gate = nrm((nm, N_EXPERTS, D, FFN_EXPERT), D ** -0.5)
    moe_w_up = nrm((nm, N_EXPERTS, D, FFN_EXPERT), D ** -0.5)
    moe_w_down = nrm((nm, N_EXPERTS, FFN_EXPERT, D), FFN_EXPERT ** -0.5)
    g_final = 1.0 + nrm((D,), 0.02)
    return {'x': x, 'c': c, 'ctx': ctx, 'c_ctx': c_ctx, 'w_mod': w_mod, 'b_mod': b_mod,
            'g_mix': g_mix, 'g_ffn': g_ffn, 'w_in': w_in, 'conv_w': conv_w,
            'ssd_conv_w': ssd_conv_w, 'ssd_conv_b': ssd_conv_b, 'ssd_a_log': ssd_a_log,
            'ssd_dt_bias': ssd_dt_bias, 'ssd_d': ssd_d, 'ssd_norm': ssd_norm,
            'da_lambda': da_lambda, 'da_subln': da_subln, 'w_out': w_out,
            'ffn_w_gate': ffn_w_gate, 'ffn_w_up': ffn_w_up, 'ffn_w_down': ffn_w_down,
            'moe_w_router': moe_w_router, 'moe_b_router': moe_b_router, 'moe_w_gate': moe_w_gate,
            'moe_w_up': moe_w_up, 'moe_w_down': moe_w_down, 'g_final': g_final}


def reference(x, c, ctx, c_ctx, w_mod, b_mod, g_mix, g_ffn, w_in, conv_w, ssd_conv_w, ssd_conv_b,
              ssd_a_log, ssd_dt_bias, ssd_d, ssd_norm, da_lambda, da_subln, w_out,
              ffn_w_gate, ffn_w_up, ffn_w_down, moe_w_router, moe_b_router, moe_w_gate,
              moe_w_up, moe_w_down, g_final):
    b, n_lat, d_model = x.shape
    rope = _axial_rope_tables(n_lat)
    xl, xc = x, ctx
    for i in range(DEPTH):
        ctx_out = i < DEPTH - 1
        lam_init = 0.8 - 0.6 * math.exp(-0.3 * i)
        ml = (jax.nn.silu(c) @ w_mod[i] + b_mod[i]).reshape(b, N_MOD, 1, d_model)
        mc = (jax.nn.silu(c_ctx) @ w_mod[i] + b_mod[i]).reshape(N_MOD, d_model)
        hl = _modulate(xl, g_mix[i], ml[:, 0], ml[:, 1])
        hc = _modulate(xc, g_mix[i], mc[0], mc[1])
        ol, oc = _mixer(hl, hc, w_in[i], conv_w[i], ssd_conv_w[i], ssd_conv_b[i], ssd_a_log[i],
                        ssd_dt_bias[i], ssd_d[i], ssd_norm[i], da_lambda[i], da_subln[i], w_out[i],
                        lam_init, rope, ctx_out)
        xl = xl + ml[:, 2] * ol
        if i % 2 == 0:
            j = i // 2
            chan = lambda h: _swiglu(h, ffn_w_gate[j], ffn_w_up[j], ffn_w_down[j])
        else:
            j = i // 2
            chan = lambda h: _moe(h, moe_w_router[j], moe_b_router[j], moe_w_gate[j],
                                  moe_w_up[j], moe_w_down[j])
        xl = xl + ml[:, 5] * chan(_modulate(xl, g_ffn[i], ml[:, 3], ml[:, 4]))
        if ctx_out:
            xc = xc + mc[2] * oc
            xc = xc + mc[5] * chan(_modulate(xc, g_ffn[i], mc[3], mc[4]))
    return _rms(xl, g_final)
```

```python
import functools
import math

import jax
import jax.numpy as jnp
from jax import lax
from jax.experimental import pallas as pl
from jax.experimental.pallas import tpu as pltpu

NORM_EPS = 1e-6
N_MOD = 6
GRID_W = 64

SHORT_CONV = 3
CONV_W = 512

SSD_HEADS = 8
SSD_HEAD_DIM = 64
SSD_INNER = SSD_HEADS * SSD_HEAD_DIM
SSD_STATE = 128
SSD_GROUPS = 2
SSD_CHUNK = 128
SSD_XBC = SSD_INNER + 2 * SSD_GROUPS * SSD_STATE
SSD_GROUP_W = SSD_INNER // SSD_GROUPS

DA_HEADS = 8
DA_HEAD_DIM = 64
DA_V_DIM = 2 * DA_HEAD_DIM
DA_QK = DA_HEADS * 2 * DA_HEAD_DIM
DA_WIDTH = DA_HEADS * DA_V_DIM
DA_SCALE = DA_HEAD_DIM ** -0.5
ROPE_THETA = 10000.0

COL_CONV = 0
COL_Z = COL_CONV + 3 * CONV_W
COL_Q = COL_Z + SSD_INNER
COL_XBC = COL_Q + DA_QK
COL_DT = COL_XBC + SSD_XBC
COL_K = COL_DT + 2 * SSD_HEADS
COL_V = COL_K + DA_QK

N_EXPERTS = 8

LANE = 128
SUBLANE = 8
ROW_TILE = 512
HALO_TILE = 256
MOD_CLASSES = 8

F32 = jnp.float32
BF16 = jnp.bfloat16
HIGHEST = lax.Precision.HIGHEST
MIB = 1024 * 1024


def _params(semantics, vmem_mib=None):
    kw = {"dimension_semantics": semantics}
    if vmem_mib is not None:
        kw["vmem_limit_bytes"] = vmem_mib * MIB
    return pltpu.CompilerParams(**kw)


def _silu(v):
    return v * jax.nn.sigmoid(v)


def _dot(a, b):
    return jnp.dot(a, b, preferred_element_type=F32)


def _dot_nt(a, b):
    return lax.dot_general(a, b, (((1,), (1,)), ((), ())), preferred_element_type=F32)


def _dot_tn(a, b):
    return lax.dot_general(a, b, (((0,), (0,)), ((), ())), preferred_element_type=F32)


def _modulated_norm(x, g, shift, scale):
    ms = jnp.mean(x * x, axis=-1, keepdims=True)
    y = x * lax.rsqrt(ms + NORM_EPS)
    return (y * g) * (1.0 + scale) + shift


def _mod_spec(k, width, cls_of_tile):
    return pl.BlockSpec((None, 1, width), lambda i, *_: (cls_of_tile(i), 0, k))


def _mod_kernel(c_ref, w_ref, b_ref, o_ref):
    s = _silu(c_ref[...])
    o_ref[...] = jnp.dot(s, w_ref[...], precision=HIGHEST, preferred_element_type=F32) + b_ref[...]


def _mod_vectors(cvec, w_mod, b_mod):
    depth, d, n = w_mod.shape
    tn = 1024
    return pl.pallas_call(
        _mod_kernel,
        grid=(depth, n // tn),
        in_specs=[pl.BlockSpec((MOD_CLASSES, d), lambda l, j: (0, 0)),
                  pl.BlockSpec((None, d, tn), lambda l, j: (l, 0, j)),
                  pl.BlockSpec((None, 1, tn), lambda l, j: (l, 0, j))],
        out_specs=pl.BlockSpec((None, MOD_CLASSES, tn), lambda l, j: (l, 0, j)),
        out_shape=jax.ShapeDtypeStruct((depth, MOD_CLASSES, n), F32),
        compiler_params=_params(("parallel", "parallel"), 40),
        name="mod_vectors",
    )(cvec, w_mod, b_mod.reshape(depth, 1, n))


def _norm_kernel(x_ref, g_ref, sh_ref, sc_ref, h_ref):
    h_ref[...] = _modulated_norm(x_ref[...], g_ref[...], sh_ref[...], sc_ref[...]).astype(h_ref.dtype)


def _norm_modulate(x, g, mods, k_shift, n_tiles, cls_of_tile):
    m, d = x.shape
    return pl.pallas_call(
        _norm_kernel,
        grid=(n_tiles,),
        in_specs=[pl.BlockSpec((ROW_TILE, d), lambda i: (i, 0)),
                  pl.BlockSpec((1, d), lambda i: (0, 0)),
                  _mod_spec(k_shift, d, cls_of_tile),
                  _mod_spec(k_shift + 1, d, cls_of_tile)],
        out_specs=pl.BlockSpec((ROW_TILE, d), lambda i: (i, 0)),
        out_shape=jax.ShapeDtypeStruct((m, d), BF16),
        compiler_params=_params(("parallel",), 40),
        name="norm_modulate",
    )(x, g.reshape(1, d), mods, mods)


def _mm_kernel(x_ref, w_ref, o_ref):
    o_ref[...] = _dot(x_ref[...], w_ref[...]).astype(o_ref.dtype)


def _matmul(x, w, out_dtype, tn, name):
    m, k = x.shape
    n = w.shape[1]
    return pl.pallas_call(
        _mm_kernel,
        grid=(m // ROW_TILE, n // tn),
        in_specs=[pl.BlockSpec((ROW_TILE, k), lambda i, j: (i, 0)),
                  pl.BlockSpec((k, tn), lambda i, j: (0, j))],
        out_specs=pl.BlockSpec((ROW_TILE, tn), lambda i, j: (i, j)),
        out_shape=jax.ShapeDtypeStruct((m, n), out_dtype),
        compiler_params=_params(("parallel", "parallel"), 40),
        name=name,
    )(x, w)


def _mm_rope_kernel(x_ref, w_ref, cos_ref, sin_ref, o_ref, *, n_lat_tiles, n_rope_tiles):
    i = pl.program_id(0)
    j = pl.program_id(1)
    acc = _dot(x_ref[...], w_ref[...])
    rotate = jnp.logical_and(i < n_lat_tiles, j < n_rope_tiles)

    @pl.when(rotate)
    def _():
        tm, tn = acc.shape
        lane = lax.broadcasted_iota(jnp.int32, (tm, LANE), 1)
        first = (lane % (DA_HEAD_DIM // 2)) < (DA_HEAD_DIM // 4)
        c = cos_ref[...]
        s = sin_ref[...]
        quarter = DA_HEAD_DIM // 4
        for hb in range(tn // LANE):
            blk = acc[:, hb * LANE:(hb + 1) * LANE]
            partner = jnp.where(first, pltpu.roll(blk, LANE - quarter, 1), pltpu.roll(blk, quarter, 1))
            o_ref[:, hb * LANE:(hb + 1) * LANE] = (blk * c + partner * s).astype(o_ref.dtype)

    @pl.when(jnp.logical_not(rotate))
    def _():
        o_ref[...] = acc.astype(o_ref.dtype)


def _matmul_rope(x, w, cos_t, sin_t, n_lat_tiles, tiles_per_seq):
    m, k = x.shape
    n = w.shape[1]
    tn = DA_QK
    kern = functools.partial(_mm_rope_kernel, n_lat_tiles=n_lat_tiles, n_rope_tiles=2 * DA_QK // tn)
    return pl.pallas_call(
        kern,
        grid=(m // ROW_TILE, n // tn),
        in_specs=[pl.BlockSpec((ROW_TILE, k), lambda i, j: (i, 0)),
                  pl.BlockSpec((k, tn), lambda i, j: (0, j)),
                  pl.BlockSpec((ROW_TILE, LANE), lambda i, j: (i % tiles_per_seq, 0)),
                  pl.BlockSpec((ROW_TILE, LANE), lambda i, j: (i % tiles_per_seq, 0))],
        out_specs=pl.BlockSpec((ROW_TILE, tn), lambda i, j: (i, j)),
        out_shape=jax.ShapeDtypeStruct((m, n), BF16),
        compiler_params=_params(("parallel", "parallel"), 40),
        name="qkv_proj_rope",
    )(x, w, cos_t, sin_t)


def _rope_tables(n_tokens):
    rows = n_tokens // GRID_W
    row = jnp.repeat(jnp.arange(rows, dtype=F32), GRID_W)
    col = jnp.broadcast_to(jnp.arange(GRID_W, dtype=F32), (rows, GRID_W)).reshape(-1)
    n_freq = DA_HEAD_DIM // 4
    inv = ROPE_THETA ** (-jnp.arange(n_freq, dtype=F32) / n_freq)
    ang_r = row[:, None] * inv
    ang_c = col[:, None] * inv
    cos64 = jnp.concatenate([jnp.cos(ang_r), jnp.cos(ang_r), jnp.cos(ang_c), jnp.cos(ang_c)], axis=-1)
    sin64 = jnp.concatenate([-jnp.sin(ang_r), jnp.sin(ang_r), -jnp.sin(ang_c), jnp.sin(ang_c)], axis=-1)
    return jnp.tile(cos64, (1, 2)), jnp.tile(sin64, (1, 2))


def _conv3(u, prev_row, next_row, w_ref, has_prev, has_next):
    tm = u.shape[0]
    row = lax.broadcasted_iota(jnp.int32, u.shape, 0)
    prev_row = jnp.where(has_prev, prev_row, 0.0)
    next_row = jnp.where(has_next, next_row, 0.0)
    before = jnp.where(row == 0, prev_row, pltpu.roll(u, 1, 0))
    after = jnp.where(row == tm - 1, next_row, pltpu.roll(u, tm - 1, 0))
    return before * w_ref[0:1, :] + u * w_ref[1:2, :] + after * w_ref[2:3, :]


def _prep_kernel(cv_ref, cvp_ref, cvn_ref, xb_ref, xbp_ref, xbn_ref, dtr_ref,
                 cw_ref, sw_ref, sb_ref, dtb_ref, ya_ref, xs_ref, dt_ref,
                 *, n_lat_tiles, lat_tiles_per_seq, ctx_tiles_per_seq):
    i = pl.program_id(0)
    is_lat = i < n_lat_tiles
    pos = jnp.where(is_lat, i % lat_tiles_per_seq, (i - n_lat_tiles) % ctx_tiles_per_seq)
    last = jnp.where(is_lat, lat_tiles_per_seq - 1, ctx_tiles_per_seq - 1)
    has_prev = pos != 0
    has_next = pos != last

    w = CONV_W
    cv = cv_ref[...]
    gate_b, u = cv[:, 0:w], cv[:, w:2 * w] * cv[:, 2 * w:3 * w]
    p = cvp_ref[SUBLANE - 1:SUBLANE, :]
    n = cvn_ref[0:1, :]
    u_prev = p[:, w:2 * w] * p[:, 2 * w:3 * w]
    u_next = n[:, w:2 * w] * n[:, 2 * w:3 * w]
    ya_ref[...] = (gate_b * _conv3(u, u_prev, u_next, cw_ref, has_prev, has_next)).astype(ya_ref.dtype)

    xc = _conv3(xb_ref[...], xbp_ref[SUBLANE - 1:SUBLANE, :], xbn_ref[0:1, :], sw_ref, has_prev, has_next)
    xs_ref[...] = _silu(xc + sb_ref[...])

    t = dtr_ref[...] + dtb_ref[...]
    dt = jnp.maximum(t, 0.0) + jnp.log1p(jnp.exp(-jnp.abs(t)))
    dt_ref[0] = dt
    dt_ref[1] = pltpu.roll(dt, LANE - SSD_HEADS, 1)


def _prep(p_a, p_dt, conv_w, ssd_conv_w, ssd_conv_b, dt_bias, n_lat_tiles, lat_tiles_per_seq,
          ctx_tiles_per_seq):
    m = p_a.shape[0]
    tm = HALO_TILE
    per = tm // SUBLANE
    n8 = m // SUBLANE
    cw = 3 * CONV_W
    xbc_blk = COL_Q // SSD_XBC
    prev = lambda i: jnp.maximum(i * per - 1, 0)
    nxt = lambda i: jnp.minimum((i + 1) * per, n8 - 1)
    kern = functools.partial(_prep_kernel, n_lat_tiles=n_lat_tiles, lat_tiles_per_seq=lat_tiles_per_seq,
                             ctx_tiles_per_seq=ctx_tiles_per_seq)
    return pl.pallas_call(
        kern,
        grid=(m // tm,),
        in_specs=[pl.BlockSpec((tm, cw), lambda i: (i, 0)),
                  pl.BlockSpec((SUBLANE, cw), lambda i: (prev(i), 0)),
                  pl.BlockSpec((SUBLANE, cw), lambda i: (nxt(i), 0)),
                  pl.BlockSpec((tm, SSD_XBC), lambda i: (i, xbc_blk)),
                  pl.BlockSpec((SUBLANE, SSD_XBC), lambda i: (prev(i), xbc_blk)),
                  pl.BlockSpec((SUBLANE, SSD_XBC), lambda i: (nxt(i), xbc_blk)),
                  pl.BlockSpec((tm, LANE), lambda i: (i, 0)),
                  pl.BlockSpec((SHORT_CONV, CONV_W), lambda i: (0, 0)),
                  pl.BlockSpec((SHORT_CONV, SSD_XBC), lambda i: (0, 0)),
                  pl.BlockSpec((1, SSD_XBC), lambda i: (0, 0)),
                  pl.BlockSpec((1, LANE), lambda i: (0, 0))],
        out_specs=[pl.BlockSpec((tm, CONV_W), lambda i: (i, 0)),
                   pl.BlockSpec((tm, SSD_XBC), lambda i: (i, 0)),
                   pl.BlockSpec((2, tm, LANE), lambda i: (0, i, 0))],
        out_shape=[jax.ShapeDtypeStruct((m, CONV_W), BF16),
                   jax.ShapeDtypeStruct((m, SSD_XBC), F32),
                   jax.ShapeDtypeStruct((2, m, LANE), F32)],
        compiler_params=_params(("parallel",), 40),
        name="conv_prep",
    )(p_a, p_a, p_a, p_a, p_a, p_a, p_dt, conv_w, ssd_conv_w, ssd_conv_b, dt_bias)


def _ssd_kernel(xs_ref, dt_ref, alog_ref, y_ref, state_ref):
    d = pl.program_id(1)
    step = pl.program_id(2)

    @pl.when(step == 0)
    def _():
        state_ref[...] = jnp.zeros_like(state_ref)

    t = SSD_CHUNK
    r = lax.broadcasted_iota(jnp.int32, (t, t), 0)
    c = lax.broadcasted_iota(jnp.int32, (t, t), 1)
    fwd = d == 0
    mask = jnp.logical_or(jnp.logical_and(fwd, r >= c), jnp.logical_and(jnp.logical_not(fwd), r <= c))
    tri = mask.astype(F32)

    dt = dt_ref[...]
    da = dt * (-jnp.exp(alog_ref[...]))
    cs = jnp.dot(tri, da, precision=HIGHEST, preferred_element_type=F32)
    cs_t = cs.T
    tot = jnp.sum(da, axis=0, keepdims=True)

    er = lax.broadcasted_iota(jnp.int32, (LANE, SSD_INNER), 0)
    ec = lax.broadcasted_iota(jnp.int32, (LANE, SSD_INNER), 1)
    expand = (jnp.right_shift(ec, 6) == er).astype(F32)
    assert SSD_HEAD_DIM == 64
    to_ch = lambda v: jnp.dot(v, expand, precision=HIGHEST, preferred_element_type=F32)
    dt_ch = to_ch(dt)
    cs_ch = to_ch(cs)
    tot_ch = to_ch(jnp.broadcast_to(tot, (SUBLANE, LANE)))[0:1, :]

    xdt = xs_ref[:, 0:SSD_INNER] * dt_ch
    x_state = xdt * jnp.exp(tot_ch - cs_ch)
    y_scale = jnp.exp(cs_ch)
    carry = jnp.exp(tot_ch)
    gw = SSD_GROUP_W
    heads_per_group = SSD_HEADS // SSD_GROUPS
    lane_head = jnp.right_shift(lax.broadcasted_iota(jnp.int32, (t, gw), 1), 6)

    for g in range(SSD_GROUPS):
        b_lo = SSD_INNER + g * SSD_STATE
        c_lo = SSD_INNER + SSD_GROUPS * SSD_STATE + g * SSD_STATE
        bg = xs_ref[:, b_lo:b_lo + SSD_STATE].astype(BF16)
        cg = xs_ref[:, c_lo:c_lo + SSD_STATE].astype(BF16)
        cb = _dot_nt(cg, bg)
        sg = state_ref[g]
        xdt_g = xdt[:, g * gw:(g + 1) * gw]
        y = _dot(cg, sg.astype(BF16)) * y_scale[:, g * gw:(g + 1) * gw]
        for hh in range(heads_per_group):
            h = g * heads_per_group + hh
            decay = jnp.where(mask, jnp.exp(cs[:, h:h + 1] - cs_t[h:h + 1, :]), 0.0)
            x_h = jnp.where(lane_head == hh, xdt_g, 0.0).astype(BF16)
            y = y + _dot((cb * decay).astype(BF16), x_h)
        y_ref[:, g * gw:(g + 1) * gw] = y
        ds = _dot_tn(bg, x_state[:, g * gw:(g + 1) * gw].astype(BF16))
        state_ref[g] = sg * carry[:, g * gw:(g + 1) * gw] + ds


def _ssd_scan(xs, dt2, a_log, batch, n_lat_chunks, n_ctx_chunks):
    m = xs.shape[0]
    t = SSD_CHUNK

    def chunk(b, d, s):
        j_ctx = jnp.where(d == 0, s, n_ctx_chunks - 1 - s)
        sl = s - n_ctx_chunks
        j_lat = jnp.where(d == 0, sl, n_lat_chunks - 1 - sl)
        return jnp.where(s < n_ctx_chunks, batch * n_lat_chunks + b * n_ctx_chunks + j_ctx,
                         b * n_lat_chunks + j_lat)

    return pl.pallas_call(
        _ssd_kernel,
        grid=(batch, 2, n_ctx_chunks + n_lat_chunks),
        in_specs=[pl.BlockSpec((t, SSD_XBC), lambda b, d, s: (chunk(b, d, s), 0)),
                  pl.BlockSpec((None, t, LANE), lambda b, d, s: (d, chunk(b, d, s), 0)),
                  pl.BlockSpec((None, 1, LANE), lambda b, d, s: (d, 0, 0))],
        out_specs=pl.BlockSpec((None, t, SSD_INNER), lambda b, d, s: (d, chunk(b, d, s), 0)),
        out_shape=jax.ShapeDtypeStruct((2, m, SSD_INNER), F32),
        scratch_shapes=[pltpu.VMEM((SSD_GROUPS, SSD_STATE, SSD_GROUP_W), F32)],
        compiler_params=_params(("parallel", "parallel", "arbitrary")),
        name="ssd_scan",
    )(xs, dt2, a_log)


def _ssd_gate_kernel(y_ref, xs_ref, z_ref, d_ref, g_ref, o_ref):
    yl = d_ref[...] * xs_ref[...] + y_ref[0] + y_ref[1]
    u = yl * _silu(z_ref[...])
    gw = SSD_GROUP_W
    for g in range(SSD_GROUPS):
        ug = u[:, g * gw:(g + 1) * gw]
        ug = ug * lax.rsqrt(jnp.mean(ug * ug, axis=-1, keepdims=True) + NORM_EPS)
        o_ref[:, g * gw:(g + 1) * gw] = (ug * g_ref[:, g * gw:(g + 1) * gw]).astype(o_ref.dtype)


def _ssd_gate(y, xs, p_a, d_row, norm_g, n_tiles):
    m = xs.shape[0]
    w = SSD_INNER
    return pl.pallas_call(
        _ssd_gate_kernel,
        grid=(n_tiles,),
        in_specs=[pl.BlockSpec((2, ROW_TILE, w), lambda i: (0, i, 0)),
                  pl.BlockSpec((ROW_TILE, w), lambda i: (i, 0)),
                  pl.BlockSpec((ROW_TILE, w), lambda i: (i, COL_Z // w)),
                  pl.BlockSpec((1, w), lambda i: (0, 0)),
                  pl.BlockSpec((1, w), lambda i: (0, 0))],
        out_specs=pl.BlockSpec((ROW_TILE, w), lambda i: (i, 0)),
        out_shape=jax.ShapeDtypeStruct((m, w), BF16),
        compiler_params=_params(("parallel",)),
        name="ssd_gate_norm",
    )(y, xs, p_a, d_row, norm_g)


def _attn_kernel(*refs, has_lat, lam_init):
    if has_lat:
        q_ref, kc_ref, vc_ref, kl_ref, vl_ref, lam_ref, g_ref, o_ref, m_ref, l_ref, acc_ref = refs
        j = pl.program_id(3)
        n_j = pl.num_programs(3)
    else:
        q_ref, kc_ref, vc_ref, lam_ref, g_ref, _, o_ref, m_ref, l_ref, acc_ref = refs

    def attend(k, v):
        q = q_ref[...]
        lane = lax.broadcasted_iota(jnp.int32, q.shape, 1)
        for mp in range(2):
            in_map = (lane < DA_HEAD_DIM) if mp == 0 else (lane >= DA_HEAD_DIM)
            s = _dot_nt(jnp.where(in_map, q, jnp.zeros_like(q)), k)
            m_old = m_ref[mp]
            m_new = jnp.maximum(m_old, jnp.max(s, axis=-1, keepdims=True))
            alpha = jnp.exp(m_old - m_new)
            p = jnp.exp(s - m_new)
            l_ref[mp] = alpha * l_ref[mp] + jnp.sum(p, axis=-1, keepdims=True)
            acc_ref[mp] = alpha * acc_ref[mp] + _dot(p.astype(v.dtype), v)
            m_ref[mp] = m_new

    def start():
        m_ref[...] = jnp.full(m_ref.shape, -jnp.inf, F32)
        l_ref[...] = jnp.zeros_like(l_ref)
        acc_ref[...] = jnp.zeros_like(acc_ref)
        attend(kc_ref[...], vc_ref[...])

    def finish():
        lv = lam_ref[...]
        dotp = lambda a, b: jnp.sum(lv[a:a + 1, :] * lv[b:b + 1, :], axis=-1, keepdims=True)
        lam = jnp.exp(dotp(0, 1)) - jnp.exp(dotp(2, 3)) + lam_init
        o = acc_ref[0] / l_ref[0] - lam * (acc_ref[1] / l_ref[1])
        y = o * lax.rsqrt(jnp.mean(o * o, axis=-1, keepdims=True) + NORM_EPS)
        o_ref[...] = ((y * g_ref[...]) * (1.0 - lam_init)).astype(o_ref.dtype)

    if has_lat:
        pl.when(j == 0)(start)
        attend(kl_ref[...], vl_ref[...])
        pl.when(j == n_j - 1)(finish)
    else:
        start()
        finish()


def _attention_lat(qkv, da_lambda, subln, lam_init, batch, seq, ctx_len, tq, tk):
    m = qkv.shape[0]
    h = DA_HEADS
    nq, nk = seq // tq, seq // tk
    ctx_blk0 = (batch * seq) // ctx_len
    kern = functools.partial(_attn_kernel, has_lat=True, lam_init=lam_init)
    return pl.pallas_call(
        kern,
        grid=(batch, h, nq, nk),
        in_specs=[pl.BlockSpec((tq, LANE), lambda b, hh, i, j: (b * nq + i, hh)),
                  pl.BlockSpec((ctx_len, LANE), lambda b, hh, i, j: (ctx_blk0 + b, h + hh)),
                  pl.BlockSpec((ctx_len, LANE), lambda b, hh, i, j: (ctx_blk0 + b, 2 * h + hh)),
                  pl.BlockSpec((tk, LANE), lambda b, hh, i, j: (b * nk + j, h + hh)),
                  pl.BlockSpec((tk, LANE), lambda b, hh, i, j: (b * nk + j, 2 * h + hh)),
                  pl.BlockSpec(da_lambda.shape, lambda b, hh, i, j: (0, 0)),
                  pl.BlockSpec((1, LANE), lambda b, hh, i, j: (0, 0))],
        out_specs=pl.BlockSpec((tq, LANE), lambda b, hh, i, j: (b * nq + i, hh)),
        out_shape=jax.ShapeDtypeStruct((m, DA_WIDTH), BF16),
        scratch_shapes=[pltpu.VMEM((2, tq, 1), F32), pltpu.VMEM((2, tq, 1), F32),
                        pltpu.VMEM((2, tq, DA_V_DIM), F32)],
        compiler_params=_params(("parallel", "parallel", "parallel", "arbitrary"), 40),
        name="diff_attention",
    )(qkv, qkv, qkv, qkv, qkv, da_lambda, subln)


def _attention_ctx(qkv, yc, da_lambda, subln, lam_init, batch, seq, ctx_len):
    h = DA_HEADS
    ctx_blk0 = (batch * seq) // ctx_len
    kern = functools.partial(_attn_kernel, has_lat=False, lam_init=lam_init)
    return pl.pallas_call(
        kern,
        grid=(batch, h),
        in_specs=[pl.BlockSpec((ctx_len, LANE), lambda b, hh: (ctx_blk0 + b, hh)),
                  pl.BlockSpec((ctx_len, LANE), lambda b, hh: (ctx_blk0 + b, h + hh)),
                  pl.BlockSpec((ctx_len, LANE), lambda b, hh: (ctx_blk0 + b, 2 * h + hh)),
                  pl.BlockSpec(da_lambda.shape, lambda b, hh: (0, 0)),
                  pl.BlockSpec((1, LANE), lambda b, hh: (0, 0)),
                  pl.BlockSpec(memory_space=pl.ANY)],
        out_specs=pl.BlockSpec((ctx_len, LANE), lambda b, hh: (ctx_blk0 + b, hh)),
        out_shape=jax.ShapeDtypeStruct(yc.shape, yc.dtype),
        scratch_shapes=[pltpu.VMEM((2, ctx_len, 1), F32), pltpu.VMEM((2, ctx_len, 1), F32),
                        pltpu.VMEM((2, ctx_len, DA_V_DIM), F32)],
        input_output_aliases={5: 0},
        compiler_params=_params(("parallel", "parallel")),
        name="diff_attention_ctx",
    )(qkv, qkv, qkv, da_lambda, subln, yc)


def _outproj_kernel(ya_ref, yb_ref, yc_ref, w0_ref, w1_ref, w2_ref, x_ref, gate_ref, o_ref):
    acc = _dot(ya_ref[...], w0_ref[...]) + _dot(yb_ref[...], w1_ref[...]) + _dot(yc_ref[...], w2_ref[...])
    o_ref[...] = x_ref[...] + gate_ref[...] * acc


def _out_proj(ya, yb, yc, w_out, x, mods, n_tiles, cls_of_tile):
    m, d = x.shape
    tn = 1024
    wa, wb = ya.shape[1], yb.shape[1]
    assert wa == wb and yc.shape[1] == wa + wb
    return pl.pallas_call(
        _outproj_kernel,
        grid=(n_tiles, d // tn),
        in_specs=[pl.BlockSpec((ROW_TILE, wa), lambda i, j: (i, 0)),
                  pl.BlockSpec((ROW_TILE, wb), lambda i, j: (i, 0)),
                  pl.BlockSpec((ROW_TILE, wa + wb), lambda i, j: (i, 0)),
                  pl.BlockSpec((wa, tn), lambda i, j: (0, j)),
                  pl.BlockSpec((wb, tn), lambda i, j: (1, j)),
                  pl.BlockSpec((wa + wb, tn), lambda i, j: (1, j)),
                  pl.BlockSpec((ROW_TILE, tn), lambda i, j: (i, j)),
                  pl.BlockSpec((None, 1, tn), lambda i, j: (cls_of_tile(i), 0, 2 * (d // tn) + j))],
        out_specs=pl.BlockSpec((ROW_TILE, tn), lambda i, j: (i, j)),
        out_shape=jax.ShapeDtypeStruct((m, d), F32),
        compiler_params=_params(("parallel", "parallel"), 40),
        name="out_proj_residual",
    )(ya, yb, yc, w_out, w_out, w_out, x, mods)


def _ffn_kernel(x_ref, g_ref, sh_ref, sc_ref, gate_ref, wg_ref, wu_ref, wd_ref, o_ref, h_ref, acc_ref):
    f = pl.program_id(1)

    @pl.when(f == 0)
    def _():
        h_ref[...] = _modulated_norm(x_ref[...], g_ref[...], sh_ref[...], sc_ref[...]).astype(h_ref.dtype)
        acc_ref[...] = jnp.zeros_like(acc_ref)

    h = h_ref[...]
    a = _silu(_dot(h, wg_ref[...])) * _dot(h, wu_ref[...])
    acc_ref[...] += _dot(a.astype(BF16), wd_ref[...])

    @pl.when(f == pl.num_programs(1) - 1)
    def _():
        o_ref[...] = x_ref[...] + gate_ref[...] * acc_ref[...]


def _ffn_dense(x, g, mods, wg, wu, wd, n_tiles, cls_of_tile):
    m, d = x.shape
    ff = wg.shape[1]
    tf = 512
    return pl.pallas_call(
        _ffn_kernel,
        grid=(n_tiles, ff // tf),
        in_specs=[pl.BlockSpec((ROW_TILE, d), lambda i, f: (i, 0)),
                  pl.BlockSpec((1, d), lambda i, f: (0, 0)),
                  _mod_spec(3, d, cls_of_tile), _mod_spec(4, d, cls_of_tile), _mod_spec(5, d, cls_of_tile),
                  pl.BlockSpec((d, tf), lambda i, f: (0, f)),
                  pl.BlockSpec((d, tf), lambda i, f: (0, f)),
                  pl.BlockSpec((tf, d), lambda i, f: (f, 0))],
        out_specs=pl.BlockSpec((ROW_TILE, d), lambda i, f: (i, 0)),
        out_shape=jax.ShapeDtypeStruct((m, d), F32),
        scratch_shapes=[pltpu.VMEM((ROW_TILE, d), BF16), pltpu.VMEM((ROW_TILE, d), F32)],
        compiler_params=_params(("parallel", "arbitrary"), 48),
        name="ffn_dense",
    )(x, g.reshape(1, d), mods, mods, mods, wg, wu, wd)


def _router_kernel(x_ref, g_ref, sh_ref, sc_ref, wr_ref, br_ref, h_ref, comb_ref):
    h = _modulated_norm(x_ref[...], g_ref[...], sh_ref[...], sc_ref[...])
    h_ref[...] = h.astype(h_ref.dtype)
    logits = jnp.dot(h, wr_ref[...], precision=HIGHEST, preferred_element_type=F32) + br_ref[...]
    lane = lax.broadcasted_iota(jnp.int32, logits.shape, 1).astype(F32)
    neg = -jnp.inf
    lg = jnp.where(lane < N_EXPERTS, logits, neg)
    v1 = jnp.max(lg, axis=-1, keepdims=True)
    i1 = jnp.min(jnp.where(lg == v1, lane, float(LANE)), axis=-1, keepdims=True)
    lg2 = jnp.where(lane == i1, neg, lg)
    v2 = jnp.max(lg2, axis=-1, keepdims=True)
    i2 = jnp.min(jnp.where(lg2 == v2, lane, float(LANE)), axis=-1, keepdims=True)
    e = jnp.exp(v2 - v1)
    p1 = 1.0 / (1.0 + e)
    p2 = e / (1.0 + e)
    comb_ref[...] = jnp.where(lane == i1, p1, 0.0) + jnp.where(lane == i2, p2, 0.0)


def _router(x, g, mods, wr, br, n_tiles, cls_of_tile):
    m, d = x.shape
    return pl.pallas_call(
        _router_kernel,
        grid=(n_tiles,),
        in_specs=[pl.BlockSpec((ROW_TILE, d), lambda i: (i, 0)),
                  pl.BlockSpec((1, d), lambda i: (0, 0)),
                  _mod_spec(3, d, cls_of_tile), _mod_spec(4, d, cls_of_tile),
                  pl.BlockSpec((d, LANE), lambda i: (0, 0)),
                  pl.BlockSpec((1, LANE), lambda i: (0, 0))],
        out_specs=[pl.BlockSpec((ROW_TILE, d), lambda i: (i, 0)),
                   pl.BlockSpec((ROW_TILE, LANE), lambda i: (i, 0))],
        out_shape=[jax.ShapeDtypeStruct((n_tiles * ROW_TILE, d), BF16),
                   jax.ShapeDtypeStruct((n_tiles * ROW_TILE, LANE), F32)],
        compiler_params=_params(("parallel",), 40),
        name="moe_router",
    )(x, g.reshape(1, d), mods, mods, wr, br)


def _moe_kernel(h_ref, comb_ref, x_ref, gate_ref, wg_ref, wu_ref, wd_ref, o_ref, acc_ref):
    e = pl.program_id(1)
    f = pl.program_id(2)
    first = jnp.logical_and(e == 0, f == 0)
    last = jnp.logical_and(e == pl.num_programs(1) - 1, f == pl.num_programs(2) - 1)

    @pl.when(first)
    def _():
        acc_ref[...] = jnp.zeros_like(acc_ref)

    h = h_ref[...]
    a = _silu(_dot(h, wg_ref[...])) * _dot(h, wu_ref[...])
    comb = comb_ref[...]
    lane = lax.broadcasted_iota(jnp.int32, comb.shape, 1)
    ce = jnp.sum(jnp.where(lane == e, comb, 0.0), axis=-1, keepdims=True)
    acc_ref[...] += ce * _dot(a.astype(BF16), wd_ref[...])

    @pl.when(last)
    def _():
        o_ref[...] = x_ref[...] + gate_ref[...] * acc_ref[...]


def _moe(h, comb, x, mods, wg, wu, wd, n_tiles, cls_of_tile):
    m, d = x.shape
    n_e, _, ff = wg.shape
    tf = 256
    return pl.pallas_call(
        _moe_kernel,
        grid=(n_tiles, n_e, ff // tf),
        in_specs=[pl.BlockSpec((ROW_TILE, d), lambda i, e, f: (i, 0)),
                  pl.BlockSpec((ROW_TILE, LANE), lambda i, e, f: (i, 0)),
                  pl.BlockSpec((ROW_TILE, d), lambda i, e, f: (i, 0)),
                  _mod_spec(5, d, cls_of_tile),
                  pl.BlockSpec((None, d, tf), lambda i, e, f: (e, 0, f)),
                  pl.BlockSpec((None, d, tf), lambda i, e, f: (e, 0, f)),
                  pl.BlockSpec((None, tf, d), lambda i, e, f: (e, f, 0))],
        out_specs=pl.BlockSpec((ROW_TILE, d), lambda i, e, f: (i, 0)),
        out_shape=jax.ShapeDtypeStruct((m, d), F32),
        scratch_shapes=[pltpu.VMEM((ROW_TILE, d), F32)],
        compiler_params=_params(("parallel", "arbitrary", "arbitrary"), 48),
        name="moe_experts",
    )(h, comb, x, mods, wg, wu, wd)


def _final_kernel(x_ref, g_ref, o_ref):
    x = x_ref[...]
    o_ref[...] = (x * lax.rsqrt(jnp.mean(x * x, axis=-1, keepdims=True) + NORM_EPS)) * g_ref[...]


def _final_norm(x, g, n_tiles):
    d = x.shape[1]
    return pl.pallas_call(
        _final_kernel,
        grid=(n_tiles,),
        in_specs=[pl.BlockSpec((ROW_TILE, d), lambda i: (i, 0)),
                  pl.BlockSpec((1, d), lambda i: (0, 0))],
        out_specs=pl.BlockSpec((ROW_TILE, d), lambda i: (i, 0)),
        out_shape=jax.ShapeDtypeStruct((n_tiles * ROW_TILE, d), F32),
        compiler_params=_params(("parallel",), 40),
        name="final_norm",
    )(x, g.reshape(1, d))


def _pad_lanes(v, width=LANE):
    return jnp.pad(v, [(0, 0)] * (v.ndim - 1) + [(0, width - v.shape[-1])])


def kernel(x, c, ctx, c_ctx, w_mod, b_mod, g_mix, g_ffn, w_in, conv_w, ssd_conv_w, ssd_conv_b, ssd_a_log, ssd_dt_bias, ssd_d, ssd_norm, da_lambda, da_subln, w_out, ffn_w_gate, ffn_w_up, ffn_w_down, moe_w_router, moe_b_router, moe_w_gate, moe_w_up, moe_w_down, g_final):
    batch, seq, d = x.shape
    ctx_len = ctx.shape[1]
    depth = w_mod.shape[0]
    n_lat = batch * seq
    m = n_lat + batch * ctx_len
    assert seq % ROW_TILE == 0 and (batch * ctx_len) % ROW_TILE == 0 and n_lat % ctx_len == 0
    assert ctx_len % HALO_TILE == 0 and ctx_len % SSD_CHUNK == 0 and batch < MOD_CLASSES
    n_lat_tiles = n_lat // ROW_TILE
    n_all_tiles = m // ROW_TILE
    tiles_per_seq = seq // ROW_TILE
    cls_of_tile = lambda i: jnp.minimum(i // tiles_per_seq, batch)

    xs_all = jnp.concatenate([x.reshape(n_lat, d), ctx.reshape(batch * ctx_len, d)], axis=0)
    cvec = jnp.zeros((MOD_CLASSES, d), F32).at[:batch].set(c).at[batch].set(c_ctx)
    mods_all = _mod_vectors(cvec, w_mod, b_mod)
    cos_t, sin_t = _rope_tables(seq)

    for i in range(depth):
        ctx_out = i < depth - 1
        lam_init = 0.8 - 0.6 * math.exp(-0.3 * i)
        n_tiles = n_all_tiles if ctx_out else n_lat_tiles
        mods = mods_all[i].reshape(MOD_CLASSES, 1, N_MOD * d)
        wi = w_in[i]
        w_a = jnp.concatenate([wi[:, COL_CONV:COL_Q], wi[:, COL_XBC:COL_DT]], axis=1).astype(BF16)
        w_b = jnp.concatenate([wi[:, COL_Q:COL_XBC] * DA_SCALE, wi[:, COL_K:COL_V],
                               wi[:, COL_V:COL_V + DA_WIDTH]], axis=1).astype(BF16)
        w_c = _pad_lanes(wi[:, COL_DT:COL_K]).astype(BF16)

        h = _norm_modulate(xs_all, g_mix[i], mods, 0, n_all_tiles, cls_of_tile)
        p_a = _matmul(h, w_a, F32, 1024, "in_proj_conv_z_xbc")
        qkv = _matmul_rope(h, w_b, cos_t, sin_t, n_lat_tiles, tiles_per_seq)
        p_dt = _matmul(h, w_c, F32, LANE, "in_proj_dt")

        ya, xbc, dt2 = _prep(p_a, p_dt, conv_w[i], ssd_conv_w[i], ssd_conv_b[i].reshape(1, -1),
                             _pad_lanes(ssd_dt_bias[i].reshape(1, -1)), n_lat // HALO_TILE,
                             seq // HALO_TILE, ctx_len // HALO_TILE)
        y_dir = _ssd_scan(xbc, dt2, _pad_lanes(ssd_a_log[i]).reshape(2, 1, LANE), batch,
                          seq // SSD_CHUNK, ctx_len // SSD_CHUNK)
        yb = _ssd_gate(y_dir, xbc, p_a, jnp.repeat(ssd_d[i], SSD_HEAD_DIM).reshape(1, -1),
                       ssd_norm[i].reshape(1, -1), n_tiles)

        subln = da_subln[i].reshape(1, -1)
        yc = _attention_lat(qkv, da_lambda[i], subln, lam_init, batch, seq, ctx_len, 512, 512)
        if ctx_out:
            yc = _attention_ctx(qkv, yc, da_lambda[i], subln, lam_init, batch, seq, ctx_len)

        x_mid = _out_proj(ya, yb, yc, w_out[i].astype(BF16), xs_all, mods, n_tiles, cls_of_tile)

        j = i // 2
        if i % 2 == 0:
            xs_all = _ffn_dense(x_mid, g_ffn[i], mods, ffn_w_gate[j].astype(BF16), ffn_w_up[j].astype(BF16),
                                ffn_w_down[j].astype(BF16), n_tiles, cls_of_tile)
        else:
            hh, comb = _router(x_mid, g_ffn[i], mods, _pad_lanes(moe_w_router[j]),
                               _pad_lanes(moe_b_router[j].reshape(1, -1)), n_tiles, cls_of_tile)
            xs_all = _moe(hh, comb, x_mid, mods, moe_w_gate[j].astype(BF16), moe_w_up[j].astype(BF16),
                          moe_w_down[j].astype(BF16), n_tiles, cls_of_tile)

    out = _final_norm(xs_all, g_final, n_lat_tiles)
    return out.reshape(batch, seq, d)
```

```python
import functools
import math

import jax
import jax.numpy as jnp
from jax import lax
from jax.experimental import pallas as pl
from jax.experimental.pallas import tpu as pltpu

NORM_EPS = 1e-6
N_MOD = 6
GRID_W = 64

SHORT_CONV = 3
CONV_W = 512

SSD_HEADS = 8
SSD_HEAD_DIM = 64
SSD_INNER = SSD_HEADS * SSD_HEAD_DIM
SSD_STATE = 128
SSD_GROUPS = 2
SSD_CHUNK = 128
SSD_XBC = SSD_INNER + 2 * SSD_GROUPS * SSD_STATE
SSD_GROUP_W = SSD_INNER // SSD_GROUPS

DA_HEADS = 8
DA_HEAD_DIM = 64
DA_V_DIM = 2 * DA_HEAD_DIM
DA_QK = DA_HEADS * 2 * DA_HEAD_DIM
DA_WIDTH = DA_HEADS * DA_V_DIM
DA_SCALE = DA_HEAD_DIM ** -0.5
LOG2_E = math.log2(math.e)
ROPE_THETA = 10000.0

COL_CONV = 0
COL_Z = COL_CONV + 3 * CONV_W
COL_Q = COL_Z + SSD_INNER
COL_XBC = COL_Q + DA_QK
COL_DT = COL_XBC + SSD_XBC
COL_K = COL_DT + 2 * SSD_HEADS
COL_V = COL_K + DA_QK

N_EXPERTS = 8

LANE = 128
SUBLANE = 8
ROW_TILE = 512
HALO_TILE = 256
MOD_CLASSES = 8

F32 = jnp.float32
BF16 = jnp.bfloat16
HIGHEST = lax.Precision.HIGHEST
MIB = 1024 * 1024


def _params(semantics, vmem_mib=None):
    kw = {"dimension_semantics": semantics}
    if vmem_mib is not None:
        kw["vmem_limit_bytes"] = vmem_mib * MIB
    return pltpu.CompilerParams(**kw)


def _silu(v):
    return v * jax.nn.sigmoid(v)


def _dot(a, b):
    return jnp.dot(a, b, preferred_element_type=F32)


def _dot_nt(a, b):
    return lax.dot_general(a, b, (((1,), (1,)), ((), ())), preferred_element_type=F32)


def _dot_tn(a, b):
    return lax.dot_general(a, b, (((0,), (0,)), ((), ())), preferred_element_type=F32)


def _modulated_norm(x, g, shift, scale):
    ms = jnp.mean(x * x, axis=-1, keepdims=True)
    y = x * lax.rsqrt(ms + NORM_EPS)
    return (y * g) * (1.0 + scale) + shift


def _mod_spec(k, width, cls_of_tile):
    return pl.BlockSpec((None, 1, width), lambda i, *_: (cls_of_tile(i), 0, k))


def _mod_kernel(c_ref, w_ref, b_ref, o_ref):
    s = _silu(c_ref[...])
    o_ref[...] = jnp.dot(s, w_ref[...], precision=HIGHEST, preferred_element_type=F32) + b_ref[...]


def _mod_vectors(cvec, w_mod, b_mod):
    depth, d, n = w_mod.shape
    tn = 1024
    return pl.pallas_call(
        _mod_kernel,
        grid=(depth, n // tn),
        in_specs=[pl.BlockSpec((MOD_CLASSES, d), lambda l, j: (0, 0)),
                  pl.BlockSpec((None, d, tn), lambda l, j: (l, 0, j)),
                  pl.BlockSpec((None, 1, tn), lambda l, j: (l, 0, j))],
        out_specs=pl.BlockSpec((None, MOD_CLASSES, tn), lambda l, j: (l, 0, j)),
        out_shape=jax.ShapeDtypeStruct((depth, MOD_CLASSES, n), F32),
        compiler_params=_params(("parallel", "parallel"), 40),
        name="mod_vectors",
    )(cvec, w_mod, b_mod.reshape(depth, 1, n))


def _norm_kernel(x_ref, g_ref, sh_ref, sc_ref, h_ref):
    h_ref[...] = _modulated_norm(x_ref[...], g_ref[...], sh_ref[...], sc_ref[...]).astype(h_ref.dtype)


def _norm_modulate(x, g, mods, k_shift, n_tiles, cls_of_tile):
    m, d = x.shape
    return pl.pallas_call(
        _norm_kernel,
        grid=(n_tiles,),
        in_specs=[pl.BlockSpec((ROW_TILE, d), lambda i: (i, 0)),
                  pl.BlockSpec((1, d), lambda i: (0, 0)),
                  _mod_spec(k_shift, d, cls_of_tile),
                  _mod_spec(k_shift + 1, d, cls_of_tile)],
        out_specs=pl.BlockSpec((ROW_TILE, d), lambda i: (i, 0)),
        out_shape=jax.ShapeDtypeStruct((m, d), BF16),
        compiler_params=_params(("parallel",), 40),
        name="norm_modulate",
    )(x, g.reshape(1, d), mods, mods)


def _mm_kernel(x_ref, w_ref, o_ref):
    o_ref[...] = _dot(x_ref[...], w_ref[...]).astype(o_ref.dtype)


def _matmul(x, w, out_dtype, tn, name):
    m, k = x.shape
    n = w.shape[1]
    return pl.pallas_call(
        _mm_kernel,
        grid=(m // ROW_TILE, n // tn),
        in_specs=[pl.BlockSpec((ROW_TILE, k), lambda i, j: (i, 0)),
                  pl.BlockSpec((k, tn), lambda i, j: (0, j))],
        out_specs=pl.BlockSpec((ROW_TILE, tn), lambda i, j: (i, j)),
        out_shape=jax.ShapeDtypeStruct((m, n), out_dtype),
        compiler_params=_params(("parallel", "parallel"), 40),
        name=name,
    )(x, w)


def _mm_rope_kernel(x_ref, w_ref, cos_ref, sin_ref, o_ref, *, n_lat_tiles, n_rope_tiles):
    i = pl.program_id(0)
    j = pl.program_id(1)
    acc = _dot(x_ref[...], w_ref[...])
    rotate = jnp.logical_and(i < n_lat_tiles, j < n_rope_tiles)

    @pl.when(rotate)
    def _():
        tm, tn = acc.shape
        lane = lax.broadcasted_iota(jnp.int32, (tm, LANE), 1)
        first = (lane % (DA_HEAD_DIM // 2)) < (DA_HEAD_DIM // 4)
        c = cos_ref[...]
        s = sin_ref[...]
        quarter = DA_HEAD_DIM // 4
        for hb in range(tn // LANE):
            blk = acc[:, hb * LANE:(hb + 1) * LANE]
            partner = jnp.where(first, pltpu.roll(blk, LANE - quarter, 1), pltpu.roll(blk, quarter, 1))
            o_ref[:, hb * LANE:(hb + 1) * LANE] = (blk * c + partner * s).astype(o_ref.dtype)

    @pl.when(jnp.logical_not(rotate))
    def _():
        o_ref[...] = acc.astype(o_ref.dtype)


def _matmul_rope(x, w, cos_t, sin_t, n_lat_tiles, tiles_per_seq):
    m, k = x.shape
    n = w.shape[1]
    tn = DA_QK
    kern = functools.partial(_mm_rope_kernel, n_lat_tiles=n_lat_tiles, n_rope_tiles=2 * DA_QK // tn)
    return pl.pallas_call(
        kern,
        grid=(m // ROW_TILE, n // tn),
        in_specs=[pl.BlockSpec((ROW_TILE, k), lambda i, j: (i, 0)),
                  pl.BlockSpec((k, tn), lambda i, j: (0, j)),
                  pl.BlockSpec((ROW_TILE, LANE), lambda i, j: (i % tiles_per_seq, 0)),
                  pl.BlockSpec((ROW_TILE, LANE), lambda i, j: (i % tiles_per_seq, 0))],
        out_specs=pl.BlockSpec((ROW_TILE, tn), lambda i, j: (i, j)),
        out_shape=jax.ShapeDtypeStruct((m, n), BF16),
        compiler_params=_params(("parallel", "parallel"), 40),
        name="qkv_proj_rope",
    )(x, w, cos_t, sin_t)


def _rope_tables(n_tokens):
    rows = n_tokens // GRID_W
    row = jnp.repeat(jnp.arange(rows, dtype=F32), GRID_W)
    col = jnp.broadcast_to(jnp.arange(GRID_W, dtype=F32), (rows, GRID_W)).reshape(-1)
    n_freq = DA_HEAD_DIM // 4
    inv = ROPE_THETA ** (-jnp.arange(n_freq, dtype=F32) / n_freq)
    ang_r = row[:, None] * inv
    ang_c = col[:, None] * inv
    cos64 = jnp.concatenate([jnp.cos(ang_r), jnp.cos(ang_r), jnp.cos(ang_c), jnp.cos(ang_c)], axis=-1)
    sin64 = jnp.concatenate([-jnp.sin(ang_r), jnp.sin(ang_r), -jnp.sin(ang_c), jnp.sin(ang_c)], axis=-1)
    return jnp.tile(cos64, (1, 2)), jnp.tile(sin64, (1, 2))


def _conv3(u, prev_row, next_row, w_ref, has_prev, has_next):
    tm = u.shape[0]
    row = lax.broadcasted_iota(jnp.int32, u.shape, 0)
    prev_row = jnp.where(has_prev, prev_row, 0.0)
    next_row = jnp.where(has_next, next_row, 0.0)
    before = jnp.where(row == 0, prev_row, pltpu.roll(u, 1, 0))
    after = jnp.where(row == tm - 1, next_row, pltpu.roll(u, tm - 1, 0))
    return before * w_ref[0:1, :] + u * w_ref[1:2, :] + after * w_ref[2:3, :]


def _prep_kernel(cv_ref, cvp_ref, cvn_ref, xb_ref, xbp_ref, xbn_ref, dtr_ref,
                 cw_ref, sw_ref, sb_ref, dtb_ref, ya_ref, xs_ref, dt_ref,
                 *, n_lat_tiles, lat_tiles_per_seq, ctx_tiles_per_seq):
    i = pl.program_id(0)
    is_lat = i < n_lat_tiles
    pos = jnp.where(is_lat, i % lat_tiles_per_seq, (i - n_lat_tiles) % ctx_tiles_per_seq)
    last = jnp.where(is_lat, lat_tiles_per_seq - 1, ctx_tiles_per_seq - 1)
    has_prev = pos != 0
    has_next = pos != last

    w = CONV_W
    cv = cv_ref[...]
    gate_b, u = cv[:, 0:w], cv[:, w:2 * w] * cv[:, 2 * w:3 * w]
    p = cvp_ref[SUBLANE - 1:SUBLANE, :]
    n = cvn_ref[0:1, :]
    u_prev = p[:, w:2 * w] * p[:, 2 * w:3 * w]
    u_next = n[:, w:2 * w] * n[:, 2 * w:3 * w]
    ya_ref[...] = (gate_b * _conv3(u, u_prev, u_next, cw_ref, has_prev, has_next)).astype(ya_ref.dtype)

    xc = _conv3(xb_ref[...], xbp_ref[SUBLANE - 1:SUBLANE, :], xbn_ref[0:1, :], sw_ref, has_prev, has_next)
    xs_ref[...] = _silu(xc + sb_ref[...])

    t = dtr_ref[...] + dtb_ref[...]
    dt = jnp.maximum(t, 0.0) + jnp.log1p(jnp.exp(-jnp.abs(t)))
    dt_ref[0] = dt
    dt_ref[1] = pltpu.roll(dt, LANE - SSD_HEADS, 1)


def _prep(p_a, p_dt, conv_w, ssd_conv_w, ssd_conv_b, dt_bias, n_lat_tiles, lat_tiles_per_seq,
          ctx_tiles_per_seq):
    m = p_a.shape[0]
    tm = HALO_TILE
    per = tm // SUBLANE
    n8 = m // SUBLANE
    cw = 3 * CONV_W
    xbc_blk = COL_Q // SSD_XBC
    prev = lambda i: jnp.maximum(i * per - 1, 0)
    nxt = lambda i: jnp.minimum((i + 1) * per, n8 - 1)
    kern = functools.partial(_prep_kernel, n_lat_tiles=n_lat_tiles, lat_tiles_per_seq=lat_tiles_per_seq,
                             ctx_tiles_per_seq=ctx_tiles_per_seq)
    return pl.pallas_call(
        kern,
        grid=(m // tm,),
        in_specs=[pl.BlockSpec((tm, cw), lambda i: (i, 0)),
                  pl.BlockSpec((SUBLANE, cw), lambda i: (prev(i), 0)),
                  pl.BlockSpec((SUBLANE, cw), lambda i: (nxt(i), 0)),
                  pl.BlockSpec((tm, SSD_XBC), lambda i: (i, xbc_blk)),
                  pl.BlockSpec((SUBLANE, SSD_XBC), lambda i: (prev(i), xbc_blk)),
                  pl.BlockSpec((SUBLANE, SSD_XBC), lambda i: (nxt(i), xbc_blk)),
                  pl.BlockSpec((tm, LANE), lambda i: (i, 0)),
                  pl.BlockSpec((SHORT_CONV, CONV_W), lambda i: (0, 0)),
                  pl.BlockSpec((SHORT_CONV, SSD_XBC), lambda i: (0, 0)),
                  pl.BlockSpec((1, SSD_XBC), lambda i: (0, 0)),
                  pl.BlockSpec((1, LANE), lambda i: (0, 0))],
        out_specs=[pl.BlockSpec((tm, CONV_W), lambda i: (i, 0)),
                   pl.BlockSpec((tm, SSD_XBC), lambda i: (i, 0)),
                   pl.BlockSpec((2, tm, LANE), lambda i: (0, i, 0))],
        out_shape=[jax.ShapeDtypeStruct((m, CONV_W), BF16),
                   jax.ShapeDtypeStruct((m, SSD_XBC), F32),
                   jax.ShapeDtypeStruct((2, m, LANE), F32)],
        compiler_params=_params(("parallel",), 40),
        name="conv_prep",
    )(p_a, p_a, p_a, p_a, p_a, p_a, p_dt, conv_w, ssd_conv_w, ssd_conv_b, dt_bias)


def _ssd_kernel(xs_ref, dt_ref, alog_ref, y_ref, state_ref):
    d = pl.program_id(1)
    step = pl.program_id(2)

    @pl.when(step == 0)
    def _():
        state_ref[...] = jnp.zeros_like(state_ref)

    t = SSD_CHUNK
    r = lax.broadcasted_iota(jnp.int32, (t, t), 0)
    c = lax.broadcasted_iota(jnp.int32, (t, t), 1)
    fwd = d == 0
    mask = jnp.logical_or(jnp.logical_and(fwd, r >= c), jnp.logical_and(jnp.logical_not(fwd), r <= c))
    tri = mask.astype(F32)

    dt = dt_ref[...]
    da = dt * (-jnp.exp(alog_ref[...]))
    cs = jnp.dot(tri, da, precision=HIGHEST, preferred_element_type=F32)
    cs_t = cs.T
    tot = jnp.sum(da, axis=0, keepdims=True)

    er = lax.broadcasted_iota(jnp.int32, (LANE, SSD_INNER), 0)
    ec = lax.broadcasted_iota(jnp.int32, (LANE, SSD_INNER), 1)
    expand = (jnp.right_shift(ec, 6) == er).astype(F32)
    assert SSD_HEAD_DIM == 64
    to_ch = lambda v: jnp.dot(v, expand, precision=HIGHEST, preferred_element_type=F32)
    dt_ch = to_ch(dt)
    cs_ch = to_ch(cs)
    tot_ch = to_ch(jnp.broadcast_to(tot, (SUBLANE, LANE)))[0:1, :]

    xdt = xs_ref[:, 0:SSD_INNER] * dt_ch
    x_state = xdt * jnp.exp(tot_ch - cs_ch)
    y_scale = jnp.exp(cs_ch)
    carry = jnp.exp(tot_ch)
    gw = SSD_GROUP_W
    heads_per_group = SSD_HEADS // SSD_GROUPS
    lane_head = jnp.right_shift(lax.broadcasted_iota(jnp.int32, (t, gw), 1), 6)

    for g in range(SSD_GROUPS):
        b_lo = SSD_INNER + g * SSD_STATE
        c_lo = SSD_INNER + SSD_GROUPS * SSD_STATE + g * SSD_STATE
        bg = xs_ref[:, b_lo:b_lo + SSD_STATE].astype(BF16)
        cg = xs_ref[:, c_lo:c_lo + SSD_STATE].astype(BF16)
        cb = _dot_nt(cg, bg)
        sg = state_ref[g]
        xdt_g = xdt[:, g * gw:(g + 1) * gw]
        y = _dot(cg, sg.astype(BF16)) * y_scale[:, g * gw:(g + 1) * gw]
        for hh in range(heads_per_group):
            h = g * heads_per_group + hh
            decay = jnp.where(mask, jnp.exp(cs[:, h:h + 1] - cs_t[h:h + 1, :]), 0.0)
            x_h = jnp.where(lane_head == hh, xdt_g, 0.0).astype(BF16)
            y = y + _dot((cb * decay).astype(BF16), x_h)
        y_ref[:, g * gw:(g + 1) * gw] = y
        ds = _dot_tn(bg, x_state[:, g * gw:(g + 1) * gw].astype(BF16))
        state_ref[g] = sg * carry[:, g * gw:(g + 1) * gw] + ds


def _ssd_scan(xs, dt2, a_log, batch, n_lat_chunks, n_ctx_chunks):
    m = xs.shape[0]
    t = SSD_CHUNK

    def chunk(b, d, s):
        j_ctx = jnp.where(d == 0, s, n_ctx_chunks - 1 - s)
        sl = s - n_ctx_chunks
        j_lat = jnp.where(d == 0, sl, n_lat_chunks - 1 - sl)
        return jnp.where(s < n_ctx_chunks, batch * n_lat_chunks + b * n_ctx_chunks + j_ctx,
                         b * n_lat_chunks + j_lat)

    return pl.pallas_call(
        _ssd_kernel,
        grid=(batch, 2, n_ctx_chunks + n_lat_chunks),
        in_specs=[pl.BlockSpec((t, SSD_XBC), lambda b, d, s: (chunk(b, d, s), 0)),
                  pl.BlockSpec((None, t, LANE), lambda b, d, s: (d, chunk(b, d, s), 0)),
                  pl.BlockSpec((None, 1, LANE), lambda b, d, s: (d, 0, 0))],
        out_specs=pl.BlockSpec((None, t, SSD_INNER), lambda b, d, s: (d, chunk(b, d, s), 0)),
        out_shape=jax.ShapeDtypeStruct((2, m, SSD_INNER), F32),
        scratch_shapes=[pltpu.VMEM((SSD_GROUPS, SSD_STATE, SSD_GROUP_W), F32)],
        compiler_params=_params(("parallel", "parallel", "arbitrary")),
        name="ssd_scan",
    )(xs, dt2, a_log)


def _ssd_gate_kernel(y_ref, xs_ref, z_ref, d_ref, g_ref, o_ref):
    yl = d_ref[...] * xs_ref[...] + y_ref[0] + y_ref[1]
    u = yl * _silu(z_ref[...])
    gw = SSD_GROUP_W
    for g in range(SSD_GROUPS):
        ug = u[:, g * gw:(g + 1) * gw]
        ug = ug * lax.rsqrt(jnp.mean(ug * ug, axis=-1, keepdims=True) + NORM_EPS)
        o_ref[:, g * gw:(g + 1) * gw] = (ug * g_ref[:, g * gw:(g + 1) * gw]).astype(o_ref.dtype)


def _ssd_gate(y, xs, p_a, d_row, norm_g, n_tiles):
    m = xs.shape[0]
    w = SSD_INNER
    return pl.pallas_call(
        _ssd_gate_kernel,
        grid=(n_tiles,),
        in_specs=[pl.BlockSpec((2, ROW_TILE, w), lambda i: (0, i, 0)),
                  pl.BlockSpec((ROW_TILE, w), lambda i: (i, 0)),
                  pl.BlockSpec((ROW_TILE, w), lambda i: (i, COL_Z // w)),
                  pl.BlockSpec((1, w), lambda i: (0, 0)),
                  pl.BlockSpec((1, w), lambda i: (0, 0))],
        out_specs=pl.BlockSpec((ROW_TILE, w), lambda i: (i, 0)),
        out_shape=jax.ShapeDtypeStruct((m, w), BF16),
        compiler_params=_params(("parallel",)),
        name="ssd_gate_norm",
    )(y, xs, p_a, d_row, norm_g)


ATT_TQ = 512
ATT_TKC = 1024


def _attn_kernel(*refs, has_lat, lam_init):
    if has_lat:
        (q_ref, kc_ref, vct_ref, kl_ref, vlt_ref, lam_ref, g_ref, o_ref,
         m_ref, l_ref, acc_ref, sa_ref, sb_ref) = refs
    else:
        q_ref, kc_ref, vct_ref, lam_ref, g_ref, _, o_ref, m_ref, l_ref, acc_ref, sa_ref, sb_ref = refs

    q = q_ref[...]
    lane = lax.broadcasted_iota(jnp.int32, q.shape, 1)
    zero = jnp.zeros_like(q)
    q_maps = (jnp.where(lane < DA_HEAD_DIM, q, zero), jnp.where(lane >= DA_HEAD_DIM, q, zero))

    def scores(k, s_ref):
        n = k.shape[0]
        maxima = []
        for mp in range(2):
            s = _dot_nt(k, q_maps[mp])
            s_ref[mp, 0:n, :] = s
            maxima.append(jnp.max(s, axis=0, keepdims=True))
        return tuple(maxima)

    def update(s_ref, maxima, vt):
        n = vt.shape[1]
        for mp in range(2):
            m_old = m_ref[mp]
            m_new = jnp.maximum(m_old, maxima[mp])
            alpha = jnp.exp2(m_old - m_new)
            p = jnp.exp2(s_ref[mp, 0:n, :] - m_new[0:1, :])
            l_ref[mp] = alpha * l_ref[mp] + jnp.sum(p, axis=0, keepdims=True)
            acc_ref[mp] = alpha[0:1, :] * acc_ref[mp] + _dot(vt, p.astype(vt.dtype))
            m_ref[mp] = m_new

    m_ref[...] = jnp.full(m_ref.shape, -jnp.inf, F32)
    l_ref[...] = jnp.zeros_like(l_ref)
    acc_ref[...] = jnp.zeros_like(acc_ref)
    mx = scores(kc_ref[...], sa_ref)

    if not has_lat:
        update(sa_ref, mx, vct_ref[...])
    else:
        tkc = min(ATT_TKC, kl_ref.shape[0])
        n_chunks = kl_ref.shape[0] // tkc
        k_at = lambda c: kl_ref[pl.ds(pl.multiple_of(c * tkc, tkc), tkc), :]
        vt_at = lambda c: vlt_ref[:, pl.ds(pl.multiple_of(c * tkc, tkc), tkc)]

        mx_ctx = mx
        mx = scores(k_at(0), sb_ref)
        update(sa_ref, mx_ctx, vct_ref[...])

        def pair(c2, mx_b):
            c = 1 + 2 * c2
            mx_a = scores(k_at(c), sa_ref)
            update(sb_ref, mx_b, vt_at(c - 1))
            mx_b = scores(k_at(c + 1), sb_ref)
            update(sa_ref, mx_a, vt_at(c))
            return mx_b

        mx = lax.fori_loop(0, (n_chunks - 1) // 2, pair, mx)
        if (n_chunks - 1) % 2:
            mx_a = scores(k_at(n_chunks - 1), sa_ref)
            update(sb_ref, mx, vt_at(n_chunks - 2))
            update(sa_ref, mx_a, vt_at(n_chunks - 1))
        else:
            update(sb_ref, mx, vt_at(n_chunks - 1))

    lv = lam_ref[...]
    dotp = lambda a, b: jnp.sum(lv[a:a + 1, :] * lv[b:b + 1, :], axis=-1, keepdims=True)
    lam = jnp.exp(dotp(0, 1)) - jnp.exp(dotp(2, 3)) + lam_init
    inv_l = 1.0 / l_ref[...]
    o = acc_ref[0] * inv_l[0, 0:1, :] - lam * (acc_ref[1] * inv_l[1, 0:1, :])
    y = o * lax.rsqrt(jnp.mean(o * o, axis=0, keepdims=True) + NORM_EPS)
    o_ref[...] = ((y * g_ref[...]) * (1.0 - lam_init)).T.astype(o_ref.dtype)


def _attn_scratch(tq, keys_a, keys_b):
    return [pltpu.VMEM((2, SUBLANE, tq), F32), pltpu.VMEM((2, SUBLANE, tq), F32),
            pltpu.VMEM((2, DA_V_DIM, tq), F32),
            pltpu.VMEM((2, keys_a, tq), F32), pltpu.VMEM((2, keys_b, tq), F32)]


def _attention_lat(qkv, v_t, da_lambda, subln, lam_init, batch, seq, ctx_len):
    m = qkv.shape[0]
    h = DA_HEADS
    tq = ATT_TQ
    nq = seq // tq
    ctx_blk0 = (batch * seq) // ctx_len
    assert seq % tq == 0 and seq % min(ATT_TKC, seq) == 0
    kern = functools.partial(_attn_kernel, has_lat=True, lam_init=lam_init)
    return pl.pallas_call(
        kern,
        grid=(batch, h, nq),
        in_specs=[pl.BlockSpec((tq, LANE), lambda b, hh, i: (b * nq + i, hh)),
                  pl.BlockSpec((ctx_len, LANE), lambda b, hh, i: (ctx_blk0 + b, h + hh)),
                  pl.BlockSpec((DA_V_DIM, ctx_len), lambda b, hh, i: (hh, ctx_blk0 + b)),
                  pl.BlockSpec((seq, LANE), lambda b, hh, i: (b, h + hh)),
                  pl.BlockSpec((DA_V_DIM, seq), lambda b, hh, i: (hh, b)),
                  pl.BlockSpec(da_lambda.shape, lambda b, hh, i: (0, 0)),
                  pl.BlockSpec((DA_V_DIM, 1), lambda b, hh, i: (0, 0))],
        out_specs=pl.BlockSpec((tq, LANE), lambda b, hh, i: (b * nq + i, hh)),
        out_shape=jax.ShapeDtypeStruct((m, DA_WIDTH), BF16),
        scratch_shapes=_attn_scratch(tq, max(min(ATT_TKC, seq), ctx_len), min(ATT_TKC, seq)),
        compiler_params=_params(("parallel", "parallel", "parallel"), 48),
        name="diff_attention",
    )(qkv, qkv, v_t, qkv, v_t, da_lambda, subln)


def _attention_ctx(qkv, v_t, yc, da_lambda, subln, lam_init, batch, seq, ctx_len):
    h = DA_HEADS
    ctx_blk0 = (batch * seq) // ctx_len
    kern = functools.partial(_attn_kernel, has_lat=False, lam_init=lam_init)
    return pl.pallas_call(
        kern,
        grid=(batch, h),
        in_specs=[pl.BlockSpec((ctx_len, LANE), lambda b, hh: (ctx_blk0 + b, hh)),
                  pl.BlockSpec((ctx_len, LANE), lambda b, hh: (ctx_blk0 + b, h + hh)),
                  pl.BlockSpec((DA_V_DIM, ctx_len), lambda b, hh: (hh, ctx_blk0 + b)),
                  pl.BlockSpec(da_lambda.shape, lambda b, hh: (0, 0)),
                  pl.BlockSpec((DA_V_DIM, 1), lambda b, hh: (0, 0)),
                  pl.BlockSpec(memory_space=pl.ANY)],
        out_specs=pl.BlockSpec((ctx_len, LANE), lambda b, hh: (ctx_blk0 + b, hh)),
        out_shape=jax.ShapeDtypeStruct(yc.shape, yc.dtype),
        scratch_shapes=_attn_scratch(ctx_len, ctx_len, SUBLANE),
        input_output_aliases={5: 0},
        compiler_params=_params(("parallel", "parallel")),
        name="diff_attention_ctx",
    )(qkv, qkv, v_t, da_lambda, subln, yc)


def _outproj_kernel(ya_ref, yb_ref, yc_ref, w0_ref, w1_ref, w2_ref, x_ref, gate_ref, o_ref):
    acc = _dot(ya_ref[...], w0_ref[...]) + _dot(yb_ref[...], w1_ref[...]) + _dot(yc_ref[...], w2_ref[...])
    o_ref[...] = x_ref[...] + gate_ref[...] * acc


def _out_proj(ya, yb, yc, w_out, x, mods, n_tiles, cls_of_tile):
    m, d = x.shape
    tn = 1024
    wa, wb = ya.shape[1], yb.shape[1]
    assert wa == wb and yc.shape[1] == wa + wb
    return pl.pallas_call(
        _outproj_kernel,
        grid=(n_tiles, d // tn),
        in_specs=[pl.BlockSpec((ROW_TILE, wa), lambda i, j: (i, 0)),
                  pl.BlockSpec((ROW_TILE, wb), lambda i, j: (i, 0)),
                  pl.BlockSpec((ROW_TILE, wa + wb), lambda i, j: (i, 0)),
                  pl.BlockSpec((wa, tn), lambda i, j: (0, j)),
                  pl.BlockSpec((wb, tn), lambda i, j: (1, j)),
                  pl.BlockSpec((wa + wb, tn), lambda i, j: (1, j)),
                  pl.BlockSpec((ROW_TILE, tn), lambda i, j: (i, j)),
                  pl.BlockSpec((None, 1, tn), lambda i, j: (cls_of_tile(i), 0, 2 * (d // tn) + j))],
        out_specs=pl.BlockSpec((ROW_TILE, tn), lambda i, j: (i, j)),
        out_shape=jax.ShapeDtypeStruct((m, d), F32),
        compiler_params=_params(("parallel", "parallel"), 40),
        name="out_proj_residual",
    )(ya, yb, yc, w_out, w_out, w_out, x, mods)


def _ffn_kernel(x_ref, g_ref, sh_ref, sc_ref, gate_ref, wg_ref, wu_ref, wd_ref, o_ref, h_ref, acc_ref):
    f = pl.program_id(1)

    @pl.when(f == 0)
    def _():
        h_ref[...] = _modulated_norm(x_ref[...], g_ref[...], sh_ref[...], sc_ref[...]).astype(h_ref.dtype)
        acc_ref[...] = jnp.zeros_like(acc_ref)

    h = h_ref[...]
    a = _silu(_dot(h, wg_ref[...])) * _dot(h, wu_ref[...])
    acc_ref[...] += _dot(a.astype(BF16), wd_ref[...])

    @pl.when(f == pl.num_programs(1) - 1)
    def _():
        o_ref[...] = x_ref[...] + gate_ref[...] * acc_ref[...]


def _ffn_dense(x, g, mods, wg, wu, wd, n_tiles, cls_of_tile):
    m, d = x.shape
    ff = wg.shape[1]
    tf = 512
    return pl.pallas_call(
        _ffn_kernel,
        grid=(n_tiles, ff // tf),
        in_specs=[pl.BlockSpec((ROW_TILE, d), lambda i, f: (i, 0)),
                  pl.BlockSpec((1, d), lambda i, f: (0, 0)),
                  _mod_spec(3, d, cls_of_tile), _mod_spec(4, d, cls_of_tile), _mod_spec(5, d, cls_of_tile),
                  pl.BlockSpec((d, tf), lambda i, f: (0, f)),
                  pl.BlockSpec((d, tf), lambda i, f: (0, f)),
                  pl.BlockSpec((tf, d), lambda i, f: (f, 0))],
        out_specs=pl.BlockSpec((ROW_TILE, d), lambda i, f: (i, 0)),
        out_shape=jax.ShapeDtypeStruct((m, d), F32),
        scratch_shapes=[pltpu.VMEM((ROW_TILE, d), BF16), pltpu.VMEM((ROW_TILE, d), F32)],
        compiler_params=_params(("parallel", "arbitrary"), 48),
        name="ffn_dense",
    )(x, g.reshape(1, d), mods, mods, mods, wg, wu, wd)


def _router_kernel(x_ref, g_ref, sh_ref, sc_ref, wr_ref, br_ref, h_ref, comb_ref):
    h = _modulated_norm(x_ref[...], g_ref[...], sh_ref[...], sc_ref[...])
    h_ref[...] = h.astype(h_ref.dtype)
    logits = jnp.dot(h, wr_ref[...], precision=HIGHEST, preferred_element_type=F32) + br_ref[...]
    lane = lax.broadcasted_iota(jnp.int32, logits.shape, 1).astype(F32)
    neg = -jnp.inf
    lg = jnp.where(lane < N_EXPERTS, logits, neg)
    v1 = jnp.max(lg, axis=-1, keepdims=True)
    i1 = jnp.min(jnp.where(lg == v1, lane, float(LANE)), axis=-1, keepdims=True)
    lg2 = jnp.where(lane == i1, neg, lg)
    v2 = jnp.max(lg2, axis=-1, keepdims=True)
    i2 = jnp.min(jnp.where(lg2 == v2, lane, float(LANE)), axis=-1, keepdims=True)
    e = jnp.exp(v2 - v1)
    p1 = 1.0 / (1.0 + e)
    p2 = e / (1.0 + e)
    comb_ref[...] = jnp.where(lane == i1, p1, 0.0) + jnp.where(lane == i2, p2, 0.0)


def _router(x, g, mods, wr, br, n_tiles, cls_of_tile):
    m, d = x.shape
    return pl.pallas_call(
        _router_kernel,
        grid=(n_tiles,),
        in_specs=[pl.BlockSpec((ROW_TILE, d), lambda i: (i, 0)),
                  pl.BlockSpec((1, d), lambda i: (0, 0)),
                  _mod_spec(3, d, cls_of_tile), _mod_spec(4, d, cls_of_tile),
                  pl.BlockSpec((d, LANE), lambda i: (0, 0)),
                  pl.BlockSpec((1, LANE), lambda i: (0, 0))],
        out_specs=[pl.BlockSpec((ROW_TILE, d), lambda i: (i, 0)),
                   pl.BlockSpec((ROW_TILE, LANE), lambda i: (i, 0))],
        out_shape=[jax.ShapeDtypeStruct((n_tiles * ROW_TILE, d), BF16),
                   jax.ShapeDtypeStruct((n_tiles * ROW_TILE, LANE), F32)],
        compiler_params=_params(("parallel",), 40),
        name="moe_router",
    )(x, g.reshape(1, d), mods, mods, wr, br)


def _moe_kernel(h_ref, comb_ref, x_ref, gate_ref, wg_ref, wu_ref, wd_ref, o_ref, acc_ref):
    e = pl.program_id(1)
    f = pl.program_id(2)
    first = jnp.logical_and(e == 0, f == 0)
    last = jnp.logical_and(e == pl.num_programs(1) - 1, f == pl.num_programs(2) - 1)

    @pl.when(first)
    def _():
        acc_ref[...] = jnp.zeros_like(acc_ref)

    h = h_ref[...]
    a = _silu(_dot(h, wg_ref[...])) * _dot(h, wu_ref[...])
    comb = comb_ref[...]
    lane = lax.broadcasted_iota(jnp.int32, comb.shape, 1)
    ce = jnp.sum(jnp.where(lane == e, comb, 0.0), axis=-1, keepdims=True)
    acc_ref[...] += ce * _dot(a.astype(BF16), wd_ref[...])

    @pl.when(last)
    def _():
        o_ref[...] = x_ref[...] + gate_ref[...] * acc_ref[...]


def _moe(h, comb, x, mods, wg, wu, wd, n_tiles, cls_of_tile):
    m, d = x.shape
    n_e, _, ff = wg.shape
    tf = 256
    return pl.pallas_call(
        _moe_kernel,
        grid=(n_tiles, n_e, ff // tf),
        in_specs=[pl.BlockSpec((ROW_TILE, d), lambda i, e, f: (i, 0)),
                  pl.BlockSpec((ROW_TILE, LANE), lambda i, e, f: (i, 0)),
                  pl.BlockSpec((ROW_TILE, d), lambda i, e, f: (i, 0)),
                  _mod_spec(5, d, cls_of_tile),
                  pl.BlockSpec((None, d, tf), lambda i, e, f: (e, 0, f)),
                  pl.BlockSpec((None, d, tf), lambda i, e, f: (e, 0, f)),
                  pl.BlockSpec((None, tf, d), lambda i, e, f: (e, f, 0))],
        out_specs=pl.BlockSpec((ROW_TILE, d), lambda i, e, f: (i, 0)),
        out_shape=jax.ShapeDtypeStruct((m, d), F32),
        scratch_shapes=[pltpu.VMEM((ROW_TILE, d), F32)],
        compiler_params=_params(("parallel", "arbitrary", "arbitrary"), 48),
        name="moe_experts",
    )(h, comb, x, mods, wg, wu, wd)


def _final_kernel(x_ref, g_ref, o_ref):
    x = x_ref[...]
    o_ref[...] = (x * lax.rsqrt(jnp.mean(x * x, axis=-1, keepdims=True) + NORM_EPS)) * g_ref[...]


def _final_norm(x, g, n_tiles):
    d = x.shape[1]
    return pl.pallas_call(
        _final_kernel,
        grid=(n_tiles,),
        in_specs=[pl.BlockSpec((ROW_TILE, d), lambda i: (i, 0)),
                  pl.BlockSpec((1, d), lambda i: (0, 0))],
        out_specs=pl.BlockSpec((ROW_TILE, d), lambda i: (i, 0)),
        out_shape=jax.ShapeDtypeStruct((n_tiles * ROW_TILE, d), F32),
        compiler_params=_params(("parallel",), 40),
        name="final_norm",
    )(x, g.reshape(1, d))


def _pad_lanes(v, width=LANE):
    return jnp.pad(v, [(0, 0)] * (v.ndim - 1) + [(0, width - v.shape[-1])])


def kernel(x, c, ctx, c_ctx, w_mod, b_mod, g_mix, g_ffn, w_in, conv_w, ssd_conv_w, ssd_conv_b, ssd_a_log, ssd_dt_bias, ssd_d, ssd_norm, da_lambda, da_subln, w_out, ffn_w_gate, ffn_w_up, ffn_w_down, moe_w_router, moe_b_router, moe_w_gate, moe_w_up, moe_w_down, g_final):
    batch, seq, d = x.shape
    ctx_len = ctx.shape[1]
    depth = w_mod.shape[0]
    n_lat = batch * seq
    m = n_lat + batch * ctx_len
    assert seq % ROW_TILE == 0 and (batch * ctx_len) % ROW_TILE == 0 and n_lat % ctx_len == 0
    assert ctx_len % HALO_TILE == 0 and ctx_len % SSD_CHUNK == 0 and batch < MOD_CLASSES
    n_lat_tiles = n_lat // ROW_TILE
    n_all_tiles = m // ROW_TILE
    tiles_per_seq = seq // ROW_TILE
    cls_of_tile = lambda i: jnp.minimum(i // tiles_per_seq, batch)

    xs_all = jnp.concatenate([x.reshape(n_lat, d), ctx.reshape(batch * ctx_len, d)], axis=0)
    cvec = jnp.zeros((MOD_CLASSES, d), F32).at[:batch].set(c).at[batch].set(c_ctx)
    mods_all = _mod_vectors(cvec, w_mod, b_mod)
    cos_t, sin_t = _rope_tables(seq)

    for i in range(depth):
        ctx_out = i < depth - 1
        lam_init = 0.8 - 0.6 * math.exp(-0.3 * i)
        n_tiles = n_all_tiles if ctx_out else n_lat_tiles
        mods = mods_all[i].reshape(MOD_CLASSES, 1, N_MOD * d)
        wi = w_in[i]
        w_a = jnp.concatenate([wi[:, COL_CONV:COL_Q], wi[:, COL_XBC:COL_DT]], axis=1).astype(BF16)
        w_b = jnp.concatenate([wi[:, COL_Q:COL_XBC] * (DA_SCALE * LOG2_E), wi[:, COL_K:COL_V],
                               wi[:, COL_V:COL_V + DA_WIDTH]], axis=1).astype(BF16)
        w_c = _pad_lanes(wi[:, COL_DT:COL_K]).astype(BF16)

        h = _norm_modulate(xs_all, g_mix[i], mods, 0, n_all_tiles, cls_of_tile)
        p_a = _matmul(h, w_a, F32, 1024, "in_proj_conv_z_xbc")
        qkv = _matmul_rope(h, w_b, cos_t, sin_t, n_lat_tiles, tiles_per_seq)
        p_dt = _matmul(h, w_c, F32, LANE, "in_proj_dt")

        ya, xbc, dt2 = _prep(p_a, p_dt, conv_w[i], ssd_conv_w[i], ssd_conv_b[i].reshape(1, -1),
                             _pad_lanes(ssd_dt_bias[i].reshape(1, -1)), n_lat // HALO_TILE,
                             seq // HALO_TILE, ctx_len // HALO_TILE)
        y_dir = _ssd_scan(xbc, dt2, _pad_lanes(ssd_a_log[i]).reshape(2, 1, LANE), batch,
                          seq // SSD_CHUNK, ctx_len // SSD_CHUNK)
        yb = _ssd_gate(y_dir, xbc, p_a, jnp.repeat(ssd_d[i], SSD_HEAD_DIM).reshape(1, -1),
                       ssd_norm[i].reshape(1, -1), n_tiles)

        subln = da_subln[i].reshape(-1, 1)
        v_t = qkv[:, 2 * DA_QK:].T
        yc = _attention_lat(qkv, v_t, da_lambda[i], subln, lam_init, batch, seq, ctx_len)
        if ctx_out:
            yc = _attention_ctx(qkv, v_t, yc, da_lambda[i], subln, lam_init, batch, seq, ctx_len)

        x_mid = _out_proj(ya, yb, yc, w_out[i].astype(BF16), xs_all, mods, n_tiles, cls_of_tile)

        j = i // 2
        if i % 2 == 0:
            xs_all = _ffn_dense(x_mid, g_ffn[i], mods, ffn_w_gate[j].astype(BF16), ffn_w_up[j].astype(BF16),
                                ffn_w_down[j].astype(BF16), n_tiles, cls_of_tile)
        else:
            hh, comb = _router(x_mid, g_ffn[i], mods, _pad_lanes(moe_w_router[j]),
                               _pad_lanes(moe_b_router[j].reshape(1, -1)), n_tiles, cls_of_tile)
            xs_all = _moe(hh, comb, x_mid, mods, moe_w_gate[j].astype(BF16), moe_w_up[j].astype(BF16),
                          moe_w_down[j].astype(BF16), n_tiles, cls_of_tile)

    out = _final_norm(xs_all, g_final, n_lat_tiles)
    return out.reshape(batch, seq, d)
```

```python
import functools
import math

import jax
import jax.numpy as jnp
from jax import lax
from jax.experimental import pallas as pl
from jax.experimental.pallas import tpu as pltpu

NORM_EPS = 1e-6
N_MOD = 6
GRID_W = 64

SHORT_CONV = 3
CONV_W = 512

SSD_HEADS = 8
SSD_HEAD_DIM = 64
SSD_INNER = SSD_HEADS * SSD_HEAD_DIM
SSD_STATE = 128
SSD_GROUPS = 2
SSD_CHUNK = 128
SSD_XBC = SSD_INNER + 2 * SSD_GROUPS * SSD_STATE
SSD_GROUP_W = SSD_INNER // SSD_GROUPS

DA_HEADS = 8
DA_HEAD_DIM = 64
DA_V_DIM = 2 * DA_HEAD_DIM
DA_QK = DA_HEADS * 2 * DA_HEAD_DIM
DA_WIDTH = DA_HEADS * DA_V_DIM
DA_SCALE = DA_HEAD_DIM ** -0.5
LOG2_E = math.log2(math.e)
ROPE_THETA = 10000.0

COL_CONV = 0
COL_Z = COL_CONV + 3 * CONV_W
COL_Q = COL_Z + SSD_INNER
COL_XBC = COL_Q + DA_QK
COL_DT = COL_XBC + SSD_XBC
COL_K = COL_DT + 2 * SSD_HEADS
COL_V = COL_K + DA_QK

N_EXPERTS = 8

LANE = 128
SUBLANE = 8
ROW_TILE = 512
HALO_TILE = 256
MOD_CLASSES = 8

F32 = jnp.float32
BF16 = jnp.bfloat16
HIGHEST = lax.Precision.HIGHEST
MIB = 1024 * 1024


def _params(semantics, vmem_mib=None):
    kw = {"dimension_semantics": semantics}
    if vmem_mib is not None:
        kw["vmem_limit_bytes"] = vmem_mib * MIB
    return pltpu.CompilerParams(**kw)


def _silu(v):
    return v * jax.nn.sigmoid(v)


def _dot(a, b):
    return jnp.dot(a, b, preferred_element_type=F32)


def _dot_nt(a, b):
    return lax.dot_general(a, b, (((1,), (1,)), ((), ())), preferred_element_type=F32)


def _dot_tn(a, b):
    return lax.dot_general(a, b, (((0,), (0,)), ((), ())), preferred_element_type=F32)


def _modulated_norm(x, g, shift, scale):
    ms = jnp.mean(x * x, axis=-1, keepdims=True)
    y = x * lax.rsqrt(ms + NORM_EPS)
    return (y * g) * (1.0 + scale) + shift


def _mod_spec(k, width, cls_of_tile):
    return pl.BlockSpec((None, 1, width), lambda i, *_: (cls_of_tile(i), 0, k))


def _mod_kernel(c_ref, w_ref, b_ref, o_ref):
    s = _silu(c_ref[...])
    o_ref[...] = jnp.dot(s, w_ref[...], precision=HIGHEST, preferred_element_type=F32) + b_ref[...]


def _mod_vectors(cvec, w_mod, b_mod):
    depth, d, n = w_mod.shape
    tn = 1024
    return pl.pallas_call(
        _mod_kernel,
        grid=(depth, n // tn),
        in_specs=[pl.BlockSpec((MOD_CLASSES, d), lambda l, j: (0, 0)),
                  pl.BlockSpec((None, d, tn), lambda l, j: (l, 0, j)),
                  pl.BlockSpec((None, 1, tn), lambda l, j: (l, 0, j))],
        out_specs=pl.BlockSpec((None, MOD_CLASSES, tn), lambda l, j: (l, 0, j)),
        out_shape=jax.ShapeDtypeStruct((depth, MOD_CLASSES, n), F32),
        compiler_params=_params(("parallel", "parallel"), 40),
        name="mod_vectors",
    )(cvec, w_mod, b_mod.reshape(depth, 1, n))


def _norm_kernel(x_ref, g_ref, sh_ref, sc_ref, h_ref):
    h_ref[...] = _modulated_norm(x_ref[...], g_ref[...], sh_ref[...], sc_ref[...]).astype(h_ref.dtype)


def _norm_modulate(x, g, mods, k_shift, n_tiles, cls_of_tile):
    m, d = x.shape
    return pl.pallas_call(
        _norm_kernel,
        grid=(n_tiles,),
        in_specs=[pl.BlockSpec((ROW_TILE, d), lambda i: (i, 0)),
                  pl.BlockSpec((1, d), lambda i: (0, 0)),
                  _mod_spec(k_shift, d, cls_of_tile),
                  _mod_spec(k_shift + 1, d, cls_of_tile)],
        out_specs=pl.BlockSpec((ROW_TILE, d), lambda i: (i, 0)),
        out_shape=jax.ShapeDtypeStruct((m, d), BF16),
        compiler_params=_params(("parallel",), 40),
        name="norm_modulate",
    )(x, g.reshape(1, d), mods, mods)


def _mm_kernel(x_ref, w_ref, o_ref):
    o_ref[...] = _dot(x_ref[...], w_ref[...]).astype(o_ref.dtype)


def _matmul(x, w, out_dtype, tn, name):
    m, k = x.shape
    n = w.shape[1]
    return pl.pallas_call(
        _mm_kernel,
        grid=(m // ROW_TILE, n // tn),
        in_specs=[pl.BlockSpec((ROW_TILE, k), lambda i, j: (i, 0)),
                  pl.BlockSpec((k, tn), lambda i, j: (0, j))],
        out_specs=pl.BlockSpec((ROW_TILE, tn), lambda i, j: (i, j)),
        out_shape=jax.ShapeDtypeStruct((m, n), out_dtype),
        compiler_params=_params(("parallel", "parallel"), 40),
        name=name,
    )(x, w)


def _mm_rope_kernel(x_ref, w_ref, cos_ref, sin_ref, o_ref, *, n_lat_tiles, n_rope_tiles):
    i = pl.program_id(0)
    j = pl.program_id(1)
    acc = _dot(x_ref[...], w_ref[...])
    rotate = jnp.logical_and(i < n_lat_tiles, j < n_rope_tiles)

    @pl.when(rotate)
    def _():
        tm, tn = acc.shape
        lane = lax.broadcasted_iota(jnp.int32, (tm, LANE), 1)
        first = (lane % (DA_HEAD_DIM // 2)) < (DA_HEAD_DIM // 4)
        c = cos_ref[...]
        s = sin_ref[...]
        quarter = DA_HEAD_DIM // 4
        for hb in range(tn // LANE):
            blk = acc[:, hb * LANE:(hb + 1) * LANE]
            partner = jnp.where(first, pltpu.roll(blk, LANE - quarter, 1), pltpu.roll(blk, quarter, 1))
            o_ref[:, hb * LANE:(hb + 1) * LANE] = (blk * c + partner * s).astype(o_ref.dtype)

    @pl.when(jnp.logical_not(rotate))
    def _():
        o_ref[...] = acc.astype(o_ref.dtype)


def _matmul_rope(x, w, cos_t, sin_t, n_lat_tiles, tiles_per_seq):
    m, k = x.shape
    n = w.shape[1]
    tn = DA_QK
    kern = functools.partial(_mm_rope_kernel, n_lat_tiles=n_lat_tiles, n_rope_tiles=2 * DA_QK // tn)
    return pl.pallas_call(
        kern,
        grid=(m // ROW_TILE, n // tn),
        in_specs=[pl.BlockSpec((ROW_TILE, k), lambda i, j: (i, 0)),
                  pl.BlockSpec((k, tn), lambda i, j: (0, j)),
                  pl.BlockSpec((ROW_TILE, LANE), lambda i, j: (i % tiles_per_seq, 0)),
                  pl.BlockSpec((ROW_TILE, LANE), lambda i, j: (i % tiles_per_seq, 0))],
        out_specs=pl.BlockSpec((ROW_TILE, tn), lambda i, j: (i, j)),
        out_shape=jax.ShapeDtypeStruct((m, n), BF16),
        compiler_params=_params(("parallel", "parallel"), 40),
        name="qkv_proj_rope",
    )(x, w, cos_t, sin_t)


def _rope_tables(n_tokens):
    rows = n_tokens // GRID_W
    row = jnp.repeat(jnp.arange(rows, dtype=F32), GRID_W)
    col = jnp.broadcast_to(jnp.arange(GRID_W, dtype=F32), (rows, GRID_W)).reshape(-1)
    n_freq = DA_HEAD_DIM // 4
    inv = ROPE_THETA ** (-jnp.arange(n_freq, dtype=F32) / n_freq)
    ang_r = row[:, None] * inv
    ang_c = col[:, None] * inv
    cos64 = jnp.concatenate([jnp.cos(ang_r), jnp.cos(ang_r), jnp.cos(ang_c), jnp.cos(ang_c)], axis=-1)
    sin64 = jnp.concatenate([-jnp.sin(ang_r), jnp.sin(ang_r), -jnp.sin(ang_c), jnp.sin(ang_c)], axis=-1)
    return jnp.tile(cos64, (1, 2)), jnp.tile(sin64, (1, 2))


def _conv3(u, prev_row, next_row, w_ref, has_prev, has_next):
    tm = u.shape[0]
    row = lax.broadcasted_iota(jnp.int32, u.shape, 0)
    prev_row = jnp.where(has_prev, prev_row, 0.0)
    next_row = jnp.where(has_next, next_row, 0.0)
    before = jnp.where(row == 0, prev_row, pltpu.roll(u, 1, 0))
    after = jnp.where(row == tm - 1, next_row, pltpu.roll(u, tm - 1, 0))
    return before * w_ref[0:1, :] + u * w_ref[1:2, :] + after * w_ref[2:3, :]


def _prep_kernel(cv_ref, cvp_ref, cvn_ref, xb_ref, xbp_ref, xbn_ref, dtr_ref,
                 cw_ref, sw_ref, sb_ref, dtb_ref, ya_ref, xs_ref, dt_ref,
                 *, n_lat_tiles, lat_tiles_per_seq, ctx_tiles_per_seq):
    i = pl.program_id(0)
    is_lat = i < n_lat_tiles
    pos = jnp.where(is_lat, i % lat_tiles_per_seq, (i - n_lat_tiles) % ctx_tiles_per_seq)
    last = jnp.where(is_lat, lat_tiles_per_seq - 1, ctx_tiles_per_seq - 1)
    has_prev = pos != 0
    has_next = pos != last

    w = CONV_W
    cv = cv_ref[...]
    gate_b, u = cv[:, 0:w], cv[:, w:2 * w] * cv[:, 2 * w:3 * w]
    p = cvp_ref[SUBLANE - 1:SUBLANE, :]
    n = cvn_ref[0:1, :]
    u_prev = p[:, w:2 * w] * p[:, 2 * w:3 * w]
    u_next = n[:, w:2 * w] * n[:, 2 * w:3 * w]
    ya_ref[...] = (gate_b * _conv3(u, u_prev, u_next, cw_ref, has_prev, has_next)).astype(ya_ref.dtype)

    xc = _conv3(xb_ref[...], xbp_ref[SUBLANE - 1:SUBLANE, :], xbn_ref[0:1, :], sw_ref, has_prev, has_next)
    xs_ref[...] = _silu(xc + sb_ref[...])

    t = dtr_ref[...] + dtb_ref[...]
    dt = jnp.maximum(t, 0.0) + jnp.log1p(jnp.exp(-jnp.abs(t)))
    dt_ref[0] = dt
    dt_ref[1] = pltpu.roll(dt, LANE - SSD_HEADS, 1)


def _prep(p_a, p_dt, conv_w, ssd_conv_w, ssd_conv_b, dt_bias, n_lat_tiles, lat_tiles_per_seq,
          ctx_tiles_per_seq):
    m = p_a.shape[0]
    tm = HALO_TILE
    per = tm // SUBLANE
    n8 = m // SUBLANE
    cw = 3 * CONV_W
    xbc_blk = COL_Q // SSD_XBC
    prev = lambda i: jnp.maximum(i * per - 1, 0)
    nxt = lambda i: jnp.minimum((i + 1) * per, n8 - 1)
    kern = functools.partial(_prep_kernel, n_lat_tiles=n_lat_tiles, lat_tiles_per_seq=lat_tiles_per_seq,
                             ctx_tiles_per_seq=ctx_tiles_per_seq)
    return pl.pallas_call(
        kern,
        grid=(m // tm,),
        in_specs=[pl.BlockSpec((tm, cw), lambda i: (i, 0)),
                  pl.BlockSpec((SUBLANE, cw), lambda i: (prev(i), 0)),
                  pl.BlockSpec((SUBLANE, cw), lambda i: (nxt(i), 0)),
                  pl.BlockSpec((tm, SSD_XBC), lambda i: (i, xbc_blk)),
                  pl.BlockSpec((SUBLANE, SSD_XBC), lambda i: (prev(i), xbc_blk)),
                  pl.BlockSpec((SUBLANE, SSD_XBC), lambda i: (nxt(i), xbc_blk)),
                  pl.BlockSpec((tm, LANE), lambda i: (i, 0)),
                  pl.BlockSpec((SHORT_CONV, CONV_W), lambda i: (0, 0)),
                  pl.BlockSpec((SHORT_CONV, SSD_XBC), lambda i: (0, 0)),
                  pl.BlockSpec((1, SSD_XBC), lambda i: (0, 0)),
                  pl.BlockSpec((1, LANE), lambda i: (0, 0))],
        out_specs=[pl.BlockSpec((tm, CONV_W), lambda i: (i, 0)),
                   pl.BlockSpec((tm, SSD_XBC), lambda i: (i, 0)),
                   pl.BlockSpec((2, tm, LANE), lambda i: (0, i, 0))],
        out_shape=[jax.ShapeDtypeStruct((m, CONV_W), BF16),
                   jax.ShapeDtypeStruct((m, SSD_XBC), F32),
                   jax.ShapeDtypeStruct((2, m, LANE), F32)],
        compiler_params=_params(("parallel",), 40),
        name="conv_prep",
    )(p_a, p_a, p_a, p_a, p_a, p_a, p_dt, conv_w, ssd_conv_w, ssd_conv_b, dt_bias)


def _ssd_kernel(xs_ref, dt_ref, alog_ref, y_ref, state_ref):
    d = pl.program_id(1)
    step = pl.program_id(2)

    @pl.when(step == 0)
    def _():
        state_ref[...] = jnp.zeros_like(state_ref)

    t = SSD_CHUNK
    r = lax.broadcasted_iota(jnp.int32, (t, t), 0)
    c = lax.broadcasted_iota(jnp.int32, (t, t), 1)
    fwd = d == 0
    mask = jnp.logical_or(jnp.logical_and(fwd, r >= c), jnp.logical_and(jnp.logical_not(fwd), r <= c))
    tri = mask.astype(F32)

    dt = dt_ref[...]
    da = dt * (-jnp.exp(alog_ref[...]))
    cs = jnp.dot(tri, da, precision=HIGHEST, preferred_element_type=F32)
    cs_t = cs.T
    tot = jnp.sum(da, axis=0, keepdims=True)

    er = lax.broadcasted_iota(jnp.int32, (LANE, SSD_INNER), 0)
    ec = lax.broadcasted_iota(jnp.int32, (LANE, SSD_INNER), 1)
    expand = (jnp.right_shift(ec, 6) == er).astype(F32)
    assert SSD_HEAD_DIM == 64
    to_ch = lambda v: jnp.dot(v, expand, precision=HIGHEST, preferred_element_type=F32)
    dt_ch = to_ch(dt)
    cs_ch = to_ch(cs)
    tot_ch = to_ch(jnp.broadcast_to(tot, (SUBLANE, LANE)))[0:1, :]

    xdt = xs_ref[:, 0:SSD_INNER] * dt_ch
    x_state = xdt * jnp.exp(tot_ch - cs_ch)
    y_scale = jnp.exp(cs_ch)
    carry = jnp.exp(tot_ch)
    gw = SSD_GROUP_W
    heads_per_group = SSD_HEADS // SSD_GROUPS
    lane_head = jnp.right_shift(lax.broadcasted_iota(jnp.int32, (t, gw), 1), 6)

    for g in range(SSD_GROUPS):
        b_lo = SSD_INNER + g * SSD_STATE
        c_lo = SSD_INNER + SSD_GROUPS * SSD_STATE + g * SSD_STATE
        bg = xs_ref[:, b_lo:b_lo + SSD_STATE].astype(BF16)
        cg = xs_ref[:, c_lo:c_lo + SSD_STATE].astype(BF16)
        cb = _dot_nt(cg, bg)
        sg = state_ref[g]
        xdt_g = xdt[:, g * gw:(g + 1) * gw]
        y = _dot(cg, sg.astype(BF16)) * y_scale[:, g * gw:(g + 1) * gw]
        for hh in range(heads_per_group):
            h = g * heads_per_group + hh
            decay = jnp.where(mask, jnp.exp(cs[:, h:h + 1] - cs_t[h:h + 1, :]), 0.0)
            x_h = jnp.where(lane_head == hh, xdt_g, 0.0).astype(BF16)
            y = y + _dot((cb * decay).astype(BF16), x_h)
        y_ref[:, g * gw:(g + 1) * gw] = y
        ds = _dot_tn(bg, x_state[:, g * gw:(g + 1) * gw].astype(BF16))
        state_ref[g] = sg * carry[:, g * gw:(g + 1) * gw] + ds


def _ssd_scan(xs, dt2, a_log, batch, n_lat_chunks, n_ctx_chunks):
    m = xs.shape[0]
    t = SSD_CHUNK

    def chunk(b, d, s):
        j_ctx = jnp.where(d == 0, s, n_ctx_chunks - 1 - s)
        sl = s - n_ctx_chunks
        j_lat = jnp.where(d == 0, sl, n_lat_chunks - 1 - sl)
        return jnp.where(s < n_ctx_chunks, batch * n_lat_chunks + b * n_ctx_chunks + j_ctx,
                         b * n_lat_chunks + j_lat)

    return pl.pallas_call(
        _ssd_kernel,
        grid=(batch, 2, n_ctx_chunks + n_lat_chunks),
        in_specs=[pl.BlockSpec((t, SSD_XBC), lambda b, d, s: (chunk(b, d, s), 0)),
                  pl.BlockSpec((None, t, LANE), lambda b, d, s: (d, chunk(b, d, s), 0)),
                  pl.BlockSpec((None, 1, LANE), lambda b, d, s: (d, 0, 0))],
        out_specs=pl.BlockSpec((None, t, SSD_INNER), lambda b, d, s: (d, chunk(b, d, s), 0)),
        out_shape=jax.ShapeDtypeStruct((2, m, SSD_INNER), F32),
        scratch_shapes=[pltpu.VMEM((SSD_GROUPS, SSD_STATE, SSD_GROUP_W), F32)],
        compiler_params=_params(("parallel", "parallel", "arbitrary")),
        name="ssd_scan",
    )(xs, dt2, a_log)


def _ssd_gate_kernel(y_ref, xs_ref, z_ref, d_ref, g_ref, o_ref):
    yl = d_ref[...] * xs_ref[...] + y_ref[0] + y_ref[1]
    u = yl * _silu(z_ref[...])
    gw = SSD_GROUP_W
    for g in range(SSD_GROUPS):
        ug = u[:, g * gw:(g + 1) * gw]
        ug = ug * lax.rsqrt(jnp.mean(ug * ug, axis=-1, keepdims=True) + NORM_EPS)
        o_ref[:, g * gw:(g + 1) * gw] = (ug * g_ref[:, g * gw:(g + 1) * gw]).astype(o_ref.dtype)


def _ssd_gate(y, xs, p_a, d_row, norm_g, n_tiles):
    m = xs.shape[0]
    w = SSD_INNER
    return pl.pallas_call(
        _ssd_gate_kernel,
        grid=(n_tiles,),
        in_specs=[pl.BlockSpec((2, ROW_TILE, w), lambda i: (0, i, 0)),
                  pl.BlockSpec((ROW_TILE, w), lambda i: (i, 0)),
                  pl.BlockSpec((ROW_TILE, w), lambda i: (i, COL_Z // w)),
                  pl.BlockSpec((1, w), lambda i: (0, 0)),
                  pl.BlockSpec((1, w), lambda i: (0, 0))],
        out_specs=pl.BlockSpec((ROW_TILE, w), lambda i: (i, 0)),
        out_shape=jax.ShapeDtypeStruct((n_tiles * ROW_TILE, w), BF16),
        compiler_params=_params(("parallel",)),
        name="ssd_gate_norm",
    )(y, xs, p_a, d_row, norm_g)


ATT_TQ = 512
ATT_TKC = 1024


def _attn_kernel(*refs, has_lat, lam_init):
    if has_lat:
        (q_ref, kc_ref, vct_ref, kl_ref, vlt_ref, lam_ref, g_ref, o_ref,
         m_ref, l_ref, acc_ref, sa_ref, sb_ref) = refs
    else:
        q_ref, kc_ref, vct_ref, lam_ref, g_ref, o_ref, m_ref, l_ref, acc_ref, sa_ref, sb_ref = refs

    q = q_ref[...]
    lane = lax.broadcasted_iota(jnp.int32, q.shape, 1)
    zero = jnp.zeros_like(q)
    q_maps = (jnp.where(lane < DA_HEAD_DIM, q, zero), jnp.where(lane >= DA_HEAD_DIM, q, zero))

    def scores(k, s_ref):
        n = k.shape[0]
        maxima = []
        for mp in range(2):
            s = _dot_nt(k, q_maps[mp])
            s_ref[mp, 0:n, :] = s
            maxima.append(jnp.max(s, axis=0, keepdims=True))
        return tuple(maxima)

    def update(s_ref, maxima, vt):
        n = vt.shape[1]
        for mp in range(2):
            m_old = m_ref[mp]
            m_new = jnp.maximum(m_old, maxima[mp])
            alpha = jnp.exp2(m_old - m_new)
            p = jnp.exp2(s_ref[mp, 0:n, :] - m_new[0:1, :])
            l_ref[mp] = alpha * l_ref[mp] + jnp.sum(p, axis=0, keepdims=True)
            acc_ref[mp] = alpha[0:1, :] * acc_ref[mp] + _dot(vt, p.astype(vt.dtype))
            m_ref[mp] = m_new

    m_ref[...] = jnp.full(m_ref.shape, -jnp.inf, F32)
    l_ref[...] = jnp.zeros_like(l_ref)
    acc_ref[...] = jnp.zeros_like(acc_ref)
    mx = scores(kc_ref[...], sa_ref)

    if not has_lat:
        update(sa_ref, mx, vct_ref[...])
    else:
        tkc = min(ATT_TKC, kl_ref.shape[0])
        n_chunks = kl_ref.shape[0] // tkc
        k_at = lambda c: kl_ref[pl.ds(pl.multiple_of(c * tkc, tkc), tkc), :]
        vt_at = lambda c: vlt_ref[:, pl.ds(pl.multiple_of(c * tkc, tkc), tkc)]

        mx_ctx = mx
        mx = scores(k_at(0), sb_ref)
        update(sa_ref, mx_ctx, vct_ref[...])

        def pair(c2, mx_b):
            c = 1 + 2 * c2
            mx_a = scores(k_at(c), sa_ref)
            update(sb_ref, mx_b, vt_at(c - 1))
            mx_b = scores(k_at(c + 1), sb_ref)
            update(sa_ref, mx_a, vt_at(c))
            return mx_b

        mx = lax.fori_loop(0, (n_chunks - 1) // 2, pair, mx)
        if (n_chunks - 1) % 2:
            mx_a = scores(k_at(n_chunks - 1), sa_ref)
            update(sb_ref, mx, vt_at(n_chunks - 2))
            update(sa_ref, mx_a, vt_at(n_chunks - 1))
        else:
            update(sb_ref, mx, vt_at(n_chunks - 1))

    lv = lam_ref[...]
    dotp = lambda a, b: jnp.sum(lv[a:a + 1, :] * lv[b:b + 1, :], axis=-1, keepdims=True)
    lam = jnp.exp(dotp(0, 1)) - jnp.exp(dotp(2, 3)) + lam_init
    inv_l = 1.0 / l_ref[...]
    o = acc_ref[0] * inv_l[0, 0:1, :] - lam * (acc_ref[1] * inv_l[1, 0:1, :])
    y = o * lax.rsqrt(jnp.mean(o * o, axis=0, keepdims=True) + NORM_EPS)
    o_ref[...] = ((y * g_ref[...]) * (1.0 - lam_init)).T.astype(o_ref.dtype)


def _attn_scratch(tq, keys_a, keys_b):
    return [pltpu.VMEM((2, SUBLANE, tq), F32), pltpu.VMEM((2, SUBLANE, tq), F32),
            pltpu.VMEM((2, DA_V_DIM, tq), F32),
            pltpu.VMEM((2, keys_a, tq), F32), pltpu.VMEM((2, keys_b, tq), F32)]


def _attention_lat(qkv, v_t, da_lambda, subln, lam_init, batch, seq, ctx_len):
    h = DA_HEADS
    tq = ATT_TQ
    nq = seq // tq
    ctx_blk0 = (batch * seq) // ctx_len
    assert seq % tq == 0 and seq % min(ATT_TKC, seq) == 0
    kern = functools.partial(_attn_kernel, has_lat=True, lam_init=lam_init)
    return pl.pallas_call(
        kern,
        grid=(batch, h, nq),
        in_specs=[pl.BlockSpec((tq, LANE), lambda b, hh, i: (b * nq + i, hh)),
                  pl.BlockSpec((ctx_len, LANE), lambda b, hh, i: (ctx_blk0 + b, h + hh)),
                  pl.BlockSpec((DA_V_DIM, ctx_len), lambda b, hh, i: (hh, ctx_blk0 + b)),
                  pl.BlockSpec((seq, LANE), lambda b, hh, i: (b, h + hh)),
                  pl.BlockSpec((DA_V_DIM, seq), lambda b, hh, i: (hh, b)),
                  pl.BlockSpec(da_lambda.shape, lambda b, hh, i: (0, 0)),
                  pl.BlockSpec((DA_V_DIM, 1), lambda b, hh, i: (0, 0))],
        out_specs=pl.BlockSpec((tq, LANE), lambda b, hh, i: (b * nq + i, hh)),
        out_shape=jax.ShapeDtypeStruct((batch * seq, DA_WIDTH), BF16),
        scratch_shapes=_attn_scratch(tq, max(min(ATT_TKC, seq), ctx_len), min(ATT_TKC, seq)),
        compiler_params=_params(("parallel", "parallel", "parallel"), 48),
        name="diff_attention",
    )(qkv, qkv, v_t, qkv, v_t, da_lambda, subln)


def _attention_ctx(qkv, v_t, da_lambda, subln, lam_init, batch, seq, ctx_len):
    h = DA_HEADS
    ctx_blk0 = (batch * seq) // ctx_len
    kern = functools.partial(_attn_kernel, has_lat=False, lam_init=lam_init)
    return pl.pallas_call(
        kern,
        grid=(batch, h),
        in_specs=[pl.BlockSpec((ctx_len, LANE), lambda b, hh: (ctx_blk0 + b, hh)),
                  pl.BlockSpec((ctx_len, LANE), lambda b, hh: (ctx_blk0 + b, h + hh)),
                  pl.BlockSpec((DA_V_DIM, ctx_len), lambda b, hh: (hh, ctx_blk0 + b)),
                  pl.BlockSpec(da_lambda.shape, lambda b, hh: (0, 0)),
                  pl.BlockSpec((DA_V_DIM, 1), lambda b, hh: (0, 0))],
        out_specs=pl.BlockSpec((ctx_len, LANE), lambda b, hh: (b, hh)),
        out_shape=jax.ShapeDtypeStruct((batch * ctx_len, DA_WIDTH), BF16),
        scratch_shapes=_attn_scratch(ctx_len, ctx_len, SUBLANE),
        compiler_params=_params(("parallel", "parallel")),
        name="diff_attention_ctx",
    )(qkv, qkv, v_t, da_lambda, subln)


def _outproj_kernel(ya_ref, yb_ref, yc_ref, ycx_ref, w0_ref, w1_ref, w2_ref, x_ref, gate_ref, o_ref, *,
                    n_lat_tiles):
    yc = jnp.where(pl.program_id(0) < n_lat_tiles, yc_ref[...], ycx_ref[...])
    acc = _dot(ya_ref[...], w0_ref[...]) + _dot(yb_ref[...], w1_ref[...]) + _dot(yc, w2_ref[...])
    o_ref[...] = x_ref[...] + gate_ref[...] * acc


def _out_proj(ya, yb, yc, yc_ctx, w_out, x, mods, n_tiles, n_lat_tiles, cls_of_tile):
    m, d = x.shape
    tn = 1024
    wa, wb = ya.shape[1], yb.shape[1]
    assert wa == wb and yc.shape[1] == wa + wb
    if yc_ctx is None:
        assert n_tiles <= n_lat_tiles
        yc_ctx = yc
    return pl.pallas_call(
        functools.partial(_outproj_kernel, n_lat_tiles=n_lat_tiles),
        grid=(n_tiles, d // tn),
        in_specs=[pl.BlockSpec((ROW_TILE, wa), lambda i, j: (i, 0)),
                  pl.BlockSpec((ROW_TILE, wb), lambda i, j: (i, 0)),
                  pl.BlockSpec((ROW_TILE, wa + wb), lambda i, j: (jnp.minimum(i, n_lat_tiles - 1), 0)),
                  pl.BlockSpec((ROW_TILE, wa + wb), lambda i, j: (jnp.maximum(i - n_lat_tiles, 0), 0)),
                  pl.BlockSpec((wa, tn), lambda i, j: (0, j)),
                  pl.BlockSpec((wb, tn), lambda i, j: (1, j)),
                  pl.BlockSpec((wa + wb, tn), lambda i, j: (1, j)),
                  pl.BlockSpec((ROW_TILE, tn), lambda i, j: (i, j)),
                  pl.BlockSpec((None, 1, tn), lambda i, j: (cls_of_tile(i), 0, 2 * (d // tn) + j))],
        out_specs=pl.BlockSpec((ROW_TILE, tn), lambda i, j: (i, j)),
        out_shape=jax.ShapeDtypeStruct((n_tiles * ROW_TILE, d), F32),
        compiler_params=_params(("parallel", "parallel"), 40),
        name="out_proj_residual",
    )(ya, yb, yc, yc_ctx, w_out, w_out, w_out, x, mods)


def _ffn_kernel(x_ref, g_ref, sh_ref, sc_ref, gate_ref, wg_ref, wu_ref, wd_ref, o_ref, h_ref, acc_ref):
    f = pl.program_id(1)

    @pl.when(f == 0)
    def _():
        h_ref[...] = _modulated_norm(x_ref[...], g_ref[...], sh_ref[...], sc_ref[...]).astype(h_ref.dtype)
        acc_ref[...] = jnp.zeros_like(acc_ref)

    h = h_ref[...]
    a = _silu(_dot(h, wg_ref[...])) * _dot(h, wu_ref[...])
    acc_ref[...] += _dot(a.astype(BF16), wd_ref[...])

    @pl.when(f == pl.num_programs(1) - 1)
    def _():
        o_ref[...] = x_ref[...] + gate_ref[...] * acc_ref[...]


def _ffn_dense(x, g, mods, wg, wu, wd, n_tiles, cls_of_tile):
    m, d = x.shape
    ff = wg.shape[1]
    tf = 512
    return pl.pallas_call(
        _ffn_kernel,
        grid=(n_tiles, ff // tf),
        in_specs=[pl.BlockSpec((ROW_TILE, d), lambda i, f: (i, 0)),
                  pl.BlockSpec((1, d), lambda i, f: (0, 0)),
                  _mod_spec(3, d, cls_of_tile), _mod_spec(4, d, cls_of_tile), _mod_spec(5, d, cls_of_tile),
                  pl.BlockSpec((d, tf), lambda i, f: (0, f)),
                  pl.BlockSpec((d, tf), lambda i, f: (0, f)),
                  pl.BlockSpec((tf, d), lambda i, f: (f, 0))],
        out_specs=pl.BlockSpec((ROW_TILE, d), lambda i, f: (i, 0)),
        out_shape=jax.ShapeDtypeStruct((n_tiles * ROW_TILE, d), F32),
        scratch_shapes=[pltpu.VMEM((ROW_TILE, d), BF16), pltpu.VMEM((ROW_TILE, d), F32)],
        compiler_params=_params(("parallel", "arbitrary"), 48),
        name="ffn_dense",
    )(x, g.reshape(1, d), mods, mods, mods, wg, wu, wd)


ROUTE_I1, ROUTE_I2, ROUTE_P1, ROUTE_P2 = 0, 1, 2, 3


def _router_kernel(x_ref, g_ref, sh_ref, sc_ref, wr_ref, br_ref, h_ref, route_ref):
    h = _modulated_norm(x_ref[...], g_ref[...], sh_ref[...], sc_ref[...])
    h_ref[...] = h
    logits = jnp.dot(h, wr_ref[...], precision=HIGHEST, preferred_element_type=F32) + br_ref[...]
    lane = lax.broadcasted_iota(jnp.int32, logits.shape, 1).astype(F32)
    neg = -jnp.inf
    lg = jnp.where(lane < N_EXPERTS, logits, neg)
    v1 = jnp.max(lg, axis=-1, keepdims=True)
    i1 = jnp.min(jnp.where(lg == v1, lane, float(LANE)), axis=-1, keepdims=True)
    lg2 = jnp.where(lane == i1, neg, lg)
    v2 = jnp.max(lg2, axis=-1, keepdims=True)
    i2 = jnp.min(jnp.where(lg2 == v2, lane, float(LANE)), axis=-1, keepdims=True)
    e = jnp.exp(v2 - v1)
    p1 = 1.0 / (1.0 + e)
    p2 = e / (1.0 + e)
    rec = jnp.where(lane == ROUTE_I1, i1, 0.0) + jnp.where(lane == ROUTE_I2, i2, 0.0)
    route_ref[...] = rec + jnp.where(lane == ROUTE_P1, p1, 0.0) + jnp.where(lane == ROUTE_P2, p2, 0.0)


def _router(x, g, mods, wr, br, n_tiles, cls_of_tile):
    m, d = x.shape
    return pl.pallas_call(
        _router_kernel,
        grid=(n_tiles,),
        in_specs=[pl.BlockSpec((ROW_TILE, d), lambda i: (i, 0)),
                  pl.BlockSpec((1, d), lambda i: (0, 0)),
                  _mod_spec(3, d, cls_of_tile), _mod_spec(4, d, cls_of_tile),
                  pl.BlockSpec((d, LANE), lambda i: (0, 0)),
                  pl.BlockSpec((1, LANE), lambda i: (0, 0))],
        out_specs=[pl.BlockSpec((ROW_TILE, d), lambda i: (i, 0)),
                   pl.BlockSpec((ROW_TILE, LANE), lambda i: (i, 0))],
        out_shape=[jax.ShapeDtypeStruct((n_tiles * ROW_TILE, d), F32),
                   jax.ShapeDtypeStruct((n_tiles * ROW_TILE, LANE), F32)],
        compiler_params=_params(("parallel",), 40),
        name="moe_router",
    )(x, g.reshape(1, d), mods, mods, wr, br)


def _routing_tables(route, n_experts, tile):
    n_tok = route.shape[0]
    experts = route[:, ROUTE_I1:ROUTE_I2 + 1].astype(jnp.int32).reshape(-1)
    onehot = (experts[:, None] == jnp.arange(n_experts, dtype=jnp.int32)[None, :]).astype(jnp.int32)
    running = jnp.cumsum(onehot, axis=0)
    rank = jnp.sum(running * onehot, axis=1) - 1
    counts = running[-1]
    padded = ((counts + tile - 1) // tile) * tile
    ends = jnp.cumsum(padded)
    slot = (ends - padded)[experts] + rank
    n_rows = 2 * n_tok + n_experts * tile
    n_tiles = n_rows // tile
    token = jnp.repeat(jnp.arange(n_tok, dtype=jnp.int32), 2)
    row_token = jnp.zeros((n_rows,), jnp.int32).at[slot].set(token)
    tile_start = jnp.arange(n_tiles, dtype=jnp.int32) * tile
    tile_expert = jnp.minimum(jnp.sum((tile_start[:, None] >= ends[None, :]).astype(jnp.int32), axis=1),
                              n_experts - 1)
    n_active = (ends[-1] // tile).astype(jnp.int32).reshape(1)
    return tile_expert, row_token, n_active, slot.astype(jnp.int32)


MOE_TILE = 512


def _experts_kernel(te_ref, tok_ref, nact_ref, h_hbm, wg_ref, wu_ref, wd_ref, o_ref, xbuf, hb_ref, sem):
    j = pl.program_id(0)
    f = pl.program_id(1)
    n_act = nact_ref[0]
    slot = j % 2
    tile = hb_ref.shape[0]

    def row_copy(tok, r, s):
        return pltpu.make_async_copy(h_hbm.at[pl.ds(tok, 1), :], xbuf.at[s, pl.ds(r, 1), :], sem.at[s])

    def start_gather(t, s):
        def body(r, carry):
            row_copy(tok_ref[t * tile + r], r, s).start()
            return carry
        lax.fori_loop(0, tile, body, 0)

    def wait_gather(s):
        pltpu.make_async_copy(h_hbm.at[pl.ds(0, tile), :], xbuf.at[s], sem.at[s]).wait()

    active = j < n_act

    @pl.when(jnp.logical_and(active, f == 0))
    def _():
        @pl.when(j == 0)
        def _():
            start_gather(0, 0)

        wait_gather(slot)

        @pl.when(j + 1 < n_act)
        def _():
            start_gather(j + 1, 1 - slot)

        hb_ref[...] = xbuf[slot].astype(hb_ref.dtype)

    @pl.when(active)
    def _():
        h = hb_ref[...]
        a = _silu(_dot(h, wg_ref[...])) * _dot(h, wu_ref[...])
        y = _dot(a.astype(BF16), wd_ref[...])

        @pl.when(f == 0)
        def _():
            o_ref[...] = y

        @pl.when(f != 0)
        def _():
            o_ref[...] += y

    @pl.when(jnp.logical_and(jnp.logical_not(active), f == 0))
    def _():
        o_ref[...] = jnp.zeros_like(o_ref)


def _experts(h, tile_expert, row_token, n_active, wg, wu, wd):
    n_e, d, ff = wg.shape
    tile = MOE_TILE
    n_rows = row_token.shape[0]
    tf = 256
    nf = ff // tf
    f_of = lambda j, f, na: jnp.where(j < na[0], f, nf - 1)
    grid_spec = pltpu.PrefetchScalarGridSpec(
        num_scalar_prefetch=3,
        grid=(n_rows // tile, nf),
        in_specs=[pl.BlockSpec(memory_space=pl.ANY),
                  pl.BlockSpec((None, d, tf), lambda j, f, te, tok, na: (te[j], 0, f_of(j, f, na))),
                  pl.BlockSpec((None, d, tf), lambda j, f, te, tok, na: (te[j], 0, f_of(j, f, na))),
                  pl.BlockSpec((None, tf, d), lambda j, f, te, tok, na: (te[j], f_of(j, f, na), 0))],
        out_specs=pl.BlockSpec((tile, d), lambda j, f, te, tok, na: (j, 0)),
        scratch_shapes=[pltpu.VMEM((2, tile, d), F32), pltpu.VMEM((tile, d), BF16),
                        pltpu.SemaphoreType.DMA((2,))],
    )
    return pl.pallas_call(
        _experts_kernel,
        grid_spec=grid_spec,
        out_shape=jax.ShapeDtypeStruct((n_rows, d), F32),
        compiler_params=_params(("arbitrary", "arbitrary"), 48),
        name="moe_experts",
    )(tile_expert, row_token, n_active, h, wg, wu, wd)


COMBINE_TILE = 256


def _combine_kernel(slot_ref, y_hbm, x_ref, route_ref, gate_ref, o_ref, ybuf, sem):
    i = pl.program_id(0)
    n = pl.num_programs(0)
    buf = i % 2
    tile = x_ref.shape[0]

    def row_copy(row, k, r, s):
        return pltpu.make_async_copy(y_hbm.at[pl.ds(row, 1), :], ybuf.at[s, pl.ds(k * tile + r, 1), :], sem.at[s])

    def start_gather(t, s):
        def body(r, carry):
            base = 2 * (t * tile + r)
            row_copy(slot_ref[base], 0, r, s).start()
            row_copy(slot_ref[base + 1], 1, r, s).start()
            return carry
        lax.fori_loop(0, tile, body, 0)

    def wait_gather(s):
        pltpu.make_async_copy(y_hbm.at[pl.ds(0, 2 * tile), :], ybuf.at[s], sem.at[s]).wait()

    @pl.when(i == 0)
    def _():
        start_gather(0, 0)

    wait_gather(buf)

    @pl.when(i + 1 < n)
    def _():
        start_gather(i + 1, 1 - buf)

    route = route_ref[...]
    p1 = route[:, ROUTE_P1:ROUTE_P1 + 1]
    p2 = route[:, ROUTE_P2:ROUTE_P2 + 1]
    y = p1 * ybuf[buf, 0:tile, :] + p2 * ybuf[buf, tile:2 * tile, :]
    o_ref[...] = x_ref[...] + gate_ref[...] * y


def _combine(y_rows, slot, x, route, mods, n_tokens, cls_of_tile):
    d = x.shape[1]
    tile = COMBINE_TILE
    per = ROW_TILE // tile
    grid_spec = pltpu.PrefetchScalarGridSpec(
        num_scalar_prefetch=1,
        grid=(n_tokens // tile,),
        in_specs=[pl.BlockSpec(memory_space=pl.ANY),
                  pl.BlockSpec((tile, d), lambda i, s: (i, 0)),
                  pl.BlockSpec((tile, LANE), lambda i, s: (i, 0)),
                  pl.BlockSpec((None, 1, d), lambda i, s: (cls_of_tile(i // per), 0, 5))],
        out_specs=pl.BlockSpec((tile, d), lambda i, s: (i, 0)),
        scratch_shapes=[pltpu.VMEM((2, 2 * tile, d), F32), pltpu.SemaphoreType.DMA((2,))],
    )
    return pl.pallas_call(
        _combine_kernel,
        grid_spec=grid_spec,
        out_shape=jax.ShapeDtypeStruct((n_tokens, d), F32),
        compiler_params=_params(("arbitrary",), 48),
        name="moe_combine",
    )(slot, y_rows, x, route, mods)


def _final_kernel(x_ref, g_ref, o_ref):
    x = x_ref[...]
    o_ref[...] = (x * lax.rsqrt(jnp.mean(x * x, axis=-1, keepdims=True) + NORM_EPS)) * g_ref[...]


def _final_norm(x, g, n_tiles):
    d = x.shape[1]
    return pl.pallas_call(
        _final_kernel,
        grid=(n_tiles,),
        in_specs=[pl.BlockSpec((ROW_TILE, d), lambda i: (i, 0)),
                  pl.BlockSpec((1, d), lambda i: (0, 0))],
        out_specs=pl.BlockSpec((ROW_TILE, d), lambda i: (i, 0)),
        out_shape=jax.ShapeDtypeStruct((n_tiles * ROW_TILE, d), F32),
        compiler_params=_params(("parallel",), 40),
        name="final_norm",
    )(x, g.reshape(1, d))


def _pad_lanes(v, width=LANE):
    return jnp.pad(v, [(0, 0)] * (v.ndim - 1) + [(0, width - v.shape[-1])])


def kernel(x, c, ctx, c_ctx, w_mod, b_mod, g_mix, g_ffn, w_in, conv_w, ssd_conv_w, ssd_conv_b, ssd_a_log, ssd_dt_bias, ssd_d, ssd_norm, da_lambda, da_subln, w_out, ffn_w_gate, ffn_w_up, ffn_w_down, moe_w_router, moe_b_router, moe_w_gate, moe_w_up, moe_w_down, g_final):
    batch, seq, d = x.shape
    ctx_len = ctx.shape[1]
    depth = w_mod.shape[0]
    n_lat = batch * seq
    m = n_lat + batch * ctx_len
    assert seq % ROW_TILE == 0 and (batch * ctx_len) % ROW_TILE == 0 and n_lat % ctx_len == 0
    assert ctx_len % HALO_TILE == 0 and ctx_len % SSD_CHUNK == 0 and batch < MOD_CLASSES
    n_lat_tiles = n_lat // ROW_TILE
    n_all_tiles = m // ROW_TILE
    tiles_per_seq = seq // ROW_TILE
    cls_of_tile = lambda i: jnp.minimum(i // tiles_per_seq, batch)

    xs_all = jnp.concatenate([x.reshape(n_lat, d), ctx.reshape(batch * ctx_len, d)], axis=0)
    cvec = jnp.zeros((MOD_CLASSES, d), F32).at[:batch].set(c).at[batch].set(c_ctx)
    mods_all = _mod_vectors(cvec, w_mod, b_mod)
    cos_t, sin_t = _rope_tables(seq)

    for i in range(depth):
        ctx_out = i < depth - 1
        lam_init = 0.8 - 0.6 * math.exp(-0.3 * i)
        n_tiles = n_all_tiles if ctx_out else n_lat_tiles
        mods = mods_all[i].reshape(MOD_CLASSES, 1, N_MOD * d)
        wi = w_in[i]
        w_a = jnp.concatenate([wi[:, COL_CONV:COL_Q], wi[:, COL_XBC:COL_DT]], axis=1).astype(BF16)
        w_b = jnp.concatenate([wi[:, COL_Q:COL_XBC] * (DA_SCALE * LOG2_E), wi[:, COL_K:COL_V],
                               wi[:, COL_V:COL_V + DA_WIDTH]], axis=1).astype(BF16)
        w_c = _pad_lanes(wi[:, COL_DT:COL_K]).astype(BF16)

        h = _norm_modulate(xs_all, g_mix[i], mods, 0, n_all_tiles, cls_of_tile)
        p_a = _matmul(h, w_a, F32, 1024, "in_proj_conv_z_xbc")
        qkv = _matmul_rope(h, w_b, cos_t, sin_t, n_lat_tiles, tiles_per_seq)
        p_dt = _matmul(h, w_c, F32, LANE, "in_proj_dt")

        ya, xbc, dt2 = _prep(p_a, p_dt, conv_w[i], ssd_conv_w[i], ssd_conv_b[i].reshape(1, -1),
                             _pad_lanes(ssd_dt_bias[i].reshape(1, -1)), n_lat // HALO_TILE,
                             seq // HALO_TILE, ctx_len // HALO_TILE)
        y_dir = _ssd_scan(xbc, dt2, _pad_lanes(ssd_a_log[i]).reshape(2, 1, LANE), batch,
                          seq // SSD_CHUNK, ctx_len // SSD_CHUNK)
        yb = _ssd_gate(y_dir, xbc, p_a, jnp.repeat(ssd_d[i], SSD_HEAD_DIM).reshape(1, -1),
                       ssd_norm[i].reshape(1, -1), n_tiles)

        subln = da_subln[i].reshape(-1, 1)
        v_t = qkv[:, 2 * DA_QK:].T
        yc = _attention_lat(qkv, v_t, da_lambda[i], subln, lam_init, batch, seq, ctx_len)
        yc_ctx = _attention_ctx(qkv, v_t, da_lambda[i], subln, lam_init, batch, seq, ctx_len) if ctx_out else None

        x_mid = _out_proj(ya, yb, yc, yc_ctx, w_out[i].astype(BF16), xs_all, mods, n_tiles, n_lat_tiles,
                          cls_of_tile)

        j = i // 2
        if i % 2 == 0:
            xs_all = _ffn_dense(x_mid, g_ffn[i], mods, ffn_w_gate[j].astype(BF16), ffn_w_up[j].astype(BF16),
                                ffn_w_down[j].astype(BF16), n_tiles, cls_of_tile)
        else:
            hh, route = _router(x_mid, g_ffn[i], mods, _pad_lanes(moe_w_router[j]),
                                _pad_lanes(moe_b_router[j].reshape(1, -1)), n_tiles, cls_of_tile)
            tile_expert, row_token, n_active, slot = _routing_tables(route, moe_w_gate.shape[1], MOE_TILE)
            y_rows = _experts(hh, tile_expert, row_token, n_active, moe_w_gate[j].astype(BF16),
                              moe_w_up[j].astype(BF16), moe_w_down[j].astype(BF16))
            xs_all = _combine(y_rows, slot, x_mid, route, mods, n_tiles * ROW_TILE, cls_of_tile)

    out = _final_norm(xs_all, g_final, n_lat_tiles)
    return out.reshape(batch, seq, d)
```

```python
import functools
import math

import jax
import jax.numpy as jnp
from jax import lax
from jax.experimental import pallas as pl
from jax.experimental.pallas import tpu as pltpu

NORM_EPS = 1e-6
N_MOD = 6
GRID_W = 64

SHORT_CONV = 3
CONV_W = 512

SSD_HEADS = 8
SSD_HEAD_DIM = 64
SSD_INNER = SSD_HEADS * SSD_HEAD_DIM
SSD_STATE = 128
SSD_GROUPS = 2
SSD_CHUNK = 128
SSD_XBC = SSD_INNER + 2 * SSD_GROUPS * SSD_STATE
SSD_GROUP_W = SSD_INNER // SSD_GROUPS

DA_HEADS = 8
DA_HEAD_DIM = 64
DA_V_DIM = 2 * DA_HEAD_DIM
DA_QK = DA_HEADS * 2 * DA_HEAD_DIM
DA_WIDTH = DA_HEADS * DA_V_DIM
DA_SCALE = DA_HEAD_DIM ** -0.5
LOG2_E = math.log2(math.e)
ROPE_THETA = 10000.0

COL_CONV = 0
COL_Z = COL_CONV + 3 * CONV_W
COL_Q = COL_Z + SSD_INNER
COL_XBC = COL_Q + DA_QK
COL_DT = COL_XBC + SSD_XBC
COL_K = COL_DT + 2 * SSD_HEADS
COL_V = COL_K + DA_QK

N_EXPERTS = 8

LANE = 128
SUBLANE = 8
ROW_TILE = 512
HALO_TILE = 256
MOD_CLASSES = 8

F32 = jnp.float32
BF16 = jnp.bfloat16
HIGHEST = lax.Precision.HIGHEST
MIB = 1024 * 1024


def _params(semantics, vmem_mib=None):
    kw = {"dimension_semantics": semantics}
    if vmem_mib is not None:
        kw["vmem_limit_bytes"] = vmem_mib * MIB
    return pltpu.CompilerParams(**kw)


def _silu(v):
    return v * jax.nn.sigmoid(v)


def _dot(a, b):
    return jnp.dot(a, b, preferred_element_type=F32)


def _dot_nt(a, b):
    return lax.dot_general(a, b, (((1,), (1,)), ((), ())), preferred_element_type=F32)


def _dot_tn(a, b):
    return lax.dot_general(a, b, (((0,), (0,)), ((), ())), preferred_element_type=F32)


def _modulated_norm(x, g, shift, scale):
    ms = jnp.mean(x * x, axis=-1, keepdims=True)
    y = x * lax.rsqrt(ms + NORM_EPS)
    return (y * g) * (1.0 + scale) + shift


def _mod_spec(k, width, cls_of_tile):
    return pl.BlockSpec((None, 1, width), lambda i, *_: (cls_of_tile(i), 0, k))


def _mod_kernel(c_ref, w_ref, b_ref, o_ref):
    s = _silu(c_ref[...])
    o_ref[...] = jnp.dot(s, w_ref[...], precision=HIGHEST, preferred_element_type=F32) + b_ref[...]


def _mod_vectors(cvec, w_mod, b_mod):
    depth, d, n = w_mod.shape
    tn = 1024
    return pl.pallas_call(
        _mod_kernel,
        grid=(depth, n // tn),
        in_specs=[pl.BlockSpec((MOD_CLASSES, d), lambda l, j: (0, 0)),
                  pl.BlockSpec((None, d, tn), lambda l, j: (l, 0, j)),
                  pl.BlockSpec((None, 1, tn), lambda l, j: (l, 0, j))],
        out_specs=pl.BlockSpec((None, MOD_CLASSES, tn), lambda l, j: (l, 0, j)),
        out_shape=jax.ShapeDtypeStruct((depth, MOD_CLASSES, n), F32),
        compiler_params=_params(("parallel", "parallel"), 40),
        name="mod_vectors",
    )(cvec, w_mod, b_mod.reshape(depth, 1, n))


def _norm_kernel(x_ref, g_ref, sh_ref, sc_ref, h_ref):
    h_ref[...] = _modulated_norm(x_ref[...], g_ref[...], sh_ref[...], sc_ref[...]).astype(h_ref.dtype)


def _norm_modulate(x, g, mods, k_shift, n_tiles, cls_of_tile):
    m, d = x.shape
    return pl.pallas_call(
        _norm_kernel,
        grid=(n_tiles,),
        in_specs=[pl.BlockSpec((ROW_TILE, d), lambda i: (i, 0)),
                  pl.BlockSpec((1, d), lambda i: (0, 0)),
                  _mod_spec(k_shift, d, cls_of_tile),
                  _mod_spec(k_shift + 1, d, cls_of_tile)],
        out_specs=pl.BlockSpec((ROW_TILE, d), lambda i: (i, 0)),
        out_shape=jax.ShapeDtypeStruct((m, d), BF16),
        compiler_params=_params(("parallel",), 40),
        name="norm_modulate",
    )(x, g.reshape(1, d), mods, mods)


def _mm_kernel(x_ref, w_ref, o_ref):
    o_ref[...] = _dot(x_ref[...], w_ref[...]).astype(o_ref.dtype)


def _matmul(x, w, out_dtype, tn, name):
    m, k = x.shape
    n = w.shape[1]
    return pl.pallas_call(
        _mm_kernel,
        grid=(m // ROW_TILE, n // tn),
        in_specs=[pl.BlockSpec((ROW_TILE, k), lambda i, j: (i, 0)),
                  pl.BlockSpec((k, tn), lambda i, j: (0, j))],
        out_specs=pl.BlockSpec((ROW_TILE, tn), lambda i, j: (i, j)),
        out_shape=jax.ShapeDtypeStruct((m, n), out_dtype),
        compiler_params=_params(("parallel", "parallel"), 40),
        name=name,
    )(x, w)


def _mm_rope_kernel(x_ref, w_ref, cos_ref, sin_ref, o_ref, *, n_lat_tiles, n_rope_tiles):
    i = pl.program_id(0)
    j = pl.program_id(1)
    acc = _dot(x_ref[...], w_ref[...])
    rotate = jnp.logical_and(i < n_lat_tiles, j < n_rope_tiles)

    @pl.when(rotate)
    def _():
        tm, tn = acc.shape
        lane = lax.broadcasted_iota(jnp.int32, (tm, LANE), 1)
        first = (lane % (DA_HEAD_DIM // 2)) < (DA_HEAD_DIM // 4)
        c = cos_ref[...]
        s = sin_ref[...]
        quarter = DA_HEAD_DIM // 4
        for hb in range(tn // LANE):
            blk = acc[:, hb * LANE:(hb + 1) * LANE]
            partner = jnp.where(first, pltpu.roll(blk, LANE - quarter, 1), pltpu.roll(blk, quarter, 1))
            o_ref[:, hb * LANE:(hb + 1) * LANE] = (blk * c + partner * s).astype(o_ref.dtype)

    @pl.when(jnp.logical_not(rotate))
    def _():
        o_ref[...] = acc.astype(o_ref.dtype)


def _matmul_rope(x, w, cos_t, sin_t, n_lat_tiles, tiles_per_seq):
    m, k = x.shape
    n = w.shape[1]
    tn = DA_QK
    kern = functools.partial(_mm_rope_kernel, n_lat_tiles=n_lat_tiles, n_rope_tiles=2 * DA_QK // tn)
    return pl.pallas_call(
        kern,
        grid=(m // ROW_TILE, n // tn),
        in_specs=[pl.BlockSpec((ROW_TILE, k), lambda i, j: (i, 0)),
                  pl.BlockSpec((k, tn), lambda i, j: (0, j)),
                  pl.BlockSpec((ROW_TILE, LANE), lambda i, j: (i % tiles_per_seq, 0)),
                  pl.BlockSpec((ROW_TILE, LANE), lambda i, j: (i % tiles_per_seq, 0))],
        out_specs=pl.BlockSpec((ROW_TILE, tn), lambda i, j: (i, j)),
        out_shape=jax.ShapeDtypeStruct((m, n), BF16),
        compiler_params=_params(("parallel", "parallel"), 40),
        name="qkv_proj_rope",
    )(x, w, cos_t, sin_t)


def _rope_tables(n_tokens):
    rows = n_tokens // GRID_W
    row = jnp.repeat(jnp.arange(rows, dtype=F32), GRID_W)
    col = jnp.broadcast_to(jnp.arange(GRID_W, dtype=F32), (rows, GRID_W)).reshape(-1)
    n_freq = DA_HEAD_DIM // 4
    inv = ROPE_THETA ** (-jnp.arange(n_freq, dtype=F32) / n_freq)
    ang_r = row[:, None] * inv
    ang_c = col[:, None] * inv
    cos64 = jnp.concatenate([jnp.cos(ang_r), jnp.cos(ang_r), jnp.cos(ang_c), jnp.cos(ang_c)], axis=-1)
    sin64 = jnp.concatenate([-jnp.sin(ang_r), jnp.sin(ang_r), -jnp.sin(ang_c), jnp.sin(ang_c)], axis=-1)
    return jnp.tile(cos64, (1, 2)), jnp.tile(sin64, (1, 2))


def _conv3(u, prev_row, next_row, w_ref, has_prev, has_next):
    tm = u.shape[0]
    row = lax.broadcasted_iota(jnp.int32, u.shape, 0)
    prev_row = jnp.where(has_prev, prev_row, 0.0)
    next_row = jnp.where(has_next, next_row, 0.0)
    before = jnp.where(row == 0, prev_row, pltpu.roll(u, 1, 0))
    after = jnp.where(row == tm - 1, next_row, pltpu.roll(u, tm - 1, 0))
    return before * w_ref[0:1, :] + u * w_ref[1:2, :] + after * w_ref[2:3, :]


def _prep_kernel(cv_ref, cvp_ref, cvn_ref, xb_ref, xbp_ref, xbn_ref, dtr_ref,
                 cw_ref, sw_ref, sb_ref, dtb_ref, ya_ref, xs_ref, dt_ref,
                 *, n_lat_tiles, lat_tiles_per_seq, ctx_tiles_per_seq):
    i = pl.program_id(0)
    is_lat = i < n_lat_tiles
    pos = jnp.where(is_lat, i % lat_tiles_per_seq, (i - n_lat_tiles) % ctx_tiles_per_seq)
    last = jnp.where(is_lat, lat_tiles_per_seq - 1, ctx_tiles_per_seq - 1)
    has_prev = pos != 0
    has_next = pos != last

    w = CONV_W
    cv = cv_ref[...]
    gate_b, u = cv[:, 0:w], cv[:, w:2 * w] * cv[:, 2 * w:3 * w]
    p = cvp_ref[SUBLANE - 1:SUBLANE, :]
    n = cvn_ref[0:1, :]
    u_prev = p[:, w:2 * w] * p[:, 2 * w:3 * w]
    u_next = n[:, w:2 * w] * n[:, 2 * w:3 * w]
    ya_ref[...] = (gate_b * _conv3(u, u_prev, u_next, cw_ref, has_prev, has_next)).astype(ya_ref.dtype)

    xc = _conv3(xb_ref[...], xbp_ref[SUBLANE - 1:SUBLANE, :], xbn_ref[0:1, :], sw_ref, has_prev, has_next)
    xs_ref[...] = _silu(xc + sb_ref[...])

    t = dtr_ref[...] + dtb_ref[...]
    dt = jnp.maximum(t, 0.0) + jnp.log1p(jnp.exp(-jnp.abs(t)))
    dt_ref[0] = dt
    dt_ref[1] = pltpu.roll(dt, LANE - SSD_HEADS, 1)


def _prep(p_a, p_dt, conv_w, ssd_conv_w, ssd_conv_b, dt_bias, n_lat_tiles, lat_tiles_per_seq,
          ctx_tiles_per_seq):
    m = p_a.shape[0]
    tm = HALO_TILE
    per = tm // SUBLANE
    n8 = m // SUBLANE
    cw = 3 * CONV_W
    xbc_blk = COL_Q // SSD_XBC
    prev = lambda i: jnp.maximum(i * per - 1, 0)
    nxt = lambda i: jnp.minimum((i + 1) * per, n8 - 1)
    kern = functools.partial(_prep_kernel, n_lat_tiles=n_lat_tiles, lat_tiles_per_seq=lat_tiles_per_seq,
                             ctx_tiles_per_seq=ctx_tiles_per_seq)
    return pl.pallas_call(
        kern,
        grid=(m // tm,),
        in_specs=[pl.BlockSpec((tm, cw), lambda i: (i, 0)),
                  pl.BlockSpec((SUBLANE, cw), lambda i: (prev(i), 0)),
                  pl.BlockSpec((SUBLANE, cw), lambda i: (nxt(i), 0)),
                  pl.BlockSpec((tm, SSD_XBC), lambda i: (i, xbc_blk)),
                  pl.BlockSpec((SUBLANE, SSD_XBC), lambda i: (prev(i), xbc_blk)),
                  pl.BlockSpec((SUBLANE, SSD_XBC), lambda i: (nxt(i), xbc_blk)),
                  pl.BlockSpec((tm, LANE), lambda i: (i, 0)),
                  pl.BlockSpec((SHORT_CONV, CONV_W), lambda i: (0, 0)),
                  pl.BlockSpec((SHORT_CONV, SSD_XBC), lambda i: (0, 0)),
                  pl.BlockSpec((1, SSD_XBC), lambda i: (0, 0)),
                  pl.BlockSpec((1, LANE), lambda i: (0, 0))],
        out_specs=[pl.BlockSpec((tm, CONV_W), lambda i: (i, 0)),
                   pl.BlockSpec((tm, SSD_XBC), lambda i: (i, 0)),
                   pl.BlockSpec((2, tm, LANE), lambda i: (0, i, 0))],
        out_shape=[jax.ShapeDtypeStruct((m, CONV_W), BF16),
                   jax.ShapeDtypeStruct((m, SSD_XBC), F32),
                   jax.ShapeDtypeStruct((2, m, LANE), F32)],
        compiler_params=_params(("parallel",), 40),
        name="conv_prep",
    )(p_a, p_a, p_a, p_a, p_a, p_a, p_dt, conv_w, ssd_conv_w, ssd_conv_b, dt_bias)


def _split3(v):
    hi = v.astype(BF16)
    rest = v - hi.astype(F32)
    mid = rest.astype(BF16)
    return hi, mid, (rest - mid.astype(F32)).astype(BF16)


def _dot_f32_lhs(a, b01):
    return sum(_dot(piece, b01) for piece in _split3(a))


def _dot_f32_rhs(a01, b):
    return sum(_dot(a01, piece) for piece in _split3(b))


def _ssd_kernel(xs_ref, dt_ref, alog_ref, alog_ch_ref, y_ref, state_ref):
    d = pl.program_id(1)
    step = pl.program_id(2)

    @pl.when(step == 0)
    def _():
        state_ref[...] = jnp.zeros_like(state_ref)

    t = SSD_CHUNK
    r = lax.broadcasted_iota(jnp.int32, (t, t), 0)
    c = lax.broadcasted_iota(jnp.int32, (t, t), 1)
    fwd = d == 0
    mask = jnp.logical_or(jnp.logical_and(fwd, r >= c), jnp.logical_and(jnp.logical_not(fwd), r <= c))
    tri = mask.astype(BF16)

    er = lax.broadcasted_iota(jnp.int32, (LANE, SSD_INNER), 0)
    ec = lax.broadcasted_iota(jnp.int32, (LANE, SSD_INNER), 1)
    expand = (jnp.right_shift(ec, 6) == er).astype(BF16)

    dt = dt_ref[...]
    cs = _dot_f32_rhs(tri, dt * (-jnp.exp(alog_ref[...])))
    cs_t = cs.T
    dt_ch = _dot_f32_lhs(dt, expand)
    da_ch = dt_ch * (-jnp.exp(alog_ch_ref[...]))
    cs_ch = _dot_f32_rhs(tri, da_ch)
    tot_ch = jnp.sum(da_ch, axis=0, keepdims=True)

    xdt = xs_ref[:, 0:SSD_INNER] * dt_ch
    x_state = xdt * jnp.exp(tot_ch - cs_ch)
    y_scale = jnp.exp(cs_ch)
    carry = jnp.exp(tot_ch)
    gw = SSD_GROUP_W
    heads_per_group = SSD_HEADS // SSD_GROUPS
    lane_head = jnp.right_shift(lax.broadcasted_iota(jnp.int32, (t, gw), 1), 6)

    for g in range(SSD_GROUPS):
        b_lo = SSD_INNER + g * SSD_STATE
        c_lo = SSD_INNER + SSD_GROUPS * SSD_STATE + g * SSD_STATE
        bg = xs_ref[:, b_lo:b_lo + SSD_STATE].astype(BF16)
        cg = xs_ref[:, c_lo:c_lo + SSD_STATE].astype(BF16)
        cb = _dot_nt(cg, bg)
        sg = state_ref[g]
        xdt_g = xdt[:, g * gw:(g + 1) * gw]
        y = _dot(cg, sg.astype(BF16)) * y_scale[:, g * gw:(g + 1) * gw]
        for hh in range(heads_per_group):
            h = g * heads_per_group + hh
            decay = jnp.where(mask, jnp.exp(cs[:, h:h + 1] - cs_t[h:h + 1, :]), 0.0)
            x_h = jnp.where(lane_head == hh, xdt_g, 0.0).astype(BF16)
            y = y + _dot((cb * decay).astype(BF16), x_h)
        y_ref[:, g * gw:(g + 1) * gw] = y
        ds = _dot_tn(bg, x_state[:, g * gw:(g + 1) * gw].astype(BF16))
        state_ref[g] = sg * carry[:, g * gw:(g + 1) * gw] + ds


def _ssd_scan(xs, dt2, a_log, batch, n_lat_chunks, n_ctx_chunks):
    m = xs.shape[0]
    t = SSD_CHUNK
    assert SSD_HEAD_DIM == 64 and SSD_HEADS <= LANE
    a_log_ch = jnp.repeat(a_log, SSD_HEAD_DIM, axis=1).reshape(2, 1, SSD_INNER)
    a_log = _pad_lanes(a_log).reshape(2, 1, LANE)

    def chunk(b, d, s):
        j_ctx = jnp.where(d == 0, s, n_ctx_chunks - 1 - s)
        sl = s - n_ctx_chunks
        j_lat = jnp.where(d == 0, sl, n_lat_chunks - 1 - sl)
        return jnp.where(s < n_ctx_chunks, batch * n_lat_chunks + b * n_ctx_chunks + j_ctx,
                         b * n_lat_chunks + j_lat)

    return pl.pallas_call(
        _ssd_kernel,
        grid=(batch, 2, n_ctx_chunks + n_lat_chunks),
        in_specs=[pl.BlockSpec((t, SSD_XBC), lambda b, d, s: (chunk(b, d, s), 0)),
                  pl.BlockSpec((None, t, LANE), lambda b, d, s: (d, chunk(b, d, s), 0)),
                  pl.BlockSpec((None, 1, LANE), lambda b, d, s: (d, 0, 0)),
                  pl.BlockSpec((None, 1, SSD_INNER), lambda b, d, s: (d, 0, 0))],
        out_specs=pl.BlockSpec((None, t, SSD_INNER), lambda b, d, s: (d, chunk(b, d, s), 0)),
        out_shape=jax.ShapeDtypeStruct((2, m, SSD_INNER), F32),
        scratch_shapes=[pltpu.VMEM((SSD_GROUPS, SSD_STATE, SSD_GROUP_W), F32)],
        compiler_params=_params(("parallel", "parallel", "arbitrary")),
        name="ssd_scan",
    )(xs, dt2, a_log, a_log_ch)


def _ssd_gate_kernel(y_ref, xs_ref, z_ref, d_ref, g_ref, o_ref):
    yl = d_ref[...] * xs_ref[...] + y_ref[0] + y_ref[1]
    u = yl * _silu(z_ref[...])
    gw = SSD_GROUP_W
    for g in range(SSD_GROUPS):
        ug = u[:, g * gw:(g + 1) * gw]
        ug = ug * lax.rsqrt(jnp.mean(ug * ug, axis=-1, keepdims=True) + NORM_EPS)
        o_ref[:, g * gw:(g + 1) * gw] = (ug * g_ref[:, g * gw:(g + 1) * gw]).astype(o_ref.dtype)


def _ssd_gate(y, xs, p_a, d_row, norm_g, n_tiles):
    m = xs.shape[0]
    w = SSD_INNER
    return pl.pallas_call(
        _ssd_gate_kernel,
        grid=(n_tiles,),
        in_specs=[pl.BlockSpec((2, ROW_TILE, w), lambda i: (0, i, 0)),
                  pl.BlockSpec((ROW_TILE, w), lambda i: (i, 0)),
                  pl.BlockSpec((ROW_TILE, w), lambda i: (i, COL_Z // w)),
                  pl.BlockSpec((1, w), lambda i: (0, 0)),
                  pl.BlockSpec((1, w), lambda i: (0, 0))],
        out_specs=pl.BlockSpec((ROW_TILE, w), lambda i: (i, 0)),
        out_shape=jax.ShapeDtypeStruct((n_tiles * ROW_TILE, w), BF16),
        compiler_params=_params(("parallel",)),
        name="ssd_gate_norm",
    )(y, xs, p_a, d_row, norm_g)


ATT_TQ = 512
ATT_TKC = 1024


def _attn_kernel(*refs, has_lat, lam_init):
    if has_lat:
        (q_ref, kc_ref, vct_ref, kl_ref, vlt_ref, lam_ref, g_ref, o_ref,
         m_ref, acc_ref, sa_ref, sb_ref) = refs
    else:
        q_ref, kc_ref, vct_ref, lam_ref, g_ref, o_ref, m_ref, acc_ref, sa_ref, sb_ref = refs

    q = q_ref[...]
    lane = lax.broadcasted_iota(jnp.int32, q.shape, 1)
    zero = jnp.zeros_like(q)
    q_maps = (jnp.where(lane < DA_HEAD_DIM, q, zero), jnp.where(lane >= DA_HEAD_DIM, q, zero))

    def scores(k, s_ref):
        n = k.shape[0]
        maxima = []
        for mp in range(2):
            s = _dot_nt(k, q_maps[mp])
            s_ref[mp, 0:n, :] = s
            maxima.append(jnp.max(s, axis=0, keepdims=True))
        return tuple(maxima)

    def update(s_ref, maxima, vt):
        n = vt.shape[1]
        for mp in range(2):
            m_old = m_ref[mp]
            m_new = jnp.maximum(m_old, maxima[mp])
            alpha = jnp.exp2(m_old - m_new)
            p = jnp.exp2(s_ref[mp, 0:n, :] - m_new[0:1, :])
            acc_ref[mp] = alpha[0:1, :] * acc_ref[mp] + _dot(vt, p.astype(vt.dtype))
            m_ref[mp] = m_new

    m_ref[...] = jnp.full(m_ref.shape, -jnp.inf, F32)
    acc_ref[...] = jnp.zeros_like(acc_ref)
    mx = scores(kc_ref[...], sa_ref)

    if not has_lat:
        update(sa_ref, mx, vct_ref[...])
    else:
        tkc = min(ATT_TKC, kl_ref.shape[0])
        n_chunks = kl_ref.shape[0] // tkc
        k_at = lambda c: kl_ref[pl.ds(pl.multiple_of(c * tkc, tkc), tkc), :]
        vt_at = lambda c: vlt_ref[:, pl.ds(pl.multiple_of(c * tkc, tkc), tkc)]

        mx_ctx = mx
        mx = scores(k_at(0), sb_ref)
        update(sa_ref, mx_ctx, vct_ref[...])

        def pair(c2, mx_b):
            c = 1 + 2 * c2
            mx_a = scores(k_at(c), sa_ref)
            update(sb_ref, mx_b, vt_at(c - 1))
            mx_b = scores(k_at(c + 1), sb_ref)
            update(sa_ref, mx_a, vt_at(c))
            return mx_b

        mx = lax.fori_loop(0, (n_chunks - 1) // 2, pair, mx)
        if (n_chunks - 1) % 2:
            mx_a = scores(k_at(n_chunks - 1), sa_ref)
            update(sb_ref, mx, vt_at(n_chunks - 2))
            update(sa_ref, mx_a, vt_at(n_chunks - 1))
        else:
            update(sb_ref, mx, vt_at(n_chunks - 1))

    lv = lam_ref[...]
    dotp = lambda a, b: jnp.sum(lv[a:a + 1, :] * lv[b:b + 1, :], axis=-1, keepdims=True)
    lam = jnp.exp(dotp(0, 1)) - jnp.exp(dotp(2, 3)) + lam_init
    vd = DA_V_DIM
    inv_l = 1.0 / acc_ref[:, vd:vd + 1, :]
    o = acc_ref[0, 0:vd, :] * inv_l[0] - lam * (acc_ref[1, 0:vd, :] * inv_l[1])
    y = o * lax.rsqrt(jnp.mean(o * o, axis=0, keepdims=True) + NORM_EPS)
    o_ref[...] = ((y * g_ref[...]) * (1.0 - lam_init)).T.astype(o_ref.dtype)


VT_ROWS = DA_V_DIM + 16


def _values_t(v):
    n = v.shape[0]
    vt = v.T.reshape(DA_HEADS, DA_V_DIM, n)
    ones = jnp.ones((DA_HEADS, 1, n), v.dtype)
    zeros = jnp.zeros((DA_HEADS, VT_ROWS - DA_V_DIM - 1, n), v.dtype)
    return jnp.concatenate([vt, ones, zeros], axis=1).reshape(DA_HEADS * VT_ROWS, n)


def _attn_scratch(tq, keys_a, keys_b):
    return [pltpu.VMEM((2, SUBLANE, tq), F32), pltpu.VMEM((2, VT_ROWS, tq), F32),
            pltpu.VMEM((2, keys_a, tq), F32), pltpu.VMEM((2, keys_b, tq), F32)]


def _attention_lat(qkv, v_t, da_lambda, subln, lam_init, batch, seq, ctx_len):
    h = DA_HEADS
    tq = ATT_TQ
    nq = seq // tq
    ctx_blk0 = (batch * seq) // ctx_len
    assert seq % tq == 0 and seq % min(ATT_TKC, seq) == 0
    kern = functools.partial(_attn_kernel, has_lat=True, lam_init=lam_init)
    return pl.pallas_call(
        kern,
        grid=(batch, h, nq),
        in_specs=[pl.BlockSpec((tq, LANE), lambda b, hh, i: (b * nq + i, hh)),
                  pl.BlockSpec((ctx_len, LANE), lambda b, hh, i: (ctx_blk0 + b, h + hh)),
                  pl.BlockSpec((VT_ROWS, ctx_len), lambda b, hh, i: (hh, ctx_blk0 + b)),
                  pl.BlockSpec((seq, LANE), lambda b, hh, i: (b, h + hh)),
                  pl.BlockSpec((VT_ROWS, seq), lambda b, hh, i: (hh, b)),
                  pl.BlockSpec(da_lambda.shape, lambda b, hh, i: (0, 0)),
                  pl.BlockSpec((DA_V_DIM, 1), lambda b, hh, i: (0, 0))],
        out_specs=pl.BlockSpec((tq, LANE), lambda b, hh, i: (b * nq + i, hh)),
        out_shape=jax.ShapeDtypeStruct((batch * seq, DA_WIDTH), BF16),
        scratch_shapes=_attn_scratch(tq, max(min(ATT_TKC, seq), ctx_len), min(ATT_TKC, seq)),
        compiler_params=_params(("parallel", "parallel", "parallel"), 48),
        name="diff_attention",
    )(qkv, qkv, v_t, qkv, v_t, da_lambda, subln)


def _attention_ctx(qkv, v_t, da_lambda, subln, lam_init, batch, seq, ctx_len):
    h = DA_HEADS
    ctx_blk0 = (batch * seq) // ctx_len
    kern = functools.partial(_attn_kernel, has_lat=False, lam_init=lam_init)
    return pl.pallas_call(
        kern,
        grid=(batch, h),
        in_specs=[pl.BlockSpec((ctx_len, LANE), lambda b, hh: (ctx_blk0 + b, hh)),
                  pl.BlockSpec((ctx_len, LANE), lambda b, hh: (ctx_blk0 + b, h + hh)),
                  pl.BlockSpec((VT_ROWS, ctx_len), lambda b, hh: (hh, ctx_blk0 + b)),
                  pl.BlockSpec(da_lambda.shape, lambda b, hh: (0, 0)),
                  pl.BlockSpec((DA_V_DIM, 1), lambda b, hh: (0, 0))],
        out_specs=pl.BlockSpec((ctx_len, LANE), lambda b, hh: (b, hh)),
        out_shape=jax.ShapeDtypeStruct((batch * ctx_len, DA_WIDTH), BF16),
        scratch_shapes=_attn_scratch(ctx_len, ctx_len, SUBLANE),
        compiler_params=_params(("parallel", "parallel")),
        name="diff_attention_ctx",
    )(qkv, qkv, v_t, da_lambda, subln)


def _outproj_kernel(ya_ref, yb_ref, yc_ref, ycx_ref, w0_ref, w1_ref, w2_ref, x_ref, gate_ref, o_ref, *,
                    n_lat_tiles):
    yc = jnp.where(pl.program_id(0) < n_lat_tiles, yc_ref[...], ycx_ref[...])
    acc = _dot(ya_ref[...], w0_ref[...]) + _dot(yb_ref[...], w1_ref[...]) + _dot(yc, w2_ref[...])
    o_ref[...] = x_ref[...] + gate_ref[...] * acc


def _out_proj(ya, yb, yc, yc_ctx, w_out, x, mods, n_tiles, n_lat_tiles, cls_of_tile):
    m, d = x.shape
    tn = 1024
    wa, wb = ya.shape[1], yb.shape[1]
    assert wa == wb and yc.shape[1] == wa + wb
    if yc_ctx is None:
        assert n_tiles <= n_lat_tiles
        yc_ctx = yc
    return pl.pallas_call(
        functools.partial(_outproj_kernel, n_lat_tiles=n_lat_tiles),
        grid=(n_tiles, d // tn),
        in_specs=[pl.BlockSpec((ROW_TILE, wa), lambda i, j: (i, 0)),
                  pl.BlockSpec((ROW_TILE, wb), lambda i, j: (i, 0)),
                  pl.BlockSpec((ROW_TILE, wa + wb), lambda i, j: (jnp.minimum(i, n_lat_tiles - 1), 0)),
                  pl.BlockSpec((ROW_TILE, wa + wb), lambda i, j: (jnp.maximum(i - n_lat_tiles, 0), 0)),
                  pl.BlockSpec((wa, tn), lambda i, j: (0, j)),
                  pl.BlockSpec((wb, tn), lambda i, j: (1, j)),
                  pl.BlockSpec((wa + wb, tn), lambda i, j: (1, j)),
                  pl.BlockSpec((ROW_TILE, tn), lambda i, j: (i, j)),
                  pl.BlockSpec((None, 1, tn), lambda i, j: (cls_of_tile(i), 0, 2 * (d // tn) + j))],
        out_specs=pl.BlockSpec((ROW_TILE, tn), lambda i, j: (i, j)),
        out_shape=jax.ShapeDtypeStruct((n_tiles * ROW_TILE, d), F32),
        compiler_params=_params(("parallel", "parallel"), 40),
        name="out_proj_residual",
    )(ya, yb, yc, yc_ctx, w_out, w_out, w_out, x, mods)


def _ffn_kernel(x_ref, g_ref, sh_ref, sc_ref, gate_ref, wg_ref, wu_ref, wd_ref, o_ref, h_ref, acc_ref):
    f = pl.program_id(1)

    @pl.when(f == 0)
    def _():
        h_ref[...] = _modulated_norm(x_ref[...], g_ref[...], sh_ref[...], sc_ref[...]).astype(h_ref.dtype)
        acc_ref[...] = jnp.zeros_like(acc_ref)

    h = h_ref[...]
    a = _silu(_dot(h, wg_ref[...])) * _dot(h, wu_ref[...])
    acc_ref[...] += _dot(a.astype(BF16), wd_ref[...])

    @pl.when(f == pl.num_programs(1) - 1)
    def _():
        o_ref[...] = x_ref[...] + gate_ref[...] * acc_ref[...]


def _ffn_dense(x, g, mods, wg, wu, wd, n_tiles, cls_of_tile):
    m, d = x.shape
    ff = wg.shape[1]
    tf = 512
    return pl.pallas_call(
        _ffn_kernel,
        grid=(n_tiles, ff // tf),
        in_specs=[pl.BlockSpec((ROW_TILE, d), lambda i, f: (i, 0)),
                  pl.BlockSpec((1, d), lambda i, f: (0, 0)),
                  _mod_spec(3, d, cls_of_tile), _mod_spec(4, d, cls_of_tile), _mod_spec(5, d, cls_of_tile),
                  pl.BlockSpec((d, tf), lambda i, f: (0, f)),
                  pl.BlockSpec((d, tf), lambda i, f: (0, f)),
                  pl.BlockSpec((tf, d), lambda i, f: (f, 0))],
        out_specs=pl.BlockSpec((ROW_TILE, d), lambda i, f: (i, 0)),
        out_shape=jax.ShapeDtypeStruct((n_tiles * ROW_TILE, d), F32),
        scratch_shapes=[pltpu.VMEM((ROW_TILE, d), BF16), pltpu.VMEM((ROW_TILE, d), F32)],
        compiler_params=_params(("parallel", "arbitrary"), 48),
        name="ffn_dense",
    )(x, g.reshape(1, d), mods, mods, mods, wg, wu, wd)


ROUTE_I1, ROUTE_I2, ROUTE_P1, ROUTE_P2 = 0, 1, 2, 3


def _router_kernel(x_ref, g_ref, sh_ref, sc_ref, wr_ref, br_ref, h_ref, route_ref):
    h = _modulated_norm(x_ref[...], g_ref[...], sh_ref[...], sc_ref[...])
    h_ref[...] = h
    logits = jnp.dot(h, wr_ref[...], precision=HIGHEST, preferred_element_type=F32) + br_ref[...]
    lane = lax.broadcasted_iota(jnp.int32, logits.shape, 1).astype(F32)
    neg = -jnp.inf
    lg = jnp.where(lane < N_EXPERTS, logits, neg)
    v1 = jnp.max(lg, axis=-1, keepdims=True)
    i1 = jnp.min(jnp.where(lg == v1, lane, float(LANE)), axis=-1, keepdims=True)
    lg2 = jnp.where(lane == i1, neg, lg)
    v2 = jnp.max(lg2, axis=-1, keepdims=True)
    i2 = jnp.min(jnp.where(lg2 == v2, lane, float(LANE)), axis=-1, keepdims=True)
    e = jnp.exp(v2 - v1)
    p1 = 1.0 / (1.0 + e)
    p2 = e / (1.0 + e)
    rec = jnp.where(lane == ROUTE_I1, i1, 0.0) + jnp.where(lane == ROUTE_I2, i2, 0.0)
    route_ref[...] = rec + jnp.where(lane == ROUTE_P1, p1, 0.0) + jnp.where(lane == ROUTE_P2, p2, 0.0)


def _router(x, g, mods, wr, br, n_tiles, cls_of_tile):
    m, d = x.shape
    return pl.pallas_call(
        _router_kernel,
        grid=(n_tiles,),
        in_specs=[pl.BlockSpec((ROW_TILE, d), lambda i: (i, 0)),
                  pl.BlockSpec((1, d), lambda i: (0, 0)),
                  _mod_spec(3, d, cls_of_tile), _mod_spec(4, d, cls_of_tile),
                  pl.BlockSpec((d, LANE), lambda i: (0, 0)),
                  pl.BlockSpec((1, LANE), lambda i: (0, 0))],
        out_specs=[pl.BlockSpec((ROW_TILE, d), lambda i: (i, 0)),
                   pl.BlockSpec((ROW_TILE, LANE), lambda i: (i, 0))],
        out_shape=[jax.ShapeDtypeStruct((n_tiles * ROW_TILE, d), F32),
                   jax.ShapeDtypeStruct((n_tiles * ROW_TILE, LANE), F32)],
        compiler_params=_params(("parallel",), 40),
        name="moe_router",
    )(x, g.reshape(1, d), mods, mods, wr, br)


def _routing_tables(route, n_experts, tile):
    n_tok = route.shape[0]
    experts = route[:, ROUTE_I1:ROUTE_I2 + 1].astype(jnp.int32).reshape(-1)
    onehot = (experts[:, None] == jnp.arange(n_experts, dtype=jnp.int32)[None, :]).astype(jnp.int32)
    running = jnp.cumsum(onehot, axis=0)
    rank = jnp.sum(running * onehot, axis=1) - 1
    counts = running[-1]
    padded = ((counts + tile - 1) // tile) * tile
    ends = jnp.cumsum(padded)
    slot = (ends - padded)[experts] + rank
    n_rows = 2 * n_tok + n_experts * tile
    n_tiles = n_rows // tile
    token = jnp.repeat(jnp.arange(n_tok, dtype=jnp.int32), 2)
    row_token = jnp.zeros((n_rows,), jnp.int32).at[slot].set(token)
    tile_start = jnp.arange(n_tiles, dtype=jnp.int32) * tile
    tile_expert = jnp.minimum(jnp.sum((tile_start[:, None] >= ends[None, :]).astype(jnp.int32), axis=1),
                              n_experts - 1)
    n_active = (ends[-1] // tile).astype(jnp.int32).reshape(1)
    return tile_expert, row_token, n_active, slot.astype(jnp.int32)


MOE_TILE = 256
MOE_FF_SPLIT = 2
GATHER_UNROLL = 8


def _experts_kernel(te_ref, tok_ref, nact_ref, h_hbm, wg_ref, wu_ref, wd_ref, o_ref, xbuf, sem):
    j = pl.program_id(0)
    n_act = nact_ref[0]
    slot = j % 2
    tile = o_ref.shape[0]

    def row_copy(tok, r, s):
        return pltpu.make_async_copy(h_hbm.at[pl.ds(tok, 1), :], xbuf.at[s, pl.ds(r, 1), :], sem.at[s])

    def start_gather(t, s):
        def body(r, carry):
            row_copy(tok_ref[t * tile + r], r, s).start()
            return carry
        lax.fori_loop(0, tile, body, 0, unroll=GATHER_UNROLL)

    def wait_gather(s):
        pltpu.make_async_copy(h_hbm.at[pl.ds(0, tile), :], xbuf.at[s], sem.at[s]).wait()

    @pl.when(j < n_act)
    def _():
        @pl.when(j == 0)
        def _():
            start_gather(0, 0)

        wait_gather(slot)

        @pl.when(j + 1 < n_act)
        def _():
            start_gather(j + 1, 1 - slot)

        h = xbuf[slot].astype(BF16)
        ff = wg_ref.shape[1]
        w = ff // MOE_FF_SPLIT
        y = None
        for part in range(MOE_FF_SPLIT):
            lo = part * w
            a = _silu(_dot(h, wg_ref[:, lo:lo + w])) * _dot(h, wu_ref[:, lo:lo + w])
            yp = _dot(a.astype(BF16), wd_ref[lo:lo + w, :])
            y = yp if y is None else y + yp
        o_ref[...] = y

    @pl.when(j >= n_act)
    def _():
        o_ref[...] = jnp.zeros_like(o_ref)


def _experts(h, tile_expert, row_token, n_active, wg, wu, wd):
    n_e, d, ff = wg.shape
    tile = MOE_TILE
    n_rows = row_token.shape[0]
    assert ff % (MOE_FF_SPLIT * LANE) == 0
    resident = pl.Buffered(1)
    grid_spec = pltpu.PrefetchScalarGridSpec(
        num_scalar_prefetch=3,
        grid=(n_rows // tile,),
        in_specs=[pl.BlockSpec(memory_space=pl.ANY),
                  pl.BlockSpec((None, d, ff), lambda j, te, tok, na: (te[j], 0, 0), pipeline_mode=resident),
                  pl.BlockSpec((None, d, ff), lambda j, te, tok, na: (te[j], 0, 0), pipeline_mode=resident),
                  pl.BlockSpec((None, ff, d), lambda j, te, tok, na: (te[j], 0, 0), pipeline_mode=resident)],
        out_specs=pl.BlockSpec((tile, d), lambda j, te, tok, na: (j, 0)),
        scratch_shapes=[pltpu.VMEM((2, tile, d), F32), pltpu.SemaphoreType.DMA((2,))],
    )
    return pl.pallas_call(
        _experts_kernel,
        grid_spec=grid_spec,
        out_shape=jax.ShapeDtypeStruct((n_rows, d), F32),
        compiler_params=_params(("arbitrary",), 56),
        name="moe_experts",
    )(tile_expert, row_token, n_active, h, wg, wu, wd)


COMBINE_TILE = 256


def _combine_kernel(slot_ref, y_hbm, x_ref, route_ref, gate_ref, gf_ref, o_ref, ybuf, sem, *, final_norm):
    i = pl.program_id(0)
    n = pl.num_programs(0)
    buf = i % 2
    tile = x_ref.shape[0]

    def row_copy(row, k, r, s):
        return pltpu.make_async_copy(y_hbm.at[pl.ds(row, 1), :], ybuf.at[s, pl.ds(k * tile + r, 1), :], sem.at[s])

    def start_gather(t, s):
        def body(r, carry):
            base = 2 * (t * tile + r)
            row_copy(slot_ref[base], 0, r, s).start()
            row_copy(slot_ref[base + 1], 1, r, s).start()
            return carry
        lax.fori_loop(0, tile, body, 0, unroll=GATHER_UNROLL // 2)

    def wait_gather(s):
        pltpu.make_async_copy(y_hbm.at[pl.ds(0, 2 * tile), :], ybuf.at[s], sem.at[s]).wait()

    @pl.when(i == 0)
    def _():
        start_gather(0, 0)

    wait_gather(buf)

    @pl.when(i + 1 < n)
    def _():
        start_gather(i + 1, 1 - buf)

    route = route_ref[...]
    p1 = route[:, ROUTE_P1:ROUTE_P1 + 1]
    p2 = route[:, ROUTE_P2:ROUTE_P2 + 1]
    y = p1 * ybuf[buf, 0:tile, :] + p2 * ybuf[buf, tile:2 * tile, :]
    out = x_ref[...] + gate_ref[...] * y
    if final_norm:
        out = (out * lax.rsqrt(jnp.mean(out * out, axis=-1, keepdims=True) + NORM_EPS)) * gf_ref[...]
    o_ref[...] = out


def _combine(y_rows, slot, x, route, mods, n_tokens, cls_of_tile, final_g):
    d = x.shape[1]
    tile = COMBINE_TILE
    per = ROW_TILE // tile
    final_norm = final_g is not None
    if not final_norm:
        final_g = jnp.ones((d,), F32)
    grid_spec = pltpu.PrefetchScalarGridSpec(
        num_scalar_prefetch=1,
        grid=(n_tokens // tile,),
        in_specs=[pl.BlockSpec(memory_space=pl.ANY),
                  pl.BlockSpec((tile, d), lambda i, s: (i, 0)),
                  pl.BlockSpec((tile, LANE), lambda i, s: (i, 0)),
                  pl.BlockSpec((None, 1, d), lambda i, s: (cls_of_tile(i // per), 0, 5)),
                  pl.BlockSpec((1, d), lambda i, s: (0, 0))],
        out_specs=pl.BlockSpec((tile, d), lambda i, s: (i, 0)),
        scratch_shapes=[pltpu.VMEM((2, 2 * tile, d), F32), pltpu.SemaphoreType.DMA((2,))],
    )
    return pl.pallas_call(
        functools.partial(_combine_kernel, final_norm=final_norm),
        grid_spec=grid_spec,
        out_shape=jax.ShapeDtypeStruct((n_tokens, d), F32),
        compiler_params=_params(("arbitrary",), 48),
        name="moe_combine",
    )(slot, y_rows, x, route, mods, final_g.reshape(1, d))


def _final_kernel(x_ref, g_ref, o_ref):
    x = x_ref[...]
    o_ref[...] = (x * lax.rsqrt(jnp.mean(x * x, axis=-1, keepdims=True) + NORM_EPS)) * g_ref[...]


def _final_norm(x, g, n_tiles):
    d = x.shape[1]
    return pl.pallas_call(
        _final_kernel,
        grid=(n_tiles,),
        in_specs=[pl.BlockSpec((ROW_TILE, d), lambda i: (i, 0)),
                  pl.BlockSpec((1, d), lambda i: (0, 0))],
        out_specs=pl.BlockSpec((ROW_TILE, d), lambda i: (i, 0)),
        out_shape=jax.ShapeDtypeStruct((n_tiles * ROW_TILE, d), F32),
        compiler_params=_params(("parallel",), 40),
        name="final_norm",
    )(x, g.reshape(1, d))


def _pad_lanes(v, width=LANE):
    return jnp.pad(v, [(0, 0)] * (v.ndim - 1) + [(0, width - v.shape[-1])])


def kernel(x, c, ctx, c_ctx, w_mod, b_mod, g_mix, g_ffn, w_in, conv_w, ssd_conv_w, ssd_conv_b, ssd_a_log, ssd_dt_bias, ssd_d, ssd_norm, da_lambda, da_subln, w_out, ffn_w_gate, ffn_w_up, ffn_w_down, moe_w_router, moe_b_router, moe_w_gate, moe_w_up, moe_w_down, g_final):
    batch, seq, d = x.shape
    ctx_len = ctx.shape[1]
    depth = w_mod.shape[0]
    n_lat = batch * seq
    m = n_lat + batch * ctx_len
    assert seq % ROW_TILE == 0 and (batch * ctx_len) % ROW_TILE == 0 and n_lat % ctx_len == 0
    assert ctx_len % HALO_TILE == 0 and ctx_len % SSD_CHUNK == 0 and batch < MOD_CLASSES
    n_lat_tiles = n_lat // ROW_TILE
    n_all_tiles = m // ROW_TILE
    tiles_per_seq = seq // ROW_TILE
    cls_of_tile = lambda i: jnp.minimum(i // tiles_per_seq, batch)

    xs_all = jnp.concatenate([x.reshape(n_lat, d), ctx.reshape(batch * ctx_len, d)], axis=0)
    cvec = jnp.zeros((MOD_CLASSES, d), F32).at[:batch].set(c).at[batch].set(c_ctx)
    mods_all = _mod_vectors(cvec, w_mod, b_mod)
    cos_t, sin_t = _rope_tables(seq)

    for i in range(depth):
        ctx_out = i < depth - 1
        lam_init = 0.8 - 0.6 * math.exp(-0.3 * i)
        n_tiles = n_all_tiles if ctx_out else n_lat_tiles
        mods = mods_all[i].reshape(MOD_CLASSES, 1, N_MOD * d)
        wi = w_in[i]
        w_a = jnp.concatenate([wi[:, COL_CONV:COL_Q], wi[:, COL_XBC:COL_DT]], axis=1).astype(BF16)
        w_b = jnp.concatenate([wi[:, COL_Q:COL_XBC] * (DA_SCALE * LOG2_E), wi[:, COL_K:COL_V],
                               wi[:, COL_V:COL_V + DA_WIDTH]], axis=1).astype(BF16)
        w_c = _pad_lanes(wi[:, COL_DT:COL_K]).astype(BF16)

        h = _norm_modulate(xs_all, g_mix[i], mods, 0, n_all_tiles, cls_of_tile)
        p_a = _matmul(h, w_a, F32, 1024, "in_proj_conv_z_xbc")
        qkv = _matmul_rope(h, w_b, cos_t, sin_t, n_lat_tiles, tiles_per_seq)
        p_dt = _matmul(h, w_c, F32, LANE, "in_proj_dt")

        ya, xbc, dt2 = _prep(p_a, p_dt, conv_w[i], ssd_conv_w[i], ssd_conv_b[i].reshape(1, -1),
                             _pad_lanes(ssd_dt_bias[i].reshape(1, -1)), n_lat // HALO_TILE,
                             seq // HALO_TILE, ctx_len // HALO_TILE)
        y_dir = _ssd_scan(xbc, dt2, ssd_a_log[i], batch, seq // SSD_CHUNK, ctx_len // SSD_CHUNK)
        yb = _ssd_gate(y_dir, xbc, p_a, jnp.repeat(ssd_d[i], SSD_HEAD_DIM).reshape(1, -1),
                       ssd_norm[i].reshape(1, -1), n_tiles)

        subln = da_subln[i].reshape(-1, 1)
        v_t = _values_t(qkv[:, 2 * DA_QK:])
        yc = _attention_lat(qkv, v_t, da_lambda[i], subln, lam_init, batch, seq, ctx_len)
        yc_ctx = _attention_ctx(qkv, v_t, da_lambda[i], subln, lam_init, batch, seq, ctx_len) if ctx_out else None

        x_mid = _out_proj(ya, yb, yc, yc_ctx, w_out[i].astype(BF16), xs_all, mods, n_tiles, n_lat_tiles,
                          cls_of_tile)

        j = i // 2
        if i % 2 == 0:
            xs_all = _ffn_dense(x_mid, g_ffn[i], mods, ffn_w_gate[j].astype(BF16), ffn_w_up[j].astype(BF16),
                                ffn_w_down[j].astype(BF16), n_tiles, cls_of_tile)
        else:
            hh, route = _router(x_mid, g_ffn[i], mods, _pad_lanes(moe_w_router[j]),
                                _pad_lanes(moe_b_router[j].reshape(1, -1)), n_tiles, cls_of_tile)
            tile_expert, row_token, n_active, slot = _routing_tables(route, moe_w_gate.shape[1], MOE_TILE)
            y_rows = _experts(hh, tile_expert, row_token, n_active, moe_w_gate[j].astype(BF16),
                              moe_w_up[j].astype(BF16), moe_w_down[j].astype(BF16))
            xs_all = _combine(y_rows, slot, x_mid, route, mods, n_tiles * ROW_TILE, cls_of_tile,
                              None if ctx_out else g_final)

    last_is_moe = depth % 2 == 0
    out = xs_all if last_is_moe else _final_norm(xs_all, g_final, n_lat_tiles)
    return out.reshape(batch, seq, d)
```

```python
import functools
import math

import jax
import jax.numpy as jnp
from jax import lax
from jax.experimental import pallas as pl
from jax.experimental.pallas import tpu as pltpu

NORM_EPS = 1e-6
N_MOD = 6
GRID_W = 64

SHORT_CONV = 3
CONV_W = 512

SSD_HEADS = 8
SSD_HEAD_DIM = 64
SSD_INNER = SSD_HEADS * SSD_HEAD_DIM
SSD_STATE = 128
SSD_GROUPS = 2
SSD_CHUNK = 128
SSD_XBC = SSD_INNER + 2 * SSD_GROUPS * SSD_STATE
SSD_GROUP_W = SSD_INNER // SSD_GROUPS

DA_HEADS = 8
DA_HEAD_DIM = 64
DA_V_DIM = 2 * DA_HEAD_DIM
DA_QK = DA_HEADS * 2 * DA_HEAD_DIM
DA_WIDTH = DA_HEADS * DA_V_DIM
DA_SCALE = DA_HEAD_DIM ** -0.5
LOG2_E = math.log2(math.e)
ROPE_THETA = 10000.0

COL_CONV = 0
COL_Z = COL_CONV + 3 * CONV_W
COL_Q = COL_Z + SSD_INNER
COL_XBC = COL_Q + DA_QK
COL_DT = COL_XBC + SSD_XBC
COL_K = COL_DT + 2 * SSD_HEADS
COL_V = COL_K + DA_QK

N_EXPERTS = 8

LANE = 128
SUBLANE = 8
ROW_TILE = 512
HALO_TILE = 256
MOD_CLASSES = 8

F32 = jnp.float32
BF16 = jnp.bfloat16
HIGHEST = lax.Precision.HIGHEST
MIB = 1024 * 1024


def _params(semantics, vmem_mib=None):
    kw = {"dimension_semantics": semantics}
    if vmem_mib is not None:
        kw["vmem_limit_bytes"] = vmem_mib * MIB
    return pltpu.CompilerParams(**kw)


def _silu(v):
    return v * jax.nn.sigmoid(v)


def _dot(a, b):
    return jnp.dot(a, b, preferred_element_type=F32)


def _dot_nt(a, b):
    return lax.dot_general(a, b, (((1,), (1,)), ((), ())), preferred_element_type=F32)


def _dot_tn(a, b):
    return lax.dot_general(a, b, (((0,), (0,)), ((), ())), preferred_element_type=F32)


def _modulated_norm(x, g, shift, scale):
    ms = jnp.mean(x * x, axis=-1, keepdims=True)
    y = x * lax.rsqrt(ms + NORM_EPS)
    return (y * g) * (1.0 + scale) + shift


def _mod_spec(k, width, cls_of_tile):
    return pl.BlockSpec((None, 1, width), lambda i, *_: (cls_of_tile(i), 0, k))


def _mod_kernel(c_ref, w_ref, b_ref, o_ref):
    s = _silu(c_ref[...])
    o_ref[...] = jnp.dot(s, w_ref[...], precision=HIGHEST, preferred_element_type=F32) + b_ref[...]


def _mod_vectors(cvec, w_mod, b_mod):
    depth, d, n = w_mod.shape
    tn = 1024
    return pl.pallas_call(
        _mod_kernel,
        grid=(depth, n // tn),
        in_specs=[pl.BlockSpec((MOD_CLASSES, d), lambda l, j: (0, 0)),
                  pl.BlockSpec((None, d, tn), lambda l, j: (l, 0, j)),
                  pl.BlockSpec((None, 1, tn), lambda l, j: (l, 0, j))],
        out_specs=pl.BlockSpec((None, MOD_CLASSES, tn), lambda l, j: (l, 0, j)),
        out_shape=jax.ShapeDtypeStruct((depth, MOD_CLASSES, n), F32),
        compiler_params=_params(("parallel", "parallel"), 40),
        name="mod_vectors",
    )(cvec, w_mod, b_mod.reshape(depth, 1, n))


def _lat_ctx_specs(width, n_lat_tiles):
    return [pl.BlockSpec((ROW_TILE, width), lambda i, *_: (jnp.minimum(i, n_lat_tiles - 1), 0)),
            pl.BlockSpec((ROW_TILE, width), lambda i, *_: (jnp.maximum(i - n_lat_tiles, 0), 0))]


def _lat_or_ctx(lat_ref, ctx_ref, n_lat_tiles):
    return jnp.where(pl.program_id(0) < n_lat_tiles, lat_ref[...], ctx_ref[...])


def _norm_kernel(xl_ref, xc_ref, g_ref, sh_ref, sc_ref, h_ref, *, n_lat_tiles):
    x = _lat_or_ctx(xl_ref, xc_ref, n_lat_tiles)
    h_ref[...] = _modulated_norm(x, g_ref[...], sh_ref[...], sc_ref[...]).astype(h_ref.dtype)


def _norm_modulate(x_lat, x_ctx, g, mods, k_shift, n_tiles, n_lat_tiles, cls_of_tile):
    d = x_lat.shape[1]
    return pl.pallas_call(
        functools.partial(_norm_kernel, n_lat_tiles=n_lat_tiles),
        grid=(n_tiles,),
        in_specs=_lat_ctx_specs(d, n_lat_tiles) + [
            pl.BlockSpec((1, d), lambda i: (0, 0)),
            _mod_spec(k_shift, d, cls_of_tile),
            _mod_spec(k_shift + 1, d, cls_of_tile)],
        out_specs=pl.BlockSpec((ROW_TILE, d), lambda i: (i, 0)),
        out_shape=jax.ShapeDtypeStruct((n_tiles * ROW_TILE, d), BF16),
        compiler_params=_params(("parallel",), 40),
        name="norm_modulate",
    )(x_lat, x_ctx, g.reshape(1, d), mods, mods)


def _mm_kernel(x_ref, w_ref, o_ref):
    o_ref[...] = _dot(x_ref[...], w_ref[...]).astype(o_ref.dtype)


def _matmul(x, w, out_dtype, tn, name):
    m, k = x.shape
    n = w.shape[1]
    return pl.pallas_call(
        _mm_kernel,
        grid=(m // ROW_TILE, n // tn),
        in_specs=[pl.BlockSpec((ROW_TILE, k), lambda i, j: (i, 0)),
                  pl.BlockSpec((k, tn), lambda i, j: (0, j))],
        out_specs=pl.BlockSpec((ROW_TILE, tn), lambda i, j: (i, j)),
        out_shape=jax.ShapeDtypeStruct((m, n), out_dtype),
        compiler_params=_params(("parallel", "parallel"), 40),
        name=name,
    )(x, w)


def _mm_rope_kernel(x_ref, w_ref, cos_ref, sin_ref, o_ref, *, n_lat_tiles, n_rope_tiles):
    i = pl.program_id(0)
    j = pl.program_id(1)
    acc = _dot(x_ref[...], w_ref[...])
    rotate = jnp.logical_and(i < n_lat_tiles, j < n_rope_tiles)

    @pl.when(rotate)
    def _():
        tm, tn = acc.shape
        lane = lax.broadcasted_iota(jnp.int32, (tm, LANE), 1)
        first = (lane % (DA_HEAD_DIM // 2)) < (DA_HEAD_DIM // 4)
        c = cos_ref[...]
        s = sin_ref[...]
        quarter = DA_HEAD_DIM // 4
        for hb in range(tn // LANE):
            blk = acc[:, hb * LANE:(hb + 1) * LANE]
            partner = jnp.where(first, pltpu.roll(blk, LANE - quarter, 1), pltpu.roll(blk, quarter, 1))
            o_ref[:, hb * LANE:(hb + 1) * LANE] = (blk * c + partner * s).astype(o_ref.dtype)

    @pl.when(jnp.logical_not(rotate))
    def _():
        o_ref[...] = acc.astype(o_ref.dtype)


def _matmul_rope(x, w, cos_t, sin_t, n_lat_tiles, tiles_per_seq):
    m, k = x.shape
    n = w.shape[1]
    tn = DA_QK
    kern = functools.partial(_mm_rope_kernel, n_lat_tiles=n_lat_tiles, n_rope_tiles=2 * DA_QK // tn)
    return pl.pallas_call(
        kern,
        grid=(m // ROW_TILE, n // tn),
        in_specs=[pl.BlockSpec((ROW_TILE, k), lambda i, j: (i, 0)),
                  pl.BlockSpec((k, tn), lambda i, j: (0, j)),
                  pl.BlockSpec((ROW_TILE, LANE), lambda i, j: (i % tiles_per_seq, 0)),
                  pl.BlockSpec((ROW_TILE, LANE), lambda i, j: (i % tiles_per_seq, 0))],
        out_specs=pl.BlockSpec((ROW_TILE, tn), lambda i, j: (i, j)),
        out_shape=jax.ShapeDtypeStruct((m, n), BF16),
        compiler_params=_params(("parallel", "parallel"), 40),
        name="qkv_proj_rope",
    )(x, w, cos_t, sin_t)


def _rope_tables(n_tokens):
    rows = n_tokens // GRID_W
    row = jnp.repeat(jnp.arange(rows, dtype=F32), GRID_W)
    col = jnp.broadcast_to(jnp.arange(GRID_W, dtype=F32), (rows, GRID_W)).reshape(-1)
    n_freq = DA_HEAD_DIM // 4
    inv = ROPE_THETA ** (-jnp.arange(n_freq, dtype=F32) / n_freq)
    ang_r = row[:, None] * inv
    ang_c = col[:, None] * inv
    cos64 = jnp.concatenate([jnp.cos(ang_r), jnp.cos(ang_r), jnp.cos(ang_c), jnp.cos(ang_c)], axis=-1)
    sin64 = jnp.concatenate([-jnp.sin(ang_r), jnp.sin(ang_r), -jnp.sin(ang_c), jnp.sin(ang_c)], axis=-1)
    return jnp.tile(cos64, (1, 2)), jnp.tile(sin64, (1, 2))


def _conv3(u, prev_row, next_row, w_ref, has_prev, has_next):
    tm = u.shape[0]
    row = lax.broadcasted_iota(jnp.int32, u.shape, 0)
    prev_row = jnp.where(has_prev, prev_row, 0.0)
    next_row = jnp.where(has_next, next_row, 0.0)
    before = jnp.where(row == 0, prev_row, pltpu.roll(u, 1, 0))
    after = jnp.where(row == tm - 1, next_row, pltpu.roll(u, tm - 1, 0))
    return before * w_ref[0:1, :] + u * w_ref[1:2, :] + after * w_ref[2:3, :]


def _prep_kernel(cv_ref, cvp_ref, cvn_ref, xb_ref, xbp_ref, xbn_ref, dtr_ref,
                 cw_ref, sw_ref, sb_ref, dtb_ref, ya_ref, xs_ref, dt_ref,
                 *, n_lat_tiles, lat_tiles_per_seq, ctx_tiles_per_seq):
    i = pl.program_id(0)
    is_lat = i < n_lat_tiles
    pos = jnp.where(is_lat, i % lat_tiles_per_seq, (i - n_lat_tiles) % ctx_tiles_per_seq)
    last = jnp.where(is_lat, lat_tiles_per_seq - 1, ctx_tiles_per_seq - 1)
    has_prev = pos != 0
    has_next = pos != last

    w = CONV_W
    cv = cv_ref[...]
    gate_b, u = cv[:, 0:w], cv[:, w:2 * w] * cv[:, 2 * w:3 * w]
    p = cvp_ref[SUBLANE - 1:SUBLANE, :]
    n = cvn_ref[0:1, :]
    u_prev = p[:, w:2 * w] * p[:, 2 * w:3 * w]
    u_next = n[:, w:2 * w] * n[:, 2 * w:3 * w]
    ya_ref[...] = (gate_b * _conv3(u, u_prev, u_next, cw_ref, has_prev, has_next)).astype(ya_ref.dtype)

    xc = _conv3(xb_ref[...], xbp_ref[SUBLANE - 1:SUBLANE, :], xbn_ref[0:1, :], sw_ref, has_prev, has_next)
    xs_ref[...] = _silu(xc + sb_ref[...])

    t = dtr_ref[...] + dtb_ref[...]
    dt = jnp.maximum(t, 0.0) + jnp.log1p(jnp.exp(-jnp.abs(t)))
    dt_ref[0] = dt
    dt_ref[1] = pltpu.roll(dt, LANE - SSD_HEADS, 1)


def _prep(p_a, p_dt, conv_w, ssd_conv_w, ssd_conv_b, dt_bias, n_lat_tiles, lat_tiles_per_seq,
          ctx_tiles_per_seq):
    m = p_a.shape[0]
    tm = HALO_TILE
    per = tm // SUBLANE
    n8 = m // SUBLANE
    cw = 3 * CONV_W
    xbc_blk = COL_Q // SSD_XBC
    prev = lambda i: jnp.maximum(i * per - 1, 0)
    nxt = lambda i: jnp.minimum((i + 1) * per, n8 - 1)
    kern = functools.partial(_prep_kernel, n_lat_tiles=n_lat_tiles, lat_tiles_per_seq=lat_tiles_per_seq,
                             ctx_tiles_per_seq=ctx_tiles_per_seq)
    return pl.pallas_call(
        kern,
        grid=(m // tm,),
        in_specs=[pl.BlockSpec((tm, cw), lambda i: (i, 0)),
                  pl.BlockSpec((SUBLANE, cw), lambda i: (prev(i), 0)),
                  pl.BlockSpec((SUBLANE, cw), lambda i: (nxt(i), 0)),
                  pl.BlockSpec((tm, SSD_XBC), lambda i: (i, xbc_blk)),
                  pl.BlockSpec((SUBLANE, SSD_XBC), lambda i: (prev(i), xbc_blk)),
                  pl.BlockSpec((SUBLANE, SSD_XBC), lambda i: (nxt(i), xbc_blk)),
                  pl.BlockSpec((tm, LANE), lambda i: (i, 0)),
                  pl.BlockSpec((SHORT_CONV, CONV_W), lambda i: (0, 0)),
                  pl.BlockSpec((SHORT_CONV, SSD_XBC), lambda i: (0, 0)),
                  pl.BlockSpec((1, SSD_XBC), lambda i: (0, 0)),
                  pl.BlockSpec((1, LANE), lambda i: (0, 0))],
        out_specs=[pl.BlockSpec((tm, CONV_W), lambda i: (i, 0)),
                   pl.BlockSpec((tm, SSD_XBC), lambda i: (i, 0)),
                   pl.BlockSpec((2, tm, LANE), lambda i: (0, i, 0))],
        out_shape=[jax.ShapeDtypeStruct((m, CONV_W), BF16),
                   jax.ShapeDtypeStruct((m, SSD_XBC), F32),
                   jax.ShapeDtypeStruct((2, m, LANE), F32)],
        compiler_params=_params(("parallel",), 40),
        name="conv_prep",
    )(p_a, p_a, p_a, p_a, p_a, p_a, p_dt, conv_w, ssd_conv_w, ssd_conv_b, dt_bias)


def _split3(v):
    hi = v.astype(BF16)
    rest = v - hi.astype(F32)
    mid = rest.astype(BF16)
    return hi, mid, (rest - mid.astype(F32)).astype(BF16)


def _dot_f32_lhs(a, b01):
    return sum(_dot(piece, b01) for piece in _split3(a))


def _dot_f32_rhs(a01, b):
    return sum(_dot(a01, piece) for piece in _split3(b))


def _ssd_chunk(direction, xs_ref, dt_ref, alog_ref, alog_ch_ref, y_ref, state_ref):
    t = SSD_CHUNK
    r = lax.broadcasted_iota(jnp.int32, (t, t), 0)
    c = lax.broadcasted_iota(jnp.int32, (t, t), 1)
    mask = (r >= c) if direction == 0 else (r <= c)
    tri = mask.astype(BF16)

    er = lax.broadcasted_iota(jnp.int32, (LANE, SSD_INNER), 0)
    ec = lax.broadcasted_iota(jnp.int32, (LANE, SSD_INNER), 1)
    expand = (jnp.right_shift(ec, 6) == er).astype(BF16)

    dt = dt_ref[direction]
    cs = _dot_f32_rhs(tri, dt * (-jnp.exp(alog_ref[direction])))
    cs_t = cs.T
    dt_ch = _dot_f32_lhs(dt, expand)
    da_ch = dt_ch * (-jnp.exp(alog_ch_ref[direction]))
    cs_ch = _dot_f32_rhs(tri, da_ch)
    tot_ch = jnp.sum(da_ch, axis=0, keepdims=True)

    xdt = xs_ref[:, 0:SSD_INNER] * dt_ch
    x_state = xdt * jnp.exp(tot_ch - cs_ch)
    y_scale = jnp.exp(cs_ch)
    carry = jnp.exp(tot_ch)
    gw = SSD_GROUP_W
    heads_per_group = SSD_HEADS // SSD_GROUPS
    lane_head = jnp.right_shift(lax.broadcasted_iota(jnp.int32, (t, gw), 1), 6)

    for g in range(SSD_GROUPS):
        b_lo = SSD_INNER + g * SSD_STATE
        c_lo = SSD_INNER + SSD_GROUPS * SSD_STATE + g * SSD_STATE
        bg = xs_ref[:, b_lo:b_lo + SSD_STATE].astype(BF16)
        cg = xs_ref[:, c_lo:c_lo + SSD_STATE].astype(BF16)
        cb = _dot_nt(cg, bg)
        sg = state_ref[direction, g]
        xdt_g = xdt[:, g * gw:(g + 1) * gw]
        y = _dot(cg, sg.astype(BF16)) * y_scale[:, g * gw:(g + 1) * gw]
        for hh in range(heads_per_group):
            h = g * heads_per_group + hh
            decay = jnp.where(mask, jnp.exp(cs[:, h:h + 1] - cs_t[h:h + 1, :]), 0.0)
            x_h = jnp.where(lane_head == hh, xdt_g, 0.0).astype(BF16)
            y = y + _dot((cb * decay).astype(BF16), x_h)
        y_ref[:, g * gw:(g + 1) * gw] = y
        ds = _dot_tn(bg, x_state[:, g * gw:(g + 1) * gw].astype(BF16))
        state_ref[direction, g] = sg * carry[:, g * gw:(g + 1) * gw] + ds


def _ssd_kernel(xf_ref, dtf_ref, xb_ref, dtb_ref, alog_ref, alog_ch_ref, yf_ref, yb_ref, state_ref):
    @pl.when(pl.program_id(1) == 0)
    def _():
        state_ref[...] = jnp.zeros_like(state_ref)

    _ssd_chunk(0, xf_ref, dtf_ref, alog_ref, alog_ch_ref, yf_ref, state_ref)
    _ssd_chunk(1, xb_ref, dtb_ref, alog_ref, alog_ch_ref, yb_ref, state_ref)


def _ssd_scan(xs, dt2, a_log, batch, n_lat_chunks, n_ctx_chunks):
    m = xs.shape[0]
    t = SSD_CHUNK
    assert SSD_HEAD_DIM == 64 and SSD_HEADS <= LANE
    a_log_ch = jnp.repeat(a_log, SSD_HEAD_DIM, axis=1).reshape(2, 1, SSD_INNER)
    a_log = _pad_lanes(a_log).reshape(2, 1, LANE)

    def chunk(b, d, s):
        j_ctx = s if d == 0 else n_ctx_chunks - 1 - s
        sl = s - n_ctx_chunks
        j_lat = sl if d == 0 else n_lat_chunks - 1 - sl
        return jnp.where(s < n_ctx_chunks, batch * n_lat_chunks + b * n_ctx_chunks + j_ctx,
                         b * n_lat_chunks + j_lat)

    x_spec = lambda d: pl.BlockSpec((t, SSD_XBC), lambda b, s: (chunk(b, d, s), 0))
    dt_spec = lambda d: pl.BlockSpec((2, t, LANE), lambda b, s: (0, chunk(b, d, s), 0))
    y_spec = lambda d: pl.BlockSpec((t, SSD_INNER), lambda b, s: (chunk(b, d, s), 0))
    return pl.pallas_call(
        _ssd_kernel,
        grid=(batch, n_ctx_chunks + n_lat_chunks),
        in_specs=[x_spec(0), dt_spec(0), x_spec(1), dt_spec(1),
                  pl.BlockSpec((2, 1, LANE), lambda b, s: (0, 0, 0)),
                  pl.BlockSpec((2, 1, SSD_INNER), lambda b, s: (0, 0, 0))],
        out_specs=[y_spec(0), y_spec(1)],
        out_shape=[jax.ShapeDtypeStruct((m, SSD_INNER), F32), jax.ShapeDtypeStruct((m, SSD_INNER), F32)],
        scratch_shapes=[pltpu.VMEM((2, SSD_GROUPS, SSD_STATE, SSD_GROUP_W), F32)],
        compiler_params=_params(("parallel", "arbitrary")),
        name="ssd_scan",
    )(xs, dt2, xs, dt2, a_log, a_log_ch)


def _ssd_gate_kernel(yf_ref, yb_ref, xs_ref, z_ref, d_ref, g_ref, o_ref):
    yl = d_ref[...] * xs_ref[...] + yf_ref[...] + yb_ref[...]
    u = yl * _silu(z_ref[...])
    gw = SSD_GROUP_W
    for g in range(SSD_GROUPS):
        ug = u[:, g * gw:(g + 1) * gw]
        ug = ug * lax.rsqrt(jnp.mean(ug * ug, axis=-1, keepdims=True) + NORM_EPS)
        o_ref[:, g * gw:(g + 1) * gw] = (ug * g_ref[:, g * gw:(g + 1) * gw]).astype(o_ref.dtype)


def _ssd_gate(y_fwd, y_bwd, xs, p_a, d_row, norm_g, n_tiles):
    w = SSD_INNER
    return pl.pallas_call(
        _ssd_gate_kernel,
        grid=(n_tiles,),
        in_specs=[pl.BlockSpec((ROW_TILE, w), lambda i: (i, 0)),
                  pl.BlockSpec((ROW_TILE, w), lambda i: (i, 0)),
                  pl.BlockSpec((ROW_TILE, w), lambda i: (i, 0)),
                  pl.BlockSpec((ROW_TILE, w), lambda i: (i, COL_Z // w)),
                  pl.BlockSpec((1, w), lambda i: (0, 0)),
                  pl.BlockSpec((1, w), lambda i: (0, 0))],
        out_specs=pl.BlockSpec((ROW_TILE, w), lambda i: (i, 0)),
        out_shape=jax.ShapeDtypeStruct((n_tiles * ROW_TILE, w), BF16),
        compiler_params=_params(("parallel",)),
        name="ssd_gate_norm",
    )(y_fwd, y_bwd, xs, p_a, d_row, norm_g)


ATT_TQ = 512
ATT_TKC = 1024


def _attn_kernel(*refs, has_lat, lam_init):
    if has_lat:
        (q_ref, kc_ref, vct_ref, kl_ref, vlt_ref, lam_ref, g_ref, o_ref,
         m_ref, acc_ref, sa_ref, sb_ref) = refs
    else:
        q_ref, kc_ref, vct_ref, lam_ref, g_ref, o_ref, m_ref, acc_ref, sa_ref, sb_ref = refs

    q = q_ref[...]
    lane = lax.broadcasted_iota(jnp.int32, q.shape, 1)
    zero = jnp.zeros_like(q)
    q_maps = (jnp.where(lane < DA_HEAD_DIM, q, zero), jnp.where(lane >= DA_HEAD_DIM, q, zero))

    def scores(k, s_ref):
        n = k.shape[0]
        maxima = []
        for mp in range(2):
            s = _dot_nt(k, q_maps[mp])
            s_ref[mp, 0:n, :] = s
            maxima.append(jnp.max(s, axis=0, keepdims=True))
        return tuple(maxima)

    def update(s_ref, maxima, vt):
        n = vt.shape[1]
        for mp in range(2):
            m_old = m_ref[mp]
            m_new = jnp.maximum(m_old, maxima[mp])
            alpha = jnp.exp2(m_old - m_new)
            p = jnp.exp2(s_ref[mp, 0:n, :] - m_new[0:1, :])
            acc_ref[mp] = alpha[0:1, :] * acc_ref[mp] + _dot(vt, p.astype(vt.dtype))
            m_ref[mp] = m_new

    m_ref[...] = jnp.full(m_ref.shape, -jnp.inf, F32)
    acc_ref[...] = jnp.zeros_like(acc_ref)
    mx = scores(kc_ref[...], sa_ref)

    if not has_lat:
        update(sa_ref, mx, vct_ref[...])
    else:
        tkc = min(ATT_TKC, kl_ref.shape[0])
        n_chunks = kl_ref.shape[0] // tkc
        k_at = lambda c: kl_ref[pl.ds(pl.multiple_of(c * tkc, tkc), tkc), :]
        vt_at = lambda c: vlt_ref[:, pl.ds(pl.multiple_of(c * tkc, tkc), tkc)]

        mx_ctx = mx
        mx = scores(k_at(0), sb_ref)
        update(sa_ref, mx_ctx, vct_ref[...])

        def pair(c2, mx_b):
            c = 1 + 2 * c2
            mx_a = scores(k_at(c), sa_ref)
            update(sb_ref, mx_b, vt_at(c - 1))
            mx_b = scores(k_at(c + 1), sb_ref)
            update(sa_ref, mx_a, vt_at(c))
            return mx_b

        mx = lax.fori_loop(0, (n_chunks - 1) // 2, pair, mx)
        if (n_chunks - 1) % 2:
            mx_a = scores(k_at(n_chunks - 1), sa_ref)
            update(sb_ref, mx, vt_at(n_chunks - 2))
            update(sa_ref, mx_a, vt_at(n_chunks - 1))
        else:
            update(sb_ref, mx, vt_at(n_chunks - 1))

    lv = lam_ref[...]
    dotp = lambda a, b: jnp.sum(lv[a:a + 1, :] * lv[b:b + 1, :], axis=-1, keepdims=True)
    lam = jnp.exp(dotp(0, 1)) - jnp.exp(dotp(2, 3)) + lam_init
    vd = DA_V_DIM
    inv_l = 1.0 / acc_ref[:, vd:vd + 1, :]
    o = acc_ref[0, 0:vd, :] * inv_l[0] - lam * (acc_ref[1, 0:vd, :] * inv_l[1])
    y = o * lax.rsqrt(jnp.mean(o * o, axis=0, keepdims=True) + NORM_EPS)
    o_ref[...] = ((y * g_ref[...]) * (1.0 - lam_init)).T.astype(o_ref.dtype)


VT_ROWS = DA_V_DIM + 16


def _values_t(v):
    n = v.shape[0]
    vt = v.T.reshape(DA_HEADS, DA_V_DIM, n)
    ones = jnp.ones((DA_HEADS, 1, n), v.dtype)
    zeros = jnp.zeros((DA_HEADS, VT_ROWS - DA_V_DIM - 1, n), v.dtype)
    return jnp.concatenate([vt, ones, zeros], axis=1).reshape(DA_HEADS * VT_ROWS, n)


def _attn_scratch(tq, keys_a, keys_b):
    return [pltpu.VMEM((2, SUBLANE, tq), F32), pltpu.VMEM((2, VT_ROWS, tq), F32),
            pltpu.VMEM((2, keys_a, tq), F32), pltpu.VMEM((2, keys_b, tq), F32)]


def _attention_lat(qkv, v_t, da_lambda, subln, lam_init, batch, seq, ctx_len):
    h = DA_HEADS
    tq = ATT_TQ
    nq = seq // tq
    ctx_blk0 = (batch * seq) // ctx_len
    assert seq % tq == 0 and seq % min(ATT_TKC, seq) == 0
    kern = functools.partial(_attn_kernel, has_lat=True, lam_init=lam_init)
    return pl.pallas_call(
        kern,
        grid=(batch, h, nq),
        in_specs=[pl.BlockSpec((tq, LANE), lambda b, hh, i: (b * nq + i, hh)),
                  pl.BlockSpec((ctx_len, LANE), lambda b, hh, i: (ctx_blk0 + b, h + hh)),
                  pl.BlockSpec((VT_ROWS, ctx_len), lambda b, hh, i: (hh, ctx_blk0 + b)),
                  pl.BlockSpec((seq, LANE), lambda b, hh, i: (b, h + hh)),
                  pl.BlockSpec((VT_ROWS, seq), lambda b, hh, i: (hh, b)),
                  pl.BlockSpec(da_lambda.shape, lambda b, hh, i: (0, 0)),
                  pl.BlockSpec((DA_V_DIM, 1), lambda b, hh, i: (0, 0))],
        out_specs=pl.BlockSpec((tq, LANE), lambda b, hh, i: (b * nq + i, hh)),
        out_shape=jax.ShapeDtypeStruct((batch * seq, DA_WIDTH), BF16),
        scratch_shapes=_attn_scratch(tq, max(min(ATT_TKC, seq), ctx_len), min(ATT_TKC, seq)),
        compiler_params=_params(("parallel", "parallel", "parallel"), 48),
        name="diff_attention",
    )(qkv, qkv, v_t, qkv, v_t, da_lambda, subln)


def _attention_ctx(qkv, v_t, da_lambda, subln, lam_init, batch, seq, ctx_len):
    h = DA_HEADS
    ctx_blk0 = (batch * seq) // ctx_len
    kern = functools.partial(_attn_kernel, has_lat=False, lam_init=lam_init)
    return pl.pallas_call(
        kern,
        grid=(batch, h),
        in_specs=[pl.BlockSpec((ctx_len, LANE), lambda b, hh: (ctx_blk0 + b, hh)),
                  pl.BlockSpec((ctx_len, LANE), lambda b, hh: (ctx_blk0 + b, h + hh)),
                  pl.BlockSpec((VT_ROWS, ctx_len), lambda b, hh: (hh, ctx_blk0 + b)),
                  pl.BlockSpec(da_lambda.shape, lambda b, hh: (0, 0)),
                  pl.BlockSpec((DA_V_DIM, 1), lambda b, hh: (0, 0))],
        out_specs=pl.BlockSpec((ctx_len, LANE), lambda b, hh: (b, hh)),
        out_shape=jax.ShapeDtypeStruct((batch * ctx_len, DA_WIDTH), BF16),
        scratch_shapes=_attn_scratch(ctx_len, ctx_len, SUBLANE),
        compiler_params=_params(("parallel", "parallel")),
        name="diff_attention_ctx",
    )(qkv, qkv, v_t, da_lambda, subln)


def _outproj_kernel(ya_ref, yb_ref, yc_ref, ycx_ref, w0_ref, w1_ref, w2_ref, xl_ref, xc_ref, gate_ref, o_ref, *,
                    n_lat_tiles):
    yc = _lat_or_ctx(yc_ref, ycx_ref, n_lat_tiles)
    acc = _dot(ya_ref[...], w0_ref[...]) + _dot(yb_ref[...], w1_ref[...]) + _dot(yc, w2_ref[...])
    o_ref[...] = _lat_or_ctx(xl_ref, xc_ref, n_lat_tiles) + gate_ref[...] * acc


def _out_proj(ya, yb, yc, yc_ctx, w_out, x_lat, x_ctx, mods, n_tiles, n_lat_tiles, cls_of_tile):
    d = x_lat.shape[1]
    tn = 1024
    wa, wb = ya.shape[1], yb.shape[1]
    assert wa == wb and yc.shape[1] == wa + wb
    if yc_ctx is None:
        assert n_tiles <= n_lat_tiles
        yc_ctx = yc
    lat_tile = lambda i: jnp.minimum(i, n_lat_tiles - 1)
    ctx_tile = lambda i: jnp.maximum(i - n_lat_tiles, 0)
    return pl.pallas_call(
        functools.partial(_outproj_kernel, n_lat_tiles=n_lat_tiles),
        grid=(n_tiles, d // tn),
        in_specs=[pl.BlockSpec((ROW_TILE, wa), lambda i, j: (i, 0)),
                  pl.BlockSpec((ROW_TILE, wb), lambda i, j: (i, 0))] + _lat_ctx_specs(wa + wb, n_lat_tiles) + [
                  pl.BlockSpec((wa, tn), lambda i, j: (0, j)),
                  pl.BlockSpec((wb, tn), lambda i, j: (1, j)),
                  pl.BlockSpec((wa + wb, tn), lambda i, j: (1, j)),
                  pl.BlockSpec((ROW_TILE, tn), lambda i, j: (lat_tile(i), j)),
                  pl.BlockSpec((ROW_TILE, tn), lambda i, j: (ctx_tile(i), j)),
                  pl.BlockSpec((None, 1, tn), lambda i, j: (cls_of_tile(i), 0, 2 * (d // tn) + j))],
        out_specs=pl.BlockSpec((ROW_TILE, tn), lambda i, j: (i, j)),
        out_shape=jax.ShapeDtypeStruct((n_tiles * ROW_TILE, d), F32),
        compiler_params=_params(("parallel", "parallel"), 40),
        name="out_proj_residual",
    )(ya, yb, yc, yc_ctx, w_out, w_out, w_out, x_lat, x_ctx, mods)


def _ffn_kernel(x_ref, g_ref, sh_ref, sc_ref, gate_ref, wg_ref, wu_ref, wd_ref, o_ref, h_ref, acc_ref):
    f = pl.program_id(1)

    @pl.when(f == 0)
    def _():
        h_ref[...] = _modulated_norm(x_ref[...], g_ref[...], sh_ref[...], sc_ref[...]).astype(h_ref.dtype)
        acc_ref[...] = jnp.zeros_like(acc_ref)

    h = h_ref[...]
    a = _silu(_dot(h, wg_ref[...])) * _dot(h, wu_ref[...])
    acc_ref[...] += _dot(a.astype(BF16), wd_ref[...])

    @pl.when(f == pl.num_programs(1) - 1)
    def _():
        o_ref[...] = x_ref[...] + gate_ref[...] * acc_ref[...]


def _ffn_dense(x, g, mods, wg, wu, wd, n_tiles, cls_of_tile):
    m, d = x.shape
    ff = wg.shape[1]
    tf = 512
    return pl.pallas_call(
        _ffn_kernel,
        grid=(n_tiles, ff // tf),
        in_specs=[pl.BlockSpec((ROW_TILE, d), lambda i, f: (i, 0)),
                  pl.BlockSpec((1, d), lambda i, f: (0, 0)),
                  _mod_spec(3, d, cls_of_tile), _mod_spec(4, d, cls_of_tile), _mod_spec(5, d, cls_of_tile),
                  pl.BlockSpec((d, tf), lambda i, f: (0, f)),
                  pl.BlockSpec((d, tf), lambda i, f: (0, f)),
                  pl.BlockSpec((tf, d), lambda i, f: (f, 0))],
        out_specs=pl.BlockSpec((ROW_TILE, d), lambda i, f: (i, 0)),
        out_shape=jax.ShapeDtypeStruct((n_tiles * ROW_TILE, d), F32),
        scratch_shapes=[pltpu.VMEM((ROW_TILE, d), BF16), pltpu.VMEM((ROW_TILE, d), F32)],
        compiler_params=_params(("parallel", "arbitrary"), 48),
        name="ffn_dense",
    )(x, g.reshape(1, d), mods, mods, mods, wg, wu, wd)


ROUTE_I1, ROUTE_I2, ROUTE_P1, ROUTE_P2 = 0, 1, 2, 3


def _router_kernel(x_ref, g_ref, sh_ref, sc_ref, wr_ref, br_ref, h_ref, route_ref):
    h = _modulated_norm(x_ref[...], g_ref[...], sh_ref[...], sc_ref[...])
    h_ref[...] = h
    logits = jnp.dot(h, wr_ref[...], precision=HIGHEST, preferred_element_type=F32) + br_ref[...]
    lane = lax.broadcasted_iota(jnp.int32, logits.shape, 1).astype(F32)
    neg = -jnp.inf
    lg = jnp.where(lane < N_EXPERTS, logits, neg)
    v1 = jnp.max(lg, axis=-1, keepdims=True)
    i1 = jnp.min(jnp.where(lg == v1, lane, float(LANE)), axis=-1, keepdims=True)
    lg2 = jnp.where(lane == i1, neg, lg)
    v2 = jnp.max(lg2, axis=-1, keepdims=True)
    i2 = jnp.min(jnp.where(lg2 == v2, lane, float(LANE)), axis=-1, keepdims=True)
    e = jnp.exp(v2 - v1)
    p1 = 1.0 / (1.0 + e)
    p2 = e / (1.0 + e)
    rec = jnp.where(lane == ROUTE_I1, i1, 0.0) + jnp.where(lane == ROUTE_I2, i2, 0.0)
    route_ref[...] = rec + jnp.where(lane == ROUTE_P1, p1, 0.0) + jnp.where(lane == ROUTE_P2, p2, 0.0)


def _router(x, g, mods, wr, br, n_tiles, cls_of_tile):
    m, d = x.shape
    return pl.pallas_call(
        _router_kernel,
        grid=(n_tiles,),
        in_specs=[pl.BlockSpec((ROW_TILE, d), lambda i: (i, 0)),
                  pl.BlockSpec((1, d), lambda i: (0, 0)),
                  _mod_spec(3, d, cls_of_tile), _mod_spec(4, d, cls_of_tile),
                  pl.BlockSpec((d, LANE), lambda i: (0, 0)),
                  pl.BlockSpec((1, LANE), lambda i: (0, 0))],
        out_specs=[pl.BlockSpec((ROW_TILE, d), lambda i: (i, 0)),
                   pl.BlockSpec((ROW_TILE, LANE), lambda i: (i, 0))],
        out_shape=[jax.ShapeDtypeStruct((n_tiles * ROW_TILE, d), F32),
                   jax.ShapeDtypeStruct((n_tiles * ROW_TILE, LANE), F32)],
        compiler_params=_params(("parallel",), 40),
        name="moe_router",
    )(x, g.reshape(1, d), mods, mods, wr, br)


def _routing_tables(route, n_experts, tile):
    n_tok = route.shape[0]
    experts = route[:, ROUTE_I1:ROUTE_I2 + 1].astype(jnp.int32).reshape(-1)
    onehot = (experts[:, None] == jnp.arange(n_experts, dtype=jnp.int32)[None, :]).astype(jnp.int32)
    running = jnp.cumsum(onehot, axis=0)
    rank = jnp.sum(running * onehot, axis=1) - 1
    counts = running[-1]
    padded = ((counts + tile - 1) // tile) * tile
    ends = jnp.cumsum(padded)
    slot = (ends - padded)[experts] + rank
    n_rows = 2 * n_tok + n_experts * tile
    n_tiles = n_rows // tile
    token = jnp.repeat(jnp.arange(n_tok, dtype=jnp.int32), 2)
    row_token = jnp.zeros((n_rows,), jnp.int32).at[slot].set(token)
    tile_start = jnp.arange(n_tiles, dtype=jnp.int32) * tile
    tile_expert = jnp.minimum(jnp.sum((tile_start[:, None] >= ends[None, :]).astype(jnp.int32), axis=1),
                              n_experts - 1)
    n_active = (ends[-1] // tile).astype(jnp.int32).reshape(1)
    return tile_expert, row_token, n_active, slot.astype(jnp.int32)


MOE_TILE = 256
MOE_FF_SPLIT = 2
GATHER_UNROLL = 8


def _experts_kernel(te_ref, tok_ref, nact_ref, h_hbm, wg_ref, wu_ref, wd_ref, o_ref, xbuf, sem):
    j = pl.program_id(0)
    n_act = nact_ref[0]
    slot = j % 2
    tile = o_ref.shape[0]

    def row_copy(tok, r, s):
        return pltpu.make_async_copy(h_hbm.at[pl.ds(tok, 1), :], xbuf.at[s, pl.ds(r, 1), :], sem.at[s])

    def start_gather(t, s):
        def body(r, carry):
            row_copy(tok_ref[t * tile + r], r, s).start()
            return carry
        lax.fori_loop(0, tile, body, 0, unroll=GATHER_UNROLL)

    def wait_gather(s):
        pltpu.make_async_copy(h_hbm.at[pl.ds(0, tile), :], xbuf.at[s], sem.at[s]).wait()

    @pl.when(j < n_act)
    def _():
        @pl.when(j == 0)
        def _():
            start_gather(0, 0)

        wait_gather(slot)

        @pl.when(j + 1 < n_act)
        def _():
            start_gather(j + 1, 1 - slot)

        h = xbuf[slot].astype(BF16)
        ff = wg_ref.shape[1]
        w = ff // MOE_FF_SPLIT
        y = None
        for part in range(MOE_FF_SPLIT):
            lo = part * w
            a = _silu(_dot(h, wg_ref[:, lo:lo + w])) * _dot(h, wu_ref[:, lo:lo + w])
            yp = _dot(a.astype(BF16), wd_ref[lo:lo + w, :])
            y = yp if y is None else y + yp
        o_ref[...] = y

    @pl.when(j >= n_act)
    def _():
        o_ref[...] = jnp.zeros_like(o_ref)


def _experts(h, tile_expert, row_token, n_active, wg, wu, wd):
    n_e, d, ff = wg.shape
    tile = MOE_TILE
    n_rows = row_token.shape[0]
    assert ff % (MOE_FF_SPLIT * LANE) == 0
    resident = pl.Buffered(1)
    grid_spec = pltpu.PrefetchScalarGridSpec(
        num_scalar_prefetch=3,
        grid=(n_rows // tile,),
        in_specs=[pl.BlockSpec(memory_space=pl.ANY),
                  pl.BlockSpec((None, d, ff), lambda j, te, tok, na: (te[j], 0, 0), pipeline_mode=resident),
                  pl.BlockSpec((None, d, ff), lambda j, te, tok, na: (te[j], 0, 0), pipeline_mode=resident),
                  pl.BlockSpec((None, ff, d), lambda j, te, tok, na: (te[j], 0, 0), pipeline_mode=resident)],
        out_specs=pl.BlockSpec((tile, d), lambda j, te, tok, na: (j, 0)),
        scratch_shapes=[pltpu.VMEM((2, tile, d), F32), pltpu.SemaphoreType.DMA((2,))],
    )
    return pl.pallas_call(
        _experts_kernel,
        grid_spec=grid_spec,
        out_shape=jax.ShapeDtypeStruct((n_rows, d), F32),
        compiler_params=_params(("arbitrary",), 56),
        name="moe_experts",
    )(tile_expert, row_token, n_active, h, wg, wu, wd)


COMBINE_TILE = 256


def _combine_kernel(slot_ref, y_hbm, x_ref, route_ref, gate_ref, gf_ref, o_ref, ybuf, sem, *, final_norm):
    i = pl.program_id(0)
    n = pl.num_programs(0)
    buf = i % 2
    tile = x_ref.shape[0]

    def row_copy(row, k, r, s):
        return pltpu.make_async_copy(y_hbm.at[pl.ds(row, 1), :], ybuf.at[s, pl.ds(k * tile + r, 1), :], sem.at[s])

    def start_gather(t, s):
        def body(r, carry):
            base = 2 * (t * tile + r)
            row_copy(slot_ref[base], 0, r, s).start()
            row_copy(slot_ref[base + 1], 1, r, s).start()
            return carry
        lax.fori_loop(0, tile, body, 0, unroll=GATHER_UNROLL // 2)

    def wait_gather(s):
        pltpu.make_async_copy(y_hbm.at[pl.ds(0, 2 * tile), :], ybuf.at[s], sem.at[s]).wait()

    @pl.when(i == 0)
    def _():
        start_gather(0, 0)

    wait_gather(buf)

    @pl.when(i + 1 < n)
    def _():
        start_gather(i + 1, 1 - buf)

    route = route_ref[...]
    p1 = route[:, ROUTE_P1:ROUTE_P1 + 1]
    p2 = route[:, ROUTE_P2:ROUTE_P2 + 1]
    y = p1 * ybuf[buf, 0:tile, :] + p2 * ybuf[buf, tile:2 * tile, :]
    out = x_ref[...] + gate_ref[...] * y
    if final_norm:
        out = (out * lax.rsqrt(jnp.mean(out * out, axis=-1, keepdims=True) + NORM_EPS)) * gf_ref[...]
    o_ref[...] = out


def _combine(y_rows, slot, x, route, mods, n_tokens, cls_of_tile, final_g):
    d = x.shape[1]
    tile = COMBINE_TILE
    per = ROW_TILE // tile
    final_norm = final_g is not None
    if not final_norm:
        final_g = jnp.ones((d,), F32)
    grid_spec = pltpu.PrefetchScalarGridSpec(
        num_scalar_prefetch=1,
        grid=(n_tokens // tile,),
        in_specs=[pl.BlockSpec(memory_space=pl.ANY),
                  pl.BlockSpec((tile, d), lambda i, s: (i, 0)),
                  pl.BlockSpec((tile, LANE), lambda i, s: (i, 0)),
                  pl.BlockSpec((None, 1, d), lambda i, s: (cls_of_tile(i // per), 0, 5)),
                  pl.BlockSpec((1, d), lambda i, s: (0, 0))],
        out_specs=pl.BlockSpec((tile, d), lambda i, s: (i, 0)),
        scratch_shapes=[pltpu.VMEM((2, 2 * tile, d), F32), pltpu.SemaphoreType.DMA((2,))],
    )
    return pl.pallas_call(
        functools.partial(_combine_kernel, final_norm=final_norm),
        grid_spec=grid_spec,
        out_shape=jax.ShapeDtypeStruct((n_tokens, d), F32),
        compiler_params=_params(("arbitrary",), 48),
        name="moe_combine",
    )(slot, y_rows, x, route, mods, final_g.reshape(1, d))


def _final_kernel(x_ref, g_ref, o_ref):
    x = x_ref[...]
    o_ref[...] = (x * lax.rsqrt(jnp.mean(x * x, axis=-1, keepdims=True) + NORM_EPS)) * g_ref[...]


def _final_norm(x, g, n_tiles):
    d = x.shape[1]
    return pl.pallas_call(
        _final_kernel,
        grid=(n_tiles,),
        in_specs=[pl.BlockSpec((ROW_TILE, d), lambda i: (i, 0)),
                  pl.BlockSpec((1, d), lambda i: (0, 0))],
        out_specs=pl.BlockSpec((ROW_TILE, d), lambda i: (i, 0)),
        out_shape=jax.ShapeDtypeStruct((n_tiles * ROW_TILE, d), F32),
        compiler_params=_params(("parallel",), 40),
        name="final_norm",
    )(x, g.reshape(1, d))


def _pad_lanes(v, width=LANE):
    return jnp.pad(v, [(0, 0)] * (v.ndim - 1) + [(0, width - v.shape[-1])])


def kernel(x, c, ctx, c_ctx, w_mod, b_mod, g_mix, g_ffn, w_in, conv_w, ssd_conv_w, ssd_conv_b, ssd_a_log, ssd_dt_bias, ssd_d, ssd_norm, da_lambda, da_subln, w_out, ffn_w_gate, ffn_w_up, ffn_w_down, moe_w_router, moe_b_router, moe_w_gate, moe_w_up, moe_w_down, g_final):
    batch, seq, d = x.shape
    ctx_len = ctx.shape[1]
    depth = w_mod.shape[0]
    n_lat = batch * seq
    m = n_lat + batch * ctx_len
    assert seq % ROW_TILE == 0 and (batch * ctx_len) % ROW_TILE == 0 and n_lat % ctx_len == 0
    assert ctx_len % HALO_TILE == 0 and ctx_len % SSD_CHUNK == 0 and batch < MOD_CLASSES
    n_lat_tiles = n_lat // ROW_TILE
    n_all_tiles = m // ROW_TILE
    tiles_per_seq = seq // ROW_TILE
    cls_of_tile = lambda i: jnp.minimum(i // tiles_per_seq, batch)

    x_lat, x_ctx = x.reshape(n_lat, d), ctx.reshape(batch * ctx_len, d)
    cvec = jnp.zeros((MOD_CLASSES, d), F32).at[:batch].set(c).at[batch].set(c_ctx)
    mods_all = _mod_vectors(cvec, w_mod, b_mod)
    cos_t, sin_t = _rope_tables(seq)

    for i in range(depth):
        ctx_out = i < depth - 1
        lam_init = 0.8 - 0.6 * math.exp(-0.3 * i)
        n_tiles = n_all_tiles if ctx_out else n_lat_tiles
        mods = mods_all[i].reshape(MOD_CLASSES, 1, N_MOD * d)
        wi = w_in[i]
        w_a = jnp.concatenate([wi[:, COL_CONV:COL_Q], wi[:, COL_XBC:COL_DT]], axis=1).astype(BF16)
        w_b = jnp.concatenate([wi[:, COL_Q:COL_XBC] * (DA_SCALE * LOG2_E), wi[:, COL_K:COL_V],
                               wi[:, COL_V:COL_V + DA_WIDTH]], axis=1).astype(BF16)
        w_c = _pad_lanes(wi[:, COL_DT:COL_K]).astype(BF16)

        h = _norm_modulate(x_lat, x_ctx, g_mix[i], mods, 0, n_all_tiles, n_lat_tiles, cls_of_tile)
        p_a = _matmul(h, w_a, F32, 1024, "in_proj_conv_z_xbc")
        qkv = _matmul_rope(h, w_b, cos_t, sin_t, n_lat_tiles, tiles_per_seq)
        p_dt = _matmul(h, w_c, F32, LANE, "in_proj_dt")

        ya, xbc, dt2 = _prep(p_a, p_dt, conv_w[i], ssd_conv_w[i], ssd_conv_b[i].reshape(1, -1),
                             _pad_lanes(ssd_dt_bias[i].reshape(1, -1)), n_lat // HALO_TILE,
                             seq // HALO_TILE, ctx_len // HALO_TILE)
        y_fwd, y_bwd = _ssd_scan(xbc, dt2, ssd_a_log[i], batch, seq // SSD_CHUNK, ctx_len // SSD_CHUNK)
        yb = _ssd_gate(y_fwd, y_bwd, xbc, p_a, jnp.repeat(ssd_d[i], SSD_HEAD_DIM).reshape(1, -1),
                       ssd_norm[i].reshape(1, -1), n_tiles)

        subln = da_subln[i].reshape(-1, 1)
        v_t = _values_t(qkv[:, 2 * DA_QK:])
        yc = _attention_lat(qkv, v_t, da_lambda[i], subln, lam_init, batch, seq, ctx_len)
        yc_ctx = _attention_ctx(qkv, v_t, da_lambda[i], subln, lam_init, batch, seq, ctx_len) if ctx_out else None

        x_mid = _out_proj(ya, yb, yc, yc_ctx, w_out[i].astype(BF16), x_lat, x_ctx, mods, n_tiles, n_lat_tiles,
                          cls_of_tile)

        j = i // 2
        if i % 2 == 0:
            xs_all = _ffn_dense(x_mid, g_ffn[i], mods, ffn_w_gate[j].astype(BF16), ffn_w_up[j].astype(BF16),
                                ffn_w_down[j].astype(BF16), n_tiles, cls_of_tile)
        else:
            hh, route = _router(x_mid, g_ffn[i], mods, _pad_lanes(moe_w_router[j]),
                                _pad_lanes(moe_b_router[j].reshape(1, -1)), n_tiles, cls_of_tile)
            tile_expert, row_token, n_active, slot = _routing_tables(route, moe_w_gate.shape[1], MOE_TILE)
            y_rows = _experts(hh, tile_expert, row_token, n_active, moe_w_gate[j].astype(BF16),
                              moe_w_up[j].astype(BF16), moe_w_down[j].astype(BF16))
            xs_all = _combine(y_rows, slot, x_mid, route, mods, n_tiles * ROW_TILE, cls_of_tile,
                              None if ctx_out else g_final)
        x_lat = xs_all
        x_ctx = xs_all[n_lat:] if ctx_out else None

    last_is_moe = depth % 2 == 0
    out = x_lat if last_is_moe else _final_norm(x_lat, g_final, n_lat_tiles)
    return out.reshape(batch, seq, d)
```

```python
import functools
import math

import jax
import jax.numpy as jnp
from jax import lax
from jax.experimental import pallas as pl
from jax.experimental.pallas import tpu as pltpu

NORM_EPS = 1e-6
N_MOD = 6
GRID_W = 64

SHORT_CONV = 3
CONV_W = 512

SSD_HEADS = 8
SSD_HEAD_DIM = 64
SSD_INNER = SSD_HEADS * SSD_HEAD_DIM
SSD_STATE = 128
SSD_GROUPS = 2
SSD_CHUNK = 128
SSD_XBC = SSD_INNER + 2 * SSD_GROUPS * SSD_STATE
SSD_GROUP_W = SSD_INNER // SSD_GROUPS

DA_HEADS = 8
DA_HEAD_DIM = 64
DA_V_DIM = 2 * DA_HEAD_DIM
DA_QK = DA_HEADS * 2 * DA_HEAD_DIM
DA_WIDTH = DA_HEADS * DA_V_DIM
DA_SCALE = DA_HEAD_DIM ** -0.5
LOG2_E = math.log2(math.e)
ROPE_THETA = 10000.0

COL_CONV = 0
COL_Z = COL_CONV + 3 * CONV_W
COL_Q = COL_Z + SSD_INNER
COL_XBC = COL_Q + DA_QK
COL_DT = COL_XBC + SSD_XBC
COL_K = COL_DT + 2 * SSD_HEADS
COL_V = COL_K + DA_QK

N_EXPERTS = 8

LANE = 128
SUBLANE = 8
ROW_TILE = 512
HALO_TILE = 256
MOD_CLASSES = 8

F32 = jnp.float32
BF16 = jnp.bfloat16
HIGHEST = lax.Precision.HIGHEST
MIB = 1024 * 1024


def _params(semantics, vmem_mib=None):
    kw = {"dimension_semantics": semantics}
    if vmem_mib is not None:
        kw["vmem_limit_bytes"] = vmem_mib * MIB
    return pltpu.CompilerParams(**kw)


def _silu(v):
    return v * jax.nn.sigmoid(v)


def _dot(a, b):
    return jnp.dot(a, b, preferred_element_type=F32)


def _dot_nt(a, b):
    return lax.dot_general(a, b, (((1,), (1,)), ((), ())), preferred_element_type=F32)


def _dot_tn(a, b):
    return lax.dot_general(a, b, (((0,), (0,)), ((), ())), preferred_element_type=F32)


def _split2(v):
    hi = v.astype(BF16)
    return hi, (v - hi.astype(F32)).astype(BF16)


def _dot_3pass(a, b):
    a_hi, a_lo = _split2(a)
    b_hi, b_lo = _split2(b)
    return _dot(a_hi, b_hi) + (_dot(a_hi, b_lo) + _dot(a_lo, b_hi))


def _modulated_norm(x, g, shift, scale):
    ms = jnp.mean(x * x, axis=-1, keepdims=True)
    y = x * lax.rsqrt(ms + NORM_EPS)
    return (y * g) * (1.0 + scale) + shift


def _mod_spec(k, width, cls_of_tile):
    return pl.BlockSpec((None, 1, width), lambda i, *_: (cls_of_tile(i), 0, k))


def _mod_kernel(c_ref, w_ref, b_ref, o_ref):
    s = _silu(c_ref[...])
    o_ref[...] = _dot_3pass(s, w_ref[...]) + b_ref[...]


def _mod_vectors(cvec, w_mod, b_mod):
    depth, d, n = w_mod.shape
    tn = 1024
    return pl.pallas_call(
        _mod_kernel,
        grid=(depth, n // tn),
        in_specs=[pl.BlockSpec((MOD_CLASSES, d), lambda l, j: (0, 0)),
                  pl.BlockSpec((None, d, tn), lambda l, j: (l, 0, j)),
                  pl.BlockSpec((None, 1, tn), lambda l, j: (l, 0, j))],
        out_specs=pl.BlockSpec((None, MOD_CLASSES, tn), lambda l, j: (l, 0, j)),
        out_shape=jax.ShapeDtypeStruct((depth, MOD_CLASSES, n), F32),
        compiler_params=_params(("parallel", "parallel"), 40),
        name="mod_vectors",
    )(cvec, w_mod, b_mod.reshape(depth, 1, n))


def _lat_ctx_specs(width, n_lat_tiles):
    return [pl.BlockSpec((ROW_TILE, width), lambda i, *_: (jnp.minimum(i, n_lat_tiles - 1), 0)),
            pl.BlockSpec((ROW_TILE, width), lambda i, *_: (jnp.maximum(i - n_lat_tiles, 0), 0))]


def _lat_or_ctx(lat_ref, ctx_ref, n_lat_tiles):
    return jnp.where(pl.program_id(0) < n_lat_tiles, lat_ref[...], ctx_ref[...])


def _norm_kernel(xl_ref, xc_ref, g_ref, sh_ref, sc_ref, h_ref, *, n_lat_tiles):
    x = _lat_or_ctx(xl_ref, xc_ref, n_lat_tiles)
    h_ref[...] = _modulated_norm(x, g_ref[...], sh_ref[...], sc_ref[...]).astype(h_ref.dtype)


def _norm_modulate(x_lat, x_ctx, g, mods, k_shift, n_tiles, n_lat_tiles, cls_of_tile):
    d = x_lat.shape[1]
    return pl.pallas_call(
        functools.partial(_norm_kernel, n_lat_tiles=n_lat_tiles),
        grid=(n_tiles,),
        in_specs=_lat_ctx_specs(d, n_lat_tiles) + [
            pl.BlockSpec((1, d), lambda i: (0, 0)),
            _mod_spec(k_shift, d, cls_of_tile),
            _mod_spec(k_shift + 1, d, cls_of_tile)],
        out_specs=pl.BlockSpec((ROW_TILE, d), lambda i: (i, 0)),
        out_shape=jax.ShapeDtypeStruct((n_tiles * ROW_TILE, d), BF16),
        compiler_params=_params(("parallel",), 40),
        name="norm_modulate",
    )(x_lat, x_ctx, g.reshape(1, d), mods, mods)


def _mm_kernel(x_ref, w_ref, o_ref):
    o_ref[...] = _dot(x_ref[...], w_ref[...]).astype(o_ref.dtype)


def _matmul(x, w, out_dtype, tn, name):
    m, k = x.shape
    n = w.shape[1]
    return pl.pallas_call(
        _mm_kernel,
        grid=(m // ROW_TILE, n // tn),
        in_specs=[pl.BlockSpec((ROW_TILE, k), lambda i, j: (i, 0)),
                  pl.BlockSpec((k, tn), lambda i, j: (0, j))],
        out_specs=pl.BlockSpec((ROW_TILE, tn), lambda i, j: (i, j)),
        out_shape=jax.ShapeDtypeStruct((m, n), out_dtype),
        compiler_params=_params(("parallel", "parallel"), 40),
        name=name,
    )(x, w)


def _mm_rope_kernel(x_ref, w_ref, cos_ref, sin_ref, qk_ref, vt_ref, *, n_lat_tiles, n_rope_tiles):
    i = pl.program_id(0)
    j = pl.program_id(1)
    acc = _dot(x_ref[...], w_ref[...])
    tm, tn = acc.shape
    is_qk = j < n_rope_tiles
    rotate = jnp.logical_and(i < n_lat_tiles, is_qk)

    @pl.when(rotate)
    def _():
        lane = lax.broadcasted_iota(jnp.int32, (tm, LANE), 1)
        first = (lane % (DA_HEAD_DIM // 2)) < (DA_HEAD_DIM // 4)
        c = cos_ref[...]
        s = sin_ref[...]
        quarter = DA_HEAD_DIM // 4
        for hb in range(tn // LANE):
            blk = acc[:, hb * LANE:(hb + 1) * LANE]
            partner = jnp.where(first, pltpu.roll(blk, LANE - quarter, 1), pltpu.roll(blk, quarter, 1))
            qk_ref[:, hb * LANE:(hb + 1) * LANE] = (blk * c + partner * s).astype(qk_ref.dtype)

    @pl.when(jnp.logical_and(is_qk, jnp.logical_not(rotate)))
    def _():
        qk_ref[...] = acc.astype(qk_ref.dtype)

    @pl.when(jnp.logical_not(is_qk))
    def _():
        row = lax.broadcasted_iota(jnp.int32, (VT_ROWS - DA_V_DIM, tm), 0)
        tail = jnp.where(row == 0, 1.0, 0.0).astype(vt_ref.dtype)
        for h in range(DA_HEADS):
            lo = h * VT_ROWS
            vt_ref[lo:lo + DA_V_DIM, :] = acc[:, h * DA_V_DIM:(h + 1) * DA_V_DIM].T.astype(vt_ref.dtype)
            vt_ref[lo + DA_V_DIM:lo + VT_ROWS, :] = tail


def _matmul_rope(x, w, cos_t, sin_t, n_lat_tiles, tiles_per_seq):
    m, k = x.shape
    tn = DA_QK
    n_rope_tiles = 2 * DA_QK // tn
    assert w.shape[1] == 2 * DA_QK + DA_WIDTH and tn == DA_WIDTH
    kern = functools.partial(_mm_rope_kernel, n_lat_tiles=n_lat_tiles, n_rope_tiles=n_rope_tiles)
    return pl.pallas_call(
        kern,
        grid=(m // ROW_TILE, n_rope_tiles + 1),
        in_specs=[pl.BlockSpec((ROW_TILE, k), lambda i, j: (i, 0)),
                  pl.BlockSpec((k, tn), lambda i, j: (0, j)),
                  pl.BlockSpec((ROW_TILE, LANE), lambda i, j: (i % tiles_per_seq, 0)),
                  pl.BlockSpec((ROW_TILE, LANE), lambda i, j: (i % tiles_per_seq, 0))],
        out_specs=[pl.BlockSpec((ROW_TILE, tn), lambda i, j: (i, jnp.minimum(j, n_rope_tiles - 1))),
                   pl.BlockSpec((DA_HEADS * VT_ROWS, ROW_TILE), lambda i, j: (0, i))],
        out_shape=[jax.ShapeDtypeStruct((m, 2 * DA_QK), BF16),
                   jax.ShapeDtypeStruct((DA_HEADS * VT_ROWS, m), BF16)],
        compiler_params=_params(("parallel", "arbitrary"), 40),
        name="qkv_proj_rope",
    )(x, w, cos_t, sin_t)


def _rope_tables(n_tokens):
    rows = n_tokens // GRID_W
    row = jnp.repeat(jnp.arange(rows, dtype=F32), GRID_W)
    col = jnp.broadcast_to(jnp.arange(GRID_W, dtype=F32), (rows, GRID_W)).reshape(-1)
    n_freq = DA_HEAD_DIM // 4
    inv = ROPE_THETA ** (-jnp.arange(n_freq, dtype=F32) / n_freq)
    ang_r = row[:, None] * inv
    ang_c = col[:, None] * inv
    cos64 = jnp.concatenate([jnp.cos(ang_r), jnp.cos(ang_r), jnp.cos(ang_c), jnp.cos(ang_c)], axis=-1)
    sin64 = jnp.concatenate([-jnp.sin(ang_r), jnp.sin(ang_r), -jnp.sin(ang_c), jnp.sin(ang_c)], axis=-1)
    return jnp.tile(cos64, (1, 2)), jnp.tile(sin64, (1, 2))


def _conv3(u, prev_row, next_row, w_ref, has_prev, has_next):
    tm = u.shape[0]
    row = lax.broadcasted_iota(jnp.int32, u.shape, 0)
    prev_row = jnp.where(has_prev, prev_row, 0.0)
    next_row = jnp.where(has_next, next_row, 0.0)
    before = jnp.where(row == 0, prev_row, pltpu.roll(u, 1, 0))
    after = jnp.where(row == tm - 1, next_row, pltpu.roll(u, tm - 1, 0))
    return before * w_ref[0:1, :] + u * w_ref[1:2, :] + after * w_ref[2:3, :]


def _prep_kernel(cv_ref, cvp_ref, cvn_ref, xb_ref, xbp_ref, xbn_ref, dtr_ref,
                 cw_ref, sw_ref, sb_ref, dtb_ref, ya_ref, xs_ref, dt_ref,
                 *, n_lat_tiles, lat_tiles_per_seq, ctx_tiles_per_seq):
    i = pl.program_id(0)
    is_lat = i < n_lat_tiles
    pos = jnp.where(is_lat, i % lat_tiles_per_seq, (i - n_lat_tiles) % ctx_tiles_per_seq)
    last = jnp.where(is_lat, lat_tiles_per_seq - 1, ctx_tiles_per_seq - 1)
    has_prev = pos != 0
    has_next = pos != last

    w = CONV_W
    cv = cv_ref[...]
    gate_b, u = cv[:, 0:w], cv[:, w:2 * w] * cv[:, 2 * w:3 * w]
    p = cvp_ref[SUBLANE - 1:SUBLANE, :]
    n = cvn_ref[0:1, :]
    u_prev = p[:, w:2 * w] * p[:, 2 * w:3 * w]
    u_next = n[:, w:2 * w] * n[:, 2 * w:3 * w]
    ya_ref[...] = (gate_b * _conv3(u, u_prev, u_next, cw_ref, has_prev, has_next)).astype(ya_ref.dtype)

    xc = _conv3(xb_ref[...], xbp_ref[SUBLANE - 1:SUBLANE, :], xbn_ref[0:1, :], sw_ref, has_prev, has_next)
    xs_ref[...] = _silu(xc + sb_ref[...])

    t = dtr_ref[...] + dtb_ref[...]
    dt = jnp.maximum(t, 0.0) + jnp.log1p(jnp.exp(-jnp.abs(t)))
    dt_ref[0] = dt
    dt_ref[1] = pltpu.roll(dt, LANE - SSD_HEADS, 1)


def _prep(p_a, p_dt, conv_w, ssd_conv_w, ssd_conv_b, dt_bias, n_lat_tiles, lat_tiles_per_seq,
          ctx_tiles_per_seq):
    m = p_a.shape[0]
    tm = HALO_TILE
    per = tm // SUBLANE
    n8 = m // SUBLANE
    cw = 3 * CONV_W
    xbc_blk = COL_Q // SSD_XBC
    prev = lambda i: jnp.maximum(i * per - 1, 0)
    nxt = lambda i: jnp.minimum((i + 1) * per, n8 - 1)
    kern = functools.partial(_prep_kernel, n_lat_tiles=n_lat_tiles, lat_tiles_per_seq=lat_tiles_per_seq,
                             ctx_tiles_per_seq=ctx_tiles_per_seq)
    return pl.pallas_call(
        kern,
        grid=(m // tm,),
        in_specs=[pl.BlockSpec((tm, cw), lambda i: (i, 0)),
                  pl.BlockSpec((SUBLANE, cw), lambda i: (prev(i), 0)),
                  pl.BlockSpec((SUBLANE, cw), lambda i: (nxt(i), 0)),
                  pl.BlockSpec((tm, SSD_XBC), lambda i: (i, xbc_blk)),
                  pl.BlockSpec((SUBLANE, SSD_XBC), lambda i: (prev(i), xbc_blk)),
                  pl.BlockSpec((SUBLANE, SSD_XBC), lambda i: (nxt(i), xbc_blk)),
                  pl.BlockSpec((tm, LANE), lambda i: (i, 0)),
                  pl.BlockSpec((SHORT_CONV, CONV_W), lambda i: (0, 0)),
                  pl.BlockSpec((SHORT_CONV, SSD_XBC), lambda i: (0, 0)),
                  pl.BlockSpec((1, SSD_XBC), lambda i: (0, 0)),
                  pl.BlockSpec((1, LANE), lambda i: (0, 0))],
        out_specs=[pl.BlockSpec((tm, CONV_W), lambda i: (i, 0)),
                   pl.BlockSpec((tm, SSD_XBC), lambda i: (i, 0)),
                   pl.BlockSpec((2, tm, LANE), lambda i: (0, i, 0))],
        out_shape=[jax.ShapeDtypeStruct((m, CONV_W), BF16),
                   jax.ShapeDtypeStruct((m, SSD_XBC), F32),
                   jax.ShapeDtypeStruct((2, m, LANE), F32)],
        compiler_params=_params(("parallel",), 40),
        name="conv_prep",
    )(p_a, p_a, p_a, p_a, p_a, p_a, p_dt, conv_w, ssd_conv_w, ssd_conv_b, dt_bias)


def _split3(v):
    hi = v.astype(BF16)
    rest = v - hi.astype(F32)
    mid = rest.astype(BF16)
    return hi, mid, (rest - mid.astype(F32)).astype(BF16)


def _dot_f32_lhs(a, b01):
    return sum(_dot(piece, b01) for piece in _split3(a))


def _dot_f32_rhs(a01, b):
    return sum(_dot(a01, piece) for piece in _split3(b))


def _ssd_chunk(direction, xs_ref, dt_ref, alog_ref, alog_ch_ref, y_ref, state_ref):
    t = SSD_CHUNK
    r = lax.broadcasted_iota(jnp.int32, (t, t), 0)
    c = lax.broadcasted_iota(jnp.int32, (t, t), 1)
    mask = (r >= c) if direction == 0 else (r <= c)
    tri = mask.astype(BF16)

    er = lax.broadcasted_iota(jnp.int32, (LANE, SSD_INNER), 0)
    ec = lax.broadcasted_iota(jnp.int32, (LANE, SSD_INNER), 1)
    expand = (jnp.right_shift(ec, 6) == er).astype(BF16)

    dt = dt_ref[direction]
    cs = _dot_f32_rhs(tri, dt * (-jnp.exp(alog_ref[direction])))
    cs_t = cs.T
    dt_ch = _dot_f32_lhs(dt, expand)
    da_ch = dt_ch * (-jnp.exp(alog_ch_ref[direction]))
    cs_ch = _dot_f32_rhs(tri, da_ch)
    tot_ch = jnp.sum(da_ch, axis=0, keepdims=True)

    xdt = xs_ref[:, 0:SSD_INNER] * dt_ch
    x_state = xdt * jnp.exp(tot_ch - cs_ch)
    y_scale = jnp.exp(cs_ch)
    carry = jnp.exp(tot_ch)
    gw = SSD_GROUP_W
    heads_per_group = SSD_HEADS // SSD_GROUPS
    lane_head = jnp.right_shift(lax.broadcasted_iota(jnp.int32, (t, gw), 1), 6)

    for g in range(SSD_GROUPS):
        b_lo = SSD_INNER + g * SSD_STATE
        c_lo = SSD_INNER + SSD_GROUPS * SSD_STATE + g * SSD_STATE
        bg = xs_ref[:, b_lo:b_lo + SSD_STATE].astype(BF16)
        cg = xs_ref[:, c_lo:c_lo + SSD_STATE].astype(BF16)
        cb = _dot_nt(cg, bg)
        sg = state_ref[direction, g]
        xdt_g = xdt[:, g * gw:(g + 1) * gw]
        y = _dot(cg, sg.astype(BF16)) * y_scale[:, g * gw:(g + 1) * gw]
        for hh in range(heads_per_group):
            h = g * heads_per_group + hh
            decay = jnp.where(mask, jnp.exp(cs[:, h:h + 1] - cs_t[h:h + 1, :]), 0.0)
            x_h = jnp.where(lane_head == hh, xdt_g, 0.0).astype(BF16)
            y = y + _dot((cb * decay).astype(BF16), x_h)
        y_ref[:, g * gw:(g + 1) * gw] = y
        ds = _dot_tn(bg, x_state[:, g * gw:(g + 1) * gw].astype(BF16))
        state_ref[direction, g] = sg * carry[:, g * gw:(g + 1) * gw] + ds


def _ssd_kernel(xf_ref, dtf_ref, xb_ref, dtb_ref, alog_ref, alog_ch_ref, yf_ref, yb_ref, state_ref):
    @pl.when(pl.program_id(1) == 0)
    def _():
        state_ref[...] = jnp.zeros_like(state_ref)

    _ssd_chunk(0, xf_ref, dtf_ref, alog_ref, alog_ch_ref, yf_ref, state_ref)
    _ssd_chunk(1, xb_ref, dtb_ref, alog_ref, alog_ch_ref, yb_ref, state_ref)


def _ssd_scan(xs, dt2, a_log, batch, n_lat_chunks, n_ctx_chunks):
    m = xs.shape[0]
    t = SSD_CHUNK
    assert SSD_HEAD_DIM == 64 and SSD_HEADS <= LANE
    a_log_ch = jnp.repeat(a_log, SSD_HEAD_DIM, axis=1).reshape(2, 1, SSD_INNER)
    a_log = _pad_lanes(a_log).reshape(2, 1, LANE)

    def chunk(b, d, s):
        j_ctx = s if d == 0 else n_ctx_chunks - 1 - s
        sl = s - n_ctx_chunks
        j_lat = sl if d == 0 else n_lat_chunks - 1 - sl
        return jnp.where(s < n_ctx_chunks, batch * n_lat_chunks + b * n_ctx_chunks + j_ctx,
                         b * n_lat_chunks + j_lat)

    x_spec = lambda d: pl.BlockSpec((t, SSD_XBC), lambda b, s: (chunk(b, d, s), 0))
    dt_spec = lambda d: pl.BlockSpec((2, t, LANE), lambda b, s: (0, chunk(b, d, s), 0))
    y_spec = lambda d: pl.BlockSpec((t, SSD_INNER), lambda b, s: (chunk(b, d, s), 0))
    return pl.pallas_call(
        _ssd_kernel,
        grid=(batch, n_ctx_chunks + n_lat_chunks),
        in_specs=[x_spec(0), dt_spec(0), x_spec(1), dt_spec(1),
                  pl.BlockSpec((2, 1, LANE), lambda b, s: (0, 0, 0)),
                  pl.BlockSpec((2, 1, SSD_INNER), lambda b, s: (0, 0, 0))],
        out_specs=[y_spec(0), y_spec(1)],
        out_shape=[jax.ShapeDtypeStruct((m, SSD_INNER), F32), jax.ShapeDtypeStruct((m, SSD_INNER), F32)],
        scratch_shapes=[pltpu.VMEM((2, SSD_GROUPS, SSD_STATE, SSD_GROUP_W), F32)],
        compiler_params=_params(("parallel", "arbitrary")),
        name="ssd_scan",
    )(xs, dt2, xs, dt2, a_log, a_log_ch)


def _ssd_gate_kernel(yf_ref, yb_ref, xs_ref, z_ref, d_ref, g_ref, o_ref):
    yl = d_ref[...] * xs_ref[...] + yf_ref[...] + yb_ref[...]
    u = yl * _silu(z_ref[...])
    gw = SSD_GROUP_W
    for g in range(SSD_GROUPS):
        ug = u[:, g * gw:(g + 1) * gw]
        ug = ug * lax.rsqrt(jnp.mean(ug * ug, axis=-1, keepdims=True) + NORM_EPS)
        o_ref[:, g * gw:(g + 1) * gw] = (ug * g_ref[:, g * gw:(g + 1) * gw]).astype(o_ref.dtype)


def _ssd_gate(y_fwd, y_bwd, xs, p_a, d_row, norm_g, n_tiles):
    w = SSD_INNER
    return pl.pallas_call(
        _ssd_gate_kernel,
        grid=(n_tiles,),
        in_specs=[pl.BlockSpec((ROW_TILE, w), lambda i: (i, 0)),
                  pl.BlockSpec((ROW_TILE, w), lambda i: (i, 0)),
                  pl.BlockSpec((ROW_TILE, w), lambda i: (i, 0)),
                  pl.BlockSpec((ROW_TILE, w), lambda i: (i, COL_Z // w)),
                  pl.BlockSpec((1, w), lambda i: (0, 0)),
                  pl.BlockSpec((1, w), lambda i: (0, 0))],
        out_specs=pl.BlockSpec((ROW_TILE, w), lambda i: (i, 0)),
        out_shape=jax.ShapeDtypeStruct((n_tiles * ROW_TILE, w), BF16),
        compiler_params=_params(("parallel",)),
        name="ssd_gate_norm",
    )(y_fwd, y_bwd, xs, p_a, d_row, norm_g)


ATT_TQ = 512
ATT_TKC = 1024


def _attn_kernel(*refs, has_lat, lam_init):
    if has_lat:
        (q_ref, kc_ref, vct_ref, kl_ref, vlt_ref, lam_ref, g_ref, o_ref,
         m_ref, acc_ref, sa_ref, sb_ref) = refs
    else:
        q_ref, kc_ref, vct_ref, lam_ref, g_ref, o_ref, m_ref, acc_ref, sa_ref, sb_ref = refs

    q = q_ref[...]
    lane = lax.broadcasted_iota(jnp.int32, q.shape, 1)
    zero = jnp.zeros_like(q)
    q_maps = (jnp.where(lane < DA_HEAD_DIM, q, zero), jnp.where(lane >= DA_HEAD_DIM, q, zero))

    def scores(k, s_ref):
        n = k.shape[0]
        maxima = []
        for mp in range(2):
            s = _dot_nt(k, q_maps[mp])
            s_ref[mp, 0:n, :] = s
            maxima.append(jnp.max(s, axis=0, keepdims=True))
        return tuple(maxima)

    def update(s_ref, maxima, vt):
        n = vt.shape[1]
        for mp in range(2):
            m_old = m_ref[mp]
            m_new = jnp.maximum(m_old, maxima[mp])
            alpha = jnp.exp2(m_old - m_new)
            p = jnp.exp2(s_ref[mp, 0:n, :] - m_new[0:1, :])
            acc_ref[mp] = alpha[0:1, :] * acc_ref[mp] + _dot(vt, p.astype(vt.dtype))
            m_ref[mp] = m_new

    m_ref[...] = jnp.full(m_ref.shape, -jnp.inf, F32)
    acc_ref[...] = jnp.zeros_like(acc_ref)
    mx = scores(kc_ref[...], sa_ref)

    if not has_lat:
        update(sa_ref, mx, vct_ref[...])
    else:
        tkc = min(ATT_TKC, kl_ref.shape[0])
        n_chunks = kl_ref.shape[0] // tkc
        k_at = lambda c: kl_ref[pl.ds(pl.multiple_of(c * tkc, tkc), tkc), :]
        vt_at = lambda c: vlt_ref[:, pl.ds(pl.multiple_of(c * tkc, tkc), tkc)]

        mx_ctx = mx
        mx = scores(k_at(0), sb_ref)
        update(sa_ref, mx_ctx, vct_ref[...])

        def pair(c2, mx_b):
            c = 1 + 2 * c2
            mx_a = scores(k_at(c), sa_ref)
            update(sb_ref, mx_b, vt_at(c - 1))
            mx_b = scores(k_at(c + 1), sb_ref)
            update(sa_ref, mx_a, vt_at(c))
            return mx_b

        mx = lax.fori_loop(0, (n_chunks - 1) // 2, pair, mx)
        if (n_chunks - 1) % 2:
            mx_a = scores(k_at(n_chunks - 1), sa_ref)
            update(sb_ref, mx, vt_at(n_chunks - 2))
            update(sa_ref, mx_a, vt_at(n_chunks - 1))
        else:
            update(sb_ref, mx, vt_at(n_chunks - 1))

    lv = lam_ref[...]
    dotp = lambda a, b: jnp.sum(lv[a:a + 1, :] * lv[b:b + 1, :], axis=-1, keepdims=True)
    lam = jnp.exp(dotp(0, 1)) - jnp.exp(dotp(2, 3)) + lam_init
    vd = DA_V_DIM
    inv_l = 1.0 / acc_ref[:, vd:vd + 1, :]
    o = acc_ref[0, 0:vd, :] * inv_l[0] - lam * (acc_ref[1, 0:vd, :] * inv_l[1])
    y = o * lax.rsqrt(jnp.mean(o * o, axis=0, keepdims=True) + NORM_EPS)
    o_ref[...] = ((y * g_ref[...]) * (1.0 - lam_init)).T.astype(o_ref.dtype)


VT_ROWS = DA_V_DIM + 16


def _attn_scratch(tq, keys_a, keys_b):
    return [pltpu.VMEM((2, SUBLANE, tq), F32), pltpu.VMEM((2, VT_ROWS, tq), F32),
            pltpu.VMEM((2, keys_a, tq), F32), pltpu.VMEM((2, keys_b, tq), F32)]


def _attention_lat(qkv, v_t, da_lambda, subln, lam_init, batch, seq, ctx_len):
    h = DA_HEADS
    tq = ATT_TQ
    nq = seq // tq
    ctx_blk0 = (batch * seq) // ctx_len
    assert seq % tq == 0 and seq % min(ATT_TKC, seq) == 0
    kern = functools.partial(_attn_kernel, has_lat=True, lam_init=lam_init)
    return pl.pallas_call(
        kern,
        grid=(batch, h, nq),
        in_specs=[pl.BlockSpec((tq, LANE), lambda b, hh, i: (b * nq + i, hh)),
                  pl.BlockSpec((ctx_len, LANE), lambda b, hh, i: (ctx_blk0 + b, h + hh)),
                  pl.BlockSpec((VT_ROWS, ctx_len), lambda b, hh, i: (hh, ctx_blk0 + b)),
                  pl.BlockSpec((seq, LANE), lambda b, hh, i: (b, h + hh)),
                  pl.BlockSpec((VT_ROWS, seq), lambda b, hh, i: (hh, b)),
                  pl.BlockSpec(da_lambda.shape, lambda b, hh, i: (0, 0)),
                  pl.BlockSpec((DA_V_DIM, 1), lambda b, hh, i: (0, 0))],
        out_specs=pl.BlockSpec((tq, LANE), lambda b, hh, i: (b * nq + i, hh)),
        out_shape=jax.ShapeDtypeStruct((batch * seq, DA_WIDTH), BF16),
        scratch_shapes=_attn_scratch(tq, max(min(ATT_TKC, seq), ctx_len), min(ATT_TKC, seq)),
        compiler_params=_params(("parallel", "parallel", "parallel"), 48),
        name="diff_attention",
    )(qkv, qkv, v_t, qkv, v_t, da_lambda, subln)


def _attention_ctx(qkv, v_t, da_lambda, subln, lam_init, batch, seq, ctx_len):
    h = DA_HEADS
    ctx_blk0 = (batch * seq) // ctx_len
    kern = functools.partial(_attn_kernel, has_lat=False, lam_init=lam_init)
    return pl.pallas_call(
        kern,
        grid=(batch, h),
        in_specs=[pl.BlockSpec((ctx_len, LANE), lambda b, hh: (ctx_blk0 + b, hh)),
                  pl.BlockSpec((ctx_len, LANE), lambda b, hh: (ctx_blk0 + b, h + hh)),
                  pl.BlockSpec((VT_ROWS, ctx_len), lambda b, hh: (hh, ctx_blk0 + b)),
                  pl.BlockSpec(da_lambda.shape, lambda b, hh: (0, 0)),
                  pl.BlockSpec((DA_V_DIM, 1), lambda b, hh: (0, 0))],
        out_specs=pl.BlockSpec((ctx_len, LANE), lambda b, hh: (b, hh)),
        out_shape=jax.ShapeDtypeStruct((batch * ctx_len, DA_WIDTH), BF16),
        scratch_shapes=_attn_scratch(ctx_len, ctx_len, SUBLANE),
        compiler_params=_params(("parallel", "parallel")),
        name="diff_attention_ctx",
    )(qkv, qkv, v_t, da_lambda, subln)


def _outproj_kernel(ya_ref, yb_ref, yc_ref, ycx_ref, w0_ref, w1_ref, w2_ref, xl_ref, xc_ref, gate_ref, o_ref, *,
                    n_lat_tiles):
    yc = _lat_or_ctx(yc_ref, ycx_ref, n_lat_tiles)
    acc = _dot(ya_ref[...], w0_ref[...]) + _dot(yb_ref[...], w1_ref[...]) + _dot(yc, w2_ref[...])
    o_ref[...] = _lat_or_ctx(xl_ref, xc_ref, n_lat_tiles) + gate_ref[...] * acc


def _out_proj(ya, yb, yc, yc_ctx, w_out, x_lat, x_ctx, mods, n_tiles, n_lat_tiles, cls_of_tile):
    d = x_lat.shape[1]
    tn = 1024
    wa, wb = ya.shape[1], yb.shape[1]
    assert wa == wb and yc.shape[1] == wa + wb
    if yc_ctx is None:
        assert n_tiles <= n_lat_tiles
        yc_ctx = yc
    lat_tile = lambda i: jnp.minimum(i, n_lat_tiles - 1)
    ctx_tile = lambda i: jnp.maximum(i - n_lat_tiles, 0)
    return pl.pallas_call(
        functools.partial(_outproj_kernel, n_lat_tiles=n_lat_tiles),
        grid=(n_tiles, d // tn),
        in_specs=[pl.BlockSpec((ROW_TILE, wa), lambda i, j: (i, 0)),
                  pl.BlockSpec((ROW_TILE, wb), lambda i, j: (i, 0))] + _lat_ctx_specs(wa + wb, n_lat_tiles) + [
                  pl.BlockSpec((wa, tn), lambda i, j: (0, j)),
                  pl.BlockSpec((wb, tn), lambda i, j: (1, j)),
                  pl.BlockSpec((wa + wb, tn), lambda i, j: (1, j)),
                  pl.BlockSpec((ROW_TILE, tn), lambda i, j: (lat_tile(i), j)),
                  pl.BlockSpec((ROW_TILE, tn), lambda i, j: (ctx_tile(i), j)),
                  pl.BlockSpec((None, 1, tn), lambda i, j: (cls_of_tile(i), 0, 2 * (d // tn) + j))],
        out_specs=pl.BlockSpec((ROW_TILE, tn), lambda i, j: (i, j)),
        out_shape=jax.ShapeDtypeStruct((n_tiles * ROW_TILE, d), F32),
        compiler_params=_params(("parallel", "parallel"), 40),
        name="out_proj_residual",
    )(ya, yb, yc, yc_ctx, w_out, w_out, w_out, x_lat, x_ctx, mods)


def _ffn_kernel(x_ref, g_ref, sh_ref, sc_ref, gate_ref, wg_ref, wu_ref, wd_ref, o_ref, h_ref, acc_ref):
    f = pl.program_id(1)

    @pl.when(f == 0)
    def _():
        h_ref[...] = _modulated_norm(x_ref[...], g_ref[...], sh_ref[...], sc_ref[...]).astype(h_ref.dtype)
        acc_ref[...] = jnp.zeros_like(acc_ref)

    h = h_ref[...]
    a = _silu(_dot(h, wg_ref[...])) * _dot(h, wu_ref[...])
    acc_ref[...] += _dot(a.astype(BF16), wd_ref[...])

    @pl.when(f == pl.num_programs(1) - 1)
    def _():
        o_ref[...] = x_ref[...] + gate_ref[...] * acc_ref[...]


def _ffn_dense(x, g, mods, wg, wu, wd, n_tiles, cls_of_tile):
    m, d = x.shape
    ff = wg.shape[1]
    tf = 512
    return pl.pallas_call(
        _ffn_kernel,
        grid=(n_tiles, ff // tf),
        in_specs=[pl.BlockSpec((ROW_TILE, d), lambda i, f: (i, 0)),
                  pl.BlockSpec((1, d), lambda i, f: (0, 0)),
                  _mod_spec(3, d, cls_of_tile), _mod_spec(4, d, cls_of_tile), _mod_spec(5, d, cls_of_tile),
                  pl.BlockSpec((d, tf), lambda i, f: (0, f)),
                  pl.BlockSpec((d, tf), lambda i, f: (0, f)),
                  pl.BlockSpec((tf, d), lambda i, f: (f, 0))],
        out_specs=pl.BlockSpec((ROW_TILE, d), lambda i, f: (i, 0)),
        out_shape=jax.ShapeDtypeStruct((n_tiles * ROW_TILE, d), F32),
        scratch_shapes=[pltpu.VMEM((ROW_TILE, d), BF16), pltpu.VMEM((ROW_TILE, d), F32)],
        compiler_params=_params(("parallel", "arbitrary"), 48),
        name="ffn_dense",
    )(x, g.reshape(1, d), mods, mods, mods, wg, wu, wd)


ROUTE_I1, ROUTE_I2, ROUTE_P1, ROUTE_P2 = 0, 1, 2, 3


def _router_kernel(x_ref, g_ref, sh_ref, sc_ref, wr_ref, br_ref, h_ref, route_ref):
    h = _modulated_norm(x_ref[...], g_ref[...], sh_ref[...], sc_ref[...])
    h_ref[...] = h
    logits = _dot_3pass(h, wr_ref[...]) + br_ref[...]
    lane = lax.broadcasted_iota(jnp.int32, logits.shape, 1).astype(F32)
    neg = -jnp.inf
    lg = jnp.where(lane < N_EXPERTS, logits, neg)
    v1 = jnp.max(lg, axis=-1, keepdims=True)
    i1 = jnp.min(jnp.where(lg == v1, lane, float(LANE)), axis=-1, keepdims=True)
    lg2 = jnp.where(lane == i1, neg, lg)
    v2 = jnp.max(lg2, axis=-1, keepdims=True)
    i2 = jnp.min(jnp.where(lg2 == v2, lane, float(LANE)), axis=-1, keepdims=True)
    e = jnp.exp(v2 - v1)
    p1 = 1.0 / (1.0 + e)
    p2 = e / (1.0 + e)
    rec = jnp.where(lane == ROUTE_I1, i1, 0.0) + jnp.where(lane == ROUTE_I2, i2, 0.0)
    route_ref[...] = rec + jnp.where(lane == ROUTE_P1, p1, 0.0) + jnp.where(lane == ROUTE_P2, p2, 0.0)


def _router(x, g, mods, wr, br, n_tiles, cls_of_tile):
    m, d = x.shape
    return pl.pallas_call(
        _router_kernel,
        grid=(n_tiles,),
        in_specs=[pl.BlockSpec((ROW_TILE, d), lambda i: (i, 0)),
                  pl.BlockSpec((1, d), lambda i: (0, 0)),
                  _mod_spec(3, d, cls_of_tile), _mod_spec(4, d, cls_of_tile),
                  pl.BlockSpec((d, LANE), lambda i: (0, 0)),
                  pl.BlockSpec((1, LANE), lambda i: (0, 0))],
        out_specs=[pl.BlockSpec((ROW_TILE, d), lambda i: (i, 0)),
                   pl.BlockSpec((ROW_TILE, LANE), lambda i: (i, 0))],
        out_shape=[jax.ShapeDtypeStruct((n_tiles * ROW_TILE, d), F32),
                   jax.ShapeDtypeStruct((n_tiles * ROW_TILE, LANE), F32)],
        compiler_params=_params(("parallel",), 40),
        name="moe_router",
    )(x, g.reshape(1, d), mods, mods, wr, br)


def _routing_tables(route, n_experts, tile):
    n_tok = route.shape[0]
    experts = route[:, ROUTE_I1:ROUTE_I2 + 1].astype(jnp.int32).reshape(-1)
    onehot = (experts[:, None] == jnp.arange(n_experts, dtype=jnp.int32)[None, :]).astype(jnp.int32)
    running = jnp.cumsum(onehot, axis=0)
    rank = jnp.sum(running * onehot, axis=1) - 1
    counts = running[-1]
    padded = ((counts + tile - 1) // tile) * tile
    ends = jnp.cumsum(padded)
    slot = (ends - padded)[experts] + rank
    n_rows = 2 * n_tok + n_experts * tile
    n_tiles = n_rows // tile
    token = jnp.repeat(jnp.arange(n_tok, dtype=jnp.int32), 2)
    row_token = jnp.zeros((n_rows,), jnp.int32).at[slot].set(token)
    tile_start = jnp.arange(n_tiles, dtype=jnp.int32) * tile
    tile_expert = jnp.minimum(jnp.sum((tile_start[:, None] >= ends[None, :]).astype(jnp.int32), axis=1),
                              n_experts - 1)
    n_active = (ends[-1] // tile).astype(jnp.int32).reshape(1)
    return tile_expert, row_token, n_active, slot.astype(jnp.int32)


MOE_TILE = 256
MOE_FF_SPLIT = 2
GATHER_UNROLL = 8


def _experts_kernel(te_ref, tok_ref, nact_ref, h_hbm, wg_ref, wu_ref, wd_ref, o_ref, xbuf, sem):
    j = pl.program_id(0)
    n_act = nact_ref[0]
    slot = j % 2
    tile = o_ref.shape[0]

    def row_copy(tok, r, s):
        return pltpu.make_async_copy(h_hbm.at[pl.ds(tok, 1), :], xbuf.at[s, pl.ds(r, 1), :], sem.at[s])

    def start_gather(t, s):
        def body(r, carry):
            row_copy(tok_ref[t * tile + r], r, s).start()
            return carry
        lax.fori_loop(0, tile, body, 0, unroll=GATHER_UNROLL)

    def wait_gather(s):
        pltpu.make_async_copy(h_hbm.at[pl.ds(0, tile), :], xbuf.at[s], sem.at[s]).wait()

    @pl.when(j < n_act)
    def _():
        @pl.when(j == 0)
        def _():
            start_gather(0, 0)

        wait_gather(slot)

        @pl.when(j + 1 < n_act)
        def _():
            start_gather(j + 1, 1 - slot)

        h = xbuf[slot].astype(BF16)
        ff = wg_ref.shape[1]
        w = ff // MOE_FF_SPLIT
        y = None
        for part in range(MOE_FF_SPLIT):
            lo = part * w
            a = _silu(_dot(h, wg_ref[:, lo:lo + w])) * _dot(h, wu_ref[:, lo:lo + w])
            yp = _dot(a.astype(BF16), wd_ref[lo:lo + w, :])
            y = yp if y is None else y + yp
        o_ref[...] = y

    @pl.when(j >= n_act)
    def _():
        o_ref[...] = jnp.zeros_like(o_ref)


def _experts(h, tile_expert, row_token, n_active, wg, wu, wd):
    n_e, d, ff = wg.shape
    tile = MOE_TILE
    n_rows = row_token.shape[0]
    assert ff % (MOE_FF_SPLIT * LANE) == 0
    resident = pl.Buffered(1)
    grid_spec = pltpu.PrefetchScalarGridSpec(
        num_scalar_prefetch=3,
        grid=(n_rows // tile,),
        in_specs=[pl.BlockSpec(memory_space=pl.ANY),
                  pl.BlockSpec((None, d, ff), lambda j, te, tok, na: (te[j], 0, 0), pipeline_mode=resident),
                  pl.BlockSpec((None, d, ff), lambda j, te, tok, na: (te[j], 0, 0), pipeline_mode=resident),
                  pl.BlockSpec((None, ff, d), lambda j, te, tok, na: (te[j], 0, 0), pipeline_mode=resident)],
        out_specs=pl.BlockSpec((tile, d), lambda j, te, tok, na: (j, 0)),
        scratch_shapes=[pltpu.VMEM((2, tile, d), F32), pltpu.SemaphoreType.DMA((2,))],
    )
    return pl.pallas_call(
        _experts_kernel,
        grid_spec=grid_spec,
        out_shape=jax.ShapeDtypeStruct((n_rows, d), F32),
        compiler_params=_params(("arbitrary",), 56),
        name="moe_experts",
    )(tile_expert, row_token, n_active, h, wg, wu, wd)


COMBINE_TILE = 256


def _combine_kernel(slot_ref, y_hbm, x_ref, route_ref, gate_ref, gf_ref, o_ref, ybuf, sem, *, final_norm):
    i = pl.program_id(0)
    n = pl.num_programs(0)
    buf = i % 2
    tile = x_ref.shape[0]

    def row_copy(row, k, r, s):
        return pltpu.make_async_copy(y_hbm.at[pl.ds(row, 1), :], ybuf.at[s, pl.ds(k * tile + r, 1), :], sem.at[s])

    def start_gather(t, s):
        def body(r, carry):
            base = 2 * (t * tile + r)
            row_copy(slot_ref[base], 0, r, s).start()
            row_copy(slot_ref[base + 1], 1, r, s).start()
            return carry
        lax.fori_loop(0, tile, body, 0, unroll=GATHER_UNROLL // 2)

    def wait_gather(s):
        pltpu.make_async_copy(y_hbm.at[pl.ds(0, 2 * tile), :], ybuf.at[s], sem.at[s]).wait()

    @pl.when(i == 0)
    def _():
        start_gather(0, 0)

    wait_gather(buf)

    @pl.when(i + 1 < n)
    def _():
        start_gather(i + 1, 1 - buf)

    route = route_ref[...]
    p1 = route[:, ROUTE_P1:ROUTE_P1 + 1]
    p2 = route[:, ROUTE_P2:ROUTE_P2 + 1]
    y = p1 * ybuf[buf, 0:tile, :] + p2 * ybuf[buf, tile:2 * tile, :]
    out = x_ref[...] + gate_ref[...] * y
    if final_norm:
        out = (out * lax.rsqrt(jnp.mean(out * out, axis=-1, keepdims=True) + NORM_EPS)) * gf_ref[...]
    o_ref[...] = out


def _combine(y_rows, slot, x, route, mods, n_tokens, cls_of_tile, final_g):
    d = x.shape[1]
    tile = COMBINE_TILE
    per = ROW_TILE // tile
    final_norm = final_g is not None
    if not final_norm:
        final_g = jnp.ones((d,), F32)
    grid_spec = pltpu.PrefetchScalarGridSpec(
        num_scalar_prefetch=1,
        grid=(n_tokens // tile,),
        in_specs=[pl.BlockSpec(memory_space=pl.ANY),
                  pl.BlockSpec((tile, d), lambda i, s: (i, 0)),
                  pl.BlockSpec((tile, LANE), lambda i, s: (i, 0)),
                  pl.BlockSpec((None, 1, d), lambda i, s: (cls_of_tile(i // per), 0, 5)),
                  pl.BlockSpec((1, d), lambda i, s: (0, 0))],
        out_specs=pl.BlockSpec((tile, d), lambda i, s: (i, 0)),
        scratch_shapes=[pltpu.VMEM((2, 2 * tile, d), F32), pltpu.SemaphoreType.DMA((2,))],
    )
    return pl.pallas_call(
        functools.partial(_combine_kernel, final_norm=final_norm),
        grid_spec=grid_spec,
        out_shape=jax.ShapeDtypeStruct((n_tokens, d), F32),
        compiler_params=_params(("arbitrary",), 48),
        name="moe_combine",
    )(slot, y_rows, x, route, mods, final_g.reshape(1, d))


def _final_kernel(x_ref, g_ref, o_ref):
    x = x_ref[...]
    o_ref[...] = (x * lax.rsqrt(jnp.mean(x * x, axis=-1, keepdims=True) + NORM_EPS)) * g_ref[...]


def _final_norm(x, g, n_tiles):
    d = x.shape[1]
    return pl.pallas_call(
        _final_kernel,
        grid=(n_tiles,),
        in_specs=[pl.BlockSpec((ROW_TILE, d), lambda i: (i, 0)),
                  pl.BlockSpec((1, d), lambda i: (0, 0))],
        out_specs=pl.BlockSpec((ROW_TILE, d), lambda i: (i, 0)),
        out_shape=jax.ShapeDtypeStruct((n_tiles * ROW_TILE, d), F32),
        compiler_params=_params(("parallel",), 40),
        name="final_norm",
    )(x, g.reshape(1, d))


def _pad_lanes(v, width=LANE):
    return jnp.pad(v, [(0, 0)] * (v.ndim - 1) + [(0, width - v.shape[-1])])


def kernel(x, c, ctx, c_ctx, w_mod, b_mod, g_mix, g_ffn, w_in, conv_w, ssd_conv_w, ssd_conv_b, ssd_a_log, ssd_dt_bias, ssd_d, ssd_norm, da_lambda, da_subln, w_out, ffn_w_gate, ffn_w_up, ffn_w_down, moe_w_router, moe_b_router, moe_w_gate, moe_w_up, moe_w_down, g_final):
    batch, seq, d = x.shape
    ctx_len = ctx.shape[1]
    depth = w_mod.shape[0]
    n_lat = batch * seq
    m = n_lat + batch * ctx_len
    assert seq % ROW_TILE == 0 and (batch * ctx_len) % ROW_TILE == 0 and n_lat % ctx_len == 0
    assert ctx_len % HALO_TILE == 0 and ctx_len % SSD_CHUNK == 0 and batch < MOD_CLASSES
    n_lat_tiles = n_lat // ROW_TILE
    n_all_tiles = m // ROW_TILE
    tiles_per_seq = seq // ROW_TILE
    cls_of_tile = lambda i: jnp.minimum(i // tiles_per_seq, batch)

    x_lat, x_ctx = x.reshape(n_lat, d), ctx.reshape(batch * ctx_len, d)
    cvec = jnp.zeros((MOD_CLASSES, d), F32).at[:batch].set(c).at[batch].set(c_ctx)
    mods_all = _mod_vectors(cvec, w_mod, b_mod)
    cos_t, sin_t = _rope_tables(seq)

    for i in range(depth):
        ctx_out = i < depth - 1
        lam_init = 0.8 - 0.6 * math.exp(-0.3 * i)
        n_tiles = n_all_tiles if ctx_out else n_lat_tiles
        mods = mods_all[i].reshape(MOD_CLASSES, 1, N_MOD * d)
        wi = w_in[i]
        w_a = jnp.concatenate([wi[:, COL_CONV:COL_Q], wi[:, COL_XBC:COL_DT]], axis=1).astype(BF16)
        w_b = jnp.concatenate([wi[:, COL_Q:COL_XBC] * (DA_SCALE * LOG2_E), wi[:, COL_K:COL_V],
                               wi[:, COL_V:COL_V + DA_WIDTH]], axis=1).astype(BF16)
        w_c = _pad_lanes(wi[:, COL_DT:COL_K]).astype(BF16)

        h = _norm_modulate(x_lat, x_ctx, g_mix[i], mods, 0, n_all_tiles, n_lat_tiles, cls_of_tile)
        p_a = _matmul(h, w_a, F32, 1024, "in_proj_conv_z_xbc")
        qkv, v_t = _matmul_rope(h, w_b, cos_t, sin_t, n_lat_tiles, tiles_per_seq)
        p_dt = _matmul(h, w_c, F32, LANE, "in_proj_dt")

        ya, xbc, dt2 = _prep(p_a, p_dt, conv_w[i], ssd_conv_w[i], ssd_conv_b[i].reshape(1, -1),
                             _pad_lanes(ssd_dt_bias[i].reshape(1, -1)), n_lat // HALO_TILE,
                             seq // HALO_TILE, ctx_len // HALO_TILE)
        y_fwd, y_bwd = _ssd_scan(xbc, dt2, ssd_a_log[i], batch, seq // SSD_CHUNK, ctx_len // SSD_CHUNK)
        yb = _ssd_gate(y_fwd, y_bwd, xbc, p_a, jnp.repeat(ssd_d[i], SSD_HEAD_DIM).reshape(1, -1),
                       ssd_norm[i].reshape(1, -1), n_tiles)

        subln = da_subln[i].reshape(-1, 1)
        yc = _attention_lat(qkv, v_t, da_lambda[i], subln, lam_init, batch, seq, ctx_len)
        yc_ctx = _attention_ctx(qkv, v_t, da_lambda[i], subln, lam_init, batch, seq, ctx_len) if ctx_out else None

        x_mid = _out_proj(ya, yb, yc, yc_ctx, w_out[i].astype(BF16), x_lat, x_ctx, mods, n_tiles, n_lat_tiles,
                          cls_of_tile)

        j = i // 2
        if i % 2 == 0:
            xs_all = _ffn_dense(x_mid, g_ffn[i], mods, ffn_w_gate[j].astype(BF16), ffn_w_up[j].astype(BF16),
                                ffn_w_down[j].astype(BF16), n_tiles, cls_of_tile)
        else:
            hh, route = _router(x_mid, g_ffn[i], mods, _pad_lanes(moe_w_router[j]),
                                _pad_lanes(moe_b_router[j].reshape(1, -1)), n_tiles, cls_of_tile)
            tile_expert, row_token, n_active, slot = _routing_tables(route, moe_w_gate.shape[1], MOE_TILE)
            y_rows = _experts(hh, tile_expert, row_token, n_active, moe_w_gate[j].astype(BF16),
                              moe_w_up[j].astype(BF16), moe_w_down[j].astype(BF16))
            xs_all = _combine(y_rows, slot, x_mid, route, mods, n_tiles * ROW_TILE, cls_of_tile,
                              None if ctx_out else g_final)
        x_lat = xs_all
        x_ctx = xs_all[n_lat:] if ctx_out else None

    last_is_moe = depth % 2 == 0
    out = x_lat if last_is_moe else _final_norm(x_lat, g_final, n_lat_tiles)
    return out.reshape(batch, seq, d)
```

```python
import functools
import math

import jax
import jax.numpy as jnp
from jax import lax
from jax.experimental import pallas as pl
from jax.experimental.pallas import tpu as pltpu

NORM_EPS = 1e-6
N_MOD = 6
GRID_W = 64

SHORT_CONV = 3
CONV_W = 512

SSD_HEADS = 8
SSD_HEAD_DIM = 64
SSD_INNER = SSD_HEADS * SSD_HEAD_DIM
SSD_STATE = 128
SSD_GROUPS = 2
SSD_CHUNK = 128
SSD_XBC = SSD_INNER + 2 * SSD_GROUPS * SSD_STATE
SSD_GROUP_W = SSD_INNER // SSD_GROUPS

DA_HEADS = 8
DA_HEAD_DIM = 64
DA_V_DIM = 2 * DA_HEAD_DIM
DA_QK = DA_HEADS * 2 * DA_HEAD_DIM
DA_WIDTH = DA_HEADS * DA_V_DIM
DA_SCALE = DA_HEAD_DIM ** -0.5
LOG2_E = math.log2(math.e)
ROPE_THETA = 10000.0

COL_CONV = 0
COL_Z = COL_CONV + 3 * CONV_W
COL_Q = COL_Z + SSD_INNER
COL_XBC = COL_Q + DA_QK
COL_DT = COL_XBC + SSD_XBC
COL_K = COL_DT + 2 * SSD_HEADS
COL_V = COL_K + DA_QK

N_EXPERTS = 8

LANE = 128
SUBLANE = 8
ROW_TILE = 512
HALO_TILE = 256
MOD_CLASSES = 8

F32 = jnp.float32
BF16 = jnp.bfloat16
HIGHEST = lax.Precision.HIGHEST
MIB = 1024 * 1024


def _params(semantics, vmem_mib=None):
    kw = {"dimension_semantics": semantics}
    if vmem_mib is not None:
        kw["vmem_limit_bytes"] = vmem_mib * MIB
    return pltpu.CompilerParams(**kw)


def _silu(v):
    return v * jax.nn.sigmoid(v)


def _dot(a, b):
    return jnp.dot(a, b, preferred_element_type=F32)


def _dot_nt(a, b):
    return lax.dot_general(a, b, (((1,), (1,)), ((), ())), preferred_element_type=F32)


def _dot_tn(a, b):
    return lax.dot_general(a, b, (((0,), (0,)), ((), ())), preferred_element_type=F32)


def _split2(v):
    hi = v.astype(BF16)
    return hi, (v - hi.astype(F32)).astype(BF16)


def _dot_3pass(a, b):
    a_hi, a_lo = _split2(a)
    b_hi, b_lo = _split2(b)
    return _dot(a_hi, b_hi) + (_dot(a_hi, b_lo) + _dot(a_lo, b_hi))


def _modulated_norm(x, g, shift, scale):
    ms = jnp.mean(x * x, axis=-1, keepdims=True)
    y = x * lax.rsqrt(ms + NORM_EPS)
    return (y * g) * (1.0 + scale) + shift


def _mod_spec(k, width, cls_of_tile):
    return pl.BlockSpec((None, 1, width), lambda i, *_: (cls_of_tile(i), 0, k))


def _mod_kernel(c_ref, w_ref, b_ref, o_ref):
    s = _silu(c_ref[...])
    o_ref[...] = _dot_3pass(s, w_ref[...]) + b_ref[...]


def _mod_vectors(cvec, w_mod, b_mod):
    depth, d, n = w_mod.shape
    tn = 1024
    return pl.pallas_call(
        _mod_kernel,
        grid=(depth, n // tn),
        in_specs=[pl.BlockSpec((MOD_CLASSES, d), lambda l, j: (0, 0)),
                  pl.BlockSpec((None, d, tn), lambda l, j: (l, 0, j)),
                  pl.BlockSpec((None, 1, tn), lambda l, j: (l, 0, j))],
        out_specs=pl.BlockSpec((None, MOD_CLASSES, tn), lambda l, j: (l, 0, j)),
        out_shape=jax.ShapeDtypeStruct((depth, MOD_CLASSES, n), F32),
        compiler_params=_params(("parallel", "parallel"), 40),
        name="mod_vectors",
    )(cvec, w_mod, b_mod.reshape(depth, 1, n))


def _lat_ctx_specs(width, n_lat_tiles):
    return [pl.BlockSpec((ROW_TILE, width), lambda i, *_: (jnp.minimum(i, n_lat_tiles - 1), 0)),
            pl.BlockSpec((ROW_TILE, width), lambda i, *_: (jnp.maximum(i - n_lat_tiles, 0), 0))]


def _lat_or_ctx(lat_ref, ctx_ref, n_lat_tiles):
    return jnp.where(pl.program_id(0) < n_lat_tiles, lat_ref[...], ctx_ref[...])


def _norm_kernel(xl_ref, xc_ref, g_ref, sh_ref, sc_ref, h_ref, *, n_lat_tiles):
    x = _lat_or_ctx(xl_ref, xc_ref, n_lat_tiles)
    h_ref[...] = _modulated_norm(x, g_ref[...], sh_ref[...], sc_ref[...]).astype(h_ref.dtype)


def _norm_modulate(x_lat, x_ctx, g, mods, k_shift, n_tiles, n_lat_tiles, cls_of_tile):
    d = x_lat.shape[1]
    return pl.pallas_call(
        functools.partial(_norm_kernel, n_lat_tiles=n_lat_tiles),
        grid=(n_tiles,),
        in_specs=_lat_ctx_specs(d, n_lat_tiles) + [
            pl.BlockSpec((1, d), lambda i: (0, 0)),
            _mod_spec(k_shift, d, cls_of_tile),
            _mod_spec(k_shift + 1, d, cls_of_tile)],
        out_specs=pl.BlockSpec((ROW_TILE, d), lambda i: (i, 0)),
        out_shape=jax.ShapeDtypeStruct((n_tiles * ROW_TILE, d), BF16),
        compiler_params=_params(("parallel",), 40),
        name="norm_modulate",
    )(x_lat, x_ctx, g.reshape(1, d), mods, mods)


def _mm_kernel(x_ref, w_ref, o_ref):
    o_ref[...] = _dot(x_ref[...], w_ref[...]).astype(o_ref.dtype)


def _matmul(x, w, out_dtype, tn, name):
    m, k = x.shape
    n = w.shape[1]
    return pl.pallas_call(
        _mm_kernel,
        grid=(m // ROW_TILE, n // tn),
        in_specs=[pl.BlockSpec((ROW_TILE, k), lambda i, j: (i, 0)),
                  pl.BlockSpec((k, tn), lambda i, j: (0, j))],
        out_specs=pl.BlockSpec((ROW_TILE, tn), lambda i, j: (i, j)),
        out_shape=jax.ShapeDtypeStruct((m, n), out_dtype),
        compiler_params=_params(("parallel", "parallel"), 40),
        name=name,
    )(x, w)


def _mm_rope_kernel(x_ref, w_ref, cos_ref, sin_ref, qk_ref, vt_ref, *, n_lat_tiles, n_rope_tiles):
    i = pl.program_id(0)
    j = pl.program_id(1)
    acc = _dot(x_ref[...], w_ref[...])
    tm, tn = acc.shape
    is_qk = j < n_rope_tiles
    rotate = jnp.logical_and(i < n_lat_tiles, is_qk)

    @pl.when(rotate)
    def _():
        lane = lax.broadcasted_iota(jnp.int32, (tm, LANE), 1)
        first = (lane % (DA_HEAD_DIM // 2)) < (DA_HEAD_DIM // 4)
        c = cos_ref[...]
        s = sin_ref[...]
        quarter = DA_HEAD_DIM // 4
        for hb in range(tn // LANE):
            blk = acc[:, hb * LANE:(hb + 1) * LANE]
            partner = jnp.where(first, pltpu.roll(blk, LANE - quarter, 1), pltpu.roll(blk, quarter, 1))
            qk_ref[:, hb * LANE:(hb + 1) * LANE] = (blk * c + partner * s).astype(qk_ref.dtype)

    @pl.when(jnp.logical_and(is_qk, jnp.logical_not(rotate)))
    def _():
        qk_ref[...] = acc.astype(qk_ref.dtype)

    @pl.when(jnp.logical_not(is_qk))
    def _():
        row = lax.broadcasted_iota(jnp.int32, (VT_ROWS - DA_V_DIM, tm), 0)
        tail = jnp.where(row == 0, 1.0, 0.0).astype(vt_ref.dtype)
        for h in range(DA_HEADS):
            lo = h * VT_ROWS
            vt_ref[lo:lo + DA_V_DIM, :] = acc[:, h * DA_V_DIM:(h + 1) * DA_V_DIM].T.astype(vt_ref.dtype)
            vt_ref[lo + DA_V_DIM:lo + VT_ROWS, :] = tail


def _matmul_rope(x, w, cos_t, sin_t, n_lat_tiles, tiles_per_seq):
    m, k = x.shape
    tn = DA_QK
    n_rope_tiles = 2 * DA_QK // tn
    assert w.shape[1] == 2 * DA_QK + DA_WIDTH and tn == DA_WIDTH
    kern = functools.partial(_mm_rope_kernel, n_lat_tiles=n_lat_tiles, n_rope_tiles=n_rope_tiles)
    return pl.pallas_call(
        kern,
        grid=(m // ROW_TILE, n_rope_tiles + 1),
        in_specs=[pl.BlockSpec((ROW_TILE, k), lambda i, j: (i, 0)),
                  pl.BlockSpec((k, tn), lambda i, j: (0, j)),
                  pl.BlockSpec((ROW_TILE, LANE), lambda i, j: (i % tiles_per_seq, 0)),
                  pl.BlockSpec((ROW_TILE, LANE), lambda i, j: (i % tiles_per_seq, 0))],
        out_specs=[pl.BlockSpec((ROW_TILE, tn), lambda i, j: (i, jnp.minimum(j, n_rope_tiles - 1))),
                   pl.BlockSpec((DA_HEADS * VT_ROWS, ROW_TILE), lambda i, j: (0, i))],
        out_shape=[jax.ShapeDtypeStruct((m, 2 * DA_QK), BF16),
                   jax.ShapeDtypeStruct((DA_HEADS * VT_ROWS, m), BF16)],
        compiler_params=_params(("parallel", "arbitrary"), 40),
        name="qkv_proj_rope",
    )(x, w, cos_t, sin_t)


def _rope_tables(n_tokens):
    rows = n_tokens // GRID_W
    row = jnp.repeat(jnp.arange(rows, dtype=F32), GRID_W)
    col = jnp.broadcast_to(jnp.arange(GRID_W, dtype=F32), (rows, GRID_W)).reshape(-1)
    n_freq = DA_HEAD_DIM // 4
    inv = ROPE_THETA ** (-jnp.arange(n_freq, dtype=F32) / n_freq)
    ang_r = row[:, None] * inv
    ang_c = col[:, None] * inv
    cos64 = jnp.concatenate([jnp.cos(ang_r), jnp.cos(ang_r), jnp.cos(ang_c), jnp.cos(ang_c)], axis=-1)
    sin64 = jnp.concatenate([-jnp.sin(ang_r), jnp.sin(ang_r), -jnp.sin(ang_c), jnp.sin(ang_c)], axis=-1)
    return jnp.tile(cos64, (1, 2)), jnp.tile(sin64, (1, 2))


def _conv3(u, prev_row, next_row, w_ref, has_prev, has_next):
    tm = u.shape[0]
    row = lax.broadcasted_iota(jnp.int32, u.shape, 0)
    prev_row = jnp.where(has_prev, prev_row, 0.0)
    next_row = jnp.where(has_next, next_row, 0.0)
    before = jnp.where(row == 0, prev_row, pltpu.roll(u, 1, 0))
    after = jnp.where(row == tm - 1, next_row, pltpu.roll(u, tm - 1, 0))
    return before * w_ref[0:1, :] + u * w_ref[1:2, :] + after * w_ref[2:3, :]


def _prep_kernel(cv_ref, cvp_ref, cvn_ref, xb_ref, xbp_ref, xbn_ref, dtr_ref,
                 cw_ref, sw_ref, sb_ref, dtb_ref, ya_ref, xs_ref, dt_ref,
                 *, n_lat_tiles, lat_tiles_per_seq, ctx_tiles_per_seq):
    i = pl.program_id(0)
    is_lat = i < n_lat_tiles
    pos = jnp.where(is_lat, i % lat_tiles_per_seq, (i - n_lat_tiles) % ctx_tiles_per_seq)
    last = jnp.where(is_lat, lat_tiles_per_seq - 1, ctx_tiles_per_seq - 1)
    has_prev = pos != 0
    has_next = pos != last

    w = CONV_W
    cv = cv_ref[...]
    gate_b, u = cv[:, 0:w], cv[:, w:2 * w] * cv[:, 2 * w:3 * w]
    p = cvp_ref[SUBLANE - 1:SUBLANE, :]
    n = cvn_ref[0:1, :]
    u_prev = p[:, w:2 * w] * p[:, 2 * w:3 * w]
    u_next = n[:, w:2 * w] * n[:, 2 * w:3 * w]
    ya_ref[...] = (gate_b * _conv3(u, u_prev, u_next, cw_ref, has_prev, has_next)).astype(ya_ref.dtype)

    xc = _conv3(xb_ref[...], xbp_ref[SUBLANE - 1:SUBLANE, :], xbn_ref[0:1, :], sw_ref, has_prev, has_next)
    xs_ref[...] = _silu(xc + sb_ref[...])

    t = dtr_ref[...] + dtb_ref[...]
    dt = jnp.maximum(t, 0.0) + jnp.log1p(jnp.exp(-jnp.abs(t)))
    dt_ref[0] = dt
    dt_ref[1] = pltpu.roll(dt, LANE - SSD_HEADS, 1)


def _prep(p_a, p_dt, conv_w, ssd_conv_w, ssd_conv_b, dt_bias, n_lat_tiles, lat_tiles_per_seq,
          ctx_tiles_per_seq):
    m = p_a.shape[0]
    tm = HALO_TILE
    per = tm // SUBLANE
    n8 = m // SUBLANE
    cw = 3 * CONV_W
    xbc_blk = COL_Q // SSD_XBC
    prev = lambda i: jnp.maximum(i * per - 1, 0)
    nxt = lambda i: jnp.minimum((i + 1) * per, n8 - 1)
    kern = functools.partial(_prep_kernel, n_lat_tiles=n_lat_tiles, lat_tiles_per_seq=lat_tiles_per_seq,
                             ctx_tiles_per_seq=ctx_tiles_per_seq)
    return pl.pallas_call(
        kern,
        grid=(m // tm,),
        in_specs=[pl.BlockSpec((tm, cw), lambda i: (i, 0)),
                  pl.BlockSpec((SUBLANE, cw), lambda i: (prev(i), 0)),
                  pl.BlockSpec((SUBLANE, cw), lambda i: (nxt(i), 0)),
                  pl.BlockSpec((tm, SSD_XBC), lambda i: (i, xbc_blk)),
                  pl.BlockSpec((SUBLANE, SSD_XBC), lambda i: (prev(i), xbc_blk)),
                  pl.BlockSpec((SUBLANE, SSD_XBC), lambda i: (nxt(i), xbc_blk)),
                  pl.BlockSpec((tm, LANE), lambda i: (i, 0)),
                  pl.BlockSpec((SHORT_CONV, CONV_W), lambda i: (0, 0)),
                  pl.BlockSpec((SHORT_CONV, SSD_XBC), lambda i: (0, 0)),
                  pl.BlockSpec((1, SSD_XBC), lambda i: (0, 0)),
                  pl.BlockSpec((1, LANE), lambda i: (0, 0))],
        out_specs=[pl.BlockSpec((tm, CONV_W), lambda i: (i, 0)),
                   pl.BlockSpec((tm, SSD_XBC), lambda i: (i, 0)),
                   pl.BlockSpec((2, tm, LANE), lambda i: (0, i, 0))],
        out_shape=[jax.ShapeDtypeStruct((m, CONV_W), BF16),
                   jax.ShapeDtypeStruct((m, SSD_XBC), F32),
                   jax.ShapeDtypeStruct((2, m, LANE), F32)],
        compiler_params=_params(("parallel",), 40),
        name="conv_prep",
    )(p_a, p_a, p_a, p_a, p_a, p_a, p_dt, conv_w, ssd_conv_w, ssd_conv_b, dt_bias)


def _split3(v):
    hi = v.astype(BF16)
    rest = v - hi.astype(F32)
    mid = rest.astype(BF16)
    return hi, mid, (rest - mid.astype(F32)).astype(BF16)


def _dot_f32_lhs(a, b01):
    return sum(_dot(piece, b01) for piece in _split3(a))


def _dot_f32_rhs(a01, b):
    return sum(_dot(a01, piece) for piece in _split3(b))


def _ssd_chunk(direction, xs_ref, dt_ref, alog_ref, alog_ch_ref, y_ref, state_ref):
    t = SSD_CHUNK
    r = lax.broadcasted_iota(jnp.int32, (t, t), 0)
    c = lax.broadcasted_iota(jnp.int32, (t, t), 1)
    mask = (r >= c) if direction == 0 else (r <= c)
    tri = mask.astype(BF16)

    er = lax.broadcasted_iota(jnp.int32, (LANE, SSD_INNER), 0)
    ec = lax.broadcasted_iota(jnp.int32, (LANE, SSD_INNER), 1)
    expand = (jnp.right_shift(ec, 6) == er).astype(BF16)

    dt = dt_ref[direction]
    cs = _dot_f32_rhs(tri, dt * (-jnp.exp(alog_ref[direction])))
    cs_t = cs.T
    dt_ch = _dot_f32_lhs(dt, expand)
    da_ch = dt_ch * (-jnp.exp(alog_ch_ref[direction]))
    cs_ch = _dot_f32_rhs(tri, da_ch)
    tot_ch = jnp.sum(da_ch, axis=0, keepdims=True)

    xdt = xs_ref[:, 0:SSD_INNER] * dt_ch
    x_state = xdt * jnp.exp(tot_ch - cs_ch)
    y_scale = jnp.exp(cs_ch)
    carry = jnp.exp(tot_ch)
    gw = SSD_GROUP_W
    heads_per_group = SSD_HEADS // SSD_GROUPS
    lane_head = jnp.right_shift(lax.broadcasted_iota(jnp.int32, (t, gw), 1), 6)

    for g in range(SSD_GROUPS):
        b_lo = SSD_INNER + g * SSD_STATE
        c_lo = SSD_INNER + SSD_GROUPS * SSD_STATE + g * SSD_STATE
        bg = xs_ref[:, b_lo:b_lo + SSD_STATE].astype(BF16)
        cg = xs_ref[:, c_lo:c_lo + SSD_STATE].astype(BF16)
        cb = _dot_nt(cg, bg)
        sg = state_ref[direction, g]
        xdt_g = xdt[:, g * gw:(g + 1) * gw]
        y = _dot(cg, sg.astype(BF16)) * y_scale[:, g * gw:(g + 1) * gw]
        for hh in range(heads_per_group):
            h = g * heads_per_group + hh
            decay = jnp.where(mask, jnp.exp(cs[:, h:h + 1] - cs_t[h:h + 1, :]), 0.0)
            x_h = jnp.where(lane_head == hh, xdt_g, 0.0).astype(BF16)
            y = y + _dot((cb * decay).astype(BF16), x_h)
        y_ref[:, g * gw:(g + 1) * gw] = y
        ds = _dot_tn(bg, x_state[:, g * gw:(g + 1) * gw].astype(BF16))
        state_ref[direction, g] = sg * carry[:, g * gw:(g + 1) * gw] + ds


def _ssd_kernel(xf_ref, dtf_ref, xb_ref, dtb_ref, alog_ref, alog_ch_ref, yf_ref, yb_ref, state_ref):
    @pl.when(pl.program_id(1) == 0)
    def _():
        state_ref[...] = jnp.zeros_like(state_ref)

    _ssd_chunk(0, xf_ref, dtf_ref, alog_ref, alog_ch_ref, yf_ref, state_ref)
    _ssd_chunk(1, xb_ref, dtb_ref, alog_ref, alog_ch_ref, yb_ref, state_ref)


def _ssd_scan(xs, dt2, a_log, batch, n_lat_chunks, n_ctx_chunks):
    m = xs.shape[0]
    t = SSD_CHUNK
    assert SSD_HEAD_DIM == 64 and SSD_HEADS <= LANE
    a_log_ch = jnp.repeat(a_log, SSD_HEAD_DIM, axis=1).reshape(2, 1, SSD_INNER)
    a_log = _pad_lanes(a_log).reshape(2, 1, LANE)

    def chunk(b, d, s):
        j_ctx = s if d == 0 else n_ctx_chunks - 1 - s
        sl = s - n_ctx_chunks
        j_lat = sl if d == 0 else n_lat_chunks - 1 - sl
        return jnp.where(s < n_ctx_chunks, batch * n_lat_chunks + b * n_ctx_chunks + j_ctx,
                         b * n_lat_chunks + j_lat)

    x_spec = lambda d: pl.BlockSpec((t, SSD_XBC), lambda b, s: (chunk(b, d, s), 0))
    dt_spec = lambda d: pl.BlockSpec((2, t, LANE), lambda b, s: (0, chunk(b, d, s), 0))
    y_spec = lambda d: pl.BlockSpec((t, SSD_INNER), lambda b, s: (chunk(b, d, s), 0))
    return pl.pallas_call(
        _ssd_kernel,
        grid=(batch, n_ctx_chunks + n_lat_chunks),
        in_specs=[x_spec(0), dt_spec(0), x_spec(1), dt_spec(1),
                  pl.BlockSpec((2, 1, LANE), lambda b, s: (0, 0, 0)),
                  pl.BlockSpec((2, 1, SSD_INNER), lambda b, s: (0, 0, 0))],
        out_specs=[y_spec(0), y_spec(1)],
        out_shape=[jax.ShapeDtypeStruct((m, SSD_INNER), F32), jax.ShapeDtypeStruct((m, SSD_INNER), F32)],
        scratch_shapes=[pltpu.VMEM((2, SSD_GROUPS, SSD_STATE, SSD_GROUP_W), F32)],
        compiler_params=_params(("parallel", "arbitrary")),
        name="ssd_scan",
    )(xs, dt2, xs, dt2, a_log, a_log_ch)


def _ssd_gate_kernel(yf_ref, yb_ref, xs_ref, z_ref, d_ref, g_ref, o_ref):
    yl = d_ref[...] * xs_ref[...] + yf_ref[...] + yb_ref[...]
    u = yl * _silu(z_ref[...])
    gw = SSD_GROUP_W
    for g in range(SSD_GROUPS):
        ug = u[:, g * gw:(g + 1) * gw]
        ug = ug * lax.rsqrt(jnp.mean(ug * ug, axis=-1, keepdims=True) + NORM_EPS)
        o_ref[:, g * gw:(g + 1) * gw] = (ug * g_ref[:, g * gw:(g + 1) * gw]).astype(o_ref.dtype)


def _ssd_gate(y_fwd, y_bwd, xs, p_a, d_row, norm_g, n_tiles):
    w = SSD_INNER
    return pl.pallas_call(
        _ssd_gate_kernel,
        grid=(n_tiles,),
        in_specs=[pl.BlockSpec((ROW_TILE, w), lambda i: (i, 0)),
                  pl.BlockSpec((ROW_TILE, w), lambda i: (i, 0)),
                  pl.BlockSpec((ROW_TILE, w), lambda i: (i, 0)),
                  pl.BlockSpec((ROW_TILE, w), lambda i: (i, COL_Z // w)),
                  pl.BlockSpec((1, w), lambda i: (0, 0)),
                  pl.BlockSpec((1, w), lambda i: (0, 0))],
        out_specs=pl.BlockSpec((ROW_TILE, w), lambda i: (i, 0)),
        out_shape=jax.ShapeDtypeStruct((n_tiles * ROW_TILE, w), BF16),
        compiler_params=_params(("parallel",)),
        name="ssd_gate_norm",
    )(y_fwd, y_bwd, xs, p_a, d_row, norm_g)


ATT_TQ = 1024
ATT_TKC = 512


def _attn_kernel(*refs, has_lat, lam_init):
    if has_lat:
        (q_ref, kc_ref, vct_ref, kl_ref, vlt_ref, lam_ref, g_ref, o_ref,
         m_ref, acc_ref, sa_ref, sb_ref) = refs
    else:
        q_ref, kc_ref, vct_ref, lam_ref, g_ref, o_ref, m_ref, acc_ref, sa_ref, sb_ref = refs

    q = q_ref[...]
    lane = lax.broadcasted_iota(jnp.int32, q.shape, 1)
    zero = jnp.zeros_like(q)
    q_maps = (jnp.where(lane < DA_HEAD_DIM, q, zero), jnp.where(lane >= DA_HEAD_DIM, q, zero))

    def scores(k, s_ref):
        n = k.shape[0]
        maxima = []
        for mp in range(2):
            s = _dot_nt(k, q_maps[mp])
            s_ref[mp, 0:n, :] = s
            maxima.append(jnp.max(s, axis=0, keepdims=True))
        return tuple(maxima)

    def update(s_ref, maxima, vt):
        n = vt.shape[1]
        for mp in range(2):
            m_old = m_ref[mp]
            m_new = jnp.maximum(m_old, maxima[mp])
            alpha = jnp.exp2(m_old - m_new)
            p = jnp.exp2(s_ref[mp, 0:n, :] - m_new[0:1, :])
            acc_ref[mp] = alpha[0:1, :] * acc_ref[mp] + _dot(vt, p.astype(vt.dtype))
            m_ref[mp] = m_new

    m_ref[...] = jnp.full(m_ref.shape, -jnp.inf, F32)
    acc_ref[...] = jnp.zeros_like(acc_ref)
    mx = scores(kc_ref[...], sa_ref)

    if not has_lat:
        update(sa_ref, mx, vct_ref[...])
    else:
        tkc = min(ATT_TKC, kl_ref.shape[0])
        n_chunks = kl_ref.shape[0] // tkc
        k_at = lambda c: kl_ref[pl.ds(pl.multiple_of(c * tkc, tkc), tkc), :]
        vt_at = lambda c: vlt_ref[:, pl.ds(pl.multiple_of(c * tkc, tkc), tkc)]

        mx_ctx = mx
        mx = scores(k_at(0), sb_ref)
        update(sa_ref, mx_ctx, vct_ref[...])

        def pair(c2, mx_b):
            c = 1 + 2 * c2
            mx_a = scores(k_at(c), sa_ref)
            update(sb_ref, mx_b, vt_at(c - 1))
            mx_b = scores(k_at(c + 1), sb_ref)
            update(sa_ref, mx_a, vt_at(c))
            return mx_b

        mx = lax.fori_loop(0, (n_chunks - 1) // 2, pair, mx)
        if (n_chunks - 1) % 2:
            mx_a = scores(k_at(n_chunks - 1), sa_ref)
            update(sb_ref, mx, vt_at(n_chunks - 2))
            update(sa_ref, mx_a, vt_at(n_chunks - 1))
        else:
            update(sb_ref, mx, vt_at(n_chunks - 1))

    lv = lam_ref[...]
    dotp = lambda a, b: jnp.sum(lv[a:a + 1, :] * lv[b:b + 1, :], axis=-1, keepdims=True)
    lam = jnp.exp(dotp(0, 1)) - jnp.exp(dotp(2, 3)) + lam_init
    vd = DA_V_DIM
    inv_l = 1.0 / acc_ref[:, vd:vd + 1, :]
    o = acc_ref[0, 0:vd, :] * inv_l[0] - lam * (acc_ref[1, 0:vd, :] * inv_l[1])
    y = o * lax.rsqrt(jnp.mean(o * o, axis=0, keepdims=True) + NORM_EPS)
    o_ref[...] = ((y * g_ref[...]) * (1.0 - lam_init)).T.astype(o_ref.dtype)


VT_ROWS = DA_V_DIM + 16


def _attn_scratch(tq, keys_a, keys_b):
    return [pltpu.VMEM((2, SUBLANE, tq), F32), pltpu.VMEM((2, VT_ROWS, tq), F32),
            pltpu.VMEM((2, keys_a, tq), F32), pltpu.VMEM((2, keys_b, tq), F32)]


def _attention_lat(qkv, v_t, da_lambda, subln, lam_init, batch, seq, ctx_len):
    h = DA_HEADS
    tq = ATT_TQ
    nq = seq // tq
    ctx_blk0 = (batch * seq) // ctx_len
    assert seq % tq == 0 and seq % min(ATT_TKC, seq) == 0
    kern = functools.partial(_attn_kernel, has_lat=True, lam_init=lam_init)
    return pl.pallas_call(
        kern,
        grid=(batch, h, nq),
        in_specs=[pl.BlockSpec((tq, LANE), lambda b, hh, i: (b * nq + i, hh)),
                  pl.BlockSpec((ctx_len, LANE), lambda b, hh, i: (ctx_blk0 + b, h + hh)),
                  pl.BlockSpec((VT_ROWS, ctx_len), lambda b, hh, i: (hh, ctx_blk0 + b)),
                  pl.BlockSpec((seq, LANE), lambda b, hh, i: (b, h + hh)),
                  pl.BlockSpec((VT_ROWS, seq), lambda b, hh, i: (hh, b)),
                  pl.BlockSpec(da_lambda.shape, lambda b, hh, i: (0, 0)),
                  pl.BlockSpec((DA_V_DIM, 1), lambda b, hh, i: (0, 0))],
        out_specs=pl.BlockSpec((tq, LANE), lambda b, hh, i: (b * nq + i, hh)),
        out_shape=jax.ShapeDtypeStruct((batch * seq, DA_WIDTH), BF16),
        scratch_shapes=_attn_scratch(tq, max(min(ATT_TKC, seq), ctx_len), min(ATT_TKC, seq)),
        compiler_params=_params(("parallel", "parallel", "parallel"), 48),
        name="diff_attention",
    )(qkv, qkv, v_t, qkv, v_t, da_lambda, subln)


def _attention_ctx(qkv, v_t, da_lambda, subln, lam_init, batch, seq, ctx_len):
    h = DA_HEADS
    ctx_blk0 = (batch * seq) // ctx_len
    kern = functools.partial(_attn_kernel, has_lat=False, lam_init=lam_init)
    return pl.pallas_call(
        kern,
        grid=(batch, h),
        in_specs=[pl.BlockSpec((ctx_len, LANE), lambda b, hh: (ctx_blk0 + b, hh)),
                  pl.BlockSpec((ctx_len, LANE), lambda b, hh: (ctx_blk0 + b, h + hh)),
                  pl.BlockSpec((VT_ROWS, ctx_len), lambda b, hh: (hh, ctx_blk0 + b)),
                  pl.BlockSpec(da_lambda.shape, lambda b, hh: (0, 0)),
                  pl.BlockSpec((DA_V_DIM, 1), lambda b, hh: (0, 0))],
        out_specs=pl.BlockSpec((ctx_len, LANE), lambda b, hh: (b, hh)),
        out_shape=jax.ShapeDtypeStruct((batch * ctx_len, DA_WIDTH), BF16),
        scratch_shapes=_attn_scratch(ctx_len, ctx_len, SUBLANE),
        compiler_params=_params(("parallel", "parallel")),
        name="diff_attention_ctx",
    )(qkv, qkv, v_t, da_lambda, subln)


def _outproj_kernel(ya_ref, yb_ref, yc_ref, ycx_ref, w0_ref, w1_ref, w2_ref, xl_ref, xc_ref, gate_ref, o_ref, *,
                    n_lat_tiles):
    yc = _lat_or_ctx(yc_ref, ycx_ref, n_lat_tiles)
    acc = _dot(ya_ref[...], w0_ref[...]) + _dot(yb_ref[...], w1_ref[...]) + _dot(yc, w2_ref[...])
    o_ref[...] = _lat_or_ctx(xl_ref, xc_ref, n_lat_tiles) + gate_ref[...] * acc


def _out_proj(ya, yb, yc, yc_ctx, w_out, x_lat, x_ctx, mods, n_tiles, n_lat_tiles, cls_of_tile):
    d = x_lat.shape[1]
    tn = 1024
    wa, wb = ya.shape[1], yb.shape[1]
    assert wa == wb and yc.shape[1] == wa + wb
    if yc_ctx is None:
        assert n_tiles <= n_lat_tiles
        yc_ctx = yc
    lat_tile = lambda i: jnp.minimum(i, n_lat_tiles - 1)
    ctx_tile = lambda i: jnp.maximum(i - n_lat_tiles, 0)
    return pl.pallas_call(
        functools.partial(_outproj_kernel, n_lat_tiles=n_lat_tiles),
        grid=(n_tiles, d // tn),
        in_specs=[pl.BlockSpec((ROW_TILE, wa), lambda i, j: (i, 0)),
                  pl.BlockSpec((ROW_TILE, wb), lambda i, j: (i, 0))] + _lat_ctx_specs(wa + wb, n_lat_tiles) + [
                  pl.BlockSpec((wa, tn), lambda i, j: (0, j)),
                  pl.BlockSpec((wb, tn), lambda i, j: (1, j)),
                  pl.BlockSpec((wa + wb, tn), lambda i, j: (1, j)),
                  pl.BlockSpec((ROW_TILE, tn), lambda i, j: (lat_tile(i), j)),
                  pl.BlockSpec((ROW_TILE, tn), lambda i, j: (ctx_tile(i), j)),
                  pl.BlockSpec((None, 1, tn), lambda i, j: (cls_of_tile(i), 0, 2 * (d // tn) + j))],
        out_specs=pl.BlockSpec((ROW_TILE, tn), lambda i, j: (i, j)),
        out_shape=jax.ShapeDtypeStruct((n_tiles * ROW_TILE, d), F32),
        compiler_params=_params(("parallel", "parallel"), 40),
        name="out_proj_residual",
    )(ya, yb, yc, yc_ctx, w_out, w_out, w_out, x_lat, x_ctx, mods)


def _ffn_kernel(x_ref, g_ref, sh_ref, sc_ref, gate_ref, wg_ref, wu_ref, wd_ref, o_ref, h_ref, acc_ref):
    f = pl.program_id(1)

    @pl.when(f == 0)
    def _():
        h_ref[...] = _modulated_norm(x_ref[...], g_ref[...], sh_ref[...], sc_ref[...]).astype(h_ref.dtype)
        acc_ref[...] = jnp.zeros_like(acc_ref)

    h = h_ref[...]
    a = _silu(_dot(h, wg_ref[...])) * _dot(h, wu_ref[...])
    acc_ref[...] += _dot(a.astype(BF16), wd_ref[...])

    @pl.when(f == pl.num_programs(1) - 1)
    def _():
        o_ref[...] = x_ref[...] + gate_ref[...] * acc_ref[...]


def _ffn_dense(x, g, mods, wg, wu, wd, n_tiles, cls_of_tile):
    m, d = x.shape
    ff = wg.shape[1]
    tf = 512
    return pl.pallas_call(
        _ffn_kernel,
        grid=(n_tiles, ff // tf),
        in_specs=[pl.BlockSpec((ROW_TILE, d), lambda i, f: (i, 0)),
                  pl.BlockSpec((1, d), lambda i, f: (0, 0)),
                  _mod_spec(3, d, cls_of_tile), _mod_spec(4, d, cls_of_tile), _mod_spec(5, d, cls_of_tile),
                  pl.BlockSpec((d, tf), lambda i, f: (0, f)),
                  pl.BlockSpec((d, tf), lambda i, f: (0, f)),
                  pl.BlockSpec((tf, d), lambda i, f: (f, 0))],
        out_specs=pl.BlockSpec((ROW_TILE, d), lambda i, f: (i, 0)),
        out_shape=jax.ShapeDtypeStruct((n_tiles * ROW_TILE, d), F32),
        scratch_shapes=[pltpu.VMEM((ROW_TILE, d), BF16), pltpu.VMEM((ROW_TILE, d), F32)],
        compiler_params=_params(("parallel", "arbitrary"), 48),
        name="ffn_dense",
    )(x, g.reshape(1, d), mods, mods, mods, wg, wu, wd)


ROUTE_I1, ROUTE_I2, ROUTE_P1, ROUTE_P2 = 0, 1, 2, 3


def _router_kernel(x_ref, g_ref, sh_ref, sc_ref, wr_ref, br_ref, h_ref, route_ref):
    h = _modulated_norm(x_ref[...], g_ref[...], sh_ref[...], sc_ref[...])
    h_ref[...] = h
    logits = _dot_3pass(h, wr_ref[...]) + br_ref[...]
    lane = lax.broadcasted_iota(jnp.int32, logits.shape, 1).astype(F32)
    neg = -jnp.inf
    lg = jnp.where(lane < N_EXPERTS, logits, neg)
    v1 = jnp.max(lg, axis=-1, keepdims=True)
    i1 = jnp.min(jnp.where(lg == v1, lane, float(LANE)), axis=-1, keepdims=True)
    lg2 = jnp.where(lane == i1, neg, lg)
    v2 = jnp.max(lg2, axis=-1, keepdims=True)
    i2 = jnp.min(jnp.where(lg2 == v2, lane, float(LANE)), axis=-1, keepdims=True)
    e = jnp.exp(v2 - v1)
    p1 = 1.0 / (1.0 + e)
    p2 = e / (1.0 + e)
    rec = jnp.where(lane == ROUTE_I1, i1, 0.0) + jnp.where(lane == ROUTE_I2, i2, 0.0)
    route_ref[...] = rec + jnp.where(lane == ROUTE_P1, p1, 0.0) + jnp.where(lane == ROUTE_P2, p2, 0.0)


def _router(x, g, mods, wr, br, n_tiles, cls_of_tile):
    m, d = x.shape
    return pl.pallas_call(
        _router_kernel,
        grid=(n_tiles,),
        in_specs=[pl.BlockSpec((ROW_TILE, d), lambda i: (i, 0)),
                  pl.BlockSpec((1, d), lambda i: (0, 0)),
                  _mod_spec(3, d, cls_of_tile), _mod_spec(4, d, cls_of_tile),
                  pl.BlockSpec((d, LANE), lambda i: (0, 0)),
                  pl.BlockSpec((1, LANE), lambda i: (0, 0))],
        out_specs=[pl.BlockSpec((ROW_TILE, d), lambda i: (i, 0)),
                   pl.BlockSpec((ROW_TILE, LANE), lambda i: (i, 0))],
        out_shape=[jax.ShapeDtypeStruct((n_tiles * ROW_TILE, d), F32),
                   jax.ShapeDtypeStruct((n_tiles * ROW_TILE, LANE), F32)],
        compiler_params=_params(("parallel",), 40),
        name="moe_router",
    )(x, g.reshape(1, d), mods, mods, wr, br)


def _routing_tables(route, n_experts, tile):
    n_tok = route.shape[0]
    experts = route[:, ROUTE_I1:ROUTE_I2 + 1].astype(jnp.int32).reshape(-1)
    onehot = (experts[:, None] == jnp.arange(n_experts, dtype=jnp.int32)[None, :]).astype(jnp.int32)
    running = jnp.cumsum(onehot, axis=0)
    rank = jnp.sum(running * onehot, axis=1) - 1
    counts = running[-1]
    padded = ((counts + tile - 1) // tile) * tile
    ends = jnp.cumsum(padded)
    slot = (ends - padded)[experts] + rank
    n_rows = 2 * n_tok + n_experts * tile
    n_tiles = n_rows // tile
    token = jnp.repeat(jnp.arange(n_tok, dtype=jnp.int32), 2)
    row_token = jnp.zeros((n_rows,), jnp.int32).at[slot].set(token)
    tile_start = jnp.arange(n_tiles, dtype=jnp.int32) * tile
    tile_expert = jnp.minimum(jnp.sum((tile_start[:, None] >= ends[None, :]).astype(jnp.int32), axis=1),
                              n_experts - 1)
    n_active = (ends[-1] // tile).astype(jnp.int32).reshape(1)
    return tile_expert, row_token, n_active, slot.astype(jnp.int32)


MOE_TILE = 256
MOE_FF_SPLIT = 2
GATHER_UNROLL = 8


def _experts_kernel(te_ref, tok_ref, nact_ref, h_hbm, wg_ref, wu_ref, wd_ref, o_ref, xbuf, sem):
    j = pl.program_id(0)
    n_act = nact_ref[0]
    slot = j % 2
    tile = o_ref.shape[0]

    def row_copy(tok, r, s):
        return pltpu.make_async_copy(h_hbm.at[pl.ds(tok, 1), :], xbuf.at[s, pl.ds(r, 1), :], sem.at[s])

    def start_gather(t, s):
        def body(r, carry):
            row_copy(tok_ref[t * tile + r], r, s).start()
            return carry
        lax.fori_loop(0, tile, body, 0, unroll=GATHER_UNROLL)

    def wait_gather(s):
        pltpu.make_async_copy(h_hbm.at[pl.ds(0, tile), :], xbuf.at[s], sem.at[s]).wait()

    @pl.when(j < n_act)
    def _():
        @pl.when(j == 0)
        def _():
            start_gather(0, 0)

        wait_gather(slot)

        @pl.when(j + 1 < n_act)
        def _():
            start_gather(j + 1, 1 - slot)

        h = xbuf[slot].astype(BF16)
        ff = wg_ref.shape[1]
        w = ff // MOE_FF_SPLIT
        y = None
        for part in range(MOE_FF_SPLIT):
            lo = part * w
            a = _silu(_dot(h, wg_ref[:, lo:lo + w])) * _dot(h, wu_ref[:, lo:lo + w])
            yp = _dot(a.astype(BF16), wd_ref[lo:lo + w, :])
            y = yp if y is None else y + yp
        o_ref[...] = y

    @pl.when(j >= n_act)
    def _():
        o_ref[...] = jnp.zeros_like(o_ref)


def _experts(h, tile_expert, row_token, n_active, wg, wu, wd):
    n_e, d, ff = wg.shape
    tile = MOE_TILE
    n_rows = row_token.shape[0]
    assert ff % (MOE_FF_SPLIT * LANE) == 0
    resident = pl.Buffered(1)
    grid_spec = pltpu.PrefetchScalarGridSpec(
        num_scalar_prefetch=3,
        grid=(n_rows // tile,),
        in_specs=[pl.BlockSpec(memory_space=pl.ANY),
                  pl.BlockSpec((None, d, ff), lambda j, te, tok, na: (te[j], 0, 0), pipeline_mode=resident),
                  pl.BlockSpec((None, d, ff), lambda j, te, tok, na: (te[j], 0, 0), pipeline_mode=resident),
                  pl.BlockSpec((None, ff, d), lambda j, te, tok, na: (te[j], 0, 0), pipeline_mode=resident)],
        out_specs=pl.BlockSpec((tile, d), lambda j, te, tok, na: (j, 0)),
        scratch_shapes=[pltpu.VMEM((2, tile, d), F32), pltpu.SemaphoreType.DMA((2,))],
    )
    return pl.pallas_call(
        _experts_kernel,
        grid_spec=grid_spec,
        out_shape=jax.ShapeDtypeStruct((n_rows, d), F32),
        compiler_params=_params(("arbitrary",), 56),
        name="moe_experts",
    )(tile_expert, row_token, n_active, h, wg, wu, wd)


COMBINE_TILE = 256


def _combine_kernel(slot_ref, y_hbm, x_ref, route_ref, gate_ref, gf_ref, o_ref, ybuf, sem, *, final_norm):
    i = pl.program_id(0)
    n = pl.num_programs(0)
    buf = i % 2
    tile = x_ref.shape[0]

    def row_copy(row, k, r, s):
        return pltpu.make_async_copy(y_hbm.at[pl.ds(row, 1), :], ybuf.at[s, pl.ds(k * tile + r, 1), :], sem.at[s])

    def start_gather(t, s):
        def body(r, carry):
            base = 2 * (t * tile + r)
            row_copy(slot_ref[base], 0, r, s).start()
            row_copy(slot_ref[base + 1], 1, r, s).start()
            return carry
        lax.fori_loop(0, tile, body, 0, unroll=GATHER_UNROLL // 2)

    def wait_gather(s):
        pltpu.make_async_copy(y_hbm.at[pl.ds(0, 2 * tile), :], ybuf.at[s], sem.at[s]).wait()

    @pl.when(i == 0)
    def _():
        start_gather(0, 0)

    wait_gather(buf)

    @pl.when(i + 1 < n)
    def _():
        start_gather(i + 1, 1 - buf)

    route = route_ref[...]
    p1 = route[:, ROUTE_P1:ROUTE_P1 + 1]
    p2 = route[:, ROUTE_P2:ROUTE_P2 + 1]
    y = p1 * ybuf[buf, 0:tile, :] + p2 * ybuf[buf, tile:2 * tile, :]
    out = x_ref[...] + gate_ref[...] * y
    if final_norm:
        out = (out * lax.rsqrt(jnp.mean(out * out, axis=-1, keepdims=True) + NORM_EPS)) * gf_ref[...]
    o_ref[...] = out


def _combine(y_rows, slot, x, route, mods, n_tokens, cls_of_tile, final_g):
    d = x.shape[1]
    tile = COMBINE_TILE
    per = ROW_TILE // tile
    final_norm = final_g is not None
    if not final_norm:
        final_g = jnp.ones((d,), F32)
    grid_spec = pltpu.PrefetchScalarGridSpec(
        num_scalar_prefetch=1,
        grid=(n_tokens // tile,),
        in_specs=[pl.BlockSpec(memory_space=pl.ANY),
                  pl.BlockSpec((tile, d), lambda i, s: (i, 0)),
                  pl.BlockSpec((tile, LANE), lambda i, s: (i, 0)),
                  pl.BlockSpec((None, 1, d), lambda i, s: (cls_of_tile(i // per), 0, 5)),
                  pl.BlockSpec((1, d), lambda i, s: (0, 0))],
        out_specs=pl.BlockSpec((tile, d), lambda i, s: (i, 0)),
        scratch_shapes=[pltpu.VMEM((2, 2 * tile, d), F32), pltpu.SemaphoreType.DMA((2,))],
    )
    return pl.pallas_call(
        functools.partial(_combine_kernel, final_norm=final_norm),
        grid_spec=grid_spec,
        out_shape=jax.ShapeDtypeStruct((n_tokens, d), F32),
        compiler_params=_params(("arbitrary",), 48),
        name="moe_combine",
    )(slot, y_rows, x, route, mods, final_g.reshape(1, d))


def _final_kernel(x_ref, g_ref, o_ref):
    x = x_ref[...]
    o_ref[...] = (x * lax.rsqrt(jnp.mean(x * x, axis=-1, keepdims=True) + NORM_EPS)) * g_ref[...]


def _final_norm(x, g, n_tiles):
    d = x.shape[1]
    return pl.pallas_call(
        _final_kernel,
        grid=(n_tiles,),
        in_specs=[pl.BlockSpec((ROW_TILE, d), lambda i: (i, 0)),
                  pl.BlockSpec((1, d), lambda i: (0, 0))],
        out_specs=pl.BlockSpec((ROW_TILE, d), lambda i: (i, 0)),
        out_shape=jax.ShapeDtypeStruct((n_tiles * ROW_TILE, d), F32),
        compiler_params=_params(("parallel",), 40),
        name="final_norm",
    )(x, g.reshape(1, d))


def _pad_lanes(v, width=LANE):
    return jnp.pad(v, [(0, 0)] * (v.ndim - 1) + [(0, width - v.shape[-1])])


def kernel(x, c, ctx, c_ctx, w_mod, b_mod, g_mix, g_ffn, w_in, conv_w, ssd_conv_w, ssd_conv_b, ssd_a_log, ssd_dt_bias, ssd_d, ssd_norm, da_lambda, da_subln, w_out, ffn_w_gate, ffn_w_up, ffn_w_down, moe_w_router, moe_b_router, moe_w_gate, moe_w_up, moe_w_down, g_final):
    batch, seq, d = x.shape
    ctx_len = ctx.shape[1]
    depth = w_mod.shape[0]
    n_lat = batch * seq
    m = n_lat + batch * ctx_len
    assert seq % ROW_TILE == 0 and (batch * ctx_len) % ROW_TILE == 0 and n_lat % ctx_len == 0
    assert ctx_len % HALO_TILE == 0 and ctx_len % SSD_CHUNK == 0 and batch < MOD_CLASSES
    n_lat_tiles = n_lat // ROW_TILE
    n_all_tiles = m // ROW_TILE
    tiles_per_seq = seq // ROW_TILE
    cls_of_tile = lambda i: jnp.minimum(i // tiles_per_seq, batch)

    x_lat, x_ctx = x.reshape(n_lat, d), ctx.reshape(batch * ctx_len, d)
    cvec = jnp.zeros((MOD_CLASSES, d), F32).at[:batch].set(c).at[batch].set(c_ctx)
    mods_all = _mod_vectors(cvec, w_mod, b_mod)
    cos_t, sin_t = _rope_tables(seq)

    for i in range(depth):
        ctx_out = i < depth - 1
        lam_init = 0.8 - 0.6 * math.exp(-0.3 * i)
        n_tiles = n_all_tiles if ctx_out else n_lat_tiles
        mods = mods_all[i].reshape(MOD_CLASSES, 1, N_MOD * d)
        wi = w_in[i]
        w_a = jnp.concatenate([wi[:, COL_CONV:COL_Q], wi[:, COL_XBC:COL_DT]], axis=1).astype(BF16)
        w_b = jnp.concatenate([wi[:, COL_Q:COL_XBC] * (DA_SCALE * LOG2_E), wi[:, COL_K:COL_V],
                               wi[:, COL_V:COL_V + DA_WIDTH]], axis=1).astype(BF16)
        w_c = _pad_lanes(wi[:, COL_DT:COL_K]).astype(BF16)

        h = _norm_modulate(x_lat, x_ctx, g_mix[i], mods, 0, n_all_tiles, n_lat_tiles, cls_of_tile)
        p_a = _matmul(h, w_a, F32, 1024, "in_proj_conv_z_xbc")
        qkv, v_t = _matmul_rope(h, w_b, cos_t, sin_t, n_lat_tiles, tiles_per_seq)
        p_dt = _matmul(h, w_c, F32, LANE, "in_proj_dt")

        ya, xbc, dt2 = _prep(p_a, p_dt, conv_w[i], ssd_conv_w[i], ssd_conv_b[i].reshape(1, -1),
                             _pad_lanes(ssd_dt_bias[i].reshape(1, -1)), n_lat // HALO_TILE,
                             seq // HALO_TILE, ctx_len // HALO_TILE)
        y_fwd, y_bwd = _ssd_scan(xbc, dt2, ssd_a_log[i], batch, seq // SSD_CHUNK, ctx_len // SSD_CHUNK)
        yb = _ssd_gate(y_fwd, y_bwd, xbc, p_a, jnp.repeat(ssd_d[i], SSD_HEAD_DIM).reshape(1, -1),
                       ssd_norm[i].reshape(1, -1), n_tiles)

        subln = da_subln[i].reshape(-1, 1)
        yc = _attention_lat(qkv, v_t, da_lambda[i], subln, lam_init, batch, seq, ctx_len)
        yc_ctx = _attention_ctx(qkv, v_t, da_lambda[i], subln, lam_init, batch, seq, ctx_len) if ctx_out else None

        x_mid = _out_proj(ya, yb, yc, yc_ctx, w_out[i].astype(BF16), x_lat, x_ctx, mods, n_tiles, n_lat_tiles,
                          cls_of_tile)

        j = i // 2
        if i % 2 == 0:
            xs_all = _ffn_dense(x_mid, g_ffn[i], mods, ffn_w_gate[j].astype(BF16), ffn_w_up[j].astype(BF16),
                                ffn_w_down[j].astype(BF16), n_tiles, cls_of_tile)
        else:
            hh, route = _router(x_mid, g_ffn[i], mods, _pad_lanes(moe_w_router[j]),
                                _pad_lanes(moe_b_router[j].reshape(1, -1)), n_tiles, cls_of_tile)
            tile_expert, row_token, n_active, slot = _routing_tables(route, moe_w_gate.shape[1], MOE_TILE)
            y_rows = _experts(hh, tile_expert, row_token, n_active, moe_w_gate[j].astype(BF16),
                              moe_w_up[j].astype(BF16), moe_w_down[j].astype(BF16))
            xs_all = _combine(y_rows, slot, x_mid, route, mods, n_tiles * ROW_TILE, cls_of_tile,
                              None if ctx_out else g_final)
        x_lat = xs_all
        x_ctx = xs_all[n_lat:] if ctx_out else None

    last_is_moe = depth % 2 == 0
    out = x_lat if last_is_moe else _final_norm(x_lat, g_final, n_lat_tiles)
    return out.reshape(batch, seq, d)
```

```python
import functools
import math

import jax
import jax.numpy as jnp
from jax import lax
from jax.experimental import pallas as pl
from jax.experimental.pallas import tpu as pltpu

NORM_EPS = 1e-6
N_MOD = 6
GRID_W = 64

SHORT_CONV = 3
CONV_W = 512

SSD_HEADS = 8
SSD_HEAD_DIM = 64
SSD_INNER = SSD_HEADS * SSD_HEAD_DIM
SSD_STATE = 128
SSD_GROUPS = 2
SSD_CHUNK = 128
SSD_XBC = SSD_INNER + 2 * SSD_GROUPS * SSD_STATE
SSD_GROUP_W = SSD_INNER // SSD_GROUPS

DA_HEADS = 8
DA_HEAD_DIM = 64
DA_V_DIM = 2 * DA_HEAD_DIM
DA_QK = DA_HEADS * 2 * DA_HEAD_DIM
DA_WIDTH = DA_HEADS * DA_V_DIM
DA_SCALE = DA_HEAD_DIM ** -0.5
LOG2_E = math.log2(math.e)
ROPE_THETA = 10000.0

COL_CONV = 0
COL_Z = COL_CONV + 3 * CONV_W
COL_Q = COL_Z + SSD_INNER
COL_XBC = COL_Q + DA_QK
COL_DT = COL_XBC + SSD_XBC
COL_K = COL_DT + 2 * SSD_HEADS
COL_V = COL_K + DA_QK

N_EXPERTS = 8

LANE = 128
SUBLANE = 8
ROW_TILE = 512
HALO_TILE = 256
MOD_CLASSES = 8

F32 = jnp.float32
BF16 = jnp.bfloat16
HIGHEST = lax.Precision.HIGHEST
MIB = 1024 * 1024


def _params(semantics, vmem_mib=None):
    kw = {"dimension_semantics": semantics}
    if vmem_mib is not None:
        kw["vmem_limit_bytes"] = vmem_mib * MIB
    return pltpu.CompilerParams(**kw)


def _silu(v):
    return v * jax.nn.sigmoid(v)


def _dot(a, b):
    return jnp.dot(a, b, preferred_element_type=F32)


def _dot_nt(a, b):
    return lax.dot_general(a, b, (((1,), (1,)), ((), ())), preferred_element_type=F32)


def _dot_tn(a, b):
    return lax.dot_general(a, b, (((0,), (0,)), ((), ())), preferred_element_type=F32)


def _split2(v):
    hi = v.astype(BF16)
    return hi, (v - hi.astype(F32)).astype(BF16)


def _dot_3pass(a, b):
    a_hi, a_lo = _split2(a)
    b_hi, b_lo = _split2(b)
    return _dot(a_hi, b_hi) + (_dot(a_hi, b_lo) + _dot(a_lo, b_hi))


def _modulated_norm(x, g, shift, scale):
    ms = jnp.mean(x * x, axis=-1, keepdims=True)
    y = x * lax.rsqrt(ms + NORM_EPS)
    return (y * g) * (1.0 + scale) + shift


def _mod_spec(k, width, cls_of_tile):
    return pl.BlockSpec((None, 1, width), lambda i, *_: (cls_of_tile(i), 0, k))


def _mod_kernel(c_ref, w_ref, b_ref, o_ref):
    s = _silu(c_ref[...])
    o_ref[...] = _dot_3pass(s, w_ref[...]) + b_ref[...]


def _mod_vectors(cvec, w_mod, b_mod):
    depth, d, n = w_mod.shape
    tn = 1024
    return pl.pallas_call(
        _mod_kernel,
        grid=(depth, n // tn),
        in_specs=[pl.BlockSpec((MOD_CLASSES, d), lambda l, j: (0, 0)),
                  pl.BlockSpec((None, d, tn), lambda l, j: (l, 0, j)),
                  pl.BlockSpec((None, 1, tn), lambda l, j: (l, 0, j))],
        out_specs=pl.BlockSpec((None, MOD_CLASSES, tn), lambda l, j: (l, 0, j)),
        out_shape=jax.ShapeDtypeStruct((depth, MOD_CLASSES, n), F32),
        compiler_params=_params(("parallel", "parallel"), 40),
        name="mod_vectors",
    )(cvec, w_mod, b_mod.reshape(depth, 1, n))


def _lat_ctx_specs(width, n_lat_tiles):
    return [pl.BlockSpec((ROW_TILE, width), lambda i, *_: (jnp.minimum(i, n_lat_tiles - 1), 0)),
            pl.BlockSpec((ROW_TILE, width), lambda i, *_: (jnp.maximum(i - n_lat_tiles, 0), 0))]


def _lat_or_ctx(lat_ref, ctx_ref, n_lat_tiles):
    return jnp.where(pl.program_id(0) < n_lat_tiles, lat_ref[...], ctx_ref[...])


def _norm_kernel(xl_ref, xc_ref, g_ref, sh_ref, sc_ref, h_ref, *, n_lat_tiles):
    x = _lat_or_ctx(xl_ref, xc_ref, n_lat_tiles)
    h_ref[...] = _modulated_norm(x, g_ref[...], sh_ref[...], sc_ref[...]).astype(h_ref.dtype)


def _norm_modulate(x_lat, x_ctx, g, mods, k_shift, n_tiles, n_lat_tiles, cls_of_tile):
    d = x_lat.shape[1]
    return pl.pallas_call(
        functools.partial(_norm_kernel, n_lat_tiles=n_lat_tiles),
        grid=(n_tiles,),
        in_specs=_lat_ctx_specs(d, n_lat_tiles) + [
            pl.BlockSpec((1, d), lambda i: (0, 0)),
            _mod_spec(k_shift, d, cls_of_tile),
            _mod_spec(k_shift + 1, d, cls_of_tile)],
        out_specs=pl.BlockSpec((ROW_TILE, d), lambda i: (i, 0)),
        out_shape=jax.ShapeDtypeStruct((n_tiles * ROW_TILE, d), BF16),
        compiler_params=_params(("parallel",), 40),
        name="norm_modulate",
    )(x_lat, x_ctx, g.reshape(1, d), mods, mods)


def _mm_kernel(x_ref, w_ref, o_ref):
    o_ref[...] = _dot(x_ref[...], w_ref[...]).astype(o_ref.dtype)


def _matmul(x, w, out_dtype, tn, name):
    m, k = x.shape
    n = w.shape[1]
    return pl.pallas_call(
        _mm_kernel,
        grid=(m // ROW_TILE, n // tn),
        in_specs=[pl.BlockSpec((ROW_TILE, k), lambda i, j: (i, 0)),
                  pl.BlockSpec((k, tn), lambda i, j: (0, j))],
        out_specs=pl.BlockSpec((ROW_TILE, tn), lambda i, j: (i, j)),
        out_shape=jax.ShapeDtypeStruct((m, n), out_dtype),
        compiler_params=_params(("parallel", "parallel"), 40),
        name=name,
    )(x, w)


def _mm_rope_kernel(x_ref, w_ref, cos_ref, sin_ref, qk_ref, vt_ref, *, n_lat_tiles, n_rope_tiles):
    i = pl.program_id(0)
    j = pl.program_id(1)
    acc = _dot(x_ref[...], w_ref[...])
    tm, tn = acc.shape
    is_qk = j < n_rope_tiles
    rotate = jnp.logical_and(i < n_lat_tiles, is_qk)

    @pl.when(rotate)
    def _():
        lane = lax.broadcasted_iota(jnp.int32, (tm, LANE), 1)
        first = (lane % (DA_HEAD_DIM // 2)) < (DA_HEAD_DIM // 4)
        c = cos_ref[...]
        s = sin_ref[...]
        quarter = DA_HEAD_DIM // 4
        for hb in range(tn // LANE):
            blk = acc[:, hb * LANE:(hb + 1) * LANE]
            partner = jnp.where(first, pltpu.roll(blk, LANE - quarter, 1), pltpu.roll(blk, quarter, 1))
            qk_ref[:, hb * LANE:(hb + 1) * LANE] = (blk * c + partner * s).astype(qk_ref.dtype)

    @pl.when(jnp.logical_and(is_qk, jnp.logical_not(rotate)))
    def _():
        qk_ref[...] = acc.astype(qk_ref.dtype)

    @pl.when(jnp.logical_not(is_qk))
    def _():
        row = lax.broadcasted_iota(jnp.int32, (VT_ROWS - DA_V_DIM, tm), 0)
        tail = jnp.where(row == 0, 1.0, 0.0).astype(vt_ref.dtype)
        for h in range(DA_HEADS):
            lo = h * VT_ROWS
            vt_ref[lo:lo + DA_V_DIM, :] = acc[:, h * DA_V_DIM:(h + 1) * DA_V_DIM].T.astype(vt_ref.dtype)
            vt_ref[lo + DA_V_DIM:lo + VT_ROWS, :] = tail


def _matmul_rope(x, w, cos_t, sin_t, n_lat_tiles, tiles_per_seq):
    m, k = x.shape
    tn = DA_QK
    n_rope_tiles = 2 * DA_QK // tn
    assert w.shape[1] == 2 * DA_QK + DA_WIDTH and tn == DA_WIDTH
    kern = functools.partial(_mm_rope_kernel, n_lat_tiles=n_lat_tiles, n_rope_tiles=n_rope_tiles)
    return pl.pallas_call(
        kern,
        grid=(m // ROW_TILE, n_rope_tiles + 1),
        in_specs=[pl.BlockSpec((ROW_TILE, k), lambda i, j: (i, 0)),
                  pl.BlockSpec((k, tn), lambda i, j: (0, j)),
                  pl.BlockSpec((ROW_TILE, LANE), lambda i, j: (i % tiles_per_seq, 0)),
                  pl.BlockSpec((ROW_TILE, LANE), lambda i, j: (i % tiles_per_seq, 0))],
        out_specs=[pl.BlockSpec((ROW_TILE, tn), lambda i, j: (i, jnp.minimum(j, n_rope_tiles - 1))),
                   pl.BlockSpec((DA_HEADS * VT_ROWS, ROW_TILE), lambda i, j: (0, i))],
        out_shape=[jax.ShapeDtypeStruct((m, 2 * DA_QK), BF16),
                   jax.ShapeDtypeStruct((DA_HEADS * VT_ROWS, m), BF16)],
        compiler_params=_params(("parallel", "arbitrary"), 40),
        name="qkv_proj_rope",
    )(x, w, cos_t, sin_t)


def _rope_tables(n_tokens):
    rows = n_tokens // GRID_W
    row = jnp.repeat(jnp.arange(rows, dtype=F32), GRID_W)
    col = jnp.broadcast_to(jnp.arange(GRID_W, dtype=F32), (rows, GRID_W)).reshape(-1)
    n_freq = DA_HEAD_DIM // 4
    inv = ROPE_THETA ** (-jnp.arange(n_freq, dtype=F32) / n_freq)
    ang_r = row[:, None] * inv
    ang_c = col[:, None] * inv
    cos64 = jnp.concatenate([jnp.cos(ang_r), jnp.cos(ang_r), jnp.cos(ang_c), jnp.cos(ang_c)], axis=-1)
    sin64 = jnp.concatenate([-jnp.sin(ang_r), jnp.sin(ang_r), -jnp.sin(ang_c), jnp.sin(ang_c)], axis=-1)
    return jnp.tile(cos64, (1, 2)), jnp.tile(sin64, (1, 2))


def _conv3(u, prev_row, next_row, w_ref, has_prev, has_next):
    tm = u.shape[0]
    row = lax.broadcasted_iota(jnp.int32, u.shape, 0)
    prev_row = jnp.where(has_prev, prev_row, 0.0)
    next_row = jnp.where(has_next, next_row, 0.0)
    before = jnp.where(row == 0, prev_row, pltpu.roll(u, 1, 0))
    after = jnp.where(row == tm - 1, next_row, pltpu.roll(u, tm - 1, 0))
    return before * w_ref[0:1, :] + u * w_ref[1:2, :] + after * w_ref[2:3, :]


def _prep_kernel(cv_ref, cvp_ref, cvn_ref, xb_ref, xbp_ref, xbn_ref, dtr_ref,
                 cw_ref, sw_ref, sb_ref, dtb_ref, ya_ref, xs_ref, dt_ref,
                 *, n_lat_tiles, lat_tiles_per_seq, ctx_tiles_per_seq):
    i = pl.program_id(0)
    is_lat = i < n_lat_tiles
    pos = jnp.where(is_lat, i % lat_tiles_per_seq, (i - n_lat_tiles) % ctx_tiles_per_seq)
    last = jnp.where(is_lat, lat_tiles_per_seq - 1, ctx_tiles_per_seq - 1)
    has_prev = pos != 0
    has_next = pos != last

    w = CONV_W
    cv = cv_ref[...]
    gate_b, u = cv[:, 0:w], cv[:, w:2 * w] * cv[:, 2 * w:3 * w]
    p = cvp_ref[SUBLANE - 1:SUBLANE, :]
    n = cvn_ref[0:1, :]
    u_prev = p[:, w:2 * w] * p[:, 2 * w:3 * w]
    u_next = n[:, w:2 * w] * n[:, 2 * w:3 * w]
    ya_ref[...] = (gate_b * _conv3(u, u_prev, u_next, cw_ref, has_prev, has_next)).astype(ya_ref.dtype)

    xc = _conv3(xb_ref[...], xbp_ref[SUBLANE - 1:SUBLANE, :], xbn_ref[0:1, :], sw_ref, has_prev, has_next)
    xs_ref[...] = _silu(xc + sb_ref[...])

    t = dtr_ref[...] + dtb_ref[...]
    dt = jnp.maximum(t, 0.0) + jnp.log1p(jnp.exp(-jnp.abs(t)))
    dt_ref[0] = dt
    dt_ref[1] = pltpu.roll(dt, LANE - SSD_HEADS, 1)


def _prep(p_a, p_dt, conv_w, ssd_conv_w, ssd_conv_b, dt_bias, n_lat_tiles, lat_tiles_per_seq,
          ctx_tiles_per_seq):
    m = p_a.shape[0]
    tm = HALO_TILE
    per = tm // SUBLANE
    n8 = m // SUBLANE
    cw = 3 * CONV_W
    xbc_blk = COL_Q // SSD_XBC
    prev = lambda i: jnp.maximum(i * per - 1, 0)
    nxt = lambda i: jnp.minimum((i + 1) * per, n8 - 1)
    kern = functools.partial(_prep_kernel, n_lat_tiles=n_lat_tiles, lat_tiles_per_seq=lat_tiles_per_seq,
                             ctx_tiles_per_seq=ctx_tiles_per_seq)
    return pl.pallas_call(
        kern,
        grid=(m // tm,),
        in_specs=[pl.BlockSpec((tm, cw), lambda i: (i, 0)),
                  pl.BlockSpec((SUBLANE, cw), lambda i: (prev(i), 0)),
                  pl.BlockSpec((SUBLANE, cw), lambda i: (nxt(i), 0)),
                  pl.BlockSpec((tm, SSD_XBC), lambda i: (i, xbc_blk)),
                  pl.BlockSpec((SUBLANE, SSD_XBC), lambda i: (prev(i), xbc_blk)),
                  pl.BlockSpec((SUBLANE, SSD_XBC), lambda i: (nxt(i), xbc_blk)),
                  pl.BlockSpec((tm, LANE), lambda i: (i, 0)),
                  pl.BlockSpec((SHORT_CONV, CONV_W), lambda i: (0, 0)),
                  pl.BlockSpec((SHORT_CONV, SSD_XBC), lambda i: (0, 0)),
                  pl.BlockSpec((1, SSD_XBC), lambda i: (0, 0)),
                  pl.BlockSpec((1, LANE), lambda i: (0, 0))],
        out_specs=[pl.BlockSpec((tm, CONV_W), lambda i: (i, 0)),
                   pl.BlockSpec((tm, SSD_XBC), lambda i: (i, 0)),
                   pl.BlockSpec((2, tm, LANE), lambda i: (0, i, 0))],
        out_shape=[jax.ShapeDtypeStruct((m, CONV_W), BF16),
                   jax.ShapeDtypeStruct((m, SSD_XBC), F32),
                   jax.ShapeDtypeStruct((2, m, LANE), F32)],
        compiler_params=_params(("parallel",), 40),
        name="conv_prep",
    )(p_a, p_a, p_a, p_a, p_a, p_a, p_dt, conv_w, ssd_conv_w, ssd_conv_b, dt_bias)


def _split3(v):
    hi = v.astype(BF16)
    rest = v - hi.astype(F32)
    mid = rest.astype(BF16)
    return hi, mid, (rest - mid.astype(F32)).astype(BF16)


def _dot_f32_lhs(a, b01):
    return sum(_dot(piece, b01) for piece in _split3(a))


def _dot_f32_rhs(a01, b):
    return sum(_dot(a01, piece) for piece in _split3(b))


def _ssd_chunk(direction, xs_ref, dt_ref, alog_ref, alog_ch_ref, y_ref, state_ref):
    t = SSD_CHUNK
    r = lax.broadcasted_iota(jnp.int32, (t, t), 0)
    c = lax.broadcasted_iota(jnp.int32, (t, t), 1)
    mask = (r >= c) if direction == 0 else (r <= c)
    tri = mask.astype(BF16)

    er = lax.broadcasted_iota(jnp.int32, (LANE, SSD_INNER), 0)
    ec = lax.broadcasted_iota(jnp.int32, (LANE, SSD_INNER), 1)
    expand = (jnp.right_shift(ec, 6) == er).astype(BF16)

    dt = dt_ref[direction]
    cs = _dot_f32_rhs(tri, dt * (-jnp.exp(alog_ref[direction])))
    cs_t = cs.T
    dt_ch = _dot_f32_lhs(dt, expand)
    da_ch = dt_ch * (-jnp.exp(alog_ch_ref[direction]))
    cs_ch = _dot_f32_rhs(tri, da_ch)
    tot_ch = jnp.sum(da_ch, axis=0, keepdims=True)

    xdt = xs_ref[:, 0:SSD_INNER] * dt_ch
    x_state = xdt * jnp.exp(tot_ch - cs_ch)
    y_scale = jnp.exp(cs_ch)
    carry = jnp.exp(tot_ch)
    gw = SSD_GROUP_W
    heads_per_group = SSD_HEADS // SSD_GROUPS
    lane_head = jnp.right_shift(lax.broadcasted_iota(jnp.int32, (t, gw), 1), 6)

    for g in range(SSD_GROUPS):
        b_lo = SSD_INNER + g * SSD_STATE
        c_lo = SSD_INNER + SSD_GROUPS * SSD_STATE + g * SSD_STATE
        bg = xs_ref[:, b_lo:b_lo + SSD_STATE].astype(BF16)
        cg = xs_ref[:, c_lo:c_lo + SSD_STATE].astype(BF16)
        cb = _dot_nt(cg, bg)
        sg = state_ref[direction, g]
        xdt_g = xdt[:, g * gw:(g + 1) * gw]
        y = _dot(cg, sg.astype(BF16)) * y_scale[:, g * gw:(g + 1) * gw]
        for hh in range(heads_per_group):
            h = g * heads_per_group + hh
            decay = jnp.where(mask, jnp.exp(cs[:, h:h + 1] - cs_t[h:h + 1, :]), 0.0)
            x_h = jnp.where(lane_head == hh, xdt_g, 0.0).astype(BF16)
            y = y + _dot((cb * decay).astype(BF16), x_h)
        y_ref[:, g * gw:(g + 1) * gw] = y
        ds = _dot_tn(bg, x_state[:, g * gw:(g + 1) * gw].astype(BF16))
        state_ref[direction, g] = sg * carry[:, g * gw:(g + 1) * gw] + ds


def _ssd_kernel(xf_ref, dtf_ref, xb_ref, dtb_ref, alog_ref, alog_ch_ref, yf_ref, yb_ref, state_ref):
    @pl.when(pl.program_id(1) == 0)
    def _():
        state_ref[...] = jnp.zeros_like(state_ref)

    _ssd_chunk(0, xf_ref, dtf_ref, alog_ref, alog_ch_ref, yf_ref, state_ref)
    _ssd_chunk(1, xb_ref, dtb_ref, alog_ref, alog_ch_ref, yb_ref, state_ref)


def _ssd_scan(xs, dt2, a_log, batch, n_lat_chunks, n_ctx_chunks):
    m = xs.shape[0]
    t = SSD_CHUNK
    assert SSD_HEAD_DIM == 64 and SSD_HEADS <= LANE
    a_log_ch = jnp.repeat(a_log, SSD_HEAD_DIM, axis=1).reshape(2, 1, SSD_INNER)
    a_log = _pad_lanes(a_log).reshape(2, 1, LANE)

    def chunk(b, d, s):
        j_ctx = s if d == 0 else n_ctx_chunks - 1 - s
        sl = s - n_ctx_chunks
        j_lat = sl if d == 0 else n_lat_chunks - 1 - sl
        return jnp.where(s < n_ctx_chunks, batch * n_lat_chunks + b * n_ctx_chunks + j_ctx,
                         b * n_lat_chunks + j_lat)

    x_spec = lambda d: pl.BlockSpec((t, SSD_XBC), lambda b, s: (chunk(b, d, s), 0))
    dt_spec = lambda d: pl.BlockSpec((2, t, LANE), lambda b, s: (0, chunk(b, d, s), 0))
    y_spec = lambda d: pl.BlockSpec((t, SSD_INNER), lambda b, s: (chunk(b, d, s), 0))
    return pl.pallas_call(
        _ssd_kernel,
        grid=(batch, n_ctx_chunks + n_lat_chunks),
        in_specs=[x_spec(0), dt_spec(0), x_spec(1), dt_spec(1),
                  pl.BlockSpec((2, 1, LANE), lambda b, s: (0, 0, 0)),
                  pl.BlockSpec((2, 1, SSD_INNER), lambda b, s: (0, 0, 0))],
        out_specs=[y_spec(0), y_spec(1)],
        out_shape=[jax.ShapeDtypeStruct((m, SSD_INNER), F32), jax.ShapeDtypeStruct((m, SSD_INNER), F32)],
        scratch_shapes=[pltpu.VMEM((2, SSD_GROUPS, SSD_STATE, SSD_GROUP_W), F32)],
        compiler_params=_params(("parallel", "arbitrary")),
        name="ssd_scan",
    )(xs, dt2, xs, dt2, a_log, a_log_ch)


def _ssd_gate_kernel(yf_ref, yb_ref, xs_ref, z_ref, d_ref, g_ref, o_ref):
    yl = d_ref[...] * xs_ref[...] + yf_ref[...] + yb_ref[...]
    u = yl * _silu(z_ref[...])
    gw = SSD_GROUP_W
    for g in range(SSD_GROUPS):
        ug = u[:, g * gw:(g + 1) * gw]
        ug = ug * lax.rsqrt(jnp.mean(ug * ug, axis=-1, keepdims=True) + NORM_EPS)
        o_ref[:, g * gw:(g + 1) * gw] = (ug * g_ref[:, g * gw:(g + 1) * gw]).astype(o_ref.dtype)


def _ssd_gate(y_fwd, y_bwd, xs, p_a, d_row, norm_g, n_tiles):
    w = SSD_INNER
    return pl.pallas_call(
        _ssd_gate_kernel,
        grid=(n_tiles,),
        in_specs=[pl.BlockSpec((ROW_TILE, w), lambda i: (i, 0)),
                  pl.BlockSpec((ROW_TILE, w), lambda i: (i, 0)),
                  pl.BlockSpec((ROW_TILE, w), lambda i: (i, 0)),
                  pl.BlockSpec((ROW_TILE, w), lambda i: (i, COL_Z // w)),
                  pl.BlockSpec((1, w), lambda i: (0, 0)),
                  pl.BlockSpec((1, w), lambda i: (0, 0))],
        out_specs=pl.BlockSpec((ROW_TILE, w), lambda i: (i, 0)),
        out_shape=jax.ShapeDtypeStruct((n_tiles * ROW_TILE, w), BF16),
        compiler_params=_params(("parallel",)),
        name="ssd_gate_norm",
    )(y_fwd, y_bwd, xs, p_a, d_row, norm_g)


ATT_TQ = 1024
ATT_TKC = 1024


def _attn_kernel(*refs, has_lat, lam_init):
    if has_lat:
        (q_ref, kc_ref, vct_ref, kl_ref, vlt_ref, lam_ref, g_ref, o_ref,
         m_ref, acc_ref, sa_ref, sb_ref) = refs
    else:
        q_ref, kc_ref, vct_ref, lam_ref, g_ref, o_ref, m_ref, acc_ref, sa_ref, sb_ref = refs

    q = q_ref[...]
    lane = lax.broadcasted_iota(jnp.int32, q.shape, 1)
    zero = jnp.zeros_like(q)
    q_maps = (jnp.where(lane < DA_HEAD_DIM, q, zero), jnp.where(lane >= DA_HEAD_DIM, q, zero))

    def scores(k, s_ref):
        n = k.shape[0]
        maxima = []
        for mp in range(2):
            s = _dot_nt(k, q_maps[mp])
            s_ref[mp, 0:n, :] = s
            maxima.append(jnp.max(s, axis=0, keepdims=True))
        return tuple(maxima)

    def update(s_ref, maxima, vt):
        n = vt.shape[1]
        for mp in range(2):
            m_old = m_ref[mp]
            m_new = jnp.maximum(m_old, maxima[mp])
            alpha = jnp.exp2(m_old - m_new)
            p = jnp.exp2(s_ref[mp, 0:n, :] - m_new[0:1, :])
            acc_ref[mp] = alpha[0:1, :] * acc_ref[mp] + _dot(vt, p.astype(vt.dtype))
            m_ref[mp] = m_new

    m_ref[...] = jnp.full(m_ref.shape, -jnp.inf, F32)
    acc_ref[...] = jnp.zeros_like(acc_ref)
    mx = scores(kc_ref[...], sa_ref)

    if not has_lat:
        update(sa_ref, mx, vct_ref[...])
    else:
        tkc = min(ATT_TKC, kl_ref.shape[0])
        n_chunks = kl_ref.shape[0] // tkc
        k_at = lambda c: kl_ref[pl.ds(pl.multiple_of(c * tkc, tkc), tkc), :]
        vt_at = lambda c: vlt_ref[:, pl.ds(pl.multiple_of(c * tkc, tkc), tkc)]

        mx_ctx = mx
        mx = scores(k_at(0), sb_ref)
        update(sa_ref, mx_ctx, vct_ref[...])

        def pair(c2, mx_b):
            c = 1 + 2 * c2
            mx_a = scores(k_at(c), sa_ref)
            update(sb_ref, mx_b, vt_at(c - 1))
            mx_b = scores(k_at(c + 1), sb_ref)
            update(sa_ref, mx_a, vt_at(c))
            return mx_b

        mx = lax.fori_loop(0, (n_chunks - 1) // 2, pair, mx)
        if (n_chunks - 1) % 2:
            mx_a = scores(k_at(n_chunks - 1), sa_ref)
            update(sb_ref, mx, vt_at(n_chunks - 2))
            update(sa_ref, mx_a, vt_at(n_chunks - 1))
        else:
            update(sb_ref, mx, vt_at(n_chunks - 1))

    lv = lam_ref[...]
    dotp = lambda a, b: jnp.sum(lv[a:a + 1, :] * lv[b:b + 1, :], axis=-1, keepdims=True)
    lam = jnp.exp(dotp(0, 1)) - jnp.exp(dotp(2, 3)) + lam_init
    vd = DA_V_DIM
    inv_l = 1.0 / acc_ref[:, vd:vd + 1, :]
    o = acc_ref[0, 0:vd, :] * inv_l[0] - lam * (acc_ref[1, 0:vd, :] * inv_l[1])
    y = o * lax.rsqrt(jnp.mean(o * o, axis=0, keepdims=True) + NORM_EPS)
    o_ref[...] = ((y * g_ref[...]) * (1.0 - lam_init)).T.astype(o_ref.dtype)


VT_ROWS = DA_V_DIM + 16


def _attn_scratch(tq, keys_a, keys_b):
    return [pltpu.VMEM((2, SUBLANE, tq), F32), pltpu.VMEM((2, VT_ROWS, tq), F32),
            pltpu.VMEM((2, keys_a, tq), F32), pltpu.VMEM((2, keys_b, tq), F32)]


def _attention_lat(qkv, v_t, da_lambda, subln, lam_init, batch, seq, ctx_len):
    h = DA_HEADS
    tq = ATT_TQ
    nq = seq // tq
    ctx_blk0 = (batch * seq) // ctx_len
    assert seq % tq == 0 and seq % min(ATT_TKC, seq) == 0
    kern = functools.partial(_attn_kernel, has_lat=True, lam_init=lam_init)
    return pl.pallas_call(
        kern,
        grid=(batch, h, nq),
        in_specs=[pl.BlockSpec((tq, LANE), lambda b, hh, i: (b * nq + i, hh)),
                  pl.BlockSpec((ctx_len, LANE), lambda b, hh, i: (ctx_blk0 + b, h + hh)),
                  pl.BlockSpec((VT_ROWS, ctx_len), lambda b, hh, i: (hh, ctx_blk0 + b)),
                  pl.BlockSpec((seq, LANE), lambda b, hh, i: (b, h + hh)),
                  pl.BlockSpec((VT_ROWS, seq), lambda b, hh, i: (hh, b)),
                  pl.BlockSpec(da_lambda.shape, lambda b, hh, i: (0, 0)),
                  pl.BlockSpec((DA_V_DIM, 1), lambda b, hh, i: (0, 0))],
        out_specs=pl.BlockSpec((tq, LANE), lambda b, hh, i: (b * nq + i, hh)),
        out_shape=jax.ShapeDtypeStruct((batch * seq, DA_WIDTH), BF16),
        scratch_shapes=_attn_scratch(tq, max(min(ATT_TKC, seq), ctx_len), min(ATT_TKC, seq)),
        compiler_params=_params(("parallel", "parallel", "parallel"), 48),
        name="diff_attention",
    )(qkv, qkv, v_t, qkv, v_t, da_lambda, subln)


def _attention_ctx(qkv, v_t, da_lambda, subln, lam_init, batch, seq, ctx_len):
    h = DA_HEADS
    ctx_blk0 = (batch * seq) // ctx_len
    kern = functools.partial(_attn_kernel, has_lat=False, lam_init=lam_init)
    return pl.pallas_call(
        kern,
        grid=(batch, h),
        in_specs=[pl.BlockSpec((ctx_len, LANE), lambda b, hh: (ctx_blk0 + b, hh)),
                  pl.BlockSpec((ctx_len, LANE), lambda b, hh: (ctx_blk0 + b, h + hh)),
                  pl.BlockSpec((VT_ROWS, ctx_len), lambda b, hh: (hh, ctx_blk0 + b)),
                  pl.BlockSpec(da_lambda.shape, lambda b, hh: (0, 0)),
                  pl.BlockSpec((DA_V_DIM, 1), lambda b, hh: (0, 0))],
        out_specs=pl.BlockSpec((ctx_len, LANE), lambda b, hh: (b, hh)),
        out_shape=jax.ShapeDtypeStruct((batch * ctx_len, DA_WIDTH), BF16),
        scratch_shapes=_attn_scratch(ctx_len, ctx_len, SUBLANE),
        compiler_params=_params(("parallel", "parallel")),
        name="diff_attention_ctx",
    )(qkv, qkv, v_t, da_lambda, subln)


def _outproj_kernel(ya_ref, yb_ref, yc_ref, ycx_ref, w0_ref, w1_ref, w2_ref, xl_ref, xc_ref, gate_ref, o_ref, *,
                    n_lat_tiles):
    yc = _lat_or_ctx(yc_ref, ycx_ref, n_lat_tiles)
    acc = _dot(ya_ref[...], w0_ref[...]) + _dot(yb_ref[...], w1_ref[...]) + _dot(yc, w2_ref[...])
    o_ref[...] = _lat_or_ctx(xl_ref, xc_ref, n_lat_tiles) + gate_ref[...] * acc


def _out_proj(ya, yb, yc, yc_ctx, w_out, x_lat, x_ctx, mods, n_tiles, n_lat_tiles, cls_of_tile):
    d = x_lat.shape[1]
    tn = 1024
    wa, wb = ya.shape[1], yb.shape[1]
    assert wa == wb and yc.shape[1] == wa + wb
    if yc_ctx is None:
        assert n_tiles <= n_lat_tiles
        yc_ctx = yc
    lat_tile = lambda i: jnp.minimum(i, n_lat_tiles - 1)
    ctx_tile = lambda i: jnp.maximum(i - n_lat_tiles, 0)
    return pl.pallas_call(
        functools.partial(_outproj_kernel, n_lat_tiles=n_lat_tiles),
        grid=(n_tiles, d // tn),
        in_specs=[pl.BlockSpec((ROW_TILE, wa), lambda i, j: (i, 0)),
                  pl.BlockSpec((ROW_TILE, wb), lambda i, j: (i, 0))] + _lat_ctx_specs(wa + wb, n_lat_tiles) + [
                  pl.BlockSpec((wa, tn), lambda i, j: (0, j)),
                  pl.BlockSpec((wb, tn), lambda i, j: (1, j)),
                  pl.BlockSpec((wa + wb, tn), lambda i, j: (1, j)),
                  pl.BlockSpec((ROW_TILE, tn), lambda i, j: (lat_tile(i), j)),
                  pl.BlockSpec((ROW_TILE, tn), lambda i, j: (ctx_tile(i), j)),
                  pl.BlockSpec((None, 1, tn), lambda i, j: (cls_of_tile(i), 0, 2 * (d // tn) + j))],
        out_specs=pl.BlockSpec((ROW_TILE, tn), lambda i, j: (i, j)),
        out_shape=jax.ShapeDtypeStruct((n_tiles * ROW_TILE, d), F32),
        compiler_params=_params(("parallel", "parallel"), 40),
        name="out_proj_residual",
    )(ya, yb, yc, yc_ctx, w_out, w_out, w_out, x_lat, x_ctx, mods)


def _ffn_kernel(x_ref, g_ref, sh_ref, sc_ref, gate_ref, wg_ref, wu_ref, wd_ref, o_ref, h_ref, acc_ref):
    f = pl.program_id(1)

    @pl.when(f == 0)
    def _():
        h_ref[...] = _modulated_norm(x_ref[...], g_ref[...], sh_ref[...], sc_ref[...]).astype(h_ref.dtype)
        acc_ref[...] = jnp.zeros_like(acc_ref)

    h = h_ref[...]
    a = _silu(_dot(h, wg_ref[...])) * _dot(h, wu_ref[...])
    acc_ref[...] += _dot(a.astype(BF16), wd_ref[...])

    @pl.when(f == pl.num_programs(1) - 1)
    def _():
        o_ref[...] = x_ref[...] + gate_ref[...] * acc_ref[...]


def _ffn_dense(x, g, mods, wg, wu, wd, n_tiles, cls_of_tile):
    m, d = x.shape
    ff = wg.shape[1]
    tf = 512
    return pl.pallas_call(
        _ffn_kernel,
        grid=(n_tiles, ff // tf),
        in_specs=[pl.BlockSpec((ROW_TILE, d), lambda i, f: (i, 0)),
                  pl.BlockSpec((1, d), lambda i, f: (0, 0)),
                  _mod_spec(3, d, cls_of_tile), _mod_spec(4, d, cls_of_tile), _mod_spec(5, d, cls_of_tile),
                  pl.BlockSpec((d, tf), lambda i, f: (0, f)),
                  pl.BlockSpec((d, tf), lambda i, f: (0, f)),
                  pl.BlockSpec((tf, d), lambda i, f: (f, 0))],
        out_specs=pl.BlockSpec((ROW_TILE, d), lambda i, f: (i, 0)),
        out_shape=jax.ShapeDtypeStruct((n_tiles * ROW_TILE, d), F32),
        scratch_shapes=[pltpu.VMEM((ROW_TILE, d), BF16), pltpu.VMEM((ROW_TILE, d), F32)],
        compiler_params=_params(("parallel", "arbitrary"), 48),
        name="ffn_dense",
    )(x, g.reshape(1, d), mods, mods, mods, wg, wu, wd)


ROUTE_I1, ROUTE_I2, ROUTE_P1, ROUTE_P2 = 0, 1, 2, 3


def _router_kernel(x_ref, g_ref, sh_ref, sc_ref, wr_ref, br_ref, h_ref, route_ref):
    h = _modulated_norm(x_ref[...], g_ref[...], sh_ref[...], sc_ref[...])
    h_ref[...] = h
    logits = _dot_3pass(h, wr_ref[...]) + br_ref[...]
    lane = lax.broadcasted_iota(jnp.int32, logits.shape, 1).astype(F32)
    neg = -jnp.inf
    lg = jnp.where(lane < N_EXPERTS, logits, neg)
    v1 = jnp.max(lg, axis=-1, keepdims=True)
    i1 = jnp.min(jnp.where(lg == v1, lane, float(LANE)), axis=-1, keepdims=True)
    lg2 = jnp.where(lane == i1, neg, lg)
    v2 = jnp.max(lg2, axis=-1, keepdims=True)
    i2 = jnp.min(jnp.where(lg2 == v2, lane, float(LANE)), axis=-1, keepdims=True)
    e = jnp.exp(v2 - v1)
    p1 = 1.0 / (1.0 + e)
    p2 = e / (1.0 + e)
    rec = jnp.where(lane == ROUTE_I1, i1, 0.0) + jnp.where(lane == ROUTE_I2, i2, 0.0)
    route_ref[...] = rec + jnp.where(lane == ROUTE_P1, p1, 0.0) + jnp.where(lane == ROUTE_P2, p2, 0.0)


def _router(x, g, mods, wr, br, n_tiles, cls_of_tile):
    m, d = x.shape
    return pl.pallas_call(
        _router_kernel,
        grid=(n_tiles,),
        in_specs=[pl.BlockSpec((ROW_TILE, d), lambda i: (i, 0)),
                  pl.BlockSpec((1, d), lambda i: (0, 0)),
                  _mod_spec(3, d, cls_of_tile), _mod_spec(4, d, cls_of_tile),
                  pl.BlockSpec((d, LANE), lambda i: (0, 0)),
                  pl.BlockSpec((1, LANE), lambda i: (0, 0))],
        out_specs=[pl.BlockSpec((ROW_TILE, d), lambda i: (i, 0)),
                   pl.BlockSpec((ROW_TILE, LANE), lambda i: (i, 0))],
        out_shape=[jax.ShapeDtypeStruct((n_tiles * ROW_TILE, d), F32),
                   jax.ShapeDtypeStruct((n_tiles * ROW_TILE, LANE), F32)],
        compiler_params=_params(("parallel",), 40),
        name="moe_router",
    )(x, g.reshape(1, d), mods, mods, wr, br)


def _routing_tables(route, n_experts, tile):
    n_tok = route.shape[0]
    experts = route[:, ROUTE_I1:ROUTE_I2 + 1].astype(jnp.int32).reshape(-1)
    onehot = (experts[:, None] == jnp.arange(n_experts, dtype=jnp.int32)[None, :]).astype(jnp.int32)
    running = jnp.cumsum(onehot, axis=0)
    rank = jnp.sum(running * onehot, axis=1) - 1
    counts = running[-1]
    padded = ((counts + tile - 1) // tile) * tile
    ends = jnp.cumsum(padded)
    slot = (ends - padded)[experts] + rank
    n_rows = 2 * n_tok + n_experts * tile
    n_tiles = n_rows // tile
    token = jnp.repeat(jnp.arange(n_tok, dtype=jnp.int32), 2)
    row_token = jnp.zeros((n_rows,), jnp.int32).at[slot].set(token)
    tile_start = jnp.arange(n_tiles, dtype=jnp.int32) * tile
    tile_expert = jnp.minimum(jnp.sum((tile_start[:, None] >= ends[None, :]).astype(jnp.int32), axis=1),
                              n_experts - 1)
    n_active = (ends[-1] // tile).astype(jnp.int32).reshape(1)
    return tile_expert, row_token, n_active, slot.astype(jnp.int32)


MOE_TILE = 256
MOE_FF_SPLIT = 2
GATHER_UNROLL = 8


def _experts_kernel(te_ref, tok_ref, nact_ref, h_hbm, wg_ref, wu_ref, wd_ref, o_ref, xbuf, sem):
    j = pl.program_id(0)
    n_act = nact_ref[0]
    slot = j % 2
    tile = o_ref.shape[0]

    def row_copy(tok, r, s):
        return pltpu.make_async_copy(h_hbm.at[pl.ds(tok, 1), :], xbuf.at[s, pl.ds(r, 1), :], sem.at[s])

    def start_gather(t, s):
        def body(r, carry):
            row_copy(tok_ref[t * tile + r], r, s).start()
            return carry
        lax.fori_loop(0, tile, body, 0, unroll=GATHER_UNROLL)

    def wait_gather(s):
        pltpu.make_async_copy(h_hbm.at[pl.ds(0, tile), :], xbuf.at[s], sem.at[s]).wait()

    @pl.when(j < n_act)
    def _():
        @pl.when(j == 0)
        def _():
            start_gather(0, 0)

        wait_gather(slot)

        @pl.when(j + 1 < n_act)
        def _():
            start_gather(j + 1, 1 - slot)

        h = xbuf[slot].astype(BF16)
        ff = wg_ref.shape[1]
        w = ff // MOE_FF_SPLIT
        y = None
        for part in range(MOE_FF_SPLIT):
            lo = part * w
            a = _silu(_dot(h, wg_ref[:, lo:lo + w])) * _dot(h, wu_ref[:, lo:lo + w])
            yp = _dot(a.astype(BF16), wd_ref[lo:lo + w, :])
            y = yp if y is None else y + yp
        o_ref[...] = y

    @pl.when(j >= n_act)
    def _():
        o_ref[...] = jnp.zeros_like(o_ref)


def _experts(h, tile_expert, row_token, n_active, wg, wu, wd):
    n_e, d, ff = wg.shape
    tile = MOE_TILE
    n_rows = row_token.shape[0]
    assert ff % (MOE_FF_SPLIT * LANE) == 0
    resident = pl.Buffered(1)
    grid_spec = pltpu.PrefetchScalarGridSpec(
        num_scalar_prefetch=3,
        grid=(n_rows // tile,),
        in_specs=[pl.BlockSpec(memory_space=pl.ANY),
                  pl.BlockSpec((None, d, ff), lambda j, te, tok, na: (te[j], 0, 0), pipeline_mode=resident),
                  pl.BlockSpec((None, d, ff), lambda j, te, tok, na: (te[j], 0, 0), pipeline_mode=resident),
                  pl.BlockSpec((None, ff, d), lambda j, te, tok, na: (te[j], 0, 0), pipeline_mode=resident)],
        out_specs=pl.BlockSpec((tile, d), lambda j, te, tok, na: (j, 0)),
        scratch_shapes=[pltpu.VMEM((2, tile, d), F32), pltpu.SemaphoreType.DMA((2,))],
    )
    return pl.pallas_call(
        _experts_kernel,
        grid_spec=grid_spec,
        out_shape=jax.ShapeDtypeStruct((n_rows, d), F32),
        compiler_params=_params(("arbitrary",), 56),
        name="moe_experts",
    )(tile_expert, row_token, n_active, h, wg, wu, wd)


COMBINE_TILE = 256


def _combine_kernel(slot_ref, y_hbm, x_ref, route_ref, gate_ref, gf_ref, o_ref, ybuf, sem, *, final_norm):
    i = pl.program_id(0)
    n = pl.num_programs(0)
    buf = i % 2
    tile = x_ref.shape[0]

    def row_copy(row, k, r, s):
        return pltpu.make_async_copy(y_hbm.at[pl.ds(row, 1), :], ybuf.at[s, pl.ds(k * tile + r, 1), :], sem.at[s])

    def start_gather(t, s):
        def body(r, carry):
            base = 2 * (t * tile + r)
            row_copy(slot_ref[base], 0, r, s).start()
            row_copy(slot_ref[base + 1], 1, r, s).start()
            return carry
        lax.fori_loop(0, tile, body, 0, unroll=GATHER_UNROLL // 2)

    def wait_gather(s):
        pltpu.make_async_copy(y_hbm.at[pl.ds(0, 2 * tile), :], ybuf.at[s], sem.at[s]).wait()

    @pl.when(i == 0)
    def _():
        start_gather(0, 0)

    wait_gather(buf)

    @pl.when(i + 1 < n)
    def _():
        start_gather(i + 1, 1 - buf)

    route = route_ref[...]
    p1 = route[:, ROUTE_P1:ROUTE_P1 + 1]
    p2 = route[:, ROUTE_P2:ROUTE_P2 + 1]
    y = p1 * ybuf[buf, 0:tile, :] + p2 * ybuf[buf, tile:2 * tile, :]
    out = x_ref[...] + gate_ref[...] * y
    if final_norm:
        out = (out * lax.rsqrt(jnp.mean(out * out, axis=-1, keepdims=True) + NORM_EPS)) * gf_ref[...]
    o_ref[...] = out


def _combine(y_rows, slot, x, route, mods, n_tokens, cls_of_tile, final_g):
    d = x.shape[1]
    tile = COMBINE_TILE
    per = ROW_TILE // tile
    final_norm = final_g is not None
    if not final_norm:
        final_g = jnp.ones((d,), F32)
    grid_spec = pltpu.PrefetchScalarGridSpec(
        num_scalar_prefetch=1,
        grid=(n_tokens // tile,),
        in_specs=[pl.BlockSpec(memory_space=pl.ANY),
                  pl.BlockSpec((tile, d), lambda i, s: (i, 0)),
                  pl.BlockSpec((tile, LANE), lambda i, s: (i, 0)),
                  pl.BlockSpec((None, 1, d), lambda i, s: (cls_of_tile(i // per), 0, 5)),
                  pl.BlockSpec((1, d), lambda i, s: (0, 0))],
        out_specs=pl.BlockSpec((tile, d), lambda i, s: (i, 0)),
        scratch_shapes=[pltpu.VMEM((2, 2 * tile, d), F32), pltpu.SemaphoreType.DMA((2,))],
    )
    return pl.pallas_call(
        functools.partial(_combine_kernel, final_norm=final_norm),
        grid_spec=grid_spec,
        out_shape=jax.ShapeDtypeStruct((n_tokens, d), F32),
        compiler_params=_params(("arbitrary",), 48),
        name="moe_combine",
    )(slot, y_rows, x, route, mods, final_g.reshape(1, d))


def _final_kernel(x_ref, g_ref, o_ref):
    x = x_ref[...]
    o_ref[...] = (x * lax.rsqrt(jnp.mean(x * x, axis=-1, keepdims=True) + NORM_EPS)) * g_ref[...]


def _final_norm(x, g, n_tiles):
    d = x.shape[1]
    return pl.pallas_call(
        _final_kernel,
        grid=(n_tiles,),
        in_specs=[pl.BlockSpec((ROW_TILE, d), lambda i: (i, 0)),
                  pl.BlockSpec((1, d), lambda i: (0, 0))],
        out_specs=pl.BlockSpec((ROW_TILE, d), lambda i: (i, 0)),
        out_shape=jax.ShapeDtypeStruct((n_tiles * ROW_TILE, d), F32),
        compiler_params=_params(("parallel",), 40),
        name="final_norm",
    )(x, g.reshape(1, d))


def _pad_lanes(v, width=LANE):
    return jnp.pad(v, [(0, 0)] * (v.ndim - 1) + [(0, width - v.shape[-1])])


def kernel(x, c, ctx, c_ctx, w_mod, b_mod, g_mix, g_ffn, w_in, conv_w, ssd_conv_w, ssd_conv_b, ssd_a_log, ssd_dt_bias, ssd_d, ssd_norm, da_lambda, da_subln, w_out, ffn_w_gate, ffn_w_up, ffn_w_down, moe_w_router, moe_b_router, moe_w_gate, moe_w_up, moe_w_down, g_final):
    batch, seq, d = x.shape
    ctx_len = ctx.shape[1]
    depth = w_mod.shape[0]
    n_lat = batch * seq
    m = n_lat + batch * ctx_len
    assert seq % ROW_TILE == 0 and (batch * ctx_len) % ROW_TILE == 0 and n_lat % ctx_len == 0
    assert ctx_len % HALO_TILE == 0 and ctx_len % SSD_CHUNK == 0 and batch < MOD_CLASSES
    n_lat_tiles = n_lat // ROW_TILE
    n_all_tiles = m // ROW_TILE
    tiles_per_seq = seq // ROW_TILE
    cls_of_tile = lambda i: jnp.minimum(i // tiles_per_seq, batch)

    x_lat, x_ctx = x.reshape(n_lat, d), ctx.reshape(batch * ctx_len, d)
    cvec = jnp.zeros((MOD_CLASSES, d), F32).at[:batch].set(c).at[batch].set(c_ctx)
    mods_all = _mod_vectors(cvec, w_mod, b_mod)
    cos_t, sin_t = _rope_tables(seq)

    for i in range(depth):
        ctx_out = i < depth - 1
        lam_init = 0.8 - 0.6 * math.exp(-0.3 * i)
        n_tiles = n_all_tiles if ctx_out else n_lat_tiles
        mods = mods_all[i].reshape(MOD_CLASSES, 1, N_MOD * d)
        wi = w_in[i]
        w_a = jnp.concatenate([wi[:, COL_CONV:COL_Q], wi[:, COL_XBC:COL_DT]], axis=1).astype(BF16)
        w_b = jnp.concatenate([wi[:, COL_Q:COL_XBC] * (DA_SCALE * LOG2_E), wi[:, COL_K:COL_V],
                               wi[:, COL_V:COL_V + DA_WIDTH]], axis=1).astype(BF16)
        w_c = _pad_lanes(wi[:, COL_DT:COL_K]).astype(BF16)

        h = _norm_modulate(x_lat, x_ctx, g_mix[i], mods, 0, n_all_tiles, n_lat_tiles, cls_of_tile)
        p_a = _matmul(h, w_a, F32, 1024, "in_proj_conv_z_xbc")
        qkv, v_t = _matmul_rope(h, w_b, cos_t, sin_t, n_lat_tiles, tiles_per_seq)
        p_dt = _matmul(h, w_c, F32, LANE, "in_proj_dt")

        ya, xbc, dt2 = _prep(p_a, p_dt, conv_w[i], ssd_conv_w[i], ssd_conv_b[i].reshape(1, -1),
                             _pad_lanes(ssd_dt_bias[i].reshape(1, -1)), n_lat // HALO_TILE,
                             seq // HALO_TILE, ctx_len // HALO_TILE)
        y_fwd, y_bwd = _ssd_scan(xbc, dt2, ssd_a_log[i], batch, seq // SSD_CHUNK, ctx_len // SSD_CHUNK)
        yb = _ssd_gate(y_fwd, y_bwd, xbc, p_a, jnp.repeat(ssd_d[i], SSD_HEAD_DIM).reshape(1, -1),
                       ssd_norm[i].reshape(1, -1), n_tiles)

        subln = da_subln[i].reshape(-1, 1)
        yc = _attention_lat(qkv, v_t, da_lambda[i], subln, lam_init, batch, seq, ctx_len)
        yc_ctx = _attention_ctx(qkv, v_t, da_lambda[i], subln, lam_init, batch, seq, ctx_len) if ctx_out else None

        x_mid = _out_proj(ya, yb, yc, yc_ctx, w_out[i].astype(BF16), x_lat, x_ctx, mods, n_tiles, n_lat_tiles,
                          cls_of_tile)

        j = i // 2
        if i % 2 == 0:
            xs_all = _ffn_dense(x_mid, g_ffn[i], mods, ffn_w_gate[j].astype(BF16), ffn_w_up[j].astype(BF16),
                                ffn_w_down[j].astype(BF16), n_tiles, cls_of_tile)
        else:
            hh, route = _router(x_mid, g_ffn[i], mods, _pad_lanes(moe_w_router[j]),
                                _pad_lanes(moe_b_router[j].reshape(1, -1)), n_tiles, cls_of_tile)
            tile_expert, row_token, n_active, slot = _routing_tables(route, moe_w_gate.shape[1], MOE_TILE)
            y_rows = _experts(hh, tile_expert, row_token, n_active, moe_w_gate[j].astype(BF16),
                              moe_w_up[j].astype(BF16), moe_w_down[j].astype(BF16))
            xs_all = _combine(y_rows, slot, x_mid, route, mods, n_tiles * ROW_TILE, cls_of_tile,
                              None if ctx_out else g_final)
        x_lat = xs_all
        x_ctx = xs_all[n_lat:] if ctx_out else None

    last_is_moe = depth % 2 == 0
    out = x_lat if last_is_moe else _final_norm(x_lat, g_final, n_lat_tiles)
    return out.reshape(batch, seq, d)
```

```python
import functools
import math

import jax
import jax.numpy as jnp
from jax import lax
from jax.experimental import pallas as pl
from jax.experimental.pallas import tpu as pltpu

NORM_EPS = 1e-6
N_MOD = 6
GRID_W = 64

SHORT_CONV = 3
CONV_W = 512

SSD_HEADS = 8
SSD_HEAD_DIM = 64
SSD_INNER = SSD_HEADS * SSD_HEAD_DIM
SSD_STATE = 128
SSD_GROUPS = 2
SSD_CHUNK = 128
SSD_XBC = SSD_INNER + 2 * SSD_GROUPS * SSD_STATE
SSD_GROUP_W = SSD_INNER // SSD_GROUPS

DA_HEADS = 8
DA_HEAD_DIM = 64
DA_V_DIM = 2 * DA_HEAD_DIM
DA_QK = DA_HEADS * 2 * DA_HEAD_DIM
DA_WIDTH = DA_HEADS * DA_V_DIM
DA_SCALE = DA_HEAD_DIM ** -0.5
LOG2_E = math.log2(math.e)
ROPE_THETA = 10000.0

COL_CONV = 0
COL_Z = COL_CONV + 3 * CONV_W
COL_Q = COL_Z + SSD_INNER
COL_XBC = COL_Q + DA_QK
COL_DT = COL_XBC + SSD_XBC
COL_K = COL_DT + 2 * SSD_HEADS
COL_V = COL_K + DA_QK

N_EXPERTS = 8

LANE = 128
SUBLANE = 8
ROW_TILE = 512
HALO_TILE = 256
MOD_CLASSES = 8

F32 = jnp.float32
BF16 = jnp.bfloat16
HIGHEST = lax.Precision.HIGHEST
MIB = 1024 * 1024


def _params(semantics, vmem_mib=None):
    kw = {"dimension_semantics": semantics}
    if vmem_mib is not None:
        kw["vmem_limit_bytes"] = vmem_mib * MIB
    return pltpu.CompilerParams(**kw)


def _silu(v):
    return v * jax.nn.sigmoid(v)


def _dot(a, b):
    return jnp.dot(a, b, preferred_element_type=F32)


def _dot_nt(a, b):
    return lax.dot_general(a, b, (((1,), (1,)), ((), ())), preferred_element_type=F32)


def _dot_tn(a, b):
    return lax.dot_general(a, b, (((0,), (0,)), ((), ())), preferred_element_type=F32)


def _split2(v):
    hi = v.astype(BF16)
    return hi, (v - hi.astype(F32)).astype(BF16)


def _dot_3pass(a, b):
    a_hi, a_lo = _split2(a)
    b_hi, b_lo = _split2(b)
    return _dot(a_hi, b_hi) + (_dot(a_hi, b_lo) + _dot(a_lo, b_hi))


def _modulated_norm(x, g, shift, scale):
    ms = jnp.mean(x * x, axis=-1, keepdims=True)
    y = x * lax.rsqrt(ms + NORM_EPS)
    return (y * g) * (1.0 + scale) + shift


def _mod_spec(k, width, cls_of_tile):
    return pl.BlockSpec((None, 1, width), lambda i, *_: (cls_of_tile(i), 0, k))


def _mod_kernel(c_ref, w_ref, b_ref, o_ref):
    s = _silu(c_ref[...])
    o_ref[...] = _dot_3pass(s, w_ref[...]) + b_ref[...]


def _mod_vectors(cvec, w_mod, b_mod):
    depth, d, n = w_mod.shape
    tn = 1024
    return pl.pallas_call(
        _mod_kernel,
        grid=(depth, n // tn),
        in_specs=[pl.BlockSpec((MOD_CLASSES, d), lambda l, j: (0, 0)),
                  pl.BlockSpec((None, d, tn), lambda l, j: (l, 0, j)),
                  pl.BlockSpec((None, 1, tn), lambda l, j: (l, 0, j))],
        out_specs=pl.BlockSpec((None, MOD_CLASSES, tn), lambda l, j: (l, 0, j)),
        out_shape=jax.ShapeDtypeStruct((depth, MOD_CLASSES, n), F32),
        compiler_params=_params(("parallel", "parallel"), 40),
        name="mod_vectors",
    )(cvec, w_mod, b_mod.reshape(depth, 1, n))


def _lat_ctx_specs(width, n_lat_tiles):
    return [pl.BlockSpec((ROW_TILE, width), lambda i, *_: (jnp.minimum(i, n_lat_tiles - 1), 0)),
            pl.BlockSpec((ROW_TILE, width), lambda i, *_: (jnp.maximum(i - n_lat_tiles, 0), 0))]


def _lat_or_ctx(lat_ref, ctx_ref, n_lat_tiles):
    return jnp.where(pl.program_id(0) < n_lat_tiles, lat_ref[...], ctx_ref[...])


def _norm_kernel(xl_ref, xc_ref, g_ref, sh_ref, sc_ref, h_ref, *, n_lat_tiles):
    x = _lat_or_ctx(xl_ref, xc_ref, n_lat_tiles)
    h_ref[...] = _modulated_norm(x, g_ref[...], sh_ref[...], sc_ref[...]).astype(h_ref.dtype)


def _norm_modulate(x_lat, x_ctx, g, mods, k_shift, n_tiles, n_lat_tiles, cls_of_tile):
    d = x_lat.shape[1]
    return pl.pallas_call(
        functools.partial(_norm_kernel, n_lat_tiles=n_lat_tiles),
        grid=(n_tiles,),
        in_specs=_lat_ctx_specs(d, n_lat_tiles) + [
            pl.BlockSpec((1, d), lambda i: (0, 0)),
            _mod_spec(k_shift, d, cls_of_tile),
            _mod_spec(k_shift + 1, d, cls_of_tile)],
        out_specs=pl.BlockSpec((ROW_TILE, d), lambda i: (i, 0)),
        out_shape=jax.ShapeDtypeStruct((n_tiles * ROW_TILE, d), BF16),
        compiler_params=_params(("parallel",), 40),
        name="norm_modulate",
    )(x_lat, x_ctx, g.reshape(1, d), mods, mods)


def _mm_kernel(x_ref, w_ref, o_ref):
    o_ref[...] = _dot(x_ref[...], w_ref[...]).astype(o_ref.dtype)


def _matmul(x, w, out_dtype, tn, name):
    m, k = x.shape
    n = w.shape[1]
    return pl.pallas_call(
        _mm_kernel,
        grid=(m // ROW_TILE, n // tn),
        in_specs=[pl.BlockSpec((ROW_TILE, k), lambda i, j: (i, 0)),
                  pl.BlockSpec((k, tn), lambda i, j: (0, j))],
        out_specs=pl.BlockSpec((ROW_TILE, tn), lambda i, j: (i, j)),
        out_shape=jax.ShapeDtypeStruct((m, n), out_dtype),
        compiler_params=_params(("parallel", "parallel"), 40),
        name=name,
    )(x, w)


def _mm_rope_kernel(x_ref, w_ref, cos_ref, sin_ref, qk_ref, vt_ref, *, n_lat_tiles, n_rope_tiles):
    i = pl.program_id(0)
    j = pl.program_id(1)
    acc = _dot(x_ref[...], w_ref[...])
    tm, tn = acc.shape
    is_qk = j < n_rope_tiles
    rotate = jnp.logical_and(i < n_lat_tiles, is_qk)

    @pl.when(rotate)
    def _():
        lane = lax.broadcasted_iota(jnp.int32, (tm, LANE), 1)
        first = (lane % (DA_HEAD_DIM // 2)) < (DA_HEAD_DIM // 4)
        c = cos_ref[...]
        s = sin_ref[...]
        quarter = DA_HEAD_DIM // 4
        for hb in range(tn // LANE):
            blk = acc[:, hb * LANE:(hb + 1) * LANE]
            partner = jnp.where(first, pltpu.roll(blk, LANE - quarter, 1), pltpu.roll(blk, quarter, 1))
            qk_ref[:, hb * LANE:(hb + 1) * LANE] = (blk * c + partner * s).astype(qk_ref.dtype)

    @pl.when(jnp.logical_and(is_qk, jnp.logical_not(rotate)))
    def _():
        qk_ref[...] = acc.astype(qk_ref.dtype)

    @pl.when(jnp.logical_not(is_qk))
    def _():
        row = lax.broadcasted_iota(jnp.int32, (VT_ROWS - DA_V_DIM, tm), 0)
        tail = jnp.where(row == 0, 1.0, 0.0).astype(vt_ref.dtype)
        for h in range(DA_HEADS):
            lo = h * VT_ROWS
            vt_ref[lo:lo + DA_V_DIM, :] = acc[:, h * DA_V_DIM:(h + 1) * DA_V_DIM].T.astype(vt_ref.dtype)
            vt_ref[lo + DA_V_DIM:lo + VT_ROWS, :] = tail


def _matmul_rope(x, w, cos_t, sin_t, n_lat_tiles, tiles_per_seq):
    m, k = x.shape
    tn = DA_QK
    n_rope_tiles = 2 * DA_QK // tn
    assert w.shape[1] == 2 * DA_QK + DA_WIDTH and tn == DA_WIDTH
    kern = functools.partial(_mm_rope_kernel, n_lat_tiles=n_lat_tiles, n_rope_tiles=n_rope_tiles)
    return pl.pallas_call(
        kern,
        grid=(m // ROW_TILE, n_rope_tiles + 1),
        in_specs=[pl.BlockSpec((ROW_TILE, k), lambda i, j: (i, 0)),
                  pl.BlockSpec((k, tn), lambda i, j: (0, j)),
                  pl.BlockSpec((ROW_TILE, LANE), lambda i, j: (i % tiles_per_seq, 0)),
                  pl.BlockSpec((ROW_TILE, LANE), lambda i, j: (i % tiles_per_seq, 0))],
        out_specs=[pl.BlockSpec((ROW_TILE, tn), lambda i, j: (i, jnp.minimum(j, n_rope_tiles - 1))),
                   pl.BlockSpec((DA_HEADS * VT_ROWS, ROW_TILE), lambda i, j: (0, i))],
        out_shape=[jax.ShapeDtypeStruct((m, 2 * DA_QK), BF16),
                   jax.ShapeDtypeStruct((DA_HEADS * VT_ROWS, m), BF16)],
        compiler_params=_params(("parallel", "arbitrary"), 40),
        name="qkv_proj_rope",
    )(x, w, cos_t, sin_t)


def _rope_tables(n_tokens):
    rows = n_tokens // GRID_W
    row = jnp.repeat(jnp.arange(rows, dtype=F32), GRID_W)
    col = jnp.broadcast_to(jnp.arange(GRID_W, dtype=F32), (rows, GRID_W)).reshape(-1)
    n_freq = DA_HEAD_DIM // 4
    inv = ROPE_THETA ** (-jnp.arange(n_freq, dtype=F32) / n_freq)
    ang_r = row[:, None] * inv
    ang_c = col[:, None] * inv
    cos64 = jnp.concatenate([jnp.cos(ang_r), jnp.cos(ang_r), jnp.cos(ang_c), jnp.cos(ang_c)], axis=-1)
    sin64 = jnp.concatenate([-jnp.sin(ang_r), jnp.sin(ang_r), -jnp.sin(ang_c), jnp.sin(ang_c)], axis=-1)
    return jnp.tile(cos64, (1, 2)), jnp.tile(sin64, (1, 2))


def _conv3(u, prev_row, next_row, w_ref, has_prev, has_next):
    tm = u.shape[0]
    row = lax.broadcasted_iota(jnp.int32, u.shape, 0)
    prev_row = jnp.where(has_prev, prev_row, 0.0)
    next_row = jnp.where(has_next, next_row, 0.0)
    before = jnp.where(row == 0, prev_row, pltpu.roll(u, 1, 0))
    after = jnp.where(row == tm - 1, next_row, pltpu.roll(u, tm - 1, 0))
    return before * w_ref[0:1, :] + u * w_ref[1:2, :] + after * w_ref[2:3, :]


def _prep_kernel(cv_ref, cvp_ref, cvn_ref, xb_ref, xbp_ref, xbn_ref, dtr_ref,
                 cw_ref, sw_ref, sb_ref, dtb_ref, ya_ref, xs_ref, dt_ref,
                 *, n_lat_tiles, lat_tiles_per_seq, ctx_tiles_per_seq):
    i = pl.program_id(0)
    is_lat = i < n_lat_tiles
    pos = jnp.where(is_lat, i % lat_tiles_per_seq, (i - n_lat_tiles) % ctx_tiles_per_seq)
    last = jnp.where(is_lat, lat_tiles_per_seq - 1, ctx_tiles_per_seq - 1)
    has_prev = pos != 0
    has_next = pos != last

    w = CONV_W
    cv = cv_ref[...]
    gate_b, u = cv[:, 0:w], cv[:, w:2 * w] * cv[:, 2 * w:3 * w]
    p = cvp_ref[SUBLANE - 1:SUBLANE, :]
    n = cvn_ref[0:1, :]
    u_prev = p[:, w:2 * w] * p[:, 2 * w:3 * w]
    u_next = n[:, w:2 * w] * n[:, 2 * w:3 * w]
    ya_ref[...] = (gate_b * _conv3(u, u_prev, u_next, cw_ref, has_prev, has_next)).astype(ya_ref.dtype)

    xc = _conv3(xb_ref[...], xbp_ref[SUBLANE - 1:SUBLANE, :], xbn_ref[0:1, :], sw_ref, has_prev, has_next)
    xs_ref[...] = _silu(xc + sb_ref[...])

    t = dtr_ref[...] + dtb_ref[...]
    dt = jnp.maximum(t, 0.0) + jnp.log1p(jnp.exp(-jnp.abs(t)))
    dt_ref[0] = dt
    dt_ref[1] = pltpu.roll(dt, LANE - SSD_HEADS, 1)


def _prep(p_a, p_dt, conv_w, ssd_conv_w, ssd_conv_b, dt_bias, n_lat_tiles, lat_tiles_per_seq,
          ctx_tiles_per_seq):
    m = p_a.shape[0]
    tm = HALO_TILE
    per = tm // SUBLANE
    n8 = m // SUBLANE
    cw = 3 * CONV_W
    xbc_blk = COL_Q // SSD_XBC
    prev = lambda i: jnp.maximum(i * per - 1, 0)
    nxt = lambda i: jnp.minimum((i + 1) * per, n8 - 1)
    kern = functools.partial(_prep_kernel, n_lat_tiles=n_lat_tiles, lat_tiles_per_seq=lat_tiles_per_seq,
                             ctx_tiles_per_seq=ctx_tiles_per_seq)
    return pl.pallas_call(
        kern,
        grid=(m // tm,),
        in_specs=[pl.BlockSpec((tm, cw), lambda i: (i, 0)),
                  pl.BlockSpec((SUBLANE, cw), lambda i: (prev(i), 0)),
                  pl.BlockSpec((SUBLANE, cw), lambda i: (nxt(i), 0)),
                  pl.BlockSpec((tm, SSD_XBC), lambda i: (i, xbc_blk)),
                  pl.BlockSpec((SUBLANE, SSD_XBC), lambda i: (prev(i), xbc_blk)),
                  pl.BlockSpec((SUBLANE, SSD_XBC), lambda i: (nxt(i), xbc_blk)),
                  pl.BlockSpec((tm, LANE), lambda i: (i, 0)),
                  pl.BlockSpec((SHORT_CONV, CONV_W), lambda i: (0, 0)),
                  pl.BlockSpec((SHORT_CONV, SSD_XBC), lambda i: (0, 0)),
                  pl.BlockSpec((1, SSD_XBC), lambda i: (0, 0)),
                  pl.BlockSpec((1, LANE), lambda i: (0, 0))],
        out_specs=[pl.BlockSpec((tm, CONV_W), lambda i: (i, 0)),
                   pl.BlockSpec((tm, SSD_XBC), lambda i: (i, 0)),
                   pl.BlockSpec((2, tm, LANE), lambda i: (0, i, 0))],
        out_shape=[jax.ShapeDtypeStruct((m, CONV_W), BF16),
                   jax.ShapeDtypeStruct((m, SSD_XBC), F32),
                   jax.ShapeDtypeStruct((2, m, LANE), F32)],
        compiler_params=_params(("parallel",), 40),
        name="conv_prep",
    )(p_a, p_a, p_a, p_a, p_a, p_a, p_dt, conv_w, ssd_conv_w, ssd_conv_b, dt_bias)


def _split3(v):
    hi = v.astype(BF16)
    rest = v - hi.astype(F32)
    mid = rest.astype(BF16)
    return hi, mid, (rest - mid.astype(F32)).astype(BF16)


def _dot_f32_lhs(a, b01):
    return sum(_dot(piece, b01) for piece in _split3(a))


def _dot_f32_rhs(a01, b):
    return sum(_dot(a01, piece) for piece in _split3(b))


def _ssd_chunk(direction, xs_ref, dt_ref, alog_ref, alog_ch_ref, y_ref, state_ref):
    t = SSD_CHUNK
    r = lax.broadcasted_iota(jnp.int32, (t, t), 0)
    c = lax.broadcasted_iota(jnp.int32, (t, t), 1)
    mask = (r >= c) if direction == 0 else (r <= c)
    tri = mask.astype(BF16)

    er = lax.broadcasted_iota(jnp.int32, (LANE, SSD_INNER), 0)
    ec = lax.broadcasted_iota(jnp.int32, (LANE, SSD_INNER), 1)
    expand = (jnp.right_shift(ec, 6) == er).astype(BF16)

    dt = dt_ref[direction]
    cs = _dot_f32_rhs(tri, dt * (-jnp.exp(alog_ref[direction])))
    cs_t = cs.T
    dt_ch = _dot_f32_lhs(dt, expand)
    da_ch = dt_ch * (-jnp.exp(alog_ch_ref[direction]))
    cs_ch = _dot_f32_rhs(tri, da_ch)
    tot_ch = jnp.sum(da_ch, axis=0, keepdims=True)

    xdt = xs_ref[:, 0:SSD_INNER] * dt_ch
    x_state = xdt * jnp.exp(tot_ch - cs_ch)
    y_scale = jnp.exp(cs_ch)
    carry = jnp.exp(tot_ch)
    gw = SSD_GROUP_W
    heads_per_group = SSD_HEADS // SSD_GROUPS
    lane_head = jnp.right_shift(lax.broadcasted_iota(jnp.int32, (t, gw), 1), 6)

    for g in range(SSD_GROUPS):
        b_lo = SSD_INNER + g * SSD_STATE
        c_lo = SSD_INNER + SSD_GROUPS * SSD_STATE + g * SSD_STATE
        bg = xs_ref[:, b_lo:b_lo + SSD_STATE].astype(BF16)
        cg = xs_ref[:, c_lo:c_lo + SSD_STATE].astype(BF16)
        cb = _dot_nt(cg, bg)
        sg = state_ref[direction, g]
        xdt_g = xdt[:, g * gw:(g + 1) * gw]
        y = _dot(cg, sg.astype(BF16)) * y_scale[:, g * gw:(g + 1) * gw]
        for hh in range(heads_per_group):
            h = g * heads_per_group + hh
            decay = jnp.where(mask, jnp.exp(cs[:, h:h + 1] - cs_t[h:h + 1, :]), 0.0)
            x_h = jnp.where(lane_head == hh, xdt_g, 0.0).astype(BF16)
            y = y + _dot((cb * decay).astype(BF16), x_h)
        y_ref[:, g * gw:(g + 1) * gw] = y
        ds = _dot_tn(bg, x_state[:, g * gw:(g + 1) * gw].astype(BF16))
        state_ref[direction, g] = sg * carry[:, g * gw:(g + 1) * gw] + ds


def _ssd_kernel(xf_ref, dtf_ref, xb_ref, dtb_ref, alog_ref, alog_ch_ref, yf_ref, yb_ref, state_ref):
    @pl.when(pl.program_id(1) == 0)
    def _():
        state_ref[...] = jnp.zeros_like(state_ref)

    _ssd_chunk(0, xf_ref, dtf_ref, alog_ref, alog_ch_ref, yf_ref, state_ref)
    _ssd_chunk(1, xb_ref, dtb_ref, alog_ref, alog_ch_ref, yb_ref, state_ref)


def _ssd_scan(xs, dt2, a_log, batch, n_lat_chunks, n_ctx_chunks):
    m = xs.shape[0]
    t = SSD_CHUNK
    assert SSD_HEAD_DIM == 64 and SSD_HEADS <= LANE
    a_log_ch = jnp.repeat(a_log, SSD_HEAD_DIM, axis=1).reshape(2, 1, SSD_INNER)
    a_log = _pad_lanes(a_log).reshape(2, 1, LANE)

    def chunk(b, d, s):
        j_ctx = s if d == 0 else n_ctx_chunks - 1 - s
        sl = s - n_ctx_chunks
        j_lat = sl if d == 0 else n_lat_chunks - 1 - sl
        return jnp.where(s < n_ctx_chunks, batch * n_lat_chunks + b * n_ctx_chunks + j_ctx,
                         b * n_lat_chunks + j_lat)

    x_spec = lambda d: pl.BlockSpec((t, SSD_XBC), lambda b, s: (chunk(b, d, s), 0))
    dt_spec = lambda d: pl.BlockSpec((2, t, LANE), lambda b, s: (0, chunk(b, d, s), 0))
    y_spec = lambda d: pl.BlockSpec((t, SSD_INNER), lambda b, s: (chunk(b, d, s), 0))
    return pl.pallas_call(
        _ssd_kernel,
        grid=(batch, n_ctx_chunks + n_lat_chunks),
        in_specs=[x_spec(0), dt_spec(0), x_spec(1), dt_spec(1),
                  pl.BlockSpec((2, 1, LANE), lambda b, s: (0, 0, 0)),
                  pl.BlockSpec((2, 1, SSD_INNER), lambda b, s: (0, 0, 0))],
        out_specs=[y_spec(0), y_spec(1)],
        out_shape=[jax.ShapeDtypeStruct((m, SSD_INNER), F32), jax.ShapeDtypeStruct((m, SSD_INNER), F32)],
        scratch_shapes=[pltpu.VMEM((2, SSD_GROUPS, SSD_STATE, SSD_GROUP_W), F32)],
        compiler_params=_params(("parallel", "arbitrary")),
        name="ssd_scan",
    )(xs, dt2, xs, dt2, a_log, a_log_ch)


def _ssd_gate_kernel(yf_ref, yb_ref, xs_ref, z_ref, d_ref, g_ref, o_ref):
    yl = d_ref[...] * xs_ref[...] + yf_ref[...] + yb_ref[...]
    u = yl * _silu(z_ref[...])
    gw = SSD_GROUP_W
    for g in range(SSD_GROUPS):
        ug = u[:, g * gw:(g + 1) * gw]
        ug = ug * lax.rsqrt(jnp.mean(ug * ug, axis=-1, keepdims=True) + NORM_EPS)
        o_ref[:, g * gw:(g + 1) * gw] = (ug * g_ref[:, g * gw:(g + 1) * gw]).astype(o_ref.dtype)


def _ssd_gate(y_fwd, y_bwd, xs, p_a, d_row, norm_g, n_tiles):
    w = SSD_INNER
    return pl.pallas_call(
        _ssd_gate_kernel,
        grid=(n_tiles,),
        in_specs=[pl.BlockSpec((ROW_TILE, w), lambda i: (i, 0)),
                  pl.BlockSpec((ROW_TILE, w), lambda i: (i, 0)),
                  pl.BlockSpec((ROW_TILE, w), lambda i: (i, 0)),
                  pl.BlockSpec((ROW_TILE, w), lambda i: (i, COL_Z // w)),
                  pl.BlockSpec((1, w), lambda i: (0, 0)),
                  pl.BlockSpec((1, w), lambda i: (0, 0))],
        out_specs=pl.BlockSpec((ROW_TILE, w), lambda i: (i, 0)),
        out_shape=jax.ShapeDtypeStruct((n_tiles * ROW_TILE, w), BF16),
        compiler_params=_params(("parallel",)),
        name="ssd_gate_norm",
    )(y_fwd, y_bwd, xs, p_a, d_row, norm_g)


ATT_TQ = 1024
ATT_TKC = 1024


def _attn_kernel(*refs, has_lat, lam_init):
    if has_lat:
        (q_ref, kc_ref, vct_ref, kl_ref, vlt_ref, lam_ref, g_ref, o_ref,
         m_ref, acc_ref, sa_ref, sb_ref) = refs
    else:
        q_ref, kc_ref, vct_ref, lam_ref, g_ref, o_ref, m_ref, acc_ref, sa_ref, sb_ref = refs

    q = q_ref[...]
    lane = lax.broadcasted_iota(jnp.int32, q.shape, 1)
    zero = jnp.zeros_like(q)
    q_maps = (jnp.where(lane < DA_HEAD_DIM, q, zero), jnp.where(lane >= DA_HEAD_DIM, q, zero))

    def scores(k, s_ref):
        n = k.shape[0]
        maxima = []
        for mp in range(2):
            s = _dot_nt(k, q_maps[mp])
            s_ref[mp, 0:n, :] = s
            maxima.append(jnp.max(s, axis=0, keepdims=True))
        return tuple(maxima)

    def update(s_ref, maxima, vt):
        n = vt.shape[1]
        for mp in range(2):
            m_old = m_ref[mp]
            m_new = jnp.maximum(m_old, maxima[mp])
            alpha = jnp.exp2(m_old - m_new)
            p = jnp.exp2(s_ref[mp, 0:n, :] - m_new[0:1, :])
            acc_ref[mp] = alpha[0:1, :] * acc_ref[mp] + _dot(vt, p.astype(vt.dtype))
            m_ref[mp] = m_new

    m_ref[...] = jnp.full(m_ref.shape, -jnp.inf, F32)
    acc_ref[...] = jnp.zeros_like(acc_ref)
    mx = scores(kc_ref[...], sa_ref)

    if not has_lat:
        update(sa_ref, mx, vct_ref[...])
    else:
        tkc = min(ATT_TKC, kl_ref.shape[0])
        n_chunks = kl_ref.shape[0] // tkc
        k_at = lambda c: kl_ref[pl.ds(pl.multiple_of(c * tkc, tkc), tkc), :]
        vt_at = lambda c: vlt_ref[:, pl.ds(pl.multiple_of(c * tkc, tkc), tkc)]

        mx_ctx = mx
        mx = scores(k_at(0), sb_ref)
        update(sa_ref, mx_ctx, vct_ref[...])

        def pair(c2, mx_b):
            c = 1 + 2 * c2
            mx_a = scores(k_at(c), sa_ref)
            update(sb_ref, mx_b, vt_at(c - 1))
            mx_b = scores(k_at(c + 1), sb_ref)
            update(sa_ref, mx_a, vt_at(c))
            return mx_b

        mx = lax.fori_loop(0, (n_chunks - 1) // 2, pair, mx)
        if (n_chunks - 1) % 2:
            mx_a = scores(k_at(n_chunks - 1), sa_ref)
            update(sb_ref, mx, vt_at(n_chunks - 2))
            update(sa_ref, mx_a, vt_at(n_chunks - 1))
        else:
            update(sb_ref, mx, vt_at(n_chunks - 1))

    lv = lam_ref[...]
    dotp = lambda a, b: jnp.sum(lv[a:a + 1, :] * lv[b:b + 1, :], axis=-1, keepdims=True)
    lam = jnp.exp(dotp(0, 1)) - jnp.exp(dotp(2, 3)) + lam_init
    vd = DA_V_DIM
    inv_l = 1.0 / acc_ref[:, vd:vd + 1, :]
    o = acc_ref[0, 0:vd, :] * inv_l[0] - lam * (acc_ref[1, 0:vd, :] * inv_l[1])
    y = o * lax.rsqrt(jnp.mean(o * o, axis=0, keepdims=True) + NORM_EPS)
    o_ref[...] = ((y * g_ref[...]) * (1.0 - lam_init)).T.astype(o_ref.dtype)


VT_ROWS = DA_V_DIM + 16


def _attn_scratch(tq, keys_a, keys_b):
    return [pltpu.VMEM((2, SUBLANE, tq), F32), pltpu.VMEM((2, VT_ROWS, tq), F32),
            pltpu.VMEM((2, keys_a, tq), F32), pltpu.VMEM((2, keys_b, tq), F32)]


def _attention_lat(qkv, v_t, da_lambda, subln, lam_init, batch, seq, ctx_len):
    h = DA_HEADS
    tq = ATT_TQ
    nq = seq // tq
    ctx_blk0 = (batch * seq) // ctx_len
    assert seq % tq == 0 and seq % min(ATT_TKC, seq) == 0
    kern = functools.partial(_attn_kernel, has_lat=True, lam_init=lam_init)
    return pl.pallas_call(
        kern,
        grid=(batch, h, nq),
        in_specs=[pl.BlockSpec((tq, LANE), lambda b, hh, i: (b * nq + i, hh)),
                  pl.BlockSpec((ctx_len, LANE), lambda b, hh, i: (ctx_blk0 + b, h + hh)),
                  pl.BlockSpec((VT_ROWS, ctx_len), lambda b, hh, i: (hh, ctx_blk0 + b)),
                  pl.BlockSpec((seq, LANE), lambda b, hh, i: (b, h + hh)),
                  pl.BlockSpec((VT_ROWS, seq), lambda b, hh, i: (hh, b)),
                  pl.BlockSpec(da_lambda.shape, lambda b, hh, i: (0, 0)),
                  pl.BlockSpec((DA_V_DIM, 1), lambda b, hh, i: (0, 0))],
        out_specs=pl.BlockSpec((tq, LANE), lambda b, hh, i: (b * nq + i, hh)),
        out_shape=jax.ShapeDtypeStruct((batch * seq, DA_WIDTH), BF16),
        scratch_shapes=_attn_scratch(tq, max(min(ATT_TKC, seq), ctx_len), min(ATT_TKC, seq)),
        compiler_params=_params(("parallel", "parallel", "parallel"), 48),
        name="diff_attention",
    )(qkv, qkv, v_t, qkv, v_t, da_lambda, subln)


def _attention_ctx(qkv, v_t, da_lambda, subln, lam_init, batch, seq, ctx_len):
    h = DA_HEADS
    ctx_blk0 = (batch * seq) // ctx_len
    kern = functools.partial(_attn_kernel, has_lat=False, lam_init=lam_init)
    return pl.pallas_call(
        kern,
        grid=(batch, h),
        in_specs=[pl.BlockSpec((ctx_len, LANE), lambda b, hh: (ctx_blk0 + b, hh)),
                  pl.BlockSpec((ctx_len, LANE), lambda b, hh: (ctx_blk0 + b, h + hh)),
                  pl.BlockSpec((VT_ROWS, ctx_len), lambda b, hh: (hh, ctx_blk0 + b)),
                  pl.BlockSpec(da_lambda.shape, lambda b, hh: (0, 0)),
                  pl.BlockSpec((DA_V_DIM, 1), lambda b, hh: (0, 0))],
        out_specs=pl.BlockSpec((ctx_len, LANE), lambda b, hh: (b, hh)),
        out_shape=jax.ShapeDtypeStruct((batch * ctx_len, DA_WIDTH), BF16),
        scratch_shapes=_attn_scratch(ctx_len, ctx_len, SUBLANE),
        compiler_params=_params(("parallel", "parallel")),
        name="diff_attention_ctx",
    )(qkv, qkv, v_t, da_lambda, subln)


def _outproj_kernel(ya_ref, yb_ref, yc_ref, ycx_ref, w0_ref, w1_ref, w2_ref, xl_ref, xc_ref, gate_ref, o_ref, *,
                    n_lat_tiles):
    yc = _lat_or_ctx(yc_ref, ycx_ref, n_lat_tiles)
    acc = _dot(ya_ref[...], w0_ref[...]) + _dot(yb_ref[...], w1_ref[...]) + _dot(yc, w2_ref[...])
    o_ref[...] = _lat_or_ctx(xl_ref, xc_ref, n_lat_tiles) + gate_ref[...] * acc


def _out_proj(ya, yb, yc, yc_ctx, w_out, x_lat, x_ctx, mods, n_tiles, n_lat_tiles, cls_of_tile):
    d = x_lat.shape[1]
    tn = 1024
    wa, wb = ya.shape[1], yb.shape[1]
    assert wa == wb and yc.shape[1] == wa + wb
    if yc_ctx is None:
        assert n_tiles <= n_lat_tiles
        yc_ctx = yc
    lat_tile = lambda i: jnp.minimum(i, n_lat_tiles - 1)
    ctx_tile = lambda i: jnp.maximum(i - n_lat_tiles, 0)
    return pl.pallas_call(
        functools.partial(_outproj_kernel, n_lat_tiles=n_lat_tiles),
        grid=(n_tiles, d // tn),
        in_specs=[pl.BlockSpec((ROW_TILE, wa), lambda i, j: (i, 0)),
                  pl.BlockSpec((ROW_TILE, wb), lambda i, j: (i, 0))] + _lat_ctx_specs(wa + wb, n_lat_tiles) + [
                  pl.BlockSpec((wa, tn), lambda i, j: (0, j)),
                  pl.BlockSpec((wb, tn), lambda i, j: (1, j)),
                  pl.BlockSpec((wa + wb, tn), lambda i, j: (1, j)),
                  pl.BlockSpec((ROW_TILE, tn), lambda i, j: (lat_tile(i), j)),
                  pl.BlockSpec((ROW_TILE, tn), lambda i, j: (ctx_tile(i), j)),
                  pl.BlockSpec((None, 1, tn), lambda i, j: (cls_of_tile(i), 0, 2 * (d // tn) + j))],
        out_specs=pl.BlockSpec((ROW_TILE, tn), lambda i, j: (i, j)),
        out_shape=jax.ShapeDtypeStruct((n_tiles * ROW_TILE, d), F32),
        compiler_params=_params(("parallel", "parallel"), 40),
        name="out_proj_residual",
    )(ya, yb, yc, yc_ctx, w_out, w_out, w_out, x_lat, x_ctx, mods)


def _ffn_kernel(x_ref, g_ref, sh_ref, sc_ref, gate_ref, wg_ref, wu_ref, wd_ref, o_ref, h_ref, acc_ref):
    f = pl.program_id(1)

    @pl.when(f == 0)
    def _():
        h_ref[...] = _modulated_norm(x_ref[...], g_ref[...], sh_ref[...], sc_ref[...]).astype(h_ref.dtype)
        acc_ref[...] = jnp.zeros_like(acc_ref)

    h = h_ref[...]
    a = _silu(_dot(h, wg_ref[...])) * _dot(h, wu_ref[...])
    acc_ref[...] += _dot(a.astype(BF16), wd_ref[...])

    @pl.when(f == pl.num_programs(1) - 1)
    def _():
        o_ref[...] = x_ref[...] + gate_ref[...] * acc_ref[...]


def _ffn_dense(x, g, mods, wg, wu, wd, n_tiles, cls_of_tile):
    m, d = x.shape
    ff = wg.shape[1]
    tf = 512
    return pl.pallas_call(
        _ffn_kernel,
        grid=(n_tiles, ff // tf),
        in_specs=[pl.BlockSpec((ROW_TILE, d), lambda i, f: (i, 0)),
                  pl.BlockSpec((1, d), lambda i, f: (0, 0)),
                  _mod_spec(3, d, cls_of_tile), _mod_spec(4, d, cls_of_tile), _mod_spec(5, d, cls_of_tile),
                  pl.BlockSpec((d, tf), lambda i, f: (0, f)),
                  pl.BlockSpec((d, tf), lambda i, f: (0, f)),
                  pl.BlockSpec((tf, d), lambda i, f: (f, 0))],
        out_specs=pl.BlockSpec((ROW_TILE, d), lambda i, f: (i, 0)),
        out_shape=jax.ShapeDtypeStruct((n_tiles * ROW_TILE, d), F32),
        scratch_shapes=[pltpu.VMEM((ROW_TILE, d), BF16), pltpu.VMEM((ROW_TILE, d), F32)],
        compiler_params=_params(("parallel", "arbitrary"), 48),
        name="ffn_dense",
    )(x, g.reshape(1, d), mods, mods, mods, wg, wu, wd)


ROUTE_I1, ROUTE_I2, ROUTE_P1, ROUTE_P2 = 0, 1, 2, 3


def _router_kernel(x_ref, g_ref, sh_ref, sc_ref, wr_ref, br_ref, h_ref, route_ref):
    h = _modulated_norm(x_ref[...], g_ref[...], sh_ref[...], sc_ref[...])
    h_ref[...] = h
    logits = _dot_3pass(h, wr_ref[...]) + br_ref[...]
    lane = lax.broadcasted_iota(jnp.int32, logits.shape, 1).astype(F32)
    neg = -jnp.inf
    lg = jnp.where(lane < N_EXPERTS, logits, neg)
    v1 = jnp.max(lg, axis=-1, keepdims=True)
    i1 = jnp.min(jnp.where(lg == v1, lane, float(LANE)), axis=-1, keepdims=True)
    lg2 = jnp.where(lane == i1, neg, lg)
    v2 = jnp.max(lg2, axis=-1, keepdims=True)
    i2 = jnp.min(jnp.where(lg2 == v2, lane, float(LANE)), axis=-1, keepdims=True)
    e = jnp.exp(v2 - v1)
    p1 = 1.0 / (1.0 + e)
    p2 = e / (1.0 + e)
    rec = jnp.where(lane == ROUTE_I1, i1, 0.0) + jnp.where(lane == ROUTE_I2, i2, 0.0)
    route_ref[...] = rec + jnp.where(lane == ROUTE_P1, p1, 0.0) + jnp.where(lane == ROUTE_P2, p2, 0.0)


def _router(x, g, mods, wr, br, n_tiles, cls_of_tile):
    m, d = x.shape
    return pl.pallas_call(
        _router_kernel,
        grid=(n_tiles,),
        in_specs=[pl.BlockSpec((ROW_TILE, d), lambda i: (i, 0)),
                  pl.BlockSpec((1, d), lambda i: (0, 0)),
                  _mod_spec(3, d, cls_of_tile), _mod_spec(4, d, cls_of_tile),
                  pl.BlockSpec((d, LANE), lambda i: (0, 0)),
                  pl.BlockSpec((1, LANE), lambda i: (0, 0))],
        out_specs=[pl.BlockSpec((ROW_TILE, d), lambda i: (i, 0)),
                   pl.BlockSpec((ROW_TILE, LANE), lambda i: (i, 0))],
        out_shape=[jax.ShapeDtypeStruct((n_tiles * ROW_TILE, d), F32),
                   jax.ShapeDtypeStruct((n_tiles * ROW_TILE, LANE), F32)],
        compiler_params=_params(("parallel",), 40),
        name="moe_router",
    )(x, g.reshape(1, d), mods, mods, wr, br)


def _routing_tables(route, n_experts, tile):
    n_tok = route.shape[0]
    experts = route[:, ROUTE_I1:ROUTE_I2 + 1].astype(jnp.int32).reshape(-1)
    onehot = (experts[:, None] == jnp.arange(n_experts, dtype=jnp.int32)[None, :]).astype(jnp.int32)
    running = jnp.cumsum(onehot, axis=0)
    rank = jnp.sum(running * onehot, axis=1) - 1
    counts = running[-1]
    padded = ((counts + tile - 1) // tile) * tile
    ends = jnp.cumsum(padded)
    slot = (ends - padded)[experts] + rank
    n_rows = 2 * n_tok + n_experts * tile
    n_tiles = n_rows // tile
    token = jnp.repeat(jnp.arange(n_tok, dtype=jnp.int32), 2)
    row_token = jnp.zeros((n_rows,), jnp.int32).at[slot].set(token)
    tile_start = jnp.arange(n_tiles, dtype=jnp.int32) * tile
    tile_expert = jnp.minimum(jnp.sum((tile_start[:, None] >= ends[None, :]).astype(jnp.int32), axis=1),
                              n_experts - 1)
    n_active = (ends[-1] // tile).astype(jnp.int32).reshape(1)
    return tile_expert, row_token, n_active, slot.astype(jnp.int32)


MOE_TILE = 256
MOE_FF_SPLIT = 1
GATHER_UNROLL = 8


def _experts_kernel(te_ref, tok_ref, nact_ref, h_hbm, wg_ref, wu_ref, wd_ref, o_ref, xbuf, sem):
    j = pl.program_id(0)
    n_act = nact_ref[0]
    slot = j % 2
    tile = o_ref.shape[0]

    def row_copy(tok, r, s):
        return pltpu.make_async_copy(h_hbm.at[pl.ds(tok, 1), :], xbuf.at[s, pl.ds(r, 1), :], sem.at[s])

    def start_gather(t, s):
        def body(r, carry):
            row_copy(tok_ref[t * tile + r], r, s).start()
            return carry
        lax.fori_loop(0, tile, body, 0, unroll=GATHER_UNROLL)

    def wait_gather(s):
        pltpu.make_async_copy(h_hbm.at[pl.ds(0, tile), :], xbuf.at[s], sem.at[s]).wait()

    @pl.when(j < n_act)
    def _():
        @pl.when(j == 0)
        def _():
            start_gather(0, 0)

        wait_gather(slot)

        @pl.when(j + 1 < n_act)
        def _():
            start_gather(j + 1, 1 - slot)

        h = xbuf[slot].astype(BF16)
        ff = wg_ref.shape[1]
        w = ff // MOE_FF_SPLIT
        y = None
        for part in range(MOE_FF_SPLIT):
            lo = part * w
            a = _silu(_dot(h, wg_ref[:, lo:lo + w])) * _dot(h, wu_ref[:, lo:lo + w])
            yp = _dot(a.astype(BF16), wd_ref[lo:lo + w, :])
            y = yp if y is None else y + yp
        o_ref[...] = y

    @pl.when(j >= n_act)
    def _():
        o_ref[...] = jnp.zeros_like(o_ref)


def _experts(h, tile_expert, row_token, n_active, wg, wu, wd):
    n_e, d, ff = wg.shape
    tile = MOE_TILE
    n_rows = row_token.shape[0]
    assert ff % (MOE_FF_SPLIT * LANE) == 0
    resident = pl.Buffered(1)
    grid_spec = pltpu.PrefetchScalarGridSpec(
        num_scalar_prefetch=3,
        grid=(n_rows // tile,),
        in_specs=[pl.BlockSpec(memory_space=pl.ANY),
                  pl.BlockSpec((None, d, ff), lambda j, te, tok, na: (te[j], 0, 0), pipeline_mode=resident),
                  pl.BlockSpec((None, d, ff), lambda j, te, tok, na: (te[j], 0, 0), pipeline_mode=resident),
                  pl.BlockSpec((None, ff, d), lambda j, te, tok, na: (te[j], 0, 0), pipeline_mode=resident)],
        out_specs=pl.BlockSpec((tile, d), lambda j, te, tok, na: (j, 0)),
        scratch_shapes=[pltpu.VMEM((2, tile, d), F32), pltpu.SemaphoreType.DMA((2,))],
    )
    return pl.pallas_call(
        _experts_kernel,
        grid_spec=grid_spec,
        out_shape=jax.ShapeDtypeStruct((n_rows, d), F32),
        compiler_params=_params(("arbitrary",), 56),
        name="moe_experts",
    )(tile_expert, row_token, n_active, h, wg, wu, wd)


COMBINE_TILE = 256


def _combine_kernel(slot_ref, y_hbm, x_ref, route_ref, gate_ref, gf_ref, o_ref, ybuf, sem, *, final_norm):
    i = pl.program_id(0)
    n = pl.num_programs(0)
    buf = i % 2
    tile = x_ref.shape[0]

    def row_copy(row, k, r, s):
        return pltpu.make_async_copy(y_hbm.at[pl.ds(row, 1), :], ybuf.at[s, pl.ds(k * tile + r, 1), :], sem.at[s])

    def start_gather(t, s):
        def body(r, carry):
            base = 2 * (t * tile + r)
            row_copy(slot_ref[base], 0, r, s).start()
            row_copy(slot_ref[base + 1], 1, r, s).start()
            return carry
        lax.fori_loop(0, tile, body, 0, unroll=GATHER_UNROLL // 2)

    def wait_gather(s):
        pltpu.make_async_copy(y_hbm.at[pl.ds(0, 2 * tile), :], ybuf.at[s], sem.at[s]).wait()

    @pl.when(i == 0)
    def _():
        start_gather(0, 0)

    wait_gather(buf)

    @pl.when(i + 1 < n)
    def _():
        start_gather(i + 1, 1 - buf)

    route = route_ref[...]
    p1 = route[:, ROUTE_P1:ROUTE_P1 + 1]
    p2 = route[:, ROUTE_P2:ROUTE_P2 + 1]
    y = p1 * ybuf[buf, 0:tile, :] + p2 * ybuf[buf, tile:2 * tile, :]
    out = x_ref[...] + gate_ref[...] * y
    if final_norm:
        out = (out * lax.rsqrt(jnp.mean(out * out, axis=-1, keepdims=True) + NORM_EPS)) * gf_ref[...]
    o_ref[...] = out


def _combine(y_rows, slot, x, route, mods, n_tokens, cls_of_tile, final_g):
    d = x.shape[1]
    tile = COMBINE_TILE
    per = ROW_TILE // tile
    final_norm = final_g is not None
    if not final_norm:
        final_g = jnp.ones((d,), F32)
    grid_spec = pltpu.PrefetchScalarGridSpec(
        num_scalar_prefetch=1,
        grid=(n_tokens // tile,),
        in_specs=[pl.BlockSpec(memory_space=pl.ANY),
                  pl.BlockSpec((tile, d), lambda i, s: (i, 0)),
                  pl.BlockSpec((tile, LANE), lambda i, s: (i, 0)),
                  pl.BlockSpec((None, 1, d), lambda i, s: (cls_of_tile(i // per), 0, 5)),
                  pl.BlockSpec((1, d), lambda i, s: (0, 0))],
        out_specs=pl.BlockSpec((tile, d), lambda i, s: (i, 0)),
        scratch_shapes=[pltpu.VMEM((2, 2 * tile, d), F32), pltpu.SemaphoreType.DMA((2,))],
    )
    return pl.pallas_call(
        functools.partial(_combine_kernel, final_norm=final_norm),
        grid_spec=grid_spec,
        out_shape=jax.ShapeDtypeStruct((n_tokens, d), F32),
        compiler_params=_params(("arbitrary",), 48),
        name="moe_combine",
    )(slot, y_rows, x, route, mods, final_g.reshape(1, d))


def _final_kernel(x_ref, g_ref, o_ref):
    x = x_ref[...]
    o_ref[...] = (x * lax.rsqrt(jnp.mean(x * x, axis=-1, keepdims=True) + NORM_EPS)) * g_ref[...]


def _final_norm(x, g, n_tiles):
    d = x.shape[1]
    return pl.pallas_call(
        _final_kernel,
        grid=(n_tiles,),
        in_specs=[pl.BlockSpec((ROW_TILE, d), lambda i: (i, 0)),
                  pl.BlockSpec((1, d), lambda i: (0, 0))],
        out_specs=pl.BlockSpec((ROW_TILE, d), lambda i: (i, 0)),
        out_shape=jax.ShapeDtypeStruct((n_tiles * ROW_TILE, d), F32),
        compiler_params=_params(("parallel",), 40),
        name="final_norm",
    )(x, g.reshape(1, d))


def _pad_lanes(v, width=LANE):
    return jnp.pad(v, [(0, 0)] * (v.ndim - 1) + [(0, width - v.shape[-1])])


def kernel(x, c, ctx, c_ctx, w_mod, b_mod, g_mix, g_ffn, w_in, conv_w, ssd_conv_w, ssd_conv_b, ssd_a_log, ssd_dt_bias, ssd_d, ssd_norm, da_lambda, da_subln, w_out, ffn_w_gate, ffn_w_up, ffn_w_down, moe_w_router, moe_b_router, moe_w_gate, moe_w_up, moe_w_down, g_final):
    batch, seq, d = x.shape
    ctx_len = ctx.shape[1]
    depth = w_mod.shape[0]
    n_lat = batch * seq
    m = n_lat + batch * ctx_len
    assert seq % ROW_TILE == 0 and (batch * ctx_len) % ROW_TILE == 0 and n_lat % ctx_len == 0
    assert ctx_len % HALO_TILE == 0 and ctx_len % SSD_CHUNK == 0 and batch < MOD_CLASSES
    n_lat_tiles = n_lat // ROW_TILE
    n_all_tiles = m // ROW_TILE
    tiles_per_seq = seq // ROW_TILE
    cls_of_tile = lambda i: jnp.minimum(i // tiles_per_seq, batch)

    x_lat, x_ctx = x.reshape(n_lat, d), ctx.reshape(batch * ctx_len, d)
    cvec = jnp.zeros((MOD_CLASSES, d), F32).at[:batch].set(c).at[batch].set(c_ctx)
    mods_all = _mod_vectors(cvec, w_mod, b_mod)
    cos_t, sin_t = _rope_tables(seq)

    for i in range(depth):
        ctx_out = i < depth - 1
        lam_init = 0.8 - 0.6 * math.exp(-0.3 * i)
        n_tiles = n_all_tiles if ctx_out else n_lat_tiles
        mods = mods_all[i].reshape(MOD_CLASSES, 1, N_MOD * d)
        wi = w_in[i]
        w_a = jnp.concatenate([wi[:, COL_CONV:COL_Q], wi[:, COL_XBC:COL_DT]], axis=1).astype(BF16)
        w_b = jnp.concatenate([wi[:, COL_Q:COL_XBC] * (DA_SCALE * LOG2_E), wi[:, COL_K:COL_V],
                               wi[:, COL_V:COL_V + DA_WIDTH]], axis=1).astype(BF16)
        w_c = _pad_lanes(wi[:, COL_DT:COL_K]).astype(BF16)

        h = _norm_modulate(x_lat, x_ctx, g_mix[i], mods, 0, n_all_tiles, n_lat_tiles, cls_of_tile)
        p_a = _matmul(h, w_a, F32, 1024, "in_proj_conv_z_xbc")
        qkv, v_t = _matmul_rope(h, w_b, cos_t, sin_t, n_lat_tiles, tiles_per_seq)
        p_dt = _matmul(h, w_c, F32, LANE, "in_proj_dt")

        ya, xbc, dt2 = _prep(p_a, p_dt, conv_w[i], ssd_conv_w[i], ssd_conv_b[i].reshape(1, -1),
                             _pad_lanes(ssd_dt_bias[i].reshape(1, -1)), n_lat // HALO_TILE,
                             seq // HALO_TILE, ctx_len // HALO_TILE)
        y_fwd, y_bwd = _ssd_scan(xbc, dt2, ssd_a_log[i], batch, seq // SSD_CHUNK, ctx_len // SSD_CHUNK)
        yb = _ssd_gate(y_fwd, y_bwd, xbc, p_a, jnp.repeat(ssd_d[i], SSD_HEAD_DIM).reshape(1, -1),
                       ssd_norm[i].reshape(1, -1), n_tiles)

        subln = da_subln[i].reshape(-1, 1)
        yc = _attention_lat(qkv, v_t, da_lambda[i], subln, lam_init, batch, seq, ctx_len)
        yc_ctx = _attention_ctx(qkv, v_t, da_lambda[i], subln, lam_init, batch, seq, ctx_len) if ctx_out else None

        x_mid = _out_proj(ya, yb, yc, yc_ctx, w_out[i].astype(BF16), x_lat, x_ctx, mods, n_tiles, n_lat_tiles,
                          cls_of_tile)

        j = i // 2
        if i % 2 == 0:
            xs_all = _ffn_dense(x_mid, g_ffn[i], mods, ffn_w_gate[j].astype(BF16), ffn_w_up[j].astype(BF16),
                                ffn_w_down[j].astype(BF16), n_tiles, cls_of_tile)
        else:
            hh, route = _router(x_mid, g_ffn[i], mods, _pad_lanes(moe_w_router[j]),
                                _pad_lanes(moe_b_router[j].reshape(1, -1)), n_tiles, cls_of_tile)
            tile_expert, row_token, n_active, slot = _routing_tables(route, moe_w_gate.shape[1], MOE_TILE)
            y_rows = _experts(hh, tile_expert, row_token, n_active, moe_w_gate[j].astype(BF16),
                              moe_w_up[j].astype(BF16), moe_w_down[j].astype(BF16))
            xs_all = _combine(y_rows, slot, x_mid, route, mods, n_tiles * ROW_TILE, cls_of_tile,
                              None if ctx_out else g_final)
        x_lat = xs_all
        x_ctx = xs_all[n_lat:] if ctx_out else None

    last_is_moe = depth % 2 == 0
    out = x_lat if last_is_moe else _final_norm(x_lat, g_final, n_lat_tiles)
    return out.reshape(batch, seq, d)
```

```python
import functools
import math

import jax
import jax.numpy as jnp
from jax import lax
from jax.experimental import pallas as pl
from jax.experimental.pallas import tpu as pltpu

NORM_EPS = 1e-6
N_MOD = 6
GRID_W = 64

SHORT_CONV = 3
CONV_W = 512

SSD_HEADS = 8
SSD_HEAD_DIM = 64
SSD_INNER = SSD_HEADS * SSD_HEAD_DIM
SSD_STATE = 128
SSD_GROUPS = 2
SSD_CHUNK = 128
SSD_XBC = SSD_INNER + 2 * SSD_GROUPS * SSD_STATE
SSD_GROUP_W = SSD_INNER // SSD_GROUPS

DA_HEADS = 8
DA_HEAD_DIM = 64
DA_V_DIM = 2 * DA_HEAD_DIM
DA_QK = DA_HEADS * 2 * DA_HEAD_DIM
DA_WIDTH = DA_HEADS * DA_V_DIM
DA_SCALE = DA_HEAD_DIM ** -0.5
LOG2_E = math.log2(math.e)
ROPE_THETA = 10000.0

COL_CONV = 0
COL_Z = COL_CONV + 3 * CONV_W
COL_Q = COL_Z + SSD_INNER
COL_XBC = COL_Q + DA_QK
COL_DT = COL_XBC + SSD_XBC
COL_K = COL_DT + 2 * SSD_HEADS
COL_V = COL_K + DA_QK

N_EXPERTS = 8

LANE = 128
SUBLANE = 8
ROW_TILE = 512
HALO_TILE = 256
MOD_CLASSES = 8

F32 = jnp.float32
BF16 = jnp.bfloat16
MIB = 1024 * 1024


def _params(semantics, vmem_mib=None):
    kw = {"dimension_semantics": semantics}
    if vmem_mib is not None:
        kw["vmem_limit_bytes"] = vmem_mib * MIB
    return pltpu.CompilerParams(**kw)


def _silu(v):
    return v * jax.nn.sigmoid(v)


def _dot(a, b):
    return jnp.dot(a, b, preferred_element_type=F32)


def _dot_nt(a, b):
    return lax.dot_general(a, b, (((1,), (1,)), ((), ())), preferred_element_type=F32)


def _dot_tn(a, b):
    return lax.dot_general(a, b, (((0,), (0,)), ((), ())), preferred_element_type=F32)


def _split2(v):
    hi = v.astype(BF16)
    return hi, (v - hi.astype(F32)).astype(BF16)


def _dot_3pass(a, b):
    a_hi, a_lo = _split2(a)
    b_hi, b_lo = _split2(b)
    return _dot(a_hi, b_hi) + (_dot(a_hi, b_lo) + _dot(a_lo, b_hi))


def _modulated_norm(x, g, shift, scale):
    ms = jnp.mean(x * x, axis=-1, keepdims=True)
    y = x * lax.rsqrt(ms + NORM_EPS)
    return (y * g) * (1.0 + scale) + shift


def _mod_spec(k, width, cls_of_tile):
    return pl.BlockSpec((None, 1, width), lambda i, *_: (cls_of_tile(i), 0, k))


def _mod_kernel(c_ref, w_ref, b_ref, o_ref):
    s = _silu(c_ref[...])
    o_ref[...] = _dot_3pass(s, w_ref[...]) + b_ref[...]


def _mod_vectors(cvec, w_mod, b_mod):
    depth, d, n = w_mod.shape
    tn = 1024
    return pl.pallas_call(
        _mod_kernel,
        grid=(depth, n // tn),
        in_specs=[pl.BlockSpec((MOD_CLASSES, d), lambda l, j: (0, 0)),
                  pl.BlockSpec((None, d, tn), lambda l, j: (l, 0, j)),
                  pl.BlockSpec((None, 1, tn), lambda l, j: (l, 0, j))],
        out_specs=pl.BlockSpec((None, MOD_CLASSES, tn), lambda l, j: (l, 0, j)),
        out_shape=jax.ShapeDtypeStruct((depth, MOD_CLASSES, n), F32),
        compiler_params=_params(("parallel", "parallel"), 40),
        name="mod_vectors",
    )(cvec, w_mod, b_mod.reshape(depth, 1, n))


def _lat_ctx_specs(width, n_lat_tiles):
    return [pl.BlockSpec((ROW_TILE, width), lambda i, *_: (jnp.minimum(i, n_lat_tiles - 1), 0)),
            pl.BlockSpec((ROW_TILE, width), lambda i, *_: (jnp.maximum(i - n_lat_tiles, 0), 0))]


def _lat_or_ctx(lat_ref, ctx_ref, n_lat_tiles):
    return jnp.where(pl.program_id(0) < n_lat_tiles, lat_ref[...], ctx_ref[...])


def _norm_kernel(xl_ref, xc_ref, g_ref, sh_ref, sc_ref, h_ref, *, n_lat_tiles):
    x = _lat_or_ctx(xl_ref, xc_ref, n_lat_tiles)
    h_ref[...] = _modulated_norm(x, g_ref[...], sh_ref[...], sc_ref[...]).astype(h_ref.dtype)


def _norm_modulate(x_lat, x_ctx, g, mods, k_shift, n_tiles, n_lat_tiles, cls_of_tile):
    d = x_lat.shape[1]
    return pl.pallas_call(
        functools.partial(_norm_kernel, n_lat_tiles=n_lat_tiles),
        grid=(n_tiles,),
        in_specs=_lat_ctx_specs(d, n_lat_tiles) + [
            pl.BlockSpec((1, d), lambda i: (0, 0)),
            _mod_spec(k_shift, d, cls_of_tile),
            _mod_spec(k_shift + 1, d, cls_of_tile)],
        out_specs=pl.BlockSpec((ROW_TILE, d), lambda i: (i, 0)),
        out_shape=jax.ShapeDtypeStruct((n_tiles * ROW_TILE, d), BF16),
        compiler_params=_params(("parallel",), 40),
        name="norm_modulate",
    )(x_lat, x_ctx, g.reshape(1, d), mods, mods)


def _mm_kernel(x_ref, w_ref, o_ref):
    o_ref[...] = _dot(x_ref[...], w_ref[...]).astype(o_ref.dtype)


def _matmul(x, w, out_dtype, tn, name):
    m, k = x.shape
    n = w.shape[1]
    return pl.pallas_call(
        _mm_kernel,
        grid=(m // ROW_TILE, n // tn),
        in_specs=[pl.BlockSpec((ROW_TILE, k), lambda i, j: (i, 0)),
                  pl.BlockSpec((k, tn), lambda i, j: (0, j))],
        out_specs=pl.BlockSpec((ROW_TILE, tn), lambda i, j: (i, j)),
        out_shape=jax.ShapeDtypeStruct((m, n), out_dtype),
        compiler_params=_params(("parallel", "parallel"), 40),
        name=name,
    )(x, w)


def _mm_rope_kernel(x_ref, w_ref, cos_ref, sin_ref, qk_ref, vt_ref, *, n_lat_tiles, n_rope_tiles):
    i = pl.program_id(0)
    j = pl.program_id(1)
    acc = _dot(x_ref[...], w_ref[...])
    tm, tn = acc.shape
    is_qk = j < n_rope_tiles
    rotate = jnp.logical_and(i < n_lat_tiles, is_qk)

    @pl.when(rotate)
    def _():
        c = cos_ref[...]
        s = sin_ref[...]
        for hb in range(tn // LANE):
            blk = acc[:, hb * LANE:(hb + 1) * LANE]
            partner = pltpu.roll(blk, LANE // 2, 1)
            qk_ref[:, hb * LANE:(hb + 1) * LANE] = (blk * c + partner * s).astype(qk_ref.dtype)

    @pl.when(jnp.logical_and(is_qk, jnp.logical_not(rotate)))
    def _():
        qk_ref[...] = acc.astype(qk_ref.dtype)

    @pl.when(jnp.logical_not(is_qk))
    def _():
        row = lax.broadcasted_iota(jnp.int32, (VT_ROWS - DA_V_DIM, tm), 0)
        tail = jnp.where(row == 0, 1.0, 0.0).astype(vt_ref.dtype)
        for h in range(DA_HEADS):
            lo = h * VT_ROWS
            vt_ref[lo:lo + DA_V_DIM, :] = acc[:, h * DA_V_DIM:(h + 1) * DA_V_DIM].T.astype(vt_ref.dtype)
            vt_ref[lo + DA_V_DIM:lo + VT_ROWS, :] = tail


def _matmul_rope(x, w, cos_t, sin_t, n_lat_tiles, tiles_per_seq):
    m, k = x.shape
    tn = DA_QK
    n_rope_tiles = 2 * DA_QK // tn
    assert w.shape[1] == 2 * DA_QK + DA_WIDTH and tn == DA_WIDTH
    kern = functools.partial(_mm_rope_kernel, n_lat_tiles=n_lat_tiles, n_rope_tiles=n_rope_tiles)
    return pl.pallas_call(
        kern,
        grid=(m // ROW_TILE, n_rope_tiles + 1),
        in_specs=[pl.BlockSpec((ROW_TILE, k), lambda i, j: (i, 0)),
                  pl.BlockSpec((k, tn), lambda i, j: (0, j)),
                  pl.BlockSpec((ROW_TILE, LANE), lambda i, j: (i % tiles_per_seq, 0)),
                  pl.BlockSpec((ROW_TILE, LANE), lambda i, j: (i % tiles_per_seq, 0))],
        out_specs=[pl.BlockSpec((ROW_TILE, tn), lambda i, j: (i, jnp.minimum(j, n_rope_tiles - 1))),
                   pl.BlockSpec((DA_HEADS * VT_ROWS, ROW_TILE), lambda i, j: (0, i))],
        out_shape=[jax.ShapeDtypeStruct((m, 2 * DA_QK), BF16),
                   jax.ShapeDtypeStruct((DA_HEADS * VT_ROWS, m), BF16)],
        compiler_params=_params(("parallel", "arbitrary"), 40),
        name="qkv_proj_rope",
    )(x, w, cos_t, sin_t)


def _head_lane_order():
    q = DA_HEAD_DIM // 4
    order = []
    for half in range(2):
        for mp in range(2):
            for axis in range(2):
                order += [mp * DA_HEAD_DIM + axis * 2 * q + half * q + i for i in range(q)]
    return order


def _qk_column_order():
    per_head = _head_lane_order()
    return jnp.asarray([h * 2 * DA_HEAD_DIM + o for h in range(DA_HEADS) for o in per_head], jnp.int32)


def _rope_tables(n_tokens):
    rows = n_tokens // GRID_W
    row = jnp.repeat(jnp.arange(rows, dtype=F32), GRID_W)
    col = jnp.broadcast_to(jnp.arange(GRID_W, dtype=F32), (rows, GRID_W)).reshape(-1)
    n_freq = DA_HEAD_DIM // 4
    inv = ROPE_THETA ** (-jnp.arange(n_freq, dtype=F32) / n_freq)
    ang_r = row[:, None] * inv
    ang_c = col[:, None] * inv
    cos_half = jnp.tile(jnp.concatenate([jnp.cos(ang_r), jnp.cos(ang_c)], axis=-1), (1, 2))
    sin_half = jnp.tile(jnp.concatenate([jnp.sin(ang_r), jnp.sin(ang_c)], axis=-1), (1, 2))
    return jnp.concatenate([cos_half, cos_half], axis=-1), jnp.concatenate([-sin_half, sin_half], axis=-1)


def _conv3(u, prev_row, next_row, w_ref, has_prev, has_next):
    tm = u.shape[0]
    row = lax.broadcasted_iota(jnp.int32, u.shape, 0)
    prev_row = jnp.where(has_prev, prev_row, 0.0)
    next_row = jnp.where(has_next, next_row, 0.0)
    before = jnp.where(row == 0, prev_row, pltpu.roll(u, 1, 0))
    after = jnp.where(row == tm - 1, next_row, pltpu.roll(u, tm - 1, 0))
    return before * w_ref[0:1, :] + u * w_ref[1:2, :] + after * w_ref[2:3, :]


def _prep_kernel(cv_ref, cvp_ref, cvn_ref, xb_ref, xbp_ref, xbn_ref, dtr_ref,
                 cw_ref, sw_ref, sb_ref, dtb_ref, ya_ref, xs_ref, dt_ref,
                 *, n_lat_tiles, lat_tiles_per_seq, ctx_tiles_per_seq):
    i = pl.program_id(0)
    is_lat = i < n_lat_tiles
    pos = jnp.where(is_lat, i % lat_tiles_per_seq, (i - n_lat_tiles) % ctx_tiles_per_seq)
    last = jnp.where(is_lat, lat_tiles_per_seq - 1, ctx_tiles_per_seq - 1)
    has_prev = pos != 0
    has_next = pos != last

    w = CONV_W
    cv = cv_ref[...]
    gate_b, u = cv[:, 0:w], cv[:, w:2 * w] * cv[:, 2 * w:3 * w]
    p = cvp_ref[SUBLANE - 1:SUBLANE, :]
    n = cvn_ref[0:1, :]
    u_prev = p[:, w:2 * w] * p[:, 2 * w:3 * w]
    u_next = n[:, w:2 * w] * n[:, 2 * w:3 * w]
    ya_ref[...] = (gate_b * _conv3(u, u_prev, u_next, cw_ref, has_prev, has_next)).astype(ya_ref.dtype)

    xc = _conv3(xb_ref[...], xbp_ref[SUBLANE - 1:SUBLANE, :], xbn_ref[0:1, :], sw_ref, has_prev, has_next)
    xs_ref[...] = _silu(xc + sb_ref[...])

    t = dtr_ref[...] + dtb_ref[...]
    dt = jnp.maximum(t, 0.0) + jnp.log1p(jnp.exp(-jnp.abs(t)))
    dt_ref[0] = dt
    dt_ref[1] = pltpu.roll(dt, LANE - SSD_HEADS, 1)


def _prep(p_a, p_dt, conv_w, ssd_conv_w, ssd_conv_b, dt_bias, n_lat_tiles, lat_tiles_per_seq,
          ctx_tiles_per_seq):
    m = p_a.shape[0]
    tm = HALO_TILE
    per = tm // SUBLANE
    n8 = m // SUBLANE
    cw = 3 * CONV_W
    xbc_blk = COL_Q // SSD_XBC
    prev = lambda i: jnp.maximum(i * per - 1, 0)
    nxt = lambda i: jnp.minimum((i + 1) * per, n8 - 1)
    kern = functools.partial(_prep_kernel, n_lat_tiles=n_lat_tiles, lat_tiles_per_seq=lat_tiles_per_seq,
                             ctx_tiles_per_seq=ctx_tiles_per_seq)
    return pl.pallas_call(
        kern,
        grid=(m // tm,),
        in_specs=[pl.BlockSpec((tm, cw), lambda i: (i, 0)),
                  pl.BlockSpec((SUBLANE, cw), lambda i: (prev(i), 0)),
                  pl.BlockSpec((SUBLANE, cw), lambda i: (nxt(i), 0)),
                  pl.BlockSpec((tm, SSD_XBC), lambda i: (i, xbc_blk)),
                  pl.BlockSpec((SUBLANE, SSD_XBC), lambda i: (prev(i), xbc_blk)),
                  pl.BlockSpec((SUBLANE, SSD_XBC), lambda i: (nxt(i), xbc_blk)),
                  pl.BlockSpec((tm, LANE), lambda i: (i, 0)),
                  pl.BlockSpec((SHORT_CONV, CONV_W), lambda i: (0, 0)),
                  pl.BlockSpec((SHORT_CONV, SSD_XBC), lambda i: (0, 0)),
                  pl.BlockSpec((1, SSD_XBC), lambda i: (0, 0)),
                  pl.BlockSpec((1, LANE), lambda i: (0, 0))],
        out_specs=[pl.BlockSpec((tm, CONV_W), lambda i: (i, 0)),
                   pl.BlockSpec((tm, SSD_XBC), lambda i: (i, 0)),
                   pl.BlockSpec((2, tm, LANE), lambda i: (0, i, 0))],
        out_shape=[jax.ShapeDtypeStruct((m, CONV_W), BF16),
                   jax.ShapeDtypeStruct((m, SSD_XBC), F32),
                   jax.ShapeDtypeStruct((2, m, LANE), F32)],
        compiler_params=_params(("parallel",), 40),
        name="conv_prep",
    )(p_a, p_a, p_a, p_a, p_a, p_a, p_dt, conv_w, ssd_conv_w, ssd_conv_b, dt_bias)


def _split3(v):
    hi = v.astype(BF16)
    rest = v - hi.astype(F32)
    mid = rest.astype(BF16)
    return hi, mid, (rest - mid.astype(F32)).astype(BF16)


def _dot_f32_lhs(a, b01):
    return sum(_dot(piece, b01) for piece in _split3(a))


def _dot_f32_rhs(a01, b):
    return sum(_dot(a01, piece) for piece in _split3(b))


def _ssd_chunk(direction, xs_ref, dt_ref, alog_ref, alog_ch_ref, y_ref, state_ref):
    t = SSD_CHUNK
    r = lax.broadcasted_iota(jnp.int32, (t, t), 0)
    c = lax.broadcasted_iota(jnp.int32, (t, t), 1)
    mask = (r >= c) if direction == 0 else (r <= c)
    tri = mask.astype(BF16)

    er = lax.broadcasted_iota(jnp.int32, (LANE, SSD_INNER), 0)
    ec = lax.broadcasted_iota(jnp.int32, (LANE, SSD_INNER), 1)
    expand = (jnp.right_shift(ec, 6) == er).astype(BF16)

    dt = dt_ref[direction]
    cs = _dot_f32_rhs(tri, dt * (-jnp.exp(alog_ref[direction])))
    cs_t = cs.T
    dt_ch = _dot_f32_lhs(dt, expand)
    da_ch = dt_ch * (-jnp.exp(alog_ch_ref[direction]))
    cs_ch = _dot_f32_rhs(tri, da_ch)
    tot_ch = jnp.sum(da_ch, axis=0, keepdims=True)

    xdt = xs_ref[:, 0:SSD_INNER] * dt_ch
    x_state = xdt * jnp.exp(tot_ch - cs_ch)
    y_scale = jnp.exp(cs_ch)
    carry = jnp.exp(tot_ch)
    gw = SSD_GROUP_W
    heads_per_group = SSD_HEADS // SSD_GROUPS
    lane_head = jnp.right_shift(lax.broadcasted_iota(jnp.int32, (t, gw), 1), 6)

    for g in range(SSD_GROUPS):
        b_lo = SSD_INNER + g * SSD_STATE
        c_lo = SSD_INNER + SSD_GROUPS * SSD_STATE + g * SSD_STATE
        bg = xs_ref[:, b_lo:b_lo + SSD_STATE].astype(BF16)
        cg = xs_ref[:, c_lo:c_lo + SSD_STATE].astype(BF16)
        cb = _dot_nt(cg, bg)
        sg = state_ref[direction, g]
        xdt_g = xdt[:, g * gw:(g + 1) * gw]
        y = _dot(cg, sg.astype(BF16)) * y_scale[:, g * gw:(g + 1) * gw]
        for hh in range(heads_per_group):
            h = g * heads_per_group + hh
            decay = jnp.where(mask, jnp.exp(cs[:, h:h + 1] - cs_t[h:h + 1, :]), 0.0)
            x_h = jnp.where(lane_head == hh, xdt_g, 0.0).astype(BF16)
            y = y + _dot((cb * decay).astype(BF16), x_h)
        y_ref[:, g * gw:(g + 1) * gw] = y
        ds = _dot_tn(bg, x_state[:, g * gw:(g + 1) * gw].astype(BF16))
        state_ref[direction, g] = sg * carry[:, g * gw:(g + 1) * gw] + ds


def _ssd_kernel(xf_ref, dtf_ref, xb_ref, dtb_ref, alog_ref, alog_ch_ref, yf_ref, yb_ref, state_ref):
    @pl.when(pl.program_id(1) == 0)
    def _():
        state_ref[...] = jnp.zeros_like(state_ref)

    _ssd_chunk(0, xf_ref, dtf_ref, alog_ref, alog_ch_ref, yf_ref, state_ref)
    _ssd_chunk(1, xb_ref, dtb_ref, alog_ref, alog_ch_ref, yb_ref, state_ref)


def _ssd_scan(xs, dt2, a_log, batch, n_lat_chunks, n_ctx_chunks):
    m = xs.shape[0]
    t = SSD_CHUNK
    assert SSD_HEAD_DIM == 64 and SSD_HEADS <= LANE
    a_log_ch = jnp.repeat(a_log, SSD_HEAD_DIM, axis=1).reshape(2, 1, SSD_INNER)
    a_log = _pad_lanes(a_log).reshape(2, 1, LANE)

    def chunk(b, d, s):
        j_ctx = s if d == 0 else n_ctx_chunks - 1 - s
        sl = s - n_ctx_chunks
        j_lat = sl if d == 0 else n_lat_chunks - 1 - sl
        return jnp.where(s < n_ctx_chunks, batch * n_lat_chunks + b * n_ctx_chunks + j_ctx,
                         b * n_lat_chunks + j_lat)

    x_spec = lambda d: pl.BlockSpec((t, SSD_XBC), lambda b, s: (chunk(b, d, s), 0))
    dt_spec = lambda d: pl.BlockSpec((2, t, LANE), lambda b, s: (0, chunk(b, d, s), 0))
    y_spec = lambda d: pl.BlockSpec((t, SSD_INNER), lambda b, s: (chunk(b, d, s), 0))
    return pl.pallas_call(
        _ssd_kernel,
        grid=(batch, n_ctx_chunks + n_lat_chunks),
        in_specs=[x_spec(0), dt_spec(0), x_spec(1), dt_spec(1),
                  pl.BlockSpec((2, 1, LANE), lambda b, s: (0, 0, 0)),
                  pl.BlockSpec((2, 1, SSD_INNER), lambda b, s: (0, 0, 0))],
        out_specs=[y_spec(0), y_spec(1)],
        out_shape=[jax.ShapeDtypeStruct((m, SSD_INNER), F32), jax.ShapeDtypeStruct((m, SSD_INNER), F32)],
        scratch_shapes=[pltpu.VMEM((2, SSD_GROUPS, SSD_STATE, SSD_GROUP_W), F32)],
        compiler_params=_params(("parallel", "arbitrary")),
        name="ssd_scan",
    )(xs, dt2, xs, dt2, a_log, a_log_ch)


def _ssd_gate_norm(y_fwd, y_bwd, x, z, d_skip, norm_g):
    u = (d_skip * x + y_fwd + y_bwd) * _silu(z)
    gw = SSD_GROUP_W
    parts = []
    for g in range(SSD_GROUPS):
        ug = u[:, g * gw:(g + 1) * gw]
        ug = ug * lax.rsqrt(jnp.mean(ug * ug, axis=-1, keepdims=True) + NORM_EPS)
        parts.append(ug * norm_g[:, g * gw:(g + 1) * gw])
    return jnp.concatenate(parts, axis=1)


ATT_TQ = 1024
ATT_TKC = 1024


def _attn_kernel(*refs, has_lat, lam_init):
    if has_lat:
        (q_ref, kc_ref, vct_ref, kl_ref, vlt_ref, lam_ref, g_ref, o_ref,
         m_ref, acc_ref, sa_ref, sb_ref) = refs
    else:
        q_ref, kc_ref, vct_ref, lam_ref, g_ref, o_ref, m_ref, acc_ref, sa_ref, sb_ref = refs

    q = q_ref[...]
    lane = lax.broadcasted_iota(jnp.int32, q.shape, 1)
    zero = jnp.zeros_like(q)
    in_map0 = (lane % DA_HEAD_DIM) < (DA_HEAD_DIM // 2)
    q_maps = (jnp.where(in_map0, q, zero), jnp.where(in_map0, zero, q))

    def scores(k, s_ref):
        n = k.shape[0]
        maxima = []
        for mp in range(2):
            s = _dot_nt(k, q_maps[mp])
            s_ref[mp, 0:n, :] = s
            maxima.append(jnp.max(s, axis=0, keepdims=True))
        return tuple(maxima)

    def update(s_ref, maxima, vt):
        n = vt.shape[1]
        for mp in range(2):
            m_old = m_ref[mp]
            m_new = jnp.maximum(m_old, maxima[mp])
            alpha = jnp.exp2(m_old - m_new)
            p = jnp.exp2(s_ref[mp, 0:n, :] - m_new[0:1, :])
            acc_ref[mp] = alpha[0:1, :] * acc_ref[mp] + _dot(vt, p.astype(vt.dtype))
            m_ref[mp] = m_new

    m_ref[...] = jnp.full(m_ref.shape, -jnp.inf, F32)
    acc_ref[...] = jnp.zeros_like(acc_ref)
    mx = scores(kc_ref[...], sa_ref)

    if not has_lat:
        update(sa_ref, mx, vct_ref[...])
    else:
        tkc = min(ATT_TKC, kl_ref.shape[0])
        n_chunks = kl_ref.shape[0] // tkc
        k_at = lambda c: kl_ref[pl.ds(pl.multiple_of(c * tkc, tkc), tkc), :]
        vt_at = lambda c: vlt_ref[:, pl.ds(pl.multiple_of(c * tkc, tkc), tkc)]

        mx_ctx = mx
        mx = scores(k_at(0), sb_ref)
        update(sa_ref, mx_ctx, vct_ref[...])

        def pair(c2, mx_b):
            c = 1 + 2 * c2
            mx_a = scores(k_at(c), sa_ref)
            update(sb_ref, mx_b, vt_at(c - 1))
            mx_b = scores(k_at(c + 1), sb_ref)
            update(sa_ref, mx_a, vt_at(c))
            return mx_b

        mx = lax.fori_loop(0, (n_chunks - 1) // 2, pair, mx)
        if (n_chunks - 1) % 2:
            mx_a = scores(k_at(n_chunks - 1), sa_ref)
            update(sb_ref, mx, vt_at(n_chunks - 2))
            update(sa_ref, mx_a, vt_at(n_chunks - 1))
        else:
            update(sb_ref, mx, vt_at(n_chunks - 1))

    lv = lam_ref[...]
    dotp = lambda a, b: jnp.sum(lv[a:a + 1, :] * lv[b:b + 1, :], axis=-1, keepdims=True)
    lam = jnp.exp(dotp(0, 1)) - jnp.exp(dotp(2, 3)) + lam_init
    vd = DA_V_DIM
    inv_l = 1.0 / acc_ref[:, vd:vd + 1, :]
    o = acc_ref[0, 0:vd, :] * inv_l[0] - lam * (acc_ref[1, 0:vd, :] * inv_l[1])
    y = o * lax.rsqrt(jnp.mean(o * o, axis=0, keepdims=True) + NORM_EPS)
    o_ref[...] = ((y * g_ref[...]) * (1.0 - lam_init)).T.astype(o_ref.dtype)


VT_ROWS = DA_V_DIM + 16


def _attn_scratch(tq, keys_a, keys_b):
    return [pltpu.VMEM((2, SUBLANE, tq), F32), pltpu.VMEM((2, VT_ROWS, tq), F32),
            pltpu.VMEM((2, keys_a, tq), F32), pltpu.VMEM((2, keys_b, tq), F32)]


def _attention_lat(qkv, v_t, da_lambda, subln, lam_init, batch, seq, ctx_len):
    h = DA_HEADS
    tq = ATT_TQ
    nq = seq // tq
    ctx_blk0 = (batch * seq) // ctx_len
    assert seq % tq == 0 and seq % min(ATT_TKC, seq) == 0
    kern = functools.partial(_attn_kernel, has_lat=True, lam_init=lam_init)
    return pl.pallas_call(
        kern,
        grid=(batch, h, nq),
        in_specs=[pl.BlockSpec((tq, LANE), lambda b, hh, i: (b * nq + i, hh)),
                  pl.BlockSpec((ctx_len, LANE), lambda b, hh, i: (ctx_blk0 + b, h + hh)),
                  pl.BlockSpec((VT_ROWS, ctx_len), lambda b, hh, i: (hh, ctx_blk0 + b)),
                  pl.BlockSpec((seq, LANE), lambda b, hh, i: (b, h + hh)),
                  pl.BlockSpec((VT_ROWS, seq), lambda b, hh, i: (hh, b)),
                  pl.BlockSpec(da_lambda.shape, lambda b, hh, i: (0, 0)),
                  pl.BlockSpec((DA_V_DIM, 1), lambda b, hh, i: (0, 0))],
        out_specs=pl.BlockSpec((tq, LANE), lambda b, hh, i: (b * nq + i, hh)),
        out_shape=jax.ShapeDtypeStruct((batch * seq, DA_WIDTH), BF16),
        scratch_shapes=_attn_scratch(tq, max(min(ATT_TKC, seq), ctx_len), min(ATT_TKC, seq)),
        compiler_params=_params(("parallel", "parallel", "parallel"), 48),
        name="diff_attention",
    )(qkv, qkv, v_t, qkv, v_t, da_lambda, subln)


def _attention_ctx(qkv, v_t, da_lambda, subln, lam_init, batch, seq, ctx_len):
    h = DA_HEADS
    ctx_blk0 = (batch * seq) // ctx_len
    kern = functools.partial(_attn_kernel, has_lat=False, lam_init=lam_init)
    return pl.pallas_call(
        kern,
        grid=(batch, h),
        in_specs=[pl.BlockSpec((ctx_len, LANE), lambda b, hh: (ctx_blk0 + b, hh)),
                  pl.BlockSpec((ctx_len, LANE), lambda b, hh: (ctx_blk0 + b, h + hh)),
                  pl.BlockSpec((VT_ROWS, ctx_len), lambda b, hh: (hh, ctx_blk0 + b)),
                  pl.BlockSpec(da_lambda.shape, lambda b, hh: (0, 0)),
                  pl.BlockSpec((DA_V_DIM, 1), lambda b, hh: (0, 0))],
        out_specs=pl.BlockSpec((ctx_len, LANE), lambda b, hh: (b, hh)),
        out_shape=jax.ShapeDtypeStruct((batch * ctx_len, DA_WIDTH), BF16),
        scratch_shapes=_attn_scratch(ctx_len, ctx_len, SUBLANE),
        compiler_params=_params(("parallel", "parallel")),
        name="diff_attention_ctx",
    )(qkv, qkv, v_t, da_lambda, subln)


def _outproj_kernel(ya_ref, yf_ref, yr_ref, xs_ref, z_ref, dsk_ref, gn_ref, yc_ref, ycx_ref,
                    w0_ref, w1_ref, w2_ref, xl_ref, xc_ref, gate_ref, o_ref, yb_ref, *, n_lat_tiles):
    @pl.when(pl.program_id(1) == 0)
    def _():
        yb_ref[...] = _ssd_gate_norm(yf_ref[...], yr_ref[...], xs_ref[...], z_ref[...], dsk_ref[...],
                                     gn_ref[...]).astype(yb_ref.dtype)

    yc = _lat_or_ctx(yc_ref, ycx_ref, n_lat_tiles)
    acc = _dot(ya_ref[...], w0_ref[...]) + _dot(yb_ref[...], w1_ref[...]) + _dot(yc, w2_ref[...])
    o_ref[...] = _lat_or_ctx(xl_ref, xc_ref, n_lat_tiles) + gate_ref[...] * acc


def _out_proj(ya, y_fwd, y_bwd, xbc, p_a, d_skip, norm_g, yc, yc_ctx, w_out, x_lat, x_ctx, mods,
              n_tiles, n_lat_tiles, cls_of_tile):
    d = x_lat.shape[1]
    tn = 1024
    wa, wb = ya.shape[1], SSD_INNER
    assert wa == wb and yc.shape[1] == wa + wb
    if yc_ctx is None:
        assert n_tiles <= n_lat_tiles
        yc_ctx = yc
    lat_tile = lambda i: jnp.minimum(i, n_lat_tiles - 1)
    ctx_tile = lambda i: jnp.maximum(i - n_lat_tiles, 0)
    row_blk = lambda col: pl.BlockSpec((ROW_TILE, wb), lambda i, j: (i, col))
    vec = pl.BlockSpec((1, wb), lambda i, j: (0, 0))
    return pl.pallas_call(
        functools.partial(_outproj_kernel, n_lat_tiles=n_lat_tiles),
        grid=(n_tiles, d // tn),
        in_specs=[pl.BlockSpec((ROW_TILE, wa), lambda i, j: (i, 0)),
                  row_blk(0), row_blk(0), row_blk(0), row_blk(COL_Z // wb), vec, vec]
                 + _lat_ctx_specs(wa + wb, n_lat_tiles) + [
                  pl.BlockSpec((wa, tn), lambda i, j: (0, j)),
                  pl.BlockSpec((wb, tn), lambda i, j: (1, j)),
                  pl.BlockSpec((wa + wb, tn), lambda i, j: (1, j)),
                  pl.BlockSpec((ROW_TILE, tn), lambda i, j: (lat_tile(i), j)),
                  pl.BlockSpec((ROW_TILE, tn), lambda i, j: (ctx_tile(i), j)),
                  pl.BlockSpec((None, 1, tn), lambda i, j: (cls_of_tile(i), 0, 2 * (d // tn) + j))],
        out_specs=pl.BlockSpec((ROW_TILE, tn), lambda i, j: (i, j)),
        out_shape=jax.ShapeDtypeStruct((n_tiles * ROW_TILE, d), F32),
        scratch_shapes=[pltpu.VMEM((ROW_TILE, wb), BF16)],
        compiler_params=_params(("parallel", "arbitrary"), 40),
        name="out_proj_residual",
    )(ya, y_fwd, y_bwd, xbc, p_a, d_skip, norm_g, yc, yc_ctx, w_out, w_out, w_out, x_lat, x_ctx, mods)


def _ffn_kernel(x_ref, g_ref, sh_ref, sc_ref, gate_ref, wg_ref, wu_ref, wd_ref, o_ref, h_ref, acc_ref):
    f = pl.program_id(1)

    @pl.when(f == 0)
    def _():
        h_ref[...] = _modulated_norm(x_ref[...], g_ref[...], sh_ref[...], sc_ref[...]).astype(h_ref.dtype)
        acc_ref[...] = jnp.zeros_like(acc_ref)

    h = h_ref[...]
    a = _silu(_dot(h, wg_ref[...])) * _dot(h, wu_ref[...])
    acc_ref[...] += _dot(a.astype(BF16), wd_ref[...])

    @pl.when(f == pl.num_programs(1) - 1)
    def _():
        o_ref[...] = x_ref[...] + gate_ref[...] * acc_ref[...]


def _ffn_dense(x, g, mods, wg, wu, wd, n_tiles, cls_of_tile):
    m, d = x.shape
    ff = wg.shape[1]
    tf = 512
    return pl.pallas_call(
        _ffn_kernel,
        grid=(n_tiles, ff // tf),
        in_specs=[pl.BlockSpec((ROW_TILE, d), lambda i, f: (i, 0)),
                  pl.BlockSpec((1, d), lambda i, f: (0, 0)),
                  _mod_spec(3, d, cls_of_tile), _mod_spec(4, d, cls_of_tile), _mod_spec(5, d, cls_of_tile),
                  pl.BlockSpec((d, tf), lambda i, f: (0, f)),
                  pl.BlockSpec((d, tf), lambda i, f: (0, f)),
                  pl.BlockSpec((tf, d), lambda i, f: (f, 0))],
        out_specs=pl.BlockSpec((ROW_TILE, d), lambda i, f: (i, 0)),
        out_shape=jax.ShapeDtypeStruct((n_tiles * ROW_TILE, d), F32),
        scratch_shapes=[pltpu.VMEM((ROW_TILE, d), BF16), pltpu.VMEM((ROW_TILE, d), F32)],
        compiler_params=_params(("parallel", "arbitrary"), 48),
        name="ffn_dense",
    )(x, g.reshape(1, d), mods, mods, mods, wg, wu, wd)


ROUTE_I1, ROUTE_I2, ROUTE_P1, ROUTE_P2 = 0, 1, 2, 3


def _router_kernel(x_ref, g_ref, sh_ref, sc_ref, wr_ref, br_ref, h_ref, route_ref):
    h = _modulated_norm(x_ref[...], g_ref[...], sh_ref[...], sc_ref[...])
    h_ref[...] = h
    logits = _dot_3pass(h, wr_ref[...]) + br_ref[...]
    lane = lax.broadcasted_iota(jnp.int32, logits.shape, 1).astype(F32)
    neg = -jnp.inf
    lg = jnp.where(lane < N_EXPERTS, logits, neg)
    v1 = jnp.max(lg, axis=-1, keepdims=True)
    i1 = jnp.min(jnp.where(lg == v1, lane, float(LANE)), axis=-1, keepdims=True)
    lg2 = jnp.where(lane == i1, neg, lg)
    v2 = jnp.max(lg2, axis=-1, keepdims=True)
    i2 = jnp.min(jnp.where(lg2 == v2, lane, float(LANE)), axis=-1, keepdims=True)
    e = jnp.exp(v2 - v1)
    p1 = 1.0 / (1.0 + e)
    p2 = e / (1.0 + e)
    rec = jnp.where(lane == ROUTE_I1, i1, 0.0) + jnp.where(lane == ROUTE_I2, i2, 0.0)
    route_ref[...] = rec + jnp.where(lane == ROUTE_P1, p1, 0.0) + jnp.where(lane == ROUTE_P2, p2, 0.0)


def _router(x, g, mods, wr, br, n_tiles, cls_of_tile):
    m, d = x.shape
    return pl.pallas_call(
        _router_kernel,
        grid=(n_tiles,),
        in_specs=[pl.BlockSpec((ROW_TILE, d), lambda i: (i, 0)),
                  pl.BlockSpec((1, d), lambda i: (0, 0)),
                  _mod_spec(3, d, cls_of_tile), _mod_spec(4, d, cls_of_tile),
                  pl.BlockSpec((d, LANE), lambda i: (0, 0)),
                  pl.BlockSpec((1, LANE), lambda i: (0, 0))],
        out_specs=[pl.BlockSpec((ROW_TILE, d), lambda i: (i, 0)),
                   pl.BlockSpec((ROW_TILE, LANE), lambda i: (i, 0))],
        out_shape=[jax.ShapeDtypeStruct((n_tiles * ROW_TILE, d), F32),
                   jax.ShapeDtypeStruct((n_tiles * ROW_TILE, LANE), F32)],
        compiler_params=_params(("parallel",), 40),
        name="moe_router",
    )(x, g.reshape(1, d), mods, mods, wr, br)


def _routing_tables(route, n_experts, tile):
    n_tok = route.shape[0]
    experts = route[:, ROUTE_I1:ROUTE_I2 + 1].astype(jnp.int32).reshape(-1)
    onehot = (experts[:, None] == jnp.arange(n_experts, dtype=jnp.int32)[None, :]).astype(jnp.int32)
    running = jnp.cumsum(onehot, axis=0)
    rank = jnp.sum(running * onehot, axis=1) - 1
    counts = running[-1]
    padded = ((counts + tile - 1) // tile) * tile
    ends = jnp.cumsum(padded)
    slot = (ends - padded)[experts] + rank
    n_rows = 2 * n_tok + n_experts * tile
    n_tiles = n_rows // tile
    token = jnp.repeat(jnp.arange(n_tok, dtype=jnp.int32), 2)
    row_token = jnp.zeros((n_rows,), jnp.int32).at[slot].set(token)
    tile_start = jnp.arange(n_tiles, dtype=jnp.int32) * tile
    tile_expert = jnp.minimum(jnp.sum((tile_start[:, None] >= ends[None, :]).astype(jnp.int32), axis=1),
                              n_experts - 1)
    n_active = (ends[-1] // tile).astype(jnp.int32).reshape(1)
    return tile_expert, row_token, n_active, slot.astype(jnp.int32)


MOE_TILE = 256
MOE_FF_SPLIT = 1
GATHER_UNROLL = 8


def _experts_kernel(te_ref, tok_ref, nact_ref, h_hbm, wg_ref, wu_ref, wd_ref, o_ref, xbuf, sem):
    j = pl.program_id(0)
    n_act = nact_ref[0]
    slot = j % 2
    tile = o_ref.shape[0]

    def row_copy(tok, r, s):
        return pltpu.make_async_copy(h_hbm.at[pl.ds(tok, 1), :], xbuf.at[s, pl.ds(r, 1), :], sem.at[s])

    def start_gather(t, s):
        def body(r, carry):
            row_copy(tok_ref[t * tile + r], r, s).start()
            return carry
        lax.fori_loop(0, tile, body, 0, unroll=GATHER_UNROLL)

    def wait_gather(s):
        pltpu.make_async_copy(h_hbm.at[pl.ds(0, tile), :], xbuf.at[s], sem.at[s]).wait()

    @pl.when(j < n_act)
    def _():
        @pl.when(j == 0)
        def _():
            start_gather(0, 0)

        wait_gather(slot)

        @pl.when(j + 1 < n_act)
        def _():
            start_gather(j + 1, 1 - slot)

        h = xbuf[slot].astype(BF16)
        ff = wg_ref.shape[1]
        w = ff // MOE_FF_SPLIT
        y = None
        for part in range(MOE_FF_SPLIT):
            lo = part * w
            a = _silu(_dot(h, wg_ref[:, lo:lo + w])) * _dot(h, wu_ref[:, lo:lo + w])
            yp = _dot(a.astype(BF16), wd_ref[lo:lo + w, :])
            y = yp if y is None else y + yp
        o_ref[...] = y

    @pl.when(j >= n_act)
    def _():
        o_ref[...] = jnp.zeros_like(o_ref)


def _experts(h, tile_expert, row_token, n_active, wg, wu, wd):
    n_e, d, ff = wg.shape
    tile = MOE_TILE
    n_rows = row_token.shape[0]
    assert ff % (MOE_FF_SPLIT * LANE) == 0
    resident = pl.Buffered(1)
    grid_spec = pltpu.PrefetchScalarGridSpec(
        num_scalar_prefetch=3,
        grid=(n_rows // tile,),
        in_specs=[pl.BlockSpec(memory_space=pl.ANY),
                  pl.BlockSpec((None, d, ff), lambda j, te, tok, na: (te[j], 0, 0), pipeline_mode=resident),
                  pl.BlockSpec((None, d, ff), lambda j, te, tok, na: (te[j], 0, 0), pipeline_mode=resident),
                  pl.BlockSpec((None, ff, d), lambda j, te, tok, na: (te[j], 0, 0), pipeline_mode=resident)],
        out_specs=pl.BlockSpec((tile, d), lambda j, te, tok, na: (j, 0)),
        scratch_shapes=[pltpu.VMEM((2, tile, d), F32), pltpu.SemaphoreType.DMA((2,))],
    )
    return pl.pallas_call(
        _experts_kernel,
        grid_spec=grid_spec,
        out_shape=jax.ShapeDtypeStruct((n_rows, d), F32),
        compiler_params=_params(("arbitrary",), 56),
        name="moe_experts",
    )(tile_expert, row_token, n_active, h, wg, wu, wd)


COMBINE_TILE = 256


def _combine_kernel(slot_ref, y_hbm, x_ref, route_ref, gate_ref, gf_ref, o_ref, ybuf, sem, *, final_norm):
    i = pl.program_id(0)
    n = pl.num_programs(0)
    buf = i % 2
    tile = x_ref.shape[0]

    def row_copy(row, k, r, s):
        return pltpu.make_async_copy(y_hbm.at[pl.ds(row, 1), :], ybuf.at[s, pl.ds(k * tile + r, 1), :], sem.at[s])

    def start_gather(t, s):
        def body(r, carry):
            base = 2 * (t * tile + r)
            row_copy(slot_ref[base], 0, r, s).start()
            row_copy(slot_ref[base + 1], 1, r, s).start()
            return carry
        lax.fori_loop(0, tile, body, 0, unroll=GATHER_UNROLL // 2)

    def wait_gather(s):
        pltpu.make_async_copy(y_hbm.at[pl.ds(0, 2 * tile), :], ybuf.at[s], sem.at[s]).wait()

    @pl.when(i == 0)
    def _():
        start_gather(0, 0)

    wait_gather(buf)

    @pl.when(i + 1 < n)
    def _():
        start_gather(i + 1, 1 - buf)

    route = route_ref[...]
    p1 = route[:, ROUTE_P1:ROUTE_P1 + 1]
    p2 = route[:, ROUTE_P2:ROUTE_P2 + 1]
    y = p1 * ybuf[buf, 0:tile, :] + p2 * ybuf[buf, tile:2 * tile, :]
    out = x_ref[...] + gate_ref[...] * y
    if final_norm:
        out = (out * lax.rsqrt(jnp.mean(out * out, axis=-1, keepdims=True) + NORM_EPS)) * gf_ref[...]
    o_ref[...] = out


def _combine(y_rows, slot, x, route, mods, n_tokens, cls_of_tile, final_g):
    d = x.shape[1]
    tile = COMBINE_TILE
    per = ROW_TILE // tile
    final_norm = final_g is not None
    if not final_norm:
        final_g = jnp.ones((d,), F32)
    grid_spec = pltpu.PrefetchScalarGridSpec(
        num_scalar_prefetch=1,
        grid=(n_tokens // tile,),
        in_specs=[pl.BlockSpec(memory_space=pl.ANY),
                  pl.BlockSpec((tile, d), lambda i, s: (i, 0)),
                  pl.BlockSpec((tile, LANE), lambda i, s: (i, 0)),
                  pl.BlockSpec((None, 1, d), lambda i, s: (cls_of_tile(i // per), 0, 5)),
                  pl.BlockSpec((1, d), lambda i, s: (0, 0))],
        out_specs=pl.BlockSpec((tile, d), lambda i, s: (i, 0)),
        scratch_shapes=[pltpu.VMEM((2, 2 * tile, d), F32), pltpu.SemaphoreType.DMA((2,))],
    )
    return pl.pallas_call(
        functools.partial(_combine_kernel, final_norm=final_norm),
        grid_spec=grid_spec,
        out_shape=jax.ShapeDtypeStruct((n_tokens, d), F32),
        compiler_params=_params(("arbitrary",), 48),
        name="moe_combine",
    )(slot, y_rows, x, route, mods, final_g.reshape(1, d))


def _final_kernel(x_ref, g_ref, o_ref):
    x = x_ref[...]
    o_ref[...] = (x * lax.rsqrt(jnp.mean(x * x, axis=-1, keepdims=True) + NORM_EPS)) * g_ref[...]


def _final_norm(x, g, n_tiles):
    d = x.shape[1]
    return pl.pallas_call(
        _final_kernel,
        grid=(n_tiles,),
        in_specs=[pl.BlockSpec((ROW_TILE, d), lambda i: (i, 0)),
                  pl.BlockSpec((1, d), lambda i: (0, 0))],
        out_specs=pl.BlockSpec((ROW_TILE, d), lambda i: (i, 0)),
        out_shape=jax.ShapeDtypeStruct((n_tiles * ROW_TILE, d), F32),
        compiler_params=_params(("parallel",), 40),
        name="final_norm",
    )(x, g.reshape(1, d))


def _pad_lanes(v, width=LANE):
    return jnp.pad(v, [(0, 0)] * (v.ndim - 1) + [(0, width - v.shape[-1])])


def kernel(x, c, ctx, c_ctx, w_mod, b_mod, g_mix, g_ffn, w_in, conv_w, ssd_conv_w, ssd_conv_b, ssd_a_log, ssd_dt_bias, ssd_d, ssd_norm, da_lambda, da_subln, w_out, ffn_w_gate, ffn_w_up, ffn_w_down, moe_w_router, moe_b_router, moe_w_gate, moe_w_up, moe_w_down, g_final):
    batch, seq, d = x.shape
    ctx_len = ctx.shape[1]
    depth = w_mod.shape[0]
    n_lat = batch * seq
    m = n_lat + batch * ctx_len
    assert seq % ROW_TILE == 0 and (batch * ctx_len) % ROW_TILE == 0 and n_lat % ctx_len == 0
    assert ctx_len % HALO_TILE == 0 and ctx_len % SSD_CHUNK == 0 and batch < MOD_CLASSES
    n_lat_tiles = n_lat // ROW_TILE
    n_all_tiles = m // ROW_TILE
    tiles_per_seq = seq // ROW_TILE
    cls_of_tile = lambda i: jnp.minimum(i // tiles_per_seq, batch)

    x_lat, x_ctx = x.reshape(n_lat, d), ctx.reshape(batch * ctx_len, d)
    cvec = jnp.zeros((MOD_CLASSES, d), F32).at[:batch].set(c).at[batch].set(c_ctx)
    mods_all = _mod_vectors(cvec, w_mod, b_mod)
    cos_t, sin_t = _rope_tables(seq)
    qk_order = _qk_column_order()

    for i in range(depth):
        ctx_out = i < depth - 1
        lam_init = 0.8 - 0.6 * math.exp(-0.3 * i)
        n_tiles = n_all_tiles if ctx_out else n_lat_tiles
        mods = mods_all[i].reshape(MOD_CLASSES, 1, N_MOD * d)
        wi = w_in[i]
        w_a = jnp.concatenate([wi[:, COL_CONV:COL_Q], wi[:, COL_XBC:COL_DT]], axis=1).astype(BF16)
        w_q = jnp.take(wi[:, COL_Q:COL_XBC], qk_order, axis=1) * (DA_SCALE * LOG2_E)
        w_k = jnp.take(wi[:, COL_K:COL_V], qk_order, axis=1)
        w_b = jnp.concatenate([w_q, w_k, wi[:, COL_V:COL_V + DA_WIDTH]], axis=1).astype(BF16)
        w_c = _pad_lanes(wi[:, COL_DT:COL_K]).astype(BF16)

        h = _norm_modulate(x_lat, x_ctx, g_mix[i], mods, 0, n_all_tiles, n_lat_tiles, cls_of_tile)
        p_a = _matmul(h, w_a, F32, 1024, "in_proj_conv_z_xbc")
        qkv, v_t = _matmul_rope(h, w_b, cos_t, sin_t, n_lat_tiles, tiles_per_seq)
        p_dt = _matmul(h, w_c, F32, LANE, "in_proj_dt")

        ya, xbc, dt2 = _prep(p_a, p_dt, conv_w[i], ssd_conv_w[i], ssd_conv_b[i].reshape(1, -1),
                             _pad_lanes(ssd_dt_bias[i].reshape(1, -1)), n_lat // HALO_TILE,
                             seq // HALO_TILE, ctx_len // HALO_TILE)
        y_fwd, y_bwd = _ssd_scan(xbc, dt2, ssd_a_log[i], batch, seq // SSD_CHUNK, ctx_len // SSD_CHUNK)

        subln = da_subln[i].reshape(-1, 1)
        yc = _attention_lat(qkv, v_t, da_lambda[i], subln, lam_init, batch, seq, ctx_len)
        yc_ctx = _attention_ctx(qkv, v_t, da_lambda[i], subln, lam_init, batch, seq, ctx_len) if ctx_out else None

        x_mid = _out_proj(ya, y_fwd, y_bwd, xbc, p_a, jnp.repeat(ssd_d[i], SSD_HEAD_DIM).reshape(1, -1),
                          ssd_norm[i].reshape(1, -1), yc, yc_ctx, w_out[i].astype(BF16), x_lat, x_ctx, mods,
                          n_tiles, n_lat_tiles, cls_of_tile)

        j = i // 2
        if i % 2 == 0:
            xs_all = _ffn_dense(x_mid, g_ffn[i], mods, ffn_w_gate[j].astype(BF16), ffn_w_up[j].astype(BF16),
                                ffn_w_down[j].astype(BF16), n_tiles, cls_of_tile)
        else:
            hh, route = _router(x_mid, g_ffn[i], mods, _pad_lanes(moe_w_router[j]),
                                _pad_lanes(moe_b_router[j].reshape(1, -1)), n_tiles, cls_of_tile)
            tile_expert, row_token, n_active, slot = _routing_tables(route, moe_w_gate.shape[1], MOE_TILE)
            y_rows = _experts(hh, tile_expert, row_token, n_active, moe_w_gate[j].astype(BF16),
                              moe_w_up[j].astype(BF16), moe_w_down[j].astype(BF16))
            xs_all = _combine(y_rows, slot, x_mid, route, mods, n_tiles * ROW_TILE, cls_of_tile,
                              None if ctx_out else g_final)
        x_lat = xs_all
        x_ctx = xs_all[n_lat:] if ctx_out else None

    last_is_moe = depth % 2 == 0
    out = x_lat if last_is_moe else _final_norm(x_lat, g_final, n_lat_tiles)
    return out.reshape(batch, seq, d)
```

```python
import functools
import math

import jax
import jax.numpy as jnp
from jax import lax
from jax.experimental import pallas as pl
from jax.experimental.pallas import tpu as pltpu

NORM_EPS = 1e-6
N_MOD = 6
GRID_W = 64

SHORT_CONV = 3
CONV_W = 512

SSD_HEADS = 8
SSD_HEAD_DIM = 64
SSD_INNER = SSD_HEADS * SSD_HEAD_DIM
SSD_STATE = 128
SSD_GROUPS = 2
SSD_CHUNK = 128
SSD_XBC = SSD_INNER + 2 * SSD_GROUPS * SSD_STATE
SSD_GROUP_W = SSD_INNER // SSD_GROUPS

DA_HEADS = 8
DA_HEAD_DIM = 64
DA_V_DIM = 2 * DA_HEAD_DIM
DA_QK = DA_HEADS * 2 * DA_HEAD_DIM
DA_WIDTH = DA_HEADS * DA_V_DIM
DA_SCALE = DA_HEAD_DIM ** -0.5
LOG2_E = math.log2(math.e)
ROPE_THETA = 10000.0

COL_CONV = 0
COL_Z = COL_CONV + 3 * CONV_W
COL_Q = COL_Z + SSD_INNER
COL_XBC = COL_Q + DA_QK
COL_DT = COL_XBC + SSD_XBC
COL_K = COL_DT + 2 * SSD_HEADS
COL_V = COL_K + DA_QK

N_EXPERTS = 8

LANE = 128
SUBLANE = 8
ROW_TILE = 512
HALO_TILE = 256
MOD_CLASSES = 8

F32 = jnp.float32
BF16 = jnp.bfloat16
MIB = 1024 * 1024


def _params(semantics, vmem_mib=None):
    kw = {"dimension_semantics": semantics}
    if vmem_mib is not None:
        kw["vmem_limit_bytes"] = vmem_mib * MIB
    return pltpu.CompilerParams(**kw)


def _silu(v):
    return v * jax.nn.sigmoid(v)


def _dot(a, b):
    return jnp.dot(a, b, preferred_element_type=F32)


def _dot_nt(a, b):
    return lax.dot_general(a, b, (((1,), (1,)), ((), ())), preferred_element_type=F32)


def _dot_tn(a, b):
    return lax.dot_general(a, b, (((0,), (0,)), ((), ())), preferred_element_type=F32)


def _split2(v):
    hi = v.astype(BF16)
    return hi, (v - hi.astype(F32)).astype(BF16)


def _dot_3pass(a, b):
    a_hi, a_lo = _split2(a)
    b_hi, b_lo = _split2(b)
    return _dot(a_hi, b_hi) + (_dot(a_hi, b_lo) + _dot(a_lo, b_hi))


def _modulated_norm(x, g, shift, scale):
    ms = jnp.mean(x * x, axis=-1, keepdims=True)
    y = x * lax.rsqrt(ms + NORM_EPS)
    return (y * g) * (1.0 + scale) + shift


def _mod_spec(k, width, cls_of_tile):
    return pl.BlockSpec((None, 1, width), lambda i, *_: (cls_of_tile(i), 0, k))


def _mod_kernel(c_ref, w_ref, b_ref, o_ref):
    s = _silu(c_ref[...])
    o_ref[...] = _dot_3pass(s, w_ref[...]) + b_ref[...]


def _mod_vectors(cvec, w_mod, b_mod):
    depth, d, n = w_mod.shape
    tn = 1024
    return pl.pallas_call(
        _mod_kernel,
        grid=(depth, n // tn),
        in_specs=[pl.BlockSpec((MOD_CLASSES, d), lambda l, j: (0, 0)),
                  pl.BlockSpec((None, d, tn), lambda l, j: (l, 0, j)),
                  pl.BlockSpec((None, 1, tn), lambda l, j: (l, 0, j))],
        out_specs=pl.BlockSpec((None, MOD_CLASSES, tn), lambda l, j: (l, 0, j)),
        out_shape=jax.ShapeDtypeStruct((depth, MOD_CLASSES, n), F32),
        compiler_params=_params(("parallel", "parallel"), 40),
        name="mod_vectors",
    )(cvec, w_mod, b_mod.reshape(depth, 1, n))


def _lat_ctx_specs(width, n_lat_tiles):
    return [pl.BlockSpec((ROW_TILE, width), lambda i, *_: (jnp.minimum(i, n_lat_tiles - 1), 0)),
            pl.BlockSpec((ROW_TILE, width), lambda i, *_: (jnp.maximum(i - n_lat_tiles, 0), 0))]


def _lat_or_ctx(lat_ref, ctx_ref, n_lat_tiles):
    return jnp.where(pl.program_id(0) < n_lat_tiles, lat_ref[...], ctx_ref[...])


def _norm_kernel(xl_ref, xc_ref, g_ref, sh_ref, sc_ref, h_ref, *, n_lat_tiles):
    x = _lat_or_ctx(xl_ref, xc_ref, n_lat_tiles)
    h_ref[...] = _modulated_norm(x, g_ref[...], sh_ref[...], sc_ref[...]).astype(h_ref.dtype)


def _norm_modulate(x_lat, x_ctx, g, mods, k_shift, n_tiles, n_lat_tiles, cls_of_tile):
    d = x_lat.shape[1]
    return pl.pallas_call(
        functools.partial(_norm_kernel, n_lat_tiles=n_lat_tiles),
        grid=(n_tiles,),
        in_specs=_lat_ctx_specs(d, n_lat_tiles) + [
            pl.BlockSpec((1, d), lambda i: (0, 0)),
            _mod_spec(k_shift, d, cls_of_tile),
            _mod_spec(k_shift + 1, d, cls_of_tile)],
        out_specs=pl.BlockSpec((ROW_TILE, d), lambda i: (i, 0)),
        out_shape=jax.ShapeDtypeStruct((n_tiles * ROW_TILE, d), BF16),
        compiler_params=_params(("parallel",), 40),
        name="norm_modulate",
    )(x_lat, x_ctx, g.reshape(1, d), mods, mods)


def _mm_kernel(x_ref, w_ref, o_ref):
    o_ref[...] = _dot(x_ref[...], w_ref[...]).astype(o_ref.dtype)


def _matmul(x, w, out_dtype, name):
    m, k = x.shape
    n = w.shape[1]
    return pl.pallas_call(
        _mm_kernel,
        grid=(m // ROW_TILE,),
        in_specs=[pl.BlockSpec((ROW_TILE, k), lambda i: (i, 0)),
                  pl.BlockSpec((k, n), lambda i: (0, 0), pipeline_mode=pl.Buffered(1))],
        out_specs=pl.BlockSpec((ROW_TILE, n), lambda i: (i, 0)),
        out_shape=jax.ShapeDtypeStruct((m, n), out_dtype),
        compiler_params=_params(("parallel",), 40),
        name=name,
    )(x, w)


def _mm_rope_kernel(x_ref, w_ref, cos_ref, sin_ref, qk_ref, vt_ref, *, n_lat_tiles):
    x = x_ref[...]
    tm = x.shape[0]
    qk = _dot(x, w_ref[:, 0:2 * DA_QK])
    is_lat = pl.program_id(0) < n_lat_tiles

    @pl.when(is_lat)
    def _():
        c = cos_ref[...]
        s = sin_ref[...]
        for hb in range(2 * DA_QK // LANE):
            blk = qk[:, hb * LANE:(hb + 1) * LANE]
            partner = pltpu.roll(blk, LANE // 2, 1)
            qk_ref[:, hb * LANE:(hb + 1) * LANE] = (blk * c + partner * s).astype(qk_ref.dtype)

    @pl.when(jnp.logical_not(is_lat))
    def _():
        qk_ref[...] = qk.astype(qk_ref.dtype)

    v = _dot(x, w_ref[:, 2 * DA_QK:])
    row = lax.broadcasted_iota(jnp.int32, (VT_ROWS - DA_V_DIM, tm), 0)
    tail = jnp.where(row == 0, 1.0, 0.0).astype(vt_ref.dtype)
    for h in range(DA_HEADS):
        lo = h * VT_ROWS
        vt_ref[lo:lo + DA_V_DIM, :] = v[:, h * DA_V_DIM:(h + 1) * DA_V_DIM].T.astype(vt_ref.dtype)
        vt_ref[lo + DA_V_DIM:lo + VT_ROWS, :] = tail


def _matmul_rope(x, w, cos_t, sin_t, n_lat_tiles, tiles_per_seq):
    m, k = x.shape
    n = w.shape[1]
    assert n == 2 * DA_QK + DA_WIDTH
    kern = functools.partial(_mm_rope_kernel, n_lat_tiles=n_lat_tiles)
    return pl.pallas_call(
        kern,
        grid=(m // ROW_TILE,),
        in_specs=[pl.BlockSpec((ROW_TILE, k), lambda i: (i, 0)),
                  pl.BlockSpec((k, n), lambda i: (0, 0), pipeline_mode=pl.Buffered(1)),
                  pl.BlockSpec((ROW_TILE, LANE), lambda i: (i % tiles_per_seq, 0)),
                  pl.BlockSpec((ROW_TILE, LANE), lambda i: (i % tiles_per_seq, 0))],
        out_specs=[pl.BlockSpec((ROW_TILE, 2 * DA_QK), lambda i: (i, 0)),
                   pl.BlockSpec((DA_HEADS * VT_ROWS, ROW_TILE), lambda i: (0, i))],
        out_shape=[jax.ShapeDtypeStruct((m, 2 * DA_QK), BF16),
                   jax.ShapeDtypeStruct((DA_HEADS * VT_ROWS, m), BF16)],
        compiler_params=_params(("parallel",), 40),
        name="qkv_proj_rope",
    )(x, w, cos_t, sin_t)


def _head_lane_order(w):
    k = w.shape[0]
    q = DA_HEAD_DIM // 4
    w = w.reshape(k, DA_HEADS, 2, 2, 2, q)
    return w.transpose(0, 1, 4, 2, 3, 5).reshape(k, DA_QK)


def _rope_tables(n_tokens):
    rows = n_tokens // GRID_W
    row = jnp.repeat(jnp.arange(rows, dtype=F32), GRID_W)
    col = jnp.broadcast_to(jnp.arange(GRID_W, dtype=F32), (rows, GRID_W)).reshape(-1)
    n_freq = DA_HEAD_DIM // 4
    inv = ROPE_THETA ** (-jnp.arange(n_freq, dtype=F32) / n_freq)
    ang_r = row[:, None] * inv
    ang_c = col[:, None] * inv
    cos_half = jnp.tile(jnp.concatenate([jnp.cos(ang_r), jnp.cos(ang_c)], axis=-1), (1, 2))
    sin_half = jnp.tile(jnp.concatenate([jnp.sin(ang_r), jnp.sin(ang_c)], axis=-1), (1, 2))
    return jnp.concatenate([cos_half, cos_half], axis=-1), jnp.concatenate([-sin_half, sin_half], axis=-1)


def _conv3(u, prev_row, next_row, w_ref, has_prev, has_next):
    tm = u.shape[0]
    row = lax.broadcasted_iota(jnp.int32, u.shape, 0)
    prev_row = jnp.where(has_prev, prev_row, 0.0)
    next_row = jnp.where(has_next, next_row, 0.0)
    before = jnp.where(row == 0, prev_row, pltpu.roll(u, 1, 0))
    after = jnp.where(row == tm - 1, next_row, pltpu.roll(u, tm - 1, 0))
    return before * w_ref[0:1, :] + u * w_ref[1:2, :] + after * w_ref[2:3, :]


def _prep_kernel(cv_ref, cvp_ref, cvn_ref, xb_ref, xbp_ref, xbn_ref, dtr_ref,
                 cw_ref, sw_ref, sb_ref, dtb_ref, ya_ref, xs_ref, dt_ref,
                 *, n_lat_tiles, lat_tiles_per_seq, ctx_tiles_per_seq):
    i = pl.program_id(0)
    is_lat = i < n_lat_tiles
    pos = jnp.where(is_lat, i % lat_tiles_per_seq, (i - n_lat_tiles) % ctx_tiles_per_seq)
    last = jnp.where(is_lat, lat_tiles_per_seq - 1, ctx_tiles_per_seq - 1)
    has_prev = pos != 0
    has_next = pos != last

    w = CONV_W
    cv = cv_ref[...]
    gate_b, u = cv[:, 0:w], cv[:, w:2 * w] * cv[:, 2 * w:3 * w]
    p = cvp_ref[SUBLANE - 1:SUBLANE, :]
    n = cvn_ref[0:1, :]
    u_prev = p[:, w:2 * w] * p[:, 2 * w:3 * w]
    u_next = n[:, w:2 * w] * n[:, 2 * w:3 * w]
    ya_ref[...] = (gate_b * _conv3(u, u_prev, u_next, cw_ref, has_prev, has_next)).astype(ya_ref.dtype)

    xc = _conv3(xb_ref[...], xbp_ref[SUBLANE - 1:SUBLANE, :], xbn_ref[0:1, :], sw_ref, has_prev, has_next)
    xs_ref[...] = _silu(xc + sb_ref[...])

    t = dtr_ref[...] + dtb_ref[...]
    dt = jnp.maximum(t, 0.0) + jnp.log1p(jnp.exp(-jnp.abs(t)))
    dt_ref[0] = dt
    dt_ref[1] = pltpu.roll(dt, LANE - SSD_HEADS, 1)


def _prep(p_a, p_dt, conv_w, ssd_conv_w, ssd_conv_b, dt_bias, n_lat_tiles, lat_tiles_per_seq,
          ctx_tiles_per_seq):
    m = p_a.shape[0]
    tm = HALO_TILE
    per = tm // SUBLANE
    n8 = m // SUBLANE
    cw = 3 * CONV_W
    xbc_blk = COL_Q // SSD_XBC
    prev = lambda i: jnp.maximum(i * per - 1, 0)
    nxt = lambda i: jnp.minimum((i + 1) * per, n8 - 1)
    kern = functools.partial(_prep_kernel, n_lat_tiles=n_lat_tiles, lat_tiles_per_seq=lat_tiles_per_seq,
                             ctx_tiles_per_seq=ctx_tiles_per_seq)
    return pl.pallas_call(
        kern,
        grid=(m // tm,),
        in_specs=[pl.BlockSpec((tm, cw), lambda i: (i, 0)),
                  pl.BlockSpec((SUBLANE, cw), lambda i: (prev(i), 0)),
                  pl.BlockSpec((SUBLANE, cw), lambda i: (nxt(i), 0)),
                  pl.BlockSpec((tm, SSD_XBC), lambda i: (i, xbc_blk)),
                  pl.BlockSpec((SUBLANE, SSD_XBC), lambda i: (prev(i), xbc_blk)),
                  pl.BlockSpec((SUBLANE, SSD_XBC), lambda i: (nxt(i), xbc_blk)),
                  pl.BlockSpec((tm, LANE), lambda i: (i, 0)),
                  pl.BlockSpec((SHORT_CONV, CONV_W), lambda i: (0, 0)),
                  pl.BlockSpec((SHORT_CONV, SSD_XBC), lambda i: (0, 0)),
                  pl.BlockSpec((1, SSD_XBC), lambda i: (0, 0)),
                  pl.BlockSpec((1, LANE), lambda i: (0, 0))],
        out_specs=[pl.BlockSpec((tm, CONV_W), lambda i: (i, 0)),
                   pl.BlockSpec((tm, SSD_XBC), lambda i: (i, 0)),
                   pl.BlockSpec((2, tm, LANE), lambda i: (0, i, 0))],
        out_shape=[jax.ShapeDtypeStruct((m, CONV_W), BF16),
                   jax.ShapeDtypeStruct((m, SSD_XBC), F32),
                   jax.ShapeDtypeStruct((2, m, LANE), F32)],
        compiler_params=_params(("parallel",), 40),
        name="conv_prep",
    )(p_a, p_a, p_a, p_a, p_a, p_a, p_dt, conv_w, ssd_conv_w, ssd_conv_b, dt_bias)


def _split3(v):
    hi = v.astype(BF16)
    rest = v - hi.astype(F32)
    mid = rest.astype(BF16)
    return hi, mid, (rest - mid.astype(F32)).astype(BF16)


def _dot_f32_lhs(a, b01):
    return sum(_dot(piece, b01) for piece in _split3(a))


def _dot_f32_rhs(a01, b):
    return sum(_dot(a01, piece) for piece in _split3(b))


def _ssd_chunk(direction, xs_ref, dt_ref, alog_ref, alog_ch_ref, y_ref, state_ref):
    t = SSD_CHUNK
    r = lax.broadcasted_iota(jnp.int32, (t, t), 0)
    c = lax.broadcasted_iota(jnp.int32, (t, t), 1)
    mask = (r >= c) if direction == 0 else (r <= c)
    tri = mask.astype(BF16)

    er = lax.broadcasted_iota(jnp.int32, (LANE, SSD_INNER), 0)
    ec = lax.broadcasted_iota(jnp.int32, (LANE, SSD_INNER), 1)
    expand = (jnp.right_shift(ec, 6) == er).astype(BF16)

    dt = dt_ref[direction]
    cs = _dot_f32_rhs(tri, dt * (-jnp.exp(alog_ref[direction])))
    cs_t = cs.T
    dt_ch = _dot_f32_lhs(dt, expand)
    da_ch = dt_ch * (-jnp.exp(alog_ch_ref[direction]))
    cs_ch = _dot_f32_rhs(tri, da_ch)
    tot_ch = jnp.sum(da_ch, axis=0, keepdims=True)

    xdt = xs_ref[:, 0:SSD_INNER] * dt_ch
    x_state = xdt * jnp.exp(tot_ch - cs_ch)
    y_scale = jnp.exp(cs_ch)
    carry = jnp.exp(tot_ch)
    gw = SSD_GROUP_W
    heads_per_group = SSD_HEADS // SSD_GROUPS
    lane_head = jnp.right_shift(lax.broadcasted_iota(jnp.int32, (t, gw), 1), 6)

    for g in range(SSD_GROUPS):
        b_lo = SSD_INNER + g * SSD_STATE
        c_lo = SSD_INNER + SSD_GROUPS * SSD_STATE + g * SSD_STATE
        bg = xs_ref[:, b_lo:b_lo + SSD_STATE].astype(BF16)
        cg = xs_ref[:, c_lo:c_lo + SSD_STATE].astype(BF16)
        cb = _dot_nt(cg, bg)
        sg = state_ref[direction, g]
        xdt_g = xdt[:, g * gw:(g + 1) * gw]
        y = _dot(cg, sg.astype(BF16)) * y_scale[:, g * gw:(g + 1) * gw]
        for hh in range(heads_per_group):
            h = g * heads_per_group + hh
            decay = jnp.where(mask, jnp.exp(cs[:, h:h + 1] - cs_t[h:h + 1, :]), 0.0)
            x_h = jnp.where(lane_head == hh, xdt_g, 0.0).astype(BF16)
            y = y + _dot((cb * decay).astype(BF16), x_h)
        y_ref[:, g * gw:(g + 1) * gw] = y
        ds = _dot_tn(bg, x_state[:, g * gw:(g + 1) * gw].astype(BF16))
        state_ref[direction, g] = sg * carry[:, g * gw:(g + 1) * gw] + ds


def _ssd_kernel(xf_ref, dtf_ref, xb_ref, dtb_ref, alog_ref, alog_ch_ref, yf_ref, yb_ref, state_ref):
    @pl.when(pl.program_id(1) == 0)
    def _():
        state_ref[...] = jnp.zeros_like(state_ref)

    _ssd_chunk(0, xf_ref, dtf_ref, alog_ref, alog_ch_ref, yf_ref, state_ref)
    _ssd_chunk(1, xb_ref, dtb_ref, alog_ref, alog_ch_ref, yb_ref, state_ref)


def _ssd_scan(xs, dt2, a_log, batch, n_lat_chunks, n_ctx_chunks):
    m = xs.shape[0]
    t = SSD_CHUNK
    assert SSD_HEAD_DIM == 64 and SSD_HEADS <= LANE
    a_log_ch = jnp.repeat(a_log, SSD_HEAD_DIM, axis=1).reshape(2, 1, SSD_INNER)
    a_log = _pad_lanes(a_log).reshape(2, 1, LANE)

    def chunk(b, d, s):
        j_ctx = s if d == 0 else n_ctx_chunks - 1 - s
        sl = s - n_ctx_chunks
        j_lat = sl if d == 0 else n_lat_chunks - 1 - sl
        return jnp.where(s < n_ctx_chunks, batch * n_lat_chunks + b * n_ctx_chunks + j_ctx,
                         b * n_lat_chunks + j_lat)

    x_spec = lambda d: pl.BlockSpec((t, SSD_XBC), lambda b, s: (chunk(b, d, s), 0))
    dt_spec = lambda d: pl.BlockSpec((2, t, LANE), lambda b, s: (0, chunk(b, d, s), 0))
    y_spec = lambda d: pl.BlockSpec((t, SSD_INNER), lambda b, s: (chunk(b, d, s), 0))
    return pl.pallas_call(
        _ssd_kernel,
        grid=(batch, n_ctx_chunks + n_lat_chunks),
        in_specs=[x_spec(0), dt_spec(0), x_spec(1), dt_spec(1),
                  pl.BlockSpec((2, 1, LANE), lambda b, s: (0, 0, 0)),
                  pl.BlockSpec((2, 1, SSD_INNER), lambda b, s: (0, 0, 0))],
        out_specs=[y_spec(0), y_spec(1)],
        out_shape=[jax.ShapeDtypeStruct((m, SSD_INNER), F32), jax.ShapeDtypeStruct((m, SSD_INNER), F32)],
        scratch_shapes=[pltpu.VMEM((2, SSD_GROUPS, SSD_STATE, SSD_GROUP_W), F32)],
        compiler_params=_params(("parallel", "arbitrary")),
        name="ssd_scan",
    )(xs, dt2, xs, dt2, a_log, a_log_ch)


def _ssd_gate_norm(y_fwd, y_bwd, x, z, d_skip, norm_g):
    u = (d_skip * x + y_fwd + y_bwd) * _silu(z)
    gw = SSD_GROUP_W
    parts = []
    for g in range(SSD_GROUPS):
        ug = u[:, g * gw:(g + 1) * gw]
        ug = ug * lax.rsqrt(jnp.mean(ug * ug, axis=-1, keepdims=True) + NORM_EPS)
        parts.append(ug * norm_g[:, g * gw:(g + 1) * gw])
    return jnp.concatenate(parts, axis=1)


ATT_TQ = 1024
ATT_TKC = 1024


def _attn_kernel(*refs, has_lat, lam_init):
    if has_lat:
        (q_ref, kc_ref, vct_ref, kl_ref, vlt_ref, lam_ref, g_ref, o_ref,
         m_ref, acc_ref, sa_ref, sb_ref) = refs
    else:
        q_ref, kc_ref, vct_ref, lam_ref, g_ref, o_ref, m_ref, acc_ref, sa_ref, sb_ref = refs

    q = q_ref[...]
    lane = lax.broadcasted_iota(jnp.int32, q.shape, 1)
    zero = jnp.zeros_like(q)
    in_map0 = (lane % DA_HEAD_DIM) < (DA_HEAD_DIM // 2)
    q_maps = (jnp.where(in_map0, q, zero), jnp.where(in_map0, zero, q))

    def scores(k, s_ref):
        n = k.shape[0]
        maxima = []
        for mp in range(2):
            s = _dot_nt(k, q_maps[mp])
            s_ref[mp, 0:n, :] = s
            maxima.append(jnp.max(s, axis=0, keepdims=True))
        return tuple(maxima)

    def update(s_ref, maxima, vt):
        n = vt.shape[1]
        for mp in range(2):
            m_old = m_ref[mp]
            m_new = jnp.maximum(m_old, maxima[mp])
            alpha = jnp.exp2(m_old - m_new)
            p = jnp.exp2(s_ref[mp, 0:n, :] - m_new[0:1, :])
            acc_ref[mp] = alpha[0:1, :] * acc_ref[mp] + _dot(vt, p.astype(vt.dtype))
            m_ref[mp] = m_new

    m_ref[...] = jnp.full(m_ref.shape, -jnp.inf, F32)
    acc_ref[...] = jnp.zeros_like(acc_ref)
    mx = scores(kc_ref[...], sa_ref)

    if not has_lat:
        update(sa_ref, mx, vct_ref[...])
    else:
        tkc = min(ATT_TKC, kl_ref.shape[0])
        n_chunks = kl_ref.shape[0] // tkc
        k_at = lambda c: kl_ref[pl.ds(pl.multiple_of(c * tkc, tkc), tkc), :]
        vt_at = lambda c: vlt_ref[:, pl.ds(pl.multiple_of(c * tkc, tkc), tkc)]

        mx_ctx = mx
        mx = scores(k_at(0), sb_ref)
        update(sa_ref, mx_ctx, vct_ref[...])

        def pair(c2, mx_b):
            c = 1 + 2 * c2
            mx_a = scores(k_at(c), sa_ref)
            update(sb_ref, mx_b, vt_at(c - 1))
            mx_b = scores(k_at(c + 1), sb_ref)
            update(sa_ref, mx_a, vt_at(c))
            return mx_b

        mx = lax.fori_loop(0, (n_chunks - 1) // 2, pair, mx)
        if (n_chunks - 1) % 2:
            mx_a = scores(k_at(n_chunks - 1), sa_ref)
            update(sb_ref, mx, vt_at(n_chunks - 2))
            update(sa_ref, mx_a, vt_at(n_chunks - 1))
        else:
            update(sb_ref, mx, vt_at(n_chunks - 1))

    lv = lam_ref[...]
    dotp = lambda a, b: jnp.sum(lv[a:a + 1, :] * lv[b:b + 1, :], axis=-1, keepdims=True)
    lam = jnp.exp(dotp(0, 1)) - jnp.exp(dotp(2, 3)) + lam_init
    vd = DA_V_DIM
    inv_l = 1.0 / acc_ref[:, vd:vd + 1, :]
    o = acc_ref[0, 0:vd, :] * inv_l[0] - lam * (acc_ref[1, 0:vd, :] * inv_l[1])
    y = o * lax.rsqrt(jnp.mean(o * o, axis=0, keepdims=True) + NORM_EPS)
    o_ref[...] = ((y * g_ref[...]) * (1.0 - lam_init)).T.astype(o_ref.dtype)


VT_ROWS = DA_V_DIM + 16


def _attn_scratch(tq, keys_a, keys_b):
    return [pltpu.VMEM((2, SUBLANE, tq), F32), pltpu.VMEM((2, VT_ROWS, tq), F32),
            pltpu.VMEM((2, keys_a, tq), F32), pltpu.VMEM((2, keys_b, tq), F32)]


def _attention_lat(qkv, v_t, da_lambda, subln, lam_init, batch, seq, ctx_len):
    h = DA_HEADS
    tq = ATT_TQ
    nq = seq // tq
    ctx_blk0 = (batch * seq) // ctx_len
    assert seq % tq == 0 and seq % min(ATT_TKC, seq) == 0
    kern = functools.partial(_attn_kernel, has_lat=True, lam_init=lam_init)
    return pl.pallas_call(
        kern,
        grid=(batch, h, nq),
        in_specs=[pl.BlockSpec((tq, LANE), lambda b, hh, i: (b * nq + i, hh)),
                  pl.BlockSpec((ctx_len, LANE), lambda b, hh, i: (ctx_blk0 + b, h + hh)),
                  pl.BlockSpec((VT_ROWS, ctx_len), lambda b, hh, i: (hh, ctx_blk0 + b)),
                  pl.BlockSpec((seq, LANE), lambda b, hh, i: (b, h + hh)),
                  pl.BlockSpec((VT_ROWS, seq), lambda b, hh, i: (hh, b)),
                  pl.BlockSpec(da_lambda.shape, lambda b, hh, i: (0, 0)),
                  pl.BlockSpec((DA_V_DIM, 1), lambda b, hh, i: (0, 0))],
        out_specs=pl.BlockSpec((tq, LANE), lambda b, hh, i: (b * nq + i, hh)),
        out_shape=jax.ShapeDtypeStruct((batch * seq, DA_WIDTH), BF16),
        scratch_shapes=_attn_scratch(tq, max(min(ATT_TKC, seq), ctx_len), min(ATT_TKC, seq)),
        compiler_params=_params(("parallel", "parallel", "parallel"), 48),
        name="diff_attention",
    )(qkv, qkv, v_t, qkv, v_t, da_lambda, subln)


def _attention_ctx(qkv, v_t, da_lambda, subln, lam_init, batch, seq, ctx_len):
    h = DA_HEADS
    ctx_blk0 = (batch * seq) // ctx_len
    kern = functools.partial(_attn_kernel, has_lat=False, lam_init=lam_init)
    return pl.pallas_call(
        kern,
        grid=(batch, h),
        in_specs=[pl.BlockSpec((ctx_len, LANE), lambda b, hh: (ctx_blk0 + b, hh)),
                  pl.BlockSpec((ctx_len, LANE), lambda b, hh: (ctx_blk0 + b, h + hh)),
                  pl.BlockSpec((VT_ROWS, ctx_len), lambda b, hh: (hh, ctx_blk0 + b)),
                  pl.BlockSpec(da_lambda.shape, lambda b, hh: (0, 0)),
                  pl.BlockSpec((DA_V_DIM, 1), lambda b, hh: (0, 0))],
        out_specs=pl.BlockSpec((ctx_len, LANE), lambda b, hh: (b, hh)),
        out_shape=jax.ShapeDtypeStruct((batch * ctx_len, DA_WIDTH), BF16),
        scratch_shapes=_attn_scratch(ctx_len, ctx_len, SUBLANE),
        compiler_params=_params(("parallel", "parallel")),
        name="diff_attention_ctx",
    )(qkv, qkv, v_t, da_lambda, subln)


def _outproj_kernel(ya_ref, yf_ref, yr_ref, xs_ref, z_ref, dsk_ref, gn_ref, yc_ref, ycx_ref,
                    w0_ref, w1_ref, w2_ref, xl_ref, xc_ref, gate_ref, o_ref, *, n_lat_tiles):
    yb = _ssd_gate_norm(yf_ref[...], yr_ref[...], xs_ref[...], z_ref[...], dsk_ref[...], gn_ref[...])
    yc = _lat_or_ctx(yc_ref, ycx_ref, n_lat_tiles)
    acc = _dot(ya_ref[...], w0_ref[...]) + _dot(yb.astype(BF16), w1_ref[...]) + _dot(yc, w2_ref[...])
    o_ref[...] = _lat_or_ctx(xl_ref, xc_ref, n_lat_tiles) + gate_ref[...] * acc


def _out_proj(ya, y_fwd, y_bwd, xbc, p_a, d_skip, norm_g, yc, yc_ctx, w_out, x_lat, x_ctx, mods,
              n_tiles, n_lat_tiles, cls_of_tile):
    d = x_lat.shape[1]
    wa, wb = ya.shape[1], SSD_INNER
    assert wa == wb and yc.shape[1] == wa + wb
    if yc_ctx is None:
        assert n_tiles <= n_lat_tiles
        yc_ctx = yc
    row_blk = lambda col: pl.BlockSpec((ROW_TILE, wb), lambda i: (i, col))
    vec = pl.BlockSpec((1, wb), lambda i: (0, 0))
    w_rows = lambda rows, blk: pl.BlockSpec((rows, d), lambda i: (blk, 0), pipeline_mode=pl.Buffered(1))
    return pl.pallas_call(
        functools.partial(_outproj_kernel, n_lat_tiles=n_lat_tiles),
        grid=(n_tiles,),
        in_specs=[pl.BlockSpec((ROW_TILE, wa), lambda i: (i, 0)),
                  row_blk(0), row_blk(0), row_blk(0), row_blk(COL_Z // wb), vec, vec]
                 + _lat_ctx_specs(wa + wb, n_lat_tiles)
                 + [w_rows(wa, 0), w_rows(wb, 1), w_rows(wa + wb, 1)]
                 + _lat_ctx_specs(d, n_lat_tiles)
                 + [pl.BlockSpec((None, 1, d), lambda i: (cls_of_tile(i), 0, 2))],
        out_specs=pl.BlockSpec((ROW_TILE, d), lambda i: (i, 0)),
        out_shape=jax.ShapeDtypeStruct((n_tiles * ROW_TILE, d), F32),
        compiler_params=_params(("parallel",), 56),
        name="out_proj_residual",
    )(ya, y_fwd, y_bwd, xbc, p_a, d_skip, norm_g, yc, yc_ctx, w_out, w_out, w_out, x_lat, x_ctx, mods)


def _ffn_kernel(x_ref, g_ref, sh_ref, sc_ref, gate_ref, wg_ref, wu_ref, wd_ref, o_ref, h_ref, acc_ref):
    f = pl.program_id(1)

    @pl.when(f == 0)
    def _():
        h_ref[...] = _modulated_norm(x_ref[...], g_ref[...], sh_ref[...], sc_ref[...]).astype(h_ref.dtype)
        acc_ref[...] = jnp.zeros_like(acc_ref)

    h = h_ref[...]
    a = _silu(_dot(h, wg_ref[...])) * _dot(h, wu_ref[...])
    acc_ref[...] += _dot(a.astype(BF16), wd_ref[...])

    @pl.when(f == pl.num_programs(1) - 1)
    def _():
        o_ref[...] = x_ref[...] + gate_ref[...] * acc_ref[...]


def _ffn_dense(x, g, mods, wg, wu, wd, n_tiles, cls_of_tile):
    m, d = x.shape
    ff = wg.shape[1]
    tf = 512
    return pl.pallas_call(
        _ffn_kernel,
        grid=(n_tiles, ff // tf),
        in_specs=[pl.BlockSpec((ROW_TILE, d), lambda i, f: (i, 0)),
                  pl.BlockSpec((1, d), lambda i, f: (0, 0)),
                  _mod_spec(3, d, cls_of_tile), _mod_spec(4, d, cls_of_tile), _mod_spec(5, d, cls_of_tile),
                  pl.BlockSpec((d, tf), lambda i, f: (0, f)),
                  pl.BlockSpec((d, tf), lambda i, f: (0, f)),
                  pl.BlockSpec((tf, d), lambda i, f: (f, 0))],
        out_specs=pl.BlockSpec((ROW_TILE, d), lambda i, f: (i, 0)),
        out_shape=jax.ShapeDtypeStruct((n_tiles * ROW_TILE, d), F32),
        scratch_shapes=[pltpu.VMEM((ROW_TILE, d), BF16), pltpu.VMEM((ROW_TILE, d), F32)],
        compiler_params=_params(("parallel", "arbitrary"), 48),
        name="ffn_dense",
    )(x, g.reshape(1, d), mods, mods, mods, wg, wu, wd)


ROUTE_I1, ROUTE_I2, ROUTE_P1, ROUTE_P2 = 0, 1, 2, 3


def _router_kernel(x_ref, g_ref, sh_ref, sc_ref, wr_ref, br_ref, h_ref, route_ref):
    h = _modulated_norm(x_ref[...], g_ref[...], sh_ref[...], sc_ref[...])
    h_ref[...] = h
    logits = _dot_3pass(h, wr_ref[...]) + br_ref[...]
    lane = lax.broadcasted_iota(jnp.int32, logits.shape, 1).astype(F32)
    neg = -jnp.inf
    lg = jnp.where(lane < N_EXPERTS, logits, neg)
    v1 = jnp.max(lg, axis=-1, keepdims=True)
    i1 = jnp.min(jnp.where(lg == v1, lane, float(LANE)), axis=-1, keepdims=True)
    lg2 = jnp.where(lane == i1, neg, lg)
    v2 = jnp.max(lg2, axis=-1, keepdims=True)
    i2 = jnp.min(jnp.where(lg2 == v2, lane, float(LANE)), axis=-1, keepdims=True)
    e = jnp.exp(v2 - v1)
    p1 = 1.0 / (1.0 + e)
    p2 = e / (1.0 + e)
    rec = jnp.where(lane == ROUTE_I1, i1, 0.0) + jnp.where(lane == ROUTE_I2, i2, 0.0)
    route_ref[...] = rec + jnp.where(lane == ROUTE_P1, p1, 0.0) + jnp.where(lane == ROUTE_P2, p2, 0.0)


def _router(x, g, mods, wr, br, n_tiles, cls_of_tile):
    m, d = x.shape
    return pl.pallas_call(
        _router_kernel,
        grid=(n_tiles,),
        in_specs=[pl.BlockSpec((ROW_TILE, d), lambda i: (i, 0)),
                  pl.BlockSpec((1, d), lambda i: (0, 0)),
                  _mod_spec(3, d, cls_of_tile), _mod_spec(4, d, cls_of_tile),
                  pl.BlockSpec((d, LANE), lambda i: (0, 0)),
                  pl.BlockSpec((1, LANE), lambda i: (0, 0))],
        out_specs=[pl.BlockSpec((ROW_TILE, d), lambda i: (i, 0)),
                   pl.BlockSpec((ROW_TILE, LANE), lambda i: (i, 0))],
        out_shape=[jax.ShapeDtypeStruct((n_tiles * ROW_TILE, d), F32),
                   jax.ShapeDtypeStruct((n_tiles * ROW_TILE, LANE), F32)],
        compiler_params=_params(("parallel",), 40),
        name="moe_router",
    )(x, g.reshape(1, d), mods, mods, wr, br)


def _routing_tables(route, n_experts, tile):
    n_tok = route.shape[0]
    experts = route[:, ROUTE_I1:ROUTE_I2 + 1].astype(jnp.int32).reshape(-1)
    onehot = (experts[:, None] == jnp.arange(n_experts, dtype=jnp.int32)[None, :]).astype(jnp.int32)
    running = jnp.cumsum(onehot, axis=0)
    rank = jnp.sum(running * onehot, axis=1) - 1
    counts = running[-1]
    padded = ((counts + tile - 1) // tile) * tile
    ends = jnp.cumsum(padded)
    slot = (ends - padded)[experts] + rank
    n_rows = 2 * n_tok + n_experts * tile
    n_tiles = n_rows // tile
    token = jnp.repeat(jnp.arange(n_tok, dtype=jnp.int32), 2)
    row_token = jnp.zeros((n_rows,), jnp.int32).at[slot].set(token)
    tile_start = jnp.arange(n_tiles, dtype=jnp.int32) * tile
    tile_expert = jnp.minimum(jnp.sum((tile_start[:, None] >= ends[None, :]).astype(jnp.int32), axis=1),
                              n_experts - 1)
    n_active = (ends[-1] // tile).astype(jnp.int32).reshape(1)
    return tile_expert, row_token, n_active, slot.astype(jnp.int32)


MOE_TILE = 256
MOE_FF_SPLIT = 1
GATHER_UNROLL = 8


def _experts_kernel(te_ref, tok_ref, nact_ref, h_hbm, wg_ref, wu_ref, wd_ref, o_ref, xbuf, sem):
    j = pl.program_id(0)
    n_act = nact_ref[0]
    slot = j % 2
    tile = o_ref.shape[0]

    def row_copy(tok, r, s):
        return pltpu.make_async_copy(h_hbm.at[pl.ds(tok, 1), :], xbuf.at[s, pl.ds(r, 1), :], sem.at[s])

    def start_gather(t, s):
        def body(r, carry):
            row_copy(tok_ref[t * tile + r], r, s).start()
            return carry
        lax.fori_loop(0, tile, body, 0, unroll=GATHER_UNROLL)

    def wait_gather(s):
        pltpu.make_async_copy(h_hbm.at[pl.ds(0, tile), :], xbuf.at[s], sem.at[s]).wait()

    @pl.when(j < n_act)
    def _():
        @pl.when(j == 0)
        def _():
            start_gather(0, 0)

        wait_gather(slot)

        @pl.when(j + 1 < n_act)
        def _():
            start_gather(j + 1, 1 - slot)

        h = xbuf[slot].astype(BF16)
        ff = wg_ref.shape[1]
        w = ff // MOE_FF_SPLIT
        y = None
        for part in range(MOE_FF_SPLIT):
            lo = part * w
            a = _silu(_dot(h, wg_ref[:, lo:lo + w])) * _dot(h, wu_ref[:, lo:lo + w])
            yp = _dot(a.astype(BF16), wd_ref[lo:lo + w, :])
            y = yp if y is None else y + yp
        o_ref[...] = y

    @pl.when(j >= n_act)
    def _():
        o_ref[...] = jnp.zeros_like(o_ref)


def _experts(h, tile_expert, row_token, n_active, wg, wu, wd):
    n_e, d, ff = wg.shape
    tile = MOE_TILE
    n_rows = row_token.shape[0]
    assert ff % (MOE_FF_SPLIT * LANE) == 0
    resident = pl.Buffered(1)
    grid_spec = pltpu.PrefetchScalarGridSpec(
        num_scalar_prefetch=3,
        grid=(n_rows // tile,),
        in_specs=[pl.BlockSpec(memory_space=pl.ANY),
                  pl.BlockSpec((None, d, ff), lambda j, te, tok, na: (te[j], 0, 0), pipeline_mode=resident),
                  pl.BlockSpec((None, d, ff), lambda j, te, tok, na: (te[j], 0, 0), pipeline_mode=resident),
                  pl.BlockSpec((None, ff, d), lambda j, te, tok, na: (te[j], 0, 0), pipeline_mode=resident)],
        out_specs=pl.BlockSpec((tile, d), lambda j, te, tok, na: (j, 0)),
        scratch_shapes=[pltpu.VMEM((2, tile, d), F32), pltpu.SemaphoreType.DMA((2,))],
    )
    return pl.pallas_call(
        _experts_kernel,
        grid_spec=grid_spec,
        out_shape=jax.ShapeDtypeStruct((n_rows, d), F32),
        compiler_params=_params(("arbitrary",), 56),
        name="moe_experts",
    )(tile_expert, row_token, n_active, h, wg, wu, wd)


COMBINE_TILE = 256


def _combine_kernel(slot_ref, y_hbm, x_ref, route_ref, gate_ref, gf_ref, o_ref, ybuf, sem, *, final_norm):
    i = pl.program_id(0)
    n = pl.num_programs(0)
    buf = i % 2
    tile = x_ref.shape[0]

    def row_copy(row, k, r, s):
        return pltpu.make_async_copy(y_hbm.at[pl.ds(row, 1), :], ybuf.at[s, pl.ds(k * tile + r, 1), :], sem.at[s])

    def start_gather(t, s):
        def body(r, carry):
            base = 2 * (t * tile + r)
            row_copy(slot_ref[base], 0, r, s).start()
            row_copy(slot_ref[base + 1], 1, r, s).start()
            return carry
        lax.fori_loop(0, tile, body, 0, unroll=GATHER_UNROLL // 2)

    def wait_gather(s):
        pltpu.make_async_copy(y_hbm.at[pl.ds(0, 2 * tile), :], ybuf.at[s], sem.at[s]).wait()

    @pl.when(i == 0)
    def _():
        start_gather(0, 0)

    wait_gather(buf)

    @pl.when(i + 1 < n)
    def _():
        start_gather(i + 1, 1 - buf)

    route = route_ref[...]
    p1 = route[:, ROUTE_P1:ROUTE_P1 + 1]
    p2 = route[:, ROUTE_P2:ROUTE_P2 + 1]
    y = p1 * ybuf[buf, 0:tile, :] + p2 * ybuf[buf, tile:2 * tile, :]
    out = x_ref[...] + gate_ref[...] * y
    if final_norm:
        out = (out * lax.rsqrt(jnp.mean(out * out, axis=-1, keepdims=True) + NORM_EPS)) * gf_ref[...]
    o_ref[...] = out


def _combine(y_rows, slot, x, route, mods, n_tokens, cls_of_tile, final_g):
    d = x.shape[1]
    tile = COMBINE_TILE
    per = ROW_TILE // tile
    final_norm = final_g is not None
    if not final_norm:
        final_g = jnp.ones((d,), F32)
    grid_spec = pltpu.PrefetchScalarGridSpec(
        num_scalar_prefetch=1,
        grid=(n_tokens // tile,),
        in_specs=[pl.BlockSpec(memory_space=pl.ANY),
                  pl.BlockSpec((tile, d), lambda i, s: (i, 0)),
                  pl.BlockSpec((tile, LANE), lambda i, s: (i, 0)),
                  pl.BlockSpec((None, 1, d), lambda i, s: (cls_of_tile(i // per), 0, 5)),
                  pl.BlockSpec((1, d), lambda i, s: (0, 0))],
        out_specs=pl.BlockSpec((tile, d), lambda i, s: (i, 0)),
        scratch_shapes=[pltpu.VMEM((2, 2 * tile, d), F32), pltpu.SemaphoreType.DMA((2,))],
    )
    return pl.pallas_call(
        functools.partial(_combine_kernel, final_norm=final_norm),
        grid_spec=grid_spec,
        out_shape=jax.ShapeDtypeStruct((n_tokens, d), F32),
        compiler_params=_params(("arbitrary",), 48),
        name="moe_combine",
    )(slot, y_rows, x, route, mods, final_g.reshape(1, d))


def _final_kernel(x_ref, g_ref, o_ref):
    x = x_ref[...]
    o_ref[...] = (x * lax.rsqrt(jnp.mean(x * x, axis=-1, keepdims=True) + NORM_EPS)) * g_ref[...]


def _final_norm(x, g, n_tiles):
    d = x.shape[1]
    return pl.pallas_call(
        _final_kernel,
        grid=(n_tiles,),
        in_specs=[pl.BlockSpec((ROW_TILE, d), lambda i: (i, 0)),
                  pl.BlockSpec((1, d), lambda i: (0, 0))],
        out_specs=pl.BlockSpec((ROW_TILE, d), lambda i: (i, 0)),
        out_shape=jax.ShapeDtypeStruct((n_tiles * ROW_TILE, d), F32),
        compiler_params=_params(("parallel",), 40),
        name="final_norm",
    )(x, g.reshape(1, d))


def _pad_lanes(v, width=LANE):
    return jnp.pad(v, [(0, 0)] * (v.ndim - 1) + [(0, width - v.shape[-1])])


def kernel(x, c, ctx, c_ctx, w_mod, b_mod, g_mix, g_ffn, w_in, conv_w, ssd_conv_w, ssd_conv_b, ssd_a_log, ssd_dt_bias, ssd_d, ssd_norm, da_lambda, da_subln, w_out, ffn_w_gate, ffn_w_up, ffn_w_down, moe_w_router, moe_b_router, moe_w_gate, moe_w_up, moe_w_down, g_final):
    batch, seq, d = x.shape
    ctx_len = ctx.shape[1]
    depth = w_mod.shape[0]
    n_lat = batch * seq
    m = n_lat + batch * ctx_len
    assert seq % ROW_TILE == 0 and (batch * ctx_len) % ROW_TILE == 0 and n_lat % ctx_len == 0
    assert ctx_len % HALO_TILE == 0 and ctx_len % SSD_CHUNK == 0 and batch < MOD_CLASSES
    n_lat_tiles = n_lat // ROW_TILE
    n_all_tiles = m // ROW_TILE
    tiles_per_seq = seq // ROW_TILE
    cls_of_tile = lambda i: jnp.minimum(i // tiles_per_seq, batch)

    x_lat, x_ctx = x.reshape(n_lat, d), ctx.reshape(batch * ctx_len, d)
    cvec = jnp.zeros((MOD_CLASSES, d), F32).at[:batch].set(c).at[batch].set(c_ctx)
    mods_all = _mod_vectors(cvec, w_mod, b_mod)
    cos_t, sin_t = _rope_tables(seq)

    for i in range(depth):
        ctx_out = i < depth - 1
        lam_init = 0.8 - 0.6 * math.exp(-0.3 * i)
        n_tiles = n_all_tiles if ctx_out else n_lat_tiles
        mods = mods_all[i].reshape(MOD_CLASSES, 1, N_MOD * d)
        wi = w_in[i]
        w_a = jnp.concatenate([wi[:, COL_CONV:COL_Q], wi[:, COL_XBC:COL_DT]], axis=1).astype(BF16)
        w_q = _head_lane_order(wi[:, COL_Q:COL_XBC]) * (DA_SCALE * LOG2_E)
        w_k = _head_lane_order(wi[:, COL_K:COL_V])
        w_b = jnp.concatenate([w_q, w_k, wi[:, COL_V:COL_V + DA_WIDTH]], axis=1).astype(BF16)
        w_c = _pad_lanes(wi[:, COL_DT:COL_K]).astype(BF16)

        h = _norm_modulate(x_lat, x_ctx, g_mix[i], mods, 0, n_all_tiles, n_lat_tiles, cls_of_tile)
        p_a = _matmul(h, w_a, F32, "in_proj_conv_z_xbc")
        qkv, v_t = _matmul_rope(h, w_b, cos_t, sin_t, n_lat_tiles, tiles_per_seq)
        p_dt = _matmul(h, w_c, F32, "in_proj_dt")

        ya, xbc, dt2 = _prep(p_a, p_dt, conv_w[i], ssd_conv_w[i], ssd_conv_b[i].reshape(1, -1),
                             _pad_lanes(ssd_dt_bias[i].reshape(1, -1)), n_lat // HALO_TILE,
                             seq // HALO_TILE, ctx_len // HALO_TILE)
        y_fwd, y_bwd = _ssd_scan(xbc, dt2, ssd_a_log[i], batch, seq // SSD_CHUNK, ctx_len // SSD_CHUNK)

        subln = da_subln[i].reshape(-1, 1)
        yc = _attention_lat(qkv, v_t, da_lambda[i], subln, lam_init, batch, seq, ctx_len)
        yc_ctx = _attention_ctx(qkv, v_t, da_lambda[i], subln, lam_init, batch, seq, ctx_len) if ctx_out else None

        x_mid = _out_proj(ya, y_fwd, y_bwd, xbc, p_a, jnp.repeat(ssd_d[i], SSD_HEAD_DIM).reshape(1, -1),
                          ssd_norm[i].reshape(1, -1), yc, yc_ctx, w_out[i].astype(BF16), x_lat, x_ctx, mods,
                          n_tiles, n_lat_tiles, cls_of_tile)

        j = i // 2
        if i % 2 == 0:
            xs_all = _ffn_dense(x_mid, g_ffn[i], mods, ffn_w_gate[j].astype(BF16), ffn_w_up[j].astype(BF16),
                                ffn_w_down[j].astype(BF16), n_tiles, cls_of_tile)
        else:
            hh, route = _router(x_mid, g_ffn[i], mods, _pad_lanes(moe_w_router[j]),
                                _pad_lanes(moe_b_router[j].reshape(1, -1)), n_tiles, cls_of_tile)
            tile_expert, row_token, n_active, slot = _routing_tables(route, moe_w_gate.shape[1], MOE_TILE)
            y_rows = _experts(hh, tile_expert, row_token, n_active, moe_w_gate[j].astype(BF16),
                              moe_w_up[j].astype(BF16), moe_w_down[j].astype(BF16))
            xs_all = _combine(y_rows, slot, x_mid, route, mods, n_tiles * ROW_TILE, cls_of_tile,
                              None if ctx_out else g_final)
        x_lat = xs_all
        x_ctx = xs_all[n_lat:] if ctx_out else None

    last_is_moe = depth % 2 == 0
    out = x_lat if last_is_moe else _final_norm(x_lat, g_final, n_lat_tiles)
    return out.reshape(batch, seq, d)
```

```python
import functools
import math

import jax
import jax.numpy as jnp
from jax import lax
from jax.experimental import pallas as pl
from jax.experimental.pallas import tpu as pltpu

NORM_EPS = 1e-6
N_MOD = 6
GRID_W = 64

SHORT_CONV = 3
CONV_W = 512

SSD_HEADS = 8
SSD_HEAD_DIM = 64
SSD_INNER = SSD_HEADS * SSD_HEAD_DIM
SSD_STATE = 128
SSD_GROUPS = 2
SSD_CHUNK = 128
SSD_XBC = SSD_INNER + 2 * SSD_GROUPS * SSD_STATE
SSD_GROUP_W = SSD_INNER // SSD_GROUPS

DA_HEADS = 8
DA_HEAD_DIM = 64
DA_V_DIM = 2 * DA_HEAD_DIM
DA_QK = DA_HEADS * 2 * DA_HEAD_DIM
DA_WIDTH = DA_HEADS * DA_V_DIM
DA_SCALE = DA_HEAD_DIM ** -0.5
LOG2_E = math.log2(math.e)
ROPE_THETA = 10000.0

COL_CONV = 0
COL_Z = COL_CONV + 3 * CONV_W
COL_Q = COL_Z + SSD_INNER
COL_XBC = COL_Q + DA_QK
COL_DT = COL_XBC + SSD_XBC
COL_K = COL_DT + 2 * SSD_HEADS
COL_V = COL_K + DA_QK

N_EXPERTS = 8

LANE = 128
SUBLANE = 8
ROW_TILE = 512
HALO_TILE = 256
MOD_CLASSES = 8

F32 = jnp.float32
BF16 = jnp.bfloat16
MIB = 1024 * 1024


def _params(semantics, vmem_mib=None):
    kw = {"dimension_semantics": semantics}
    if vmem_mib is not None:
        kw["vmem_limit_bytes"] = vmem_mib * MIB
    return pltpu.CompilerParams(**kw)


def _silu(v):
    return v * jax.nn.sigmoid(v)


def _dot(a, b):
    return jnp.dot(a, b, preferred_element_type=F32)


def _dot_nt(a, b):
    return lax.dot_general(a, b, (((1,), (1,)), ((), ())), preferred_element_type=F32)


def _dot_tn(a, b):
    return lax.dot_general(a, b, (((0,), (0,)), ((), ())), preferred_element_type=F32)


def _split2(v):
    hi = v.astype(BF16)
    return hi, (v - hi.astype(F32)).astype(BF16)


def _dot_3pass(a, b):
    a_hi, a_lo = _split2(a)
    b_hi, b_lo = _split2(b)
    return _dot(a_hi, b_hi) + (_dot(a_hi, b_lo) + _dot(a_lo, b_hi))


def _modulated_norm(x, g, shift, scale):
    ms = jnp.mean(x * x, axis=-1, keepdims=True)
    y = x * lax.rsqrt(ms + NORM_EPS)
    return (y * g) * (1.0 + scale) + shift


def _mod_spec(k, width, cls_of_tile):
    return pl.BlockSpec((None, 1, width), lambda i, *_: (cls_of_tile(i), 0, k))


def _mod_kernel(c_ref, w_ref, b_ref, o_ref):
    s = _silu(c_ref[...])
    o_ref[...] = _dot_3pass(s, w_ref[...]) + b_ref[...]


def _mod_vectors(cvec, w_mod, b_mod):
    depth, d, n = w_mod.shape
    tn = 1024
    return pl.pallas_call(
        _mod_kernel,
        grid=(depth, n // tn),
        in_specs=[pl.BlockSpec((MOD_CLASSES, d), lambda l, j: (0, 0)),
                  pl.BlockSpec((None, d, tn), lambda l, j: (l, 0, j)),
                  pl.BlockSpec((None, 1, tn), lambda l, j: (l, 0, j))],
        out_specs=pl.BlockSpec((None, MOD_CLASSES, tn), lambda l, j: (l, 0, j)),
        out_shape=jax.ShapeDtypeStruct((depth, MOD_CLASSES, n), F32),
        compiler_params=_params(("parallel", "parallel"), 40),
        name="mod_vectors",
    )(cvec, w_mod, b_mod.reshape(depth, 1, n))


def _lat_ctx_specs(width, n_lat_tiles):
    return [pl.BlockSpec((ROW_TILE, width), lambda i, *_: (jnp.minimum(i, n_lat_tiles - 1), 0)),
            pl.BlockSpec((ROW_TILE, width), lambda i, *_: (jnp.maximum(i - n_lat_tiles, 0), 0))]


def _lat_or_ctx(lat_ref, ctx_ref, n_lat_tiles):
    return jnp.where(pl.program_id(0) < n_lat_tiles, lat_ref[...], ctx_ref[...])


def _in_proj_kernel(xl_ref, xc_ref, g_ref, sh_ref, sc_ref, w_ref, h_ref, p_ref, *, n_lat_tiles):
    x = _lat_or_ctx(xl_ref, xc_ref, n_lat_tiles)
    h = _modulated_norm(x, g_ref[...], sh_ref[...], sc_ref[...]).astype(h_ref.dtype)
    h_ref[...] = h
    p_ref[...] = _dot(h, w_ref[...])


def _in_proj(x_lat, x_ctx, g, mods, w, n_tiles, n_lat_tiles, cls_of_tile):
    d = x_lat.shape[1]
    n = w.shape[1]
    lat_spec, ctx_spec = _lat_ctx_specs(d, n_lat_tiles)
    ctx_spec = pl.BlockSpec(ctx_spec.block_shape, ctx_spec.index_map, pipeline_mode=pl.Buffered(1))
    return pl.pallas_call(
        functools.partial(_in_proj_kernel, n_lat_tiles=n_lat_tiles),
        grid=(n_tiles,),
        in_specs=[lat_spec, ctx_spec,
                  pl.BlockSpec((1, d), lambda i: (0, 0)),
                  _mod_spec(0, d, cls_of_tile),
                  _mod_spec(1, d, cls_of_tile),
                  pl.BlockSpec((d, n), lambda i: (0, 0), pipeline_mode=pl.Buffered(1))],
        out_specs=[pl.BlockSpec((ROW_TILE, d), lambda i: (i, 0)),
                   pl.BlockSpec((ROW_TILE, n), lambda i: (i, 0))],
        out_shape=[jax.ShapeDtypeStruct((n_tiles * ROW_TILE, d), BF16),
                   jax.ShapeDtypeStruct((n_tiles * ROW_TILE, n), F32)],
        compiler_params=_params(("parallel",), 56),
        name="in_proj_norm",
    )(x_lat, x_ctx, g.reshape(1, d), mods, mods, w)


def _mm_rope_kernel(x_ref, w_ref, cos_ref, sin_ref, qk_ref, vt_ref, *, n_lat_tiles):
    x = x_ref[...]
    tm = x.shape[0]
    qk = _dot(x, w_ref[:, 0:2 * DA_QK])
    is_lat = pl.program_id(0) < n_lat_tiles

    @pl.when(is_lat)
    def _():
        c = cos_ref[...]
        s = sin_ref[...]
        for hb in range(2 * DA_QK // LANE):
            blk = qk[:, hb * LANE:(hb + 1) * LANE]
            partner = pltpu.roll(blk, LANE // 2, 1)
            qk_ref[:, hb * LANE:(hb + 1) * LANE] = (blk * c + partner * s).astype(qk_ref.dtype)

    @pl.when(jnp.logical_not(is_lat))
    def _():
        qk_ref[...] = qk.astype(qk_ref.dtype)

    v = _dot(x, w_ref[:, 2 * DA_QK:])
    row = lax.broadcasted_iota(jnp.int32, (VT_ROWS - DA_V_DIM, tm), 0)
    tail = jnp.where(row == 0, 1.0, 0.0).astype(vt_ref.dtype)
    for h in range(DA_HEADS):
        lo = h * VT_ROWS
        vt_ref[lo:lo + DA_V_DIM, :] = v[:, h * DA_V_DIM:(h + 1) * DA_V_DIM].T.astype(vt_ref.dtype)
        vt_ref[lo + DA_V_DIM:lo + VT_ROWS, :] = tail


def _matmul_rope(x, w, cos_t, sin_t, n_lat_tiles, tiles_per_seq):
    m, k = x.shape
    n = w.shape[1]
    assert n == 2 * DA_QK + DA_WIDTH
    kern = functools.partial(_mm_rope_kernel, n_lat_tiles=n_lat_tiles)
    return pl.pallas_call(
        kern,
        grid=(m // ROW_TILE,),
        in_specs=[pl.BlockSpec((ROW_TILE, k), lambda i: (i, 0)),
                  pl.BlockSpec((k, n), lambda i: (0, 0), pipeline_mode=pl.Buffered(1)),
                  pl.BlockSpec((ROW_TILE, LANE), lambda i: (i % tiles_per_seq, 0)),
                  pl.BlockSpec((ROW_TILE, LANE), lambda i: (i % tiles_per_seq, 0))],
        out_specs=[pl.BlockSpec((ROW_TILE, 2 * DA_QK), lambda i: (i, 0)),
                   pl.BlockSpec((DA_HEADS * VT_ROWS, ROW_TILE), lambda i: (0, i))],
        out_shape=[jax.ShapeDtypeStruct((m, 2 * DA_QK), BF16),
                   jax.ShapeDtypeStruct((DA_HEADS * VT_ROWS, m), BF16)],
        compiler_params=_params(("parallel",), 40),
        name="qkv_proj_rope",
    )(x, w, cos_t, sin_t)


def _head_lane_order(w):
    k = w.shape[0]
    q = DA_HEAD_DIM // 4
    w = w.reshape(k, DA_HEADS, 2, 2, 2, q)
    return w.transpose(0, 1, 4, 2, 3, 5).reshape(k, DA_QK)


def _rope_tables(n_tokens):
    rows = n_tokens // GRID_W
    row = jnp.repeat(jnp.arange(rows, dtype=F32), GRID_W)
    col = jnp.broadcast_to(jnp.arange(GRID_W, dtype=F32), (rows, GRID_W)).reshape(-1)
    n_freq = DA_HEAD_DIM // 4
    inv = ROPE_THETA ** (-jnp.arange(n_freq, dtype=F32) / n_freq)
    ang_r = row[:, None] * inv
    ang_c = col[:, None] * inv
    cos_half = jnp.tile(jnp.concatenate([jnp.cos(ang_r), jnp.cos(ang_c)], axis=-1), (1, 2))
    sin_half = jnp.tile(jnp.concatenate([jnp.sin(ang_r), jnp.sin(ang_c)], axis=-1), (1, 2))
    return jnp.concatenate([cos_half, cos_half], axis=-1), jnp.concatenate([-sin_half, sin_half], axis=-1)


def _conv3(u, prev_row, next_row, w_ref, has_prev, has_next):
    tm = u.shape[0]
    row = lax.broadcasted_iota(jnp.int32, u.shape, 0)
    prev_row = jnp.where(has_prev, prev_row, 0.0)
    next_row = jnp.where(has_next, next_row, 0.0)
    before = jnp.where(row == 0, prev_row, pltpu.roll(u, 1, 0))
    after = jnp.where(row == tm - 1, next_row, pltpu.roll(u, tm - 1, 0))
    return before * w_ref[0:1, :] + u * w_ref[1:2, :] + after * w_ref[2:3, :]


def _prep_kernel(cv_ref, cvp_ref, cvn_ref, xb_ref, xbp_ref, xbn_ref, dtr_ref,
                 cw_ref, sw_ref, sb_ref, dtb_ref, ya_ref, xs_ref, dt_ref,
                 *, n_lat_tiles, lat_tiles_per_seq, ctx_tiles_per_seq):
    i = pl.program_id(0)
    is_lat = i < n_lat_tiles
    pos = jnp.where(is_lat, i % lat_tiles_per_seq, (i - n_lat_tiles) % ctx_tiles_per_seq)
    last = jnp.where(is_lat, lat_tiles_per_seq - 1, ctx_tiles_per_seq - 1)
    has_prev = pos != 0
    has_next = pos != last

    w = CONV_W
    cv = cv_ref[...]
    gate_b, u = cv[:, 0:w], cv[:, w:2 * w] * cv[:, 2 * w:3 * w]
    p = cvp_ref[SUBLANE - 1:SUBLANE, :]
    n = cvn_ref[0:1, :]
    u_prev = p[:, w:2 * w] * p[:, 2 * w:3 * w]
    u_next = n[:, w:2 * w] * n[:, 2 * w:3 * w]
    ya_ref[...] = (gate_b * _conv3(u, u_prev, u_next, cw_ref, has_prev, has_next)).astype(ya_ref.dtype)

    xc = _conv3(xb_ref[...], xbp_ref[SUBLANE - 1:SUBLANE, :], xbn_ref[0:1, :], sw_ref, has_prev, has_next)
    xs_ref[...] = _silu(xc + sb_ref[...])

    t = dtr_ref[...] + dtb_ref[...]
    dt = jnp.maximum(t, 0.0) + jnp.log1p(jnp.exp(-jnp.abs(t)))
    dt_ref[0] = dt
    dt_ref[1] = pltpu.roll(dt, LANE - SSD_HEADS, 1)


def _prep(p_a, conv_w, ssd_conv_w, ssd_conv_b, dt_bias, n_lat_tiles, lat_tiles_per_seq,
          ctx_tiles_per_seq):
    m = p_a.shape[0]
    dt_blk = (COL_Q + SSD_XBC) // LANE
    tm = HALO_TILE
    per = tm // SUBLANE
    n8 = m // SUBLANE
    cw = 3 * CONV_W
    xbc_blk = COL_Q // SSD_XBC
    prev = lambda i: jnp.maximum(i * per - 1, 0)
    nxt = lambda i: jnp.minimum((i + 1) * per, n8 - 1)
    kern = functools.partial(_prep_kernel, n_lat_tiles=n_lat_tiles, lat_tiles_per_seq=lat_tiles_per_seq,
                             ctx_tiles_per_seq=ctx_tiles_per_seq)
    return pl.pallas_call(
        kern,
        grid=(m // tm,),
        in_specs=[pl.BlockSpec((tm, cw), lambda i: (i, 0)),
                  pl.BlockSpec((SUBLANE, cw), lambda i: (prev(i), 0)),
                  pl.BlockSpec((SUBLANE, cw), lambda i: (nxt(i), 0)),
                  pl.BlockSpec((tm, SSD_XBC), lambda i: (i, xbc_blk)),
                  pl.BlockSpec((SUBLANE, SSD_XBC), lambda i: (prev(i), xbc_blk)),
                  pl.BlockSpec((SUBLANE, SSD_XBC), lambda i: (nxt(i), xbc_blk)),
                  pl.BlockSpec((tm, LANE), lambda i: (i, dt_blk)),
                  pl.BlockSpec((SHORT_CONV, CONV_W), lambda i: (0, 0)),
                  pl.BlockSpec((SHORT_CONV, SSD_XBC), lambda i: (0, 0)),
                  pl.BlockSpec((1, SSD_XBC), lambda i: (0, 0)),
                  pl.BlockSpec((1, LANE), lambda i: (0, 0))],
        out_specs=[pl.BlockSpec((tm, CONV_W), lambda i: (i, 0)),
                   pl.BlockSpec((tm, SSD_XBC), lambda i: (i, 0)),
                   pl.BlockSpec((2, tm, LANE), lambda i: (0, i, 0))],
        out_shape=[jax.ShapeDtypeStruct((m, CONV_W), BF16),
                   jax.ShapeDtypeStruct((m, SSD_XBC), F32),
                   jax.ShapeDtypeStruct((2, m, LANE), F32)],
        compiler_params=_params(("parallel",), 40),
        name="conv_prep",
    )(p_a, p_a, p_a, p_a, p_a, p_a, p_a, conv_w, ssd_conv_w, ssd_conv_b, dt_bias)


def _split3(v):
    hi = v.astype(BF16)
    rest = v - hi.astype(F32)
    mid = rest.astype(BF16)
    return hi, mid, (rest - mid.astype(F32)).astype(BF16)


def _dot_f32_lhs(a, b01):
    return sum(_dot(piece, b01) for piece in _split3(a))


def _dot_f32_rhs(a01, b):
    return sum(_dot(a01, piece) for piece in _split3(b))


def _ssd_chunk(direction, xs_ref, dt_ref, alog_ref, alog_ch_ref, y_ref, state_ref):
    t = SSD_CHUNK
    r = lax.broadcasted_iota(jnp.int32, (t, t), 0)
    c = lax.broadcasted_iota(jnp.int32, (t, t), 1)
    mask = (r >= c) if direction == 0 else (r <= c)
    tri = mask.astype(BF16)

    er = lax.broadcasted_iota(jnp.int32, (LANE, SSD_INNER), 0)
    ec = lax.broadcasted_iota(jnp.int32, (LANE, SSD_INNER), 1)
    expand = (jnp.right_shift(ec, 6) == er).astype(BF16)

    dt = dt_ref[direction]
    cs = _dot_f32_rhs(tri, dt * (-jnp.exp(alog_ref[direction])))
    cs_t = cs.T
    dt_ch = _dot_f32_lhs(dt, expand)
    da_ch = dt_ch * (-jnp.exp(alog_ch_ref[direction]))
    cs_ch = _dot_f32_rhs(tri, da_ch)
    tot_ch = jnp.sum(da_ch, axis=0, keepdims=True)

    xdt = xs_ref[:, 0:SSD_INNER] * dt_ch
    x_state = xdt * jnp.exp(tot_ch - cs_ch)
    y_scale = jnp.exp(cs_ch)
    carry = jnp.exp(tot_ch)
    gw = SSD_GROUP_W
    heads_per_group = SSD_HEADS // SSD_GROUPS
    lane_head = jnp.right_shift(lax.broadcasted_iota(jnp.int32, (t, gw), 1), 6)

    for g in range(SSD_GROUPS):
        b_lo = SSD_INNER + g * SSD_STATE
        c_lo = SSD_INNER + SSD_GROUPS * SSD_STATE + g * SSD_STATE
        bg = xs_ref[:, b_lo:b_lo + SSD_STATE].astype(BF16)
        cg = xs_ref[:, c_lo:c_lo + SSD_STATE].astype(BF16)
        cb = _dot_nt(cg, bg)
        sg = state_ref[direction, g]
        xdt_g = xdt[:, g * gw:(g + 1) * gw]
        y = _dot(cg, sg.astype(BF16)) * y_scale[:, g * gw:(g + 1) * gw]
        for hh in range(heads_per_group):
            h = g * heads_per_group + hh
            decay = jnp.where(mask, jnp.exp(cs[:, h:h + 1] - cs_t[h:h + 1, :]), 0.0)
            x_h = jnp.where(lane_head == hh, xdt_g, 0.0).astype(BF16)
            y = y + _dot((cb * decay).astype(BF16), x_h)
        y_ref[:, g * gw:(g + 1) * gw] = y
        ds = _dot_tn(bg, x_state[:, g * gw:(g + 1) * gw].astype(BF16))
        state_ref[direction, g] = sg * carry[:, g * gw:(g + 1) * gw] + ds


def _ssd_kernel(xf_ref, dtf_ref, xb_ref, dtb_ref, alog_ref, alog_ch_ref, yf_ref, yb_ref, state_ref):
    @pl.when(pl.program_id(1) == 0)
    def _():
        state_ref[...] = jnp.zeros_like(state_ref)

    _ssd_chunk(0, xf_ref, dtf_ref, alog_ref, alog_ch_ref, yf_ref, state_ref)
    _ssd_chunk(1, xb_ref, dtb_ref, alog_ref, alog_ch_ref, yb_ref, state_ref)


def _ssd_scan(xs, dt2, a_log, batch, n_lat_chunks, n_ctx_chunks):
    m = xs.shape[0]
    t = SSD_CHUNK
    assert SSD_HEAD_DIM == 64 and SSD_HEADS <= LANE
    a_log_ch = jnp.repeat(a_log, SSD_HEAD_DIM, axis=1).reshape(2, 1, SSD_INNER)
    a_log = _pad_lanes(a_log).reshape(2, 1, LANE)

    def chunk(b, d, s):
        j_ctx = s if d == 0 else n_ctx_chunks - 1 - s
        sl = s - n_ctx_chunks
        j_lat = sl if d == 0 else n_lat_chunks - 1 - sl
        return jnp.where(s < n_ctx_chunks, batch * n_lat_chunks + b * n_ctx_chunks + j_ctx,
                         b * n_lat_chunks + j_lat)

    x_spec = lambda d: pl.BlockSpec((t, SSD_XBC), lambda b, s: (chunk(b, d, s), 0))
    dt_spec = lambda d: pl.BlockSpec((2, t, LANE), lambda b, s: (0, chunk(b, d, s), 0))
    y_spec = lambda d: pl.BlockSpec((t, SSD_INNER), lambda b, s: (chunk(b, d, s), 0))
    return pl.pallas_call(
        _ssd_kernel,
        grid=(batch, n_ctx_chunks + n_lat_chunks),
        in_specs=[x_spec(0), dt_spec(0), x_spec(1), dt_spec(1),
                  pl.BlockSpec((2, 1, LANE), lambda b, s: (0, 0, 0)),
                  pl.BlockSpec((2, 1, SSD_INNER), lambda b, s: (0, 0, 0))],
        out_specs=[y_spec(0), y_spec(1)],
        out_shape=[jax.ShapeDtypeStruct((m, SSD_INNER), F32), jax.ShapeDtypeStruct((m, SSD_INNER), F32)],
        scratch_shapes=[pltpu.VMEM((2, SSD_GROUPS, SSD_STATE, SSD_GROUP_W), F32)],
        compiler_params=_params(("parallel", "arbitrary")),
        name="ssd_scan",
    )(xs, dt2, xs, dt2, a_log, a_log_ch)


def _ssd_gate_norm(y_fwd, y_bwd, x, z, d_skip, norm_g):
    u = (d_skip * x + y_fwd + y_bwd) * _silu(z)
    gw = SSD_GROUP_W
    parts = []
    for g in range(SSD_GROUPS):
        ug = u[:, g * gw:(g + 1) * gw]
        ug = ug * lax.rsqrt(jnp.mean(ug * ug, axis=-1, keepdims=True) + NORM_EPS)
        parts.append(ug * norm_g[:, g * gw:(g + 1) * gw])
    return jnp.concatenate(parts, axis=1)


ATT_TQ = 1024
ATT_TKC = 1024


def _attn_kernel(*refs, has_lat, lam_init):
    if has_lat:
        (q_ref, kc_ref, vct_ref, kl_ref, vlt_ref, lam_ref, g_ref, o_ref,
         m_ref, acc_ref, sa_ref, sb_ref) = refs
    else:
        q_ref, kc_ref, vct_ref, lam_ref, g_ref, o_ref, m_ref, acc_ref, sa_ref, sb_ref = refs

    q = q_ref[...]
    lane = lax.broadcasted_iota(jnp.int32, q.shape, 1)
    zero = jnp.zeros_like(q)
    in_map0 = (lane % DA_HEAD_DIM) < (DA_HEAD_DIM // 2)
    q_maps = (jnp.where(in_map0, q, zero), jnp.where(in_map0, zero, q))

    def scores(k, s_ref):
        n = k.shape[0]
        maxima = []
        for mp in range(2):
            s = _dot_nt(k, q_maps[mp])
            s_ref[mp, 0:n, :] = s
            maxima.append(jnp.max(s, axis=0, keepdims=True))
        return tuple(maxima)

    def update(s_ref, maxima, vt):
        n = vt.shape[1]
        for mp in range(2):
            m_old = m_ref[mp]
            m_new = jnp.maximum(m_old, maxima[mp])
            alpha = jnp.exp2(m_old - m_new)
            p = jnp.exp2(s_ref[mp, 0:n, :] - m_new[0:1, :])
            acc_ref[mp] = alpha[0:1, :] * acc_ref[mp] + _dot(vt, p.astype(vt.dtype))
            m_ref[mp] = m_new

    m_ref[...] = jnp.full(m_ref.shape, -jnp.inf, F32)
    acc_ref[...] = jnp.zeros_like(acc_ref)
    mx = scores(kc_ref[...], sa_ref)

    if not has_lat:
        update(sa_ref, mx, vct_ref[...])
    else:
        tkc = min(ATT_TKC, kl_ref.shape[0])
        n_chunks = kl_ref.shape[0] // tkc
        k_at = lambda c: kl_ref[pl.ds(pl.multiple_of(c * tkc, tkc), tkc), :]
        vt_at = lambda c: vlt_ref[:, pl.ds(pl.multiple_of(c * tkc, tkc), tkc)]

        mx_ctx = mx
        mx = scores(k_at(0), sb_ref)
        update(sa_ref, mx_ctx, vct_ref[...])

        def pair(c2, mx_b):
            c = 1 + 2 * c2
            mx_a = scores(k_at(c), sa_ref)
            update(sb_ref, mx_b, vt_at(c - 1))
            mx_b = scores(k_at(c + 1), sb_ref)
            update(sa_ref, mx_a, vt_at(c))
            return mx_b

        mx = lax.fori_loop(0, (n_chunks - 1) // 2, pair, mx)
        if (n_chunks - 1) % 2:
            mx_a = scores(k_at(n_chunks - 1), sa_ref)
            update(sb_ref, mx, vt_at(n_chunks - 2))
            update(sa_ref, mx_a, vt_at(n_chunks - 1))
        else:
            update(sb_ref, mx, vt_at(n_chunks - 1))

    lv = lam_ref[...]
    dotp = lambda a, b: jnp.sum(lv[a:a + 1, :] * lv[b:b + 1, :], axis=-1, keepdims=True)
    lam = jnp.exp(dotp(0, 1)) - jnp.exp(dotp(2, 3)) + lam_init
    vd = DA_V_DIM
    inv_l = 1.0 / acc_ref[:, vd:vd + 1, :]
    o = acc_ref[0, 0:vd, :] * inv_l[0] - lam * (acc_ref[1, 0:vd, :] * inv_l[1])
    y = o * lax.rsqrt(jnp.mean(o * o, axis=0, keepdims=True) + NORM_EPS)
    o_ref[...] = ((y * g_ref[...]) * (1.0 - lam_init)).T.astype(o_ref.dtype)


VT_ROWS = DA_V_DIM + 16


def _attn_scratch(tq, keys_a, keys_b):
    return [pltpu.VMEM((2, SUBLANE, tq), F32), pltpu.VMEM((2, VT_ROWS, tq), F32),
            pltpu.VMEM((2, keys_a, tq), F32), pltpu.VMEM((2, keys_b, tq), F32)]


def _attention_lat(qkv, v_t, da_lambda, subln, lam_init, batch, seq, ctx_len):
    h = DA_HEADS
    tq = ATT_TQ
    nq = seq // tq
    ctx_blk0 = (batch * seq) // ctx_len
    assert seq % tq == 0 and seq % min(ATT_TKC, seq) == 0
    kern = functools.partial(_attn_kernel, has_lat=True, lam_init=lam_init)
    return pl.pallas_call(
        kern,
        grid=(batch, h, nq),
        in_specs=[pl.BlockSpec((tq, LANE), lambda b, hh, i: (b * nq + i, hh)),
                  pl.BlockSpec((ctx_len, LANE), lambda b, hh, i: (ctx_blk0 + b, h + hh)),
                  pl.BlockSpec((VT_ROWS, ctx_len), lambda b, hh, i: (hh, ctx_blk0 + b)),
                  pl.BlockSpec((seq, LANE), lambda b, hh, i: (b, h + hh)),
                  pl.BlockSpec((VT_ROWS, seq), lambda b, hh, i: (hh, b)),
                  pl.BlockSpec(da_lambda.shape, lambda b, hh, i: (0, 0)),
                  pl.BlockSpec((DA_V_DIM, 1), lambda b, hh, i: (0, 0))],
        out_specs=pl.BlockSpec((tq, LANE), lambda b, hh, i: (b * nq + i, hh)),
        out_shape=jax.ShapeDtypeStruct((batch * seq, DA_WIDTH), BF16),
        scratch_shapes=_attn_scratch(tq, max(min(ATT_TKC, seq), ctx_len), min(ATT_TKC, seq)),
        compiler_params=_params(("parallel", "parallel", "parallel"), 48),
        name="diff_attention",
    )(qkv, qkv, v_t, qkv, v_t, da_lambda, subln)


def _attention_ctx(qkv, v_t, da_lambda, subln, lam_init, batch, seq, ctx_len):
    h = DA_HEADS
    ctx_blk0 = (batch * seq) // ctx_len
    kern = functools.partial(_attn_kernel, has_lat=False, lam_init=lam_init)
    return pl.pallas_call(
        kern,
        grid=(batch, h),
        in_specs=[pl.BlockSpec((ctx_len, LANE), lambda b, hh: (ctx_blk0 + b, hh)),
                  pl.BlockSpec((ctx_len, LANE), lambda b, hh: (ctx_blk0 + b, h + hh)),
                  pl.BlockSpec((VT_ROWS, ctx_len), lambda b, hh: (hh, ctx_blk0 + b)),
                  pl.BlockSpec(da_lambda.shape, lambda b, hh: (0, 0)),
                  pl.BlockSpec((DA_V_DIM, 1), lambda b, hh: (0, 0))],
        out_specs=pl.BlockSpec((ctx_len, LANE), lambda b, hh: (b, hh)),
        out_shape=jax.ShapeDtypeStruct((batch * ctx_len, DA_WIDTH), BF16),
        scratch_shapes=_attn_scratch(ctx_len, ctx_len, SUBLANE),
        compiler_params=_params(("parallel", "parallel")),
        name="diff_attention_ctx",
    )(qkv, qkv, v_t, da_lambda, subln)


def _outproj_kernel(ya_ref, yf_ref, yr_ref, xs_ref, z_ref, dsk_ref, gn_ref, yc_ref, ycx_ref,
                    w0_ref, w1_ref, w2_ref, xl_ref, xc_ref, gate_ref, o_ref, *, n_lat_tiles):
    yb = _ssd_gate_norm(yf_ref[...], yr_ref[...], xs_ref[...], z_ref[...], dsk_ref[...], gn_ref[...])
    yc = _lat_or_ctx(yc_ref, ycx_ref, n_lat_tiles)
    acc = _dot(ya_ref[...], w0_ref[...]) + _dot(yb.astype(BF16), w1_ref[...]) + _dot(yc, w2_ref[...])
    o_ref[...] = _lat_or_ctx(xl_ref, xc_ref, n_lat_tiles) + gate_ref[...] * acc


def _out_proj(ya, y_fwd, y_bwd, xbc, p_a, d_skip, norm_g, yc, yc_ctx, w_out, x_lat, x_ctx, mods,
              n_tiles, n_lat_tiles, cls_of_tile):
    d = x_lat.shape[1]
    wa, wb = ya.shape[1], SSD_INNER
    assert wa == wb and yc.shape[1] == wa + wb
    if yc_ctx is None:
        assert n_tiles <= n_lat_tiles
        yc_ctx = yc
    row_blk = lambda col: pl.BlockSpec((ROW_TILE, wb), lambda i: (i, col))
    vec = pl.BlockSpec((1, wb), lambda i: (0, 0))
    w_rows = lambda rows, blk: pl.BlockSpec((rows, d), lambda i: (blk, 0), pipeline_mode=pl.Buffered(1))
    return pl.pallas_call(
        functools.partial(_outproj_kernel, n_lat_tiles=n_lat_tiles),
        grid=(n_tiles,),
        in_specs=[pl.BlockSpec((ROW_TILE, wa), lambda i: (i, 0)),
                  row_blk(0), row_blk(0), row_blk(0), row_blk(COL_Z // wb), vec, vec]
                 + _lat_ctx_specs(wa + wb, n_lat_tiles)
                 + [w_rows(wa, 0), w_rows(wb, 1), w_rows(wa + wb, 1)]
                 + _lat_ctx_specs(d, n_lat_tiles)
                 + [pl.BlockSpec((None, 1, d), lambda i: (cls_of_tile(i), 0, 2))],
        out_specs=pl.BlockSpec((ROW_TILE, d), lambda i: (i, 0)),
        out_shape=jax.ShapeDtypeStruct((n_tiles * ROW_TILE, d), F32),
        compiler_params=_params(("parallel",), 56),
        name="out_proj_residual",
    )(ya, y_fwd, y_bwd, xbc, p_a, d_skip, norm_g, yc, yc_ctx, w_out, w_out, w_out, x_lat, x_ctx, mods)


def _ffn_kernel(x_ref, g_ref, sh_ref, sc_ref, gate_ref, wg_ref, wu_ref, wd_ref, o_ref, h_ref, acc_ref):
    f = pl.program_id(1)

    @pl.when(f == 0)
    def _():
        h_ref[...] = _modulated_norm(x_ref[...], g_ref[...], sh_ref[...], sc_ref[...]).astype(h_ref.dtype)
        acc_ref[...] = jnp.zeros_like(acc_ref)

    h = h_ref[...]
    a = _silu(_dot(h, wg_ref[...])) * _dot(h, wu_ref[...])
    acc_ref[...] += _dot(a.astype(BF16), wd_ref[...])

    @pl.when(f == pl.num_programs(1) - 1)
    def _():
        o_ref[...] = x_ref[...] + gate_ref[...] * acc_ref[...]


def _ffn_dense(x, g, mods, wg, wu, wd, n_tiles, cls_of_tile):
    m, d = x.shape
    ff = wg.shape[1]
    tf = 512
    return pl.pallas_call(
        _ffn_kernel,
        grid=(n_tiles, ff // tf),
        in_specs=[pl.BlockSpec((ROW_TILE, d), lambda i, f: (i, 0)),
                  pl.BlockSpec((1, d), lambda i, f: (0, 0)),
                  _mod_spec(3, d, cls_of_tile), _mod_spec(4, d, cls_of_tile), _mod_spec(5, d, cls_of_tile),
                  pl.BlockSpec((d, tf), lambda i, f: (0, f)),
                  pl.BlockSpec((d, tf), lambda i, f: (0, f)),
                  pl.BlockSpec((tf, d), lambda i, f: (f, 0))],
        out_specs=pl.BlockSpec((ROW_TILE, d), lambda i, f: (i, 0)),
        out_shape=jax.ShapeDtypeStruct((n_tiles * ROW_TILE, d), F32),
        scratch_shapes=[pltpu.VMEM((ROW_TILE, d), BF16), pltpu.VMEM((ROW_TILE, d), F32)],
        compiler_params=_params(("parallel", "arbitrary"), 48),
        name="ffn_dense",
    )(x, g.reshape(1, d), mods, mods, mods, wg, wu, wd)


ROUTE_I1, ROUTE_I2, ROUTE_P1, ROUTE_P2 = 0, 1, 2, 3


def _router_kernel(x_ref, g_ref, sh_ref, sc_ref, wr_ref, br_ref, h_ref, route_ref):
    h = _modulated_norm(x_ref[...], g_ref[...], sh_ref[...], sc_ref[...])
    h_ref[...] = h
    logits = _dot_3pass(h, wr_ref[...]) + br_ref[...]
    lane = lax.broadcasted_iota(jnp.int32, logits.shape, 1).astype(F32)
    neg = -jnp.inf
    lg = jnp.where(lane < N_EXPERTS, logits, neg)
    v1 = jnp.max(lg, axis=-1, keepdims=True)
    i1 = jnp.min(jnp.where(lg == v1, lane, float(LANE)), axis=-1, keepdims=True)
    lg2 = jnp.where(lane == i1, neg, lg)
    v2 = jnp.max(lg2, axis=-1, keepdims=True)
    i2 = jnp.min(jnp.where(lg2 == v2, lane, float(LANE)), axis=-1, keepdims=True)
    e = jnp.exp(v2 - v1)
    p1 = 1.0 / (1.0 + e)
    p2 = e / (1.0 + e)
    rec = jnp.where(lane == ROUTE_I1, i1, 0.0) + jnp.where(lane == ROUTE_I2, i2, 0.0)
    route_ref[...] = rec + jnp.where(lane == ROUTE_P1, p1, 0.0) + jnp.where(lane == ROUTE_P2, p2, 0.0)


def _router(x, g, mods, wr, br, n_tiles, cls_of_tile):
    m, d = x.shape
    return pl.pallas_call(
        _router_kernel,
        grid=(n_tiles,),
        in_specs=[pl.BlockSpec((ROW_TILE, d), lambda i: (i, 0)),
                  pl.BlockSpec((1, d), lambda i: (0, 0)),
                  _mod_spec(3, d, cls_of_tile), _mod_spec(4, d, cls_of_tile),
                  pl.BlockSpec((d, LANE), lambda i: (0, 0)),
                  pl.BlockSpec((1, LANE), lambda i: (0, 0))],
        out_specs=[pl.BlockSpec((ROW_TILE, d), lambda i: (i, 0)),
                   pl.BlockSpec((ROW_TILE, LANE), lambda i: (i, 0))],
        out_shape=[jax.ShapeDtypeStruct((n_tiles * ROW_TILE, d), F32),
                   jax.ShapeDtypeStruct((n_tiles * ROW_TILE, LANE), F32)],
        compiler_params=_params(("parallel",), 40),
        name="moe_router",
    )(x, g.reshape(1, d), mods, mods, wr, br)


def _routing_tables(route, n_experts, tile):
    n_tok = route.shape[0]
    experts = route[:, ROUTE_I1:ROUTE_I2 + 1].astype(jnp.int32).reshape(-1)
    onehot = (experts[:, None] == jnp.arange(n_experts, dtype=jnp.int32)[None, :]).astype(jnp.int32)
    running = jnp.cumsum(onehot, axis=0)
    rank = jnp.sum(running * onehot, axis=1) - 1
    counts = running[-1]
    padded = ((counts + tile - 1) // tile) * tile
    ends = jnp.cumsum(padded)
    slot = (ends - padded)[experts] + rank
    n_rows = 2 * n_tok + n_experts * tile
    n_tiles = n_rows // tile
    token = jnp.repeat(jnp.arange(n_tok, dtype=jnp.int32), 2)
    row_token = jnp.zeros((n_rows,), jnp.int32).at[slot].set(token)
    tile_start = jnp.arange(n_tiles, dtype=jnp.int32) * tile
    tile_expert = jnp.minimum(jnp.sum((tile_start[:, None] >= ends[None, :]).astype(jnp.int32), axis=1),
                              n_experts - 1)
    n_active = (ends[-1] // tile).astype(jnp.int32).reshape(1)
    return tile_expert, row_token, n_active, slot.astype(jnp.int32)


MOE_TILE = 256
MOE_FF_SPLIT = 1
GATHER_UNROLL = 8


def _experts_kernel(te_ref, tok_ref, nact_ref, h_hbm, wg_ref, wu_ref, wd_ref, o_ref, xbuf, sem):
    j = pl.program_id(0)
    n_act = nact_ref[0]
    slot = j % 2
    tile = o_ref.shape[0]

    def row_copy(tok, r, s):
        return pltpu.make_async_copy(h_hbm.at[pl.ds(tok, 1), :], xbuf.at[s, pl.ds(r, 1), :], sem.at[s])

    def start_gather(t, s):
        def body(r, carry):
            row_copy(tok_ref[t * tile + r], r, s).start()
            return carry
        lax.fori_loop(0, tile, body, 0, unroll=GATHER_UNROLL)

    def wait_gather(s):
        pltpu.make_async_copy(h_hbm.at[pl.ds(0, tile), :], xbuf.at[s], sem.at[s]).wait()

    @pl.when(j < n_act)
    def _():
        @pl.when(j == 0)
        def _():
            start_gather(0, 0)

        wait_gather(slot)

        @pl.when(j + 1 < n_act)
        def _():
            start_gather(j + 1, 1 - slot)

        h = xbuf[slot].astype(BF16)
        ff = wg_ref.shape[1]
        w = ff // MOE_FF_SPLIT
        y = None
        for part in range(MOE_FF_SPLIT):
            lo = part * w
            a = _silu(_dot(h, wg_ref[:, lo:lo + w])) * _dot(h, wu_ref[:, lo:lo + w])
            yp = _dot(a.astype(BF16), wd_ref[lo:lo + w, :])
            y = yp if y is None else y + yp
        o_ref[...] = y

    @pl.when(j >= n_act)
    def _():
        o_ref[...] = jnp.zeros_like(o_ref)


def _experts(h, tile_expert, row_token, n_active, wg, wu, wd):
    n_e, d, ff = wg.shape
    tile = MOE_TILE
    n_rows = row_token.shape[0]
    assert ff % (MOE_FF_SPLIT * LANE) == 0
    resident = pl.Buffered(1)
    grid_spec = pltpu.PrefetchScalarGridSpec(
        num_scalar_prefetch=3,
        grid=(n_rows // tile,),
        in_specs=[pl.BlockSpec(memory_space=pl.ANY),
                  pl.BlockSpec((None, d, ff), lambda j, te, tok, na: (te[j], 0, 0), pipeline_mode=resident),
                  pl.BlockSpec((None, d, ff), lambda j, te, tok, na: (te[j], 0, 0), pipeline_mode=resident),
                  pl.BlockSpec((None, ff, d), lambda j, te, tok, na: (te[j], 0, 0), pipeline_mode=resident)],
        out_specs=pl.BlockSpec((tile, d), lambda j, te, tok, na: (j, 0)),
        scratch_shapes=[pltpu.VMEM((2, tile, d), F32), pltpu.SemaphoreType.DMA((2,))],
    )
    return pl.pallas_call(
        _experts_kernel,
        grid_spec=grid_spec,
        out_shape=jax.ShapeDtypeStruct((n_rows, d), F32),
        compiler_params=_params(("arbitrary",), 56),
        name="moe_experts",
    )(tile_expert, row_token, n_active, h, wg, wu, wd)


COMBINE_TILE = 256


def _combine_kernel(slot_ref, y_hbm, x_ref, route_ref, gate_ref, gf_ref, o_ref, ybuf, sem, *, final_norm):
    i = pl.program_id(0)
    n = pl.num_programs(0)
    buf = i % 2
    tile = x_ref.shape[0]

    def row_copy(row, k, r, s):
        return pltpu.make_async_copy(y_hbm.at[pl.ds(row, 1), :], ybuf.at[s, pl.ds(k * tile + r, 1), :], sem.at[s])

    def start_gather(t, s):
        def body(r, carry):
            base = 2 * (t * tile + r)
            row_copy(slot_ref[base], 0, r, s).start()
            row_copy(slot_ref[base + 1], 1, r, s).start()
            return carry
        lax.fori_loop(0, tile, body, 0, unroll=GATHER_UNROLL // 2)

    def wait_gather(s):
        pltpu.make_async_copy(y_hbm.at[pl.ds(0, 2 * tile), :], ybuf.at[s], sem.at[s]).wait()

    @pl.when(i == 0)
    def _():
        start_gather(0, 0)

    wait_gather(buf)

    @pl.when(i + 1 < n)
    def _():
        start_gather(i + 1, 1 - buf)

    route = route_ref[...]
    p1 = route[:, ROUTE_P1:ROUTE_P1 + 1]
    p2 = route[:, ROUTE_P2:ROUTE_P2 + 1]
    y = p1 * ybuf[buf, 0:tile, :] + p2 * ybuf[buf, tile:2 * tile, :]
    out = x_ref[...] + gate_ref[...] * y
    if final_norm:
        out = (out * lax.rsqrt(jnp.mean(out * out, axis=-1, keepdims=True) + NORM_EPS)) * gf_ref[...]
    o_ref[...] = out


def _combine(y_rows, slot, x, route, mods, n_tokens, cls_of_tile, final_g):
    d = x.shape[1]
    tile = COMBINE_TILE
    per = ROW_TILE // tile
    final_norm = final_g is not None
    if not final_norm:
        final_g = jnp.ones((d,), F32)
    grid_spec = pltpu.PrefetchScalarGridSpec(
        num_scalar_prefetch=1,
        grid=(n_tokens // tile,),
        in_specs=[pl.BlockSpec(memory_space=pl.ANY),
                  pl.BlockSpec((tile, d), lambda i, s: (i, 0)),
                  pl.BlockSpec((tile, LANE), lambda i, s: (i, 0)),
                  pl.BlockSpec((None, 1, d), lambda i, s: (cls_of_tile(i // per), 0, 5)),
                  pl.BlockSpec((1, d), lambda i, s: (0, 0))],
        out_specs=pl.BlockSpec((tile, d), lambda i, s: (i, 0)),
        scratch_shapes=[pltpu.VMEM((2, 2 * tile, d), F32), pltpu.SemaphoreType.DMA((2,))],
    )
    return pl.pallas_call(
        functools.partial(_combine_kernel, final_norm=final_norm),
        grid_spec=grid_spec,
        out_shape=jax.ShapeDtypeStruct((n_tokens, d), F32),
        compiler_params=_params(("arbitrary",), 48),
        name="moe_combine",
    )(slot, y_rows, x, route, mods, final_g.reshape(1, d))


def _final_kernel(x_ref, g_ref, o_ref):
    x = x_ref[...]
    o_ref[...] = (x * lax.rsqrt(jnp.mean(x * x, axis=-1, keepdims=True) + NORM_EPS)) * g_ref[...]


def _final_norm(x, g, n_tiles):
    d = x.shape[1]
    return pl.pallas_call(
        _final_kernel,
        grid=(n_tiles,),
        in_specs=[pl.BlockSpec((ROW_TILE, d), lambda i: (i, 0)),
                  pl.BlockSpec((1, d), lambda i: (0, 0))],
        out_specs=pl.BlockSpec((ROW_TILE, d), lambda i: (i, 0)),
        out_shape=jax.ShapeDtypeStruct((n_tiles * ROW_TILE, d), F32),
        compiler_params=_params(("parallel",), 40),
        name="final_norm",
    )(x, g.reshape(1, d))


def _pad_lanes(v, width=LANE):
    return jnp.pad(v, [(0, 0)] * (v.ndim - 1) + [(0, width - v.shape[-1])])


def kernel(x, c, ctx, c_ctx, w_mod, b_mod, g_mix, g_ffn, w_in, conv_w, ssd_conv_w, ssd_conv_b, ssd_a_log, ssd_dt_bias, ssd_d, ssd_norm, da_lambda, da_subln, w_out, ffn_w_gate, ffn_w_up, ffn_w_down, moe_w_router, moe_b_router, moe_w_gate, moe_w_up, moe_w_down, g_final):
    batch, seq, d = x.shape
    ctx_len = ctx.shape[1]
    depth = w_mod.shape[0]
    n_lat = batch * seq
    m = n_lat + batch * ctx_len
    assert seq % ROW_TILE == 0 and (batch * ctx_len) % ROW_TILE == 0 and n_lat % ctx_len == 0
    assert ctx_len % HALO_TILE == 0 and ctx_len % SSD_CHUNK == 0 and batch < MOD_CLASSES
    n_lat_tiles = n_lat // ROW_TILE
    n_all_tiles = m // ROW_TILE
    tiles_per_seq = seq // ROW_TILE
    cls_of_tile = lambda i: jnp.minimum(i // tiles_per_seq, batch)

    x_lat, x_ctx = x.reshape(n_lat, d), ctx.reshape(batch * ctx_len, d)
    cvec = jnp.zeros((MOD_CLASSES, d), F32).at[:batch].set(c).at[batch].set(c_ctx)
    mods_all = _mod_vectors(cvec, w_mod, b_mod)
    cos_t, sin_t = _rope_tables(seq)

    for i in range(depth):
        ctx_out = i < depth - 1
        lam_init = 0.8 - 0.6 * math.exp(-0.3 * i)
        n_tiles = n_all_tiles if ctx_out else n_lat_tiles
        mods = mods_all[i].reshape(MOD_CLASSES, 1, N_MOD * d)
        wi = w_in[i]
        w_a = jnp.concatenate([wi[:, COL_CONV:COL_Q], wi[:, COL_XBC:COL_DT],
                               _pad_lanes(wi[:, COL_DT:COL_K])], axis=1).astype(BF16)
        w_q = _head_lane_order(wi[:, COL_Q:COL_XBC]) * (DA_SCALE * LOG2_E)
        w_k = _head_lane_order(wi[:, COL_K:COL_V])
        w_b = jnp.concatenate([w_q, w_k, wi[:, COL_V:COL_V + DA_WIDTH]], axis=1).astype(BF16)

        h, p_a = _in_proj(x_lat, x_ctx, g_mix[i], mods, w_a, n_all_tiles, n_lat_tiles, cls_of_tile)
        qkv, v_t = _matmul_rope(h, w_b, cos_t, sin_t, n_lat_tiles, tiles_per_seq)

        ya, xbc, dt2 = _prep(p_a, conv_w[i], ssd_conv_w[i], ssd_conv_b[i].reshape(1, -1),
                             _pad_lanes(ssd_dt_bias[i].reshape(1, -1)), n_lat // HALO_TILE,
                             seq // HALO_TILE, ctx_len // HALO_TILE)
        y_fwd, y_bwd = _ssd_scan(xbc, dt2, ssd_a_log[i], batch, seq // SSD_CHUNK, ctx_len // SSD_CHUNK)

        subln = da_subln[i].reshape(-1, 1)
        yc = _attention_lat(qkv, v_t, da_lambda[i], subln, lam_init, batch, seq, ctx_len)
        yc_ctx = _attention_ctx(qkv, v_t, da_lambda[i], subln, lam_init, batch, seq, ctx_len) if ctx_out else None

        x_mid = _out_proj(ya, y_fwd, y_bwd, xbc, p_a, jnp.repeat(ssd_d[i], SSD_HEAD_DIM).reshape(1, -1),
                          ssd_norm[i].reshape(1, -1), yc, yc_ctx, w_out[i].astype(BF16), x_lat, x_ctx, mods,
                          n_tiles, n_lat_tiles, cls_of_tile)

        j = i // 2
        if i % 2 == 0:
            xs_all = _ffn_dense(x_mid, g_ffn[i], mods, ffn_w_gate[j].astype(BF16), ffn_w_up[j].astype(BF16),
                                ffn_w_down[j].astype(BF16), n_tiles, cls_of_tile)
        else:
            hh, route = _router(x_mid, g_ffn[i], mods, _pad_lanes(moe_w_router[j]),
                                _pad_lanes(moe_b_router[j].reshape(1, -1)), n_tiles, cls_of_tile)
            tile_expert, row_token, n_active, slot = _routing_tables(route, moe_w_gate.shape[1], MOE_TILE)
            y_rows = _experts(hh, tile_expert, row_token, n_active, moe_w_gate[j].astype(BF16),
                              moe_w_up[j].astype(BF16), moe_w_down[j].astype(BF16))
            xs_all = _combine(y_rows, slot, x_mid, route, mods, n_tiles * ROW_TILE, cls_of_tile,
                              None if ctx_out else g_final)
        x_lat = xs_all
        x_ctx = xs_all[n_lat:] if ctx_out else None

    last_is_moe = depth % 2 == 0
    out = x_lat if last_is_moe else _final_norm(x_lat, g_final, n_lat_tiles)
    return out.reshape(batch, seq, d)
```

```python
import functools
import math

import jax
import jax.numpy as jnp
from jax import lax
from jax.experimental import pallas as pl
from jax.experimental.pallas import tpu as pltpu

NORM_EPS = 1e-6
N_MOD = 6
GRID_W = 64

SHORT_CONV = 3
CONV_W = 512

SSD_HEADS = 8
SSD_HEAD_DIM = 64
SSD_INNER = SSD_HEADS * SSD_HEAD_DIM
SSD_STATE = 128
SSD_GROUPS = 2
SSD_CHUNK = 128
SSD_XBC = SSD_INNER + 2 * SSD_GROUPS * SSD_STATE
SSD_GROUP_W = SSD_INNER // SSD_GROUPS

DA_HEADS = 8
DA_HEAD_DIM = 64
DA_V_DIM = 2 * DA_HEAD_DIM
DA_QK = DA_HEADS * 2 * DA_HEAD_DIM
DA_WIDTH = DA_HEADS * DA_V_DIM
DA_SCALE = DA_HEAD_DIM ** -0.5
LOG2_E = math.log2(math.e)
ROPE_THETA = 10000.0

COL_CONV = 0
COL_Z = COL_CONV + 3 * CONV_W
COL_Q = COL_Z + SSD_INNER
COL_XBC = COL_Q + DA_QK
COL_DT = COL_XBC + SSD_XBC
COL_K = COL_DT + 2 * SSD_HEADS
COL_V = COL_K + DA_QK

N_EXPERTS = 8

LANE = 128
SUBLANE = 8
ROW_TILE = 512
HALO_TILE = 256
MOD_CLASSES = 8

F32 = jnp.float32
BF16 = jnp.bfloat16
MIB = 1024 * 1024


def _params(semantics, vmem_mib=None):
    kw = {"dimension_semantics": semantics}
    if vmem_mib is not None:
        kw["vmem_limit_bytes"] = vmem_mib * MIB
    return pltpu.CompilerParams(**kw)


def _silu(v):
    return v * jax.nn.sigmoid(v)


def _dot(a, b):
    return jnp.dot(a, b, preferred_element_type=F32)


def _dot_nt(a, b):
    return lax.dot_general(a, b, (((1,), (1,)), ((), ())), preferred_element_type=F32)


def _dot_tn(a, b):
    return lax.dot_general(a, b, (((0,), (0,)), ((), ())), preferred_element_type=F32)


def _split2(v):
    hi = v.astype(BF16)
    return hi, (v - hi.astype(F32)).astype(BF16)


def _dot_3pass(a, b):
    a_hi, a_lo = _split2(a)
    b_hi, b_lo = _split2(b)
    return _dot(a_hi, b_hi) + (_dot(a_hi, b_lo) + _dot(a_lo, b_hi))


def _modulated_norm(x, g, shift, scale):
    ms = jnp.mean(x * x, axis=-1, keepdims=True)
    y = x * lax.rsqrt(ms + NORM_EPS)
    return (y * g) * (1.0 + scale) + shift


def _mod_spec(k, width, cls_of_tile):
    return pl.BlockSpec((None, 1, width), lambda i, *_: (cls_of_tile(i), 0, k))


def _mod_kernel(c_ref, w_ref, b_ref, o_ref):
    s = _silu(c_ref[...])
    o_ref[...] = _dot_3pass(s, w_ref[...]) + b_ref[...]


def _mod_vectors(cvec, w_mod, b_mod):
    depth, d, n = w_mod.shape
    tn = 1024
    return pl.pallas_call(
        _mod_kernel,
        grid=(depth, n // tn),
        in_specs=[pl.BlockSpec((MOD_CLASSES, d), lambda l, j: (0, 0)),
                  pl.BlockSpec((None, d, tn), lambda l, j: (l, 0, j)),
                  pl.BlockSpec((None, 1, tn), lambda l, j: (l, 0, j))],
        out_specs=pl.BlockSpec((None, MOD_CLASSES, tn), lambda l, j: (l, 0, j)),
        out_shape=jax.ShapeDtypeStruct((depth, MOD_CLASSES, n), F32),
        compiler_params=_params(("parallel", "parallel"), 40),
        name="mod_vectors",
    )(cvec, w_mod, b_mod.reshape(depth, 1, n))


def _lat_ctx_specs(width, n_lat_tiles):
    return [pl.BlockSpec((ROW_TILE, width), lambda i, *_: (jnp.minimum(i, n_lat_tiles - 1), 0)),
            pl.BlockSpec((ROW_TILE, width), lambda i, *_: (jnp.maximum(i - n_lat_tiles, 0), 0))]


def _lat_or_ctx(lat_ref, ctx_ref, n_lat_tiles):
    return jnp.where(pl.program_id(0) < n_lat_tiles, lat_ref[...], ctx_ref[...])


def _in_proj_kernel(xl_ref, xc_ref, g_ref, sh_ref, sc_ref, w_ref, h_ref, p_ref, *, n_lat_tiles):
    x = _lat_or_ctx(xl_ref, xc_ref, n_lat_tiles)
    h = _modulated_norm(x, g_ref[...], sh_ref[...], sc_ref[...]).astype(h_ref.dtype)
    h_ref[...] = h
    p_ref[...] = _dot(h, w_ref[...])


def _in_proj(x_lat, x_ctx, g, mods, w, n_tiles, n_lat_tiles, cls_of_tile):
    d = x_lat.shape[1]
    n = w.shape[1]
    lat_spec, ctx_spec = _lat_ctx_specs(d, n_lat_tiles)
    ctx_spec = pl.BlockSpec(ctx_spec.block_shape, ctx_spec.index_map, pipeline_mode=pl.Buffered(1))
    return pl.pallas_call(
        functools.partial(_in_proj_kernel, n_lat_tiles=n_lat_tiles),
        grid=(n_tiles,),
        in_specs=[lat_spec, ctx_spec,
                  pl.BlockSpec((1, d), lambda i: (0, 0)),
                  _mod_spec(0, d, cls_of_tile),
                  _mod_spec(1, d, cls_of_tile),
                  pl.BlockSpec((d, n), lambda i: (0, 0), pipeline_mode=pl.Buffered(1))],
        out_specs=[pl.BlockSpec((ROW_TILE, d), lambda i: (i, 0)),
                   pl.BlockSpec((ROW_TILE, n), lambda i: (i, 0))],
        out_shape=[jax.ShapeDtypeStruct((n_tiles * ROW_TILE, d), BF16),
                   jax.ShapeDtypeStruct((n_tiles * ROW_TILE, n), F32)],
        compiler_params=_params(("parallel",), 56),
        name="in_proj_norm",
    )(x_lat, x_ctx, g.reshape(1, d), mods, mods, w)


def _mm_rope_kernel(x_ref, w_ref, cos_ref, sin_ref, qk_ref, vt_ref, *, n_lat_tiles):
    x = x_ref[...]
    tm = x.shape[0]
    qk = _dot(x, w_ref[:, 0:2 * DA_QK])
    is_lat = pl.program_id(0) < n_lat_tiles

    @pl.when(is_lat)
    def _():
        c = cos_ref[...]
        s = sin_ref[...]
        for hb in range(2 * DA_QK // LANE):
            blk = qk[:, hb * LANE:(hb + 1) * LANE]
            partner = pltpu.roll(blk, LANE // 2, 1)
            qk_ref[:, hb * LANE:(hb + 1) * LANE] = (blk * c + partner * s).astype(qk_ref.dtype)

    @pl.when(jnp.logical_not(is_lat))
    def _():
        qk_ref[...] = qk.astype(qk_ref.dtype)

    v = _dot(x, w_ref[:, 2 * DA_QK:])
    row = lax.broadcasted_iota(jnp.int32, (VT_ROWS - DA_V_DIM, tm), 0)
    tail = jnp.where(row == 0, 1.0, 0.0).astype(vt_ref.dtype)
    for h in range(DA_HEADS):
        lo = h * VT_ROWS
        vt_ref[lo:lo + DA_V_DIM, :] = v[:, h * DA_V_DIM:(h + 1) * DA_V_DIM].T.astype(vt_ref.dtype)
        vt_ref[lo + DA_V_DIM:lo + VT_ROWS, :] = tail


def _matmul_rope(x, w, cos_t, sin_t, n_lat_tiles, tiles_per_seq):
    m, k = x.shape
    n = w.shape[1]
    assert n == 2 * DA_QK + DA_WIDTH
    kern = functools.partial(_mm_rope_kernel, n_lat_tiles=n_lat_tiles)
    return pl.pallas_call(
        kern,
        grid=(m // ROW_TILE,),
        in_specs=[pl.BlockSpec((ROW_TILE, k), lambda i: (i, 0)),
                  pl.BlockSpec((k, n), lambda i: (0, 0), pipeline_mode=pl.Buffered(1)),
                  pl.BlockSpec((ROW_TILE, LANE), lambda i: (i % tiles_per_seq, 0)),
                  pl.BlockSpec((ROW_TILE, LANE), lambda i: (i % tiles_per_seq, 0))],
        out_specs=[pl.BlockSpec((ROW_TILE, 2 * DA_QK), lambda i: (i, 0)),
                   pl.BlockSpec((DA_HEADS * VT_ROWS, ROW_TILE), lambda i: (0, i))],
        out_shape=[jax.ShapeDtypeStruct((m, 2 * DA_QK), BF16),
                   jax.ShapeDtypeStruct((DA_HEADS * VT_ROWS, m), BF16)],
        compiler_params=_params(("parallel",), 40),
        name="qkv_proj_rope",
    )(x, w, cos_t, sin_t)


def _head_lane_order(w):
    k = w.shape[0]
    q = DA_HEAD_DIM // 4
    w = w.reshape(k, DA_HEADS, 2, 2, 2, q)
    return w.transpose(0, 1, 4, 2, 3, 5).reshape(k, DA_QK)


def _rope_tables(n_tokens):
    rows = n_tokens // GRID_W
    row = jnp.repeat(jnp.arange(rows, dtype=F32), GRID_W)
    col = jnp.broadcast_to(jnp.arange(GRID_W, dtype=F32), (rows, GRID_W)).reshape(-1)
    n_freq = DA_HEAD_DIM // 4
    inv = ROPE_THETA ** (-jnp.arange(n_freq, dtype=F32) / n_freq)
    ang_r = row[:, None] * inv
    ang_c = col[:, None] * inv
    cos_half = jnp.tile(jnp.concatenate([jnp.cos(ang_r), jnp.cos(ang_c)], axis=-1), (1, 2))
    sin_half = jnp.tile(jnp.concatenate([jnp.sin(ang_r), jnp.sin(ang_c)], axis=-1), (1, 2))
    return jnp.concatenate([cos_half, cos_half], axis=-1), jnp.concatenate([-sin_half, sin_half], axis=-1)


def _conv3(u, prev_row, next_row, w_ref, has_prev, has_next):
    tm = u.shape[0]
    row = lax.broadcasted_iota(jnp.int32, u.shape, 0)
    prev_row = jnp.where(has_prev, prev_row, 0.0)
    next_row = jnp.where(has_next, next_row, 0.0)
    before = jnp.where(row == 0, prev_row, pltpu.roll(u, 1, 0))
    after = jnp.where(row == tm - 1, next_row, pltpu.roll(u, tm - 1, 0))
    return before * w_ref[0:1, :] + u * w_ref[1:2, :] + after * w_ref[2:3, :]


def _prep_kernel(cv_ref, cvp_ref, cvn_ref, xb_ref, xbp_ref, xbn_ref, dtr_ref,
                 cw_ref, sw_ref, sb_ref, dtb_ref, ya_ref, xs_ref, dt_ref,
                 *, n_lat_tiles, lat_tiles_per_seq, ctx_tiles_per_seq):
    i = pl.program_id(0)
    is_lat = i < n_lat_tiles
    pos = jnp.where(is_lat, i % lat_tiles_per_seq, (i - n_lat_tiles) % ctx_tiles_per_seq)
    last = jnp.where(is_lat, lat_tiles_per_seq - 1, ctx_tiles_per_seq - 1)
    has_prev = pos != 0
    has_next = pos != last

    w = CONV_W
    cv = cv_ref[...]
    gate_b, u = cv[:, 0:w], cv[:, w:2 * w] * cv[:, 2 * w:3 * w]
    p = cvp_ref[SUBLANE - 1:SUBLANE, :]
    n = cvn_ref[0:1, :]
    u_prev = p[:, w:2 * w] * p[:, 2 * w:3 * w]
    u_next = n[:, w:2 * w] * n[:, 2 * w:3 * w]
    ya_ref[...] = (gate_b * _conv3(u, u_prev, u_next, cw_ref, has_prev, has_next)).astype(ya_ref.dtype)

    xc = _conv3(xb_ref[...], xbp_ref[SUBLANE - 1:SUBLANE, :], xbn_ref[0:1, :], sw_ref, has_prev, has_next)
    xs_ref[...] = _silu(xc + sb_ref[...])

    t = dtr_ref[...] + dtb_ref[...]
    dt = jnp.maximum(t, 0.0) + jnp.log1p(jnp.exp(-jnp.abs(t)))
    dt_ref[0] = dt
    dt_ref[1] = pltpu.roll(dt, LANE - SSD_HEADS, 1)


def _prep(p_a, conv_w, ssd_conv_w, ssd_conv_b, dt_bias, n_lat_tiles, lat_tiles_per_seq,
          ctx_tiles_per_seq):
    m = p_a.shape[0]
    dt_blk = (COL_Q + SSD_XBC) // LANE
    tm = HALO_TILE
    per = tm // SUBLANE
    n8 = m // SUBLANE
    cw = 3 * CONV_W
    xbc_blk = COL_Q // SSD_XBC
    prev = lambda i: jnp.maximum(i * per - 1, 0)
    nxt = lambda i: jnp.minimum((i + 1) * per, n8 - 1)
    kern = functools.partial(_prep_kernel, n_lat_tiles=n_lat_tiles, lat_tiles_per_seq=lat_tiles_per_seq,
                             ctx_tiles_per_seq=ctx_tiles_per_seq)
    return pl.pallas_call(
        kern,
        grid=(m // tm,),
        in_specs=[pl.BlockSpec((tm, cw), lambda i: (i, 0)),
                  pl.BlockSpec((SUBLANE, cw), lambda i: (prev(i), 0)),
                  pl.BlockSpec((SUBLANE, cw), lambda i: (nxt(i), 0)),
                  pl.BlockSpec((tm, SSD_XBC), lambda i: (i, xbc_blk)),
                  pl.BlockSpec((SUBLANE, SSD_XBC), lambda i: (prev(i), xbc_blk)),
                  pl.BlockSpec((SUBLANE, SSD_XBC), lambda i: (nxt(i), xbc_blk)),
                  pl.BlockSpec((tm, LANE), lambda i: (i, dt_blk)),
                  pl.BlockSpec((SHORT_CONV, CONV_W), lambda i: (0, 0)),
                  pl.BlockSpec((SHORT_CONV, SSD_XBC), lambda i: (0, 0)),
                  pl.BlockSpec((1, SSD_XBC), lambda i: (0, 0)),
                  pl.BlockSpec((1, LANE), lambda i: (0, 0))],
        out_specs=[pl.BlockSpec((tm, CONV_W), lambda i: (i, 0)),
                   pl.BlockSpec((tm, SSD_XBC), lambda i: (i, 0)),
                   pl.BlockSpec((2, tm, LANE), lambda i: (0, i, 0))],
        out_shape=[jax.ShapeDtypeStruct((m, CONV_W), BF16),
                   jax.ShapeDtypeStruct((m, SSD_XBC), F32),
                   jax.ShapeDtypeStruct((2, m, LANE), F32)],
        compiler_params=_params(("parallel",), 40),
        name="conv_prep",
    )(p_a, p_a, p_a, p_a, p_a, p_a, p_a, conv_w, ssd_conv_w, ssd_conv_b, dt_bias)


def _split3(v):
    hi = v.astype(BF16)
    rest = v - hi.astype(F32)
    mid = rest.astype(BF16)
    return hi, mid, (rest - mid.astype(F32)).astype(BF16)


def _dot_f32_lhs(a, b01):
    return sum(_dot(piece, b01) for piece in _split3(a))


def _dot_f32_rhs(a01, b):
    return sum(_dot(a01, piece) for piece in _split3(b))


def _ssd_chunk(direction, xs_ref, dt_ref, alog_ref, alog_ch_ref, y_ref, state_ref):
    t = SSD_CHUNK
    r = lax.broadcasted_iota(jnp.int32, (t, t), 0)
    c = lax.broadcasted_iota(jnp.int32, (t, t), 1)
    mask = (r >= c) if direction == 0 else (r <= c)
    tri = mask.astype(BF16)

    er = lax.broadcasted_iota(jnp.int32, (LANE, SSD_INNER), 0)
    ec = lax.broadcasted_iota(jnp.int32, (LANE, SSD_INNER), 1)
    expand = (jnp.right_shift(ec, 6) == er).astype(BF16)

    dt = dt_ref[direction]
    cs = _dot_f32_rhs(tri, dt * (-jnp.exp(alog_ref[direction])))
    cs_t = cs.T
    dt_ch = _dot_f32_lhs(dt, expand)
    da_ch = dt_ch * (-jnp.exp(alog_ch_ref[direction]))
    cs_ch = _dot_f32_rhs(tri, da_ch)
    tot_ch = jnp.sum(da_ch, axis=0, keepdims=True)

    xdt = xs_ref[:, 0:SSD_INNER] * dt_ch
    x_state = xdt * jnp.exp(tot_ch - cs_ch)
    y_scale = jnp.exp(cs_ch)
    carry = jnp.exp(tot_ch)
    gw = SSD_GROUP_W
    heads_per_group = SSD_HEADS // SSD_GROUPS
    lane_head = jnp.right_shift(lax.broadcasted_iota(jnp.int32, (t, gw), 1), 6)

    for g in range(SSD_GROUPS):
        b_lo = SSD_INNER + g * SSD_STATE
        c_lo = SSD_INNER + SSD_GROUPS * SSD_STATE + g * SSD_STATE
        bg = xs_ref[:, b_lo:b_lo + SSD_STATE].astype(BF16)
        cg = xs_ref[:, c_lo:c_lo + SSD_STATE].astype(BF16)
        cb = _dot_nt(cg, bg)
        sg = state_ref[direction, g]
        xdt_g = xdt[:, g * gw:(g + 1) * gw]
        y = _dot(cg, sg.astype(BF16)) * y_scale[:, g * gw:(g + 1) * gw]
        for hh in range(heads_per_group):
            h = g * heads_per_group + hh
            decay = jnp.where(mask, jnp.exp(cs[:, h:h + 1] - cs_t[h:h + 1, :]), 0.0)
            x_h = jnp.where(lane_head == hh, xdt_g, 0.0).astype(BF16)
            y = y + _dot((cb * decay).astype(BF16), x_h)
        y_ref[:, g * gw:(g + 1) * gw] = y
        ds = _dot_tn(bg, x_state[:, g * gw:(g + 1) * gw].astype(BF16))
        state_ref[direction, g] = sg * carry[:, g * gw:(g + 1) * gw] + ds


def _ssd_kernel(xf_ref, dtf_ref, xb_ref, dtb_ref, alog_ref, alog_ch_ref, yf_ref, yb_ref, state_ref):
    @pl.when(pl.program_id(1) == 0)
    def _():
        state_ref[...] = jnp.zeros_like(state_ref)

    _ssd_chunk(0, xf_ref, dtf_ref, alog_ref, alog_ch_ref, yf_ref, state_ref)
    _ssd_chunk(1, xb_ref, dtb_ref, alog_ref, alog_ch_ref, yb_ref, state_ref)


def _ssd_scan(xs, dt2, a_log, batch, n_lat_chunks, n_ctx_chunks):
    m = xs.shape[0]
    t = SSD_CHUNK
    assert SSD_HEAD_DIM == 64 and SSD_HEADS <= LANE
    a_log_ch = jnp.repeat(a_log, SSD_HEAD_DIM, axis=1).reshape(2, 1, SSD_INNER)
    a_log = _pad_lanes(a_log).reshape(2, 1, LANE)

    def chunk(b, d, s):
        j_ctx = s if d == 0 else n_ctx_chunks - 1 - s
        sl = s - n_ctx_chunks
        j_lat = sl if d == 0 else n_lat_chunks - 1 - sl
        return jnp.where(s < n_ctx_chunks, batch * n_lat_chunks + b * n_ctx_chunks + j_ctx,
                         b * n_lat_chunks + j_lat)

    x_spec = lambda d: pl.BlockSpec((t, SSD_XBC), lambda b, s: (chunk(b, d, s), 0))
    dt_spec = lambda d: pl.BlockSpec((2, t, LANE), lambda b, s: (0, chunk(b, d, s), 0))
    y_spec = lambda d: pl.BlockSpec((t, SSD_INNER), lambda b, s: (chunk(b, d, s), 0))
    return pl.pallas_call(
        _ssd_kernel,
        grid=(batch, n_ctx_chunks + n_lat_chunks),
        in_specs=[x_spec(0), dt_spec(0), x_spec(1), dt_spec(1),
                  pl.BlockSpec((2, 1, LANE), lambda b, s: (0, 0, 0)),
                  pl.BlockSpec((2, 1, SSD_INNER), lambda b, s: (0, 0, 0))],
        out_specs=[y_spec(0), y_spec(1)],
        out_shape=[jax.ShapeDtypeStruct((m, SSD_INNER), F32), jax.ShapeDtypeStruct((m, SSD_INNER), F32)],
        scratch_shapes=[pltpu.VMEM((2, SSD_GROUPS, SSD_STATE, SSD_GROUP_W), F32)],
        compiler_params=_params(("parallel", "arbitrary")),
        name="ssd_scan",
    )(xs, dt2, xs, dt2, a_log, a_log_ch)


def _ssd_gate_norm(y_fwd, y_bwd, x, z, d_skip, norm_g):
    u = (d_skip * x + y_fwd + y_bwd) * _silu(z)
    gw = SSD_GROUP_W
    parts = []
    for g in range(SSD_GROUPS):
        ug = u[:, g * gw:(g + 1) * gw]
        ug = ug * lax.rsqrt(jnp.mean(ug * ug, axis=-1, keepdims=True) + NORM_EPS)
        parts.append(ug * norm_g[:, g * gw:(g + 1) * gw])
    return jnp.concatenate(parts, axis=1)


ATT_TQ = 1024
ATT_TKC = 1024


def _attn_kernel(*refs, has_lat, lam_init):
    if has_lat:
        (q_ref, kc_ref, vct_ref, kl_ref, vlt_ref, lam_ref, g_ref, o_ref,
         m_ref, acc_ref, sa_ref, sb_ref) = refs
    else:
        q_ref, kc_ref, vct_ref, lam_ref, g_ref, o_ref, m_ref, acc_ref, sa_ref, sb_ref = refs

    q = q_ref[...]
    lane = lax.broadcasted_iota(jnp.int32, q.shape, 1)
    zero = jnp.zeros_like(q)
    in_map0 = (lane % DA_HEAD_DIM) < (DA_HEAD_DIM // 2)
    q_maps = (jnp.where(in_map0, q, zero), jnp.where(in_map0, zero, q))

    def scores(k, s_ref):
        n = k.shape[0]
        maxima = []
        for mp in range(2):
            s = _dot_nt(k, q_maps[mp])
            s_ref[mp, 0:n, :] = s
            maxima.append(jnp.max(s, axis=0, keepdims=True))
        return tuple(maxima)

    def update(s_ref, maxima, vt):
        n = vt.shape[1]
        for mp in range(2):
            m_old = m_ref[mp]
            m_new = jnp.maximum(m_old, maxima[mp])
            alpha = jnp.exp2(m_old - m_new)
            p = jnp.exp2(s_ref[mp, 0:n, :] - m_new[0:1, :])
            acc_ref[mp] = alpha[0:1, :] * acc_ref[mp] + _dot(vt, p.astype(vt.dtype))
            m_ref[mp] = m_new

    m_ref[...] = jnp.full(m_ref.shape, -jnp.inf, F32)
    acc_ref[...] = jnp.zeros_like(acc_ref)
    mx = scores(kc_ref[...], sa_ref)

    if not has_lat:
        update(sa_ref, mx, vct_ref[...])
    else:
        tkc = min(ATT_TKC, kl_ref.shape[0])
        n_chunks = kl_ref.shape[0] // tkc
        k_at = lambda c: kl_ref[pl.ds(pl.multiple_of(c * tkc, tkc), tkc), :]
        vt_at = lambda c: vlt_ref[:, pl.ds(pl.multiple_of(c * tkc, tkc), tkc)]

        mx_ctx = mx
        mx = scores(k_at(0), sb_ref)
        update(sa_ref, mx_ctx, vct_ref[...])

        def pair(c2, mx_b):
            c = 1 + 2 * c2
            mx_a = scores(k_at(c), sa_ref)
            update(sb_ref, mx_b, vt_at(c - 1))
            mx_b = scores(k_at(c + 1), sb_ref)
            update(sa_ref, mx_a, vt_at(c))
            return mx_b

        mx = lax.fori_loop(0, (n_chunks - 1) // 2, pair, mx)
        if (n_chunks - 1) % 2:
            mx_a = scores(k_at(n_chunks - 1), sa_ref)
            update(sb_ref, mx, vt_at(n_chunks - 2))
            update(sa_ref, mx_a, vt_at(n_chunks - 1))
        else:
            update(sb_ref, mx, vt_at(n_chunks - 1))

    lv = lam_ref[...]
    dotp = lambda a, b: jnp.sum(lv[a:a + 1, :] * lv[b:b + 1, :], axis=-1, keepdims=True)
    lam = jnp.exp(dotp(0, 1)) - jnp.exp(dotp(2, 3)) + lam_init
    vd = DA_V_DIM
    inv_l = 1.0 / acc_ref[:, vd:vd + 1, :]
    o = acc_ref[0, 0:vd, :] * inv_l[0] - lam * (acc_ref[1, 0:vd, :] * inv_l[1])
    y = o * lax.rsqrt(jnp.mean(o * o, axis=0, keepdims=True) + NORM_EPS)
    o_ref[...] = ((y * g_ref[...]) * (1.0 - lam_init)).T.astype(o_ref.dtype)


VT_ROWS = DA_V_DIM + 16


def _attn_scratch(tq, keys_a, keys_b):
    return [pltpu.VMEM((2, SUBLANE, tq), F32), pltpu.VMEM((2, VT_ROWS, tq), F32),
            pltpu.VMEM((2, keys_a, tq), F32), pltpu.VMEM((2, keys_b, tq), F32)]


def _attention_lat(qkv, v_t, da_lambda, subln, lam_init, batch, seq, ctx_len):
    h = DA_HEADS
    tq = ATT_TQ
    nq = seq // tq
    ctx_blk0 = (batch * seq) // ctx_len
    assert seq % tq == 0 and seq % min(ATT_TKC, seq) == 0
    kern = functools.partial(_attn_kernel, has_lat=True, lam_init=lam_init)
    return pl.pallas_call(
        kern,
        grid=(batch, h, nq),
        in_specs=[pl.BlockSpec((tq, LANE), lambda b, hh, i: (b * nq + i, hh)),
                  pl.BlockSpec((ctx_len, LANE), lambda b, hh, i: (ctx_blk0 + b, h + hh)),
                  pl.BlockSpec((VT_ROWS, ctx_len), lambda b, hh, i: (hh, ctx_blk0 + b)),
                  pl.BlockSpec((seq, LANE), lambda b, hh, i: (b, h + hh)),
                  pl.BlockSpec((VT_ROWS, seq), lambda b, hh, i: (hh, b)),
                  pl.BlockSpec(da_lambda.shape, lambda b, hh, i: (0, 0)),
                  pl.BlockSpec((DA_V_DIM, 1), lambda b, hh, i: (0, 0))],
        out_specs=pl.BlockSpec((tq, LANE), lambda b, hh, i: (b * nq + i, hh)),
        out_shape=jax.ShapeDtypeStruct((batch * seq, DA_WIDTH), BF16),
        scratch_shapes=_attn_scratch(tq, max(min(ATT_TKC, seq), ctx_len), min(ATT_TKC, seq)),
        compiler_params=_params(("parallel", "parallel", "parallel"), 48),
        name="diff_attention",
    )(qkv, qkv, v_t, qkv, v_t, da_lambda, subln)


def _attention_ctx(qkv, v_t, da_lambda, subln, lam_init, batch, seq, ctx_len):
    h = DA_HEADS
    ctx_blk0 = (batch * seq) // ctx_len
    kern = functools.partial(_attn_kernel, has_lat=False, lam_init=lam_init)
    return pl.pallas_call(
        kern,
        grid=(batch, h),
        in_specs=[pl.BlockSpec((ctx_len, LANE), lambda b, hh: (ctx_blk0 + b, hh)),
                  pl.BlockSpec((ctx_len, LANE), lambda b, hh: (ctx_blk0 + b, h + hh)),
                  pl.BlockSpec((VT_ROWS, ctx_len), lambda b, hh: (hh, ctx_blk0 + b)),
                  pl.BlockSpec(da_lambda.shape, lambda b, hh: (0, 0)),
                  pl.BlockSpec((DA_V_DIM, 1), lambda b, hh: (0, 0))],
        out_specs=pl.BlockSpec((ctx_len, LANE), lambda b, hh: (b, hh)),
        out_shape=jax.ShapeDtypeStruct((batch * ctx_len, DA_WIDTH), BF16),
        scratch_shapes=_attn_scratch(ctx_len, ctx_len, SUBLANE),
        compiler_params=_params(("parallel", "parallel")),
        name="diff_attention_ctx",
    )(qkv, qkv, v_t, da_lambda, subln)


def _outproj_kernel(ya_ref, yf_ref, yr_ref, xs_ref, z_ref, dsk_ref, gn_ref, yc_ref, ycx_ref,
                    w0_ref, w1_ref, w2_ref, xl_ref, xc_ref, gate_ref, o_ref, *, n_lat_tiles):
    yb = _ssd_gate_norm(yf_ref[...], yr_ref[...], xs_ref[...], z_ref[...], dsk_ref[...], gn_ref[...])
    yc = _lat_or_ctx(yc_ref, ycx_ref, n_lat_tiles)
    acc = _dot(ya_ref[...], w0_ref[...]) + _dot(yb.astype(BF16), w1_ref[...]) + _dot(yc, w2_ref[...])
    o_ref[...] = _lat_or_ctx(xl_ref, xc_ref, n_lat_tiles) + gate_ref[...] * acc


def _out_proj(ya, y_fwd, y_bwd, xbc, p_a, d_skip, norm_g, yc, yc_ctx, w_out, x_lat, x_ctx, mods,
              n_tiles, n_lat_tiles, cls_of_tile):
    d = x_lat.shape[1]
    wa, wb = ya.shape[1], SSD_INNER
    assert wa == wb and yc.shape[1] == wa + wb
    if yc_ctx is None:
        assert n_tiles <= n_lat_tiles
        yc_ctx = yc
    row_blk = lambda col: pl.BlockSpec((ROW_TILE, wb), lambda i: (i, col))
    vec = pl.BlockSpec((1, wb), lambda i: (0, 0))
    w_rows = lambda rows, blk: pl.BlockSpec((rows, d), lambda i: (blk, 0), pipeline_mode=pl.Buffered(1))
    return pl.pallas_call(
        functools.partial(_outproj_kernel, n_lat_tiles=n_lat_tiles),
        grid=(n_tiles,),
        in_specs=[pl.BlockSpec((ROW_TILE, wa), lambda i: (i, 0)),
                  row_blk(0), row_blk(0), row_blk(0), row_blk(COL_Z // wb), vec, vec]
                 + _lat_ctx_specs(wa + wb, n_lat_tiles)
                 + [w_rows(wa, 0), w_rows(wb, 1), w_rows(wa + wb, 1)]
                 + _lat_ctx_specs(d, n_lat_tiles)
                 + [pl.BlockSpec((None, 1, d), lambda i: (cls_of_tile(i), 0, 2))],
        out_specs=pl.BlockSpec((ROW_TILE, d), lambda i: (i, 0)),
        out_shape=jax.ShapeDtypeStruct((n_tiles * ROW_TILE, d), F32),
        compiler_params=_params(("parallel",), 56),
        name="out_proj_residual",
    )(ya, y_fwd, y_bwd, xbc, p_a, d_skip, norm_g, yc, yc_ctx, w_out, w_out, w_out, x_lat, x_ctx, mods)


def _ffn_kernel(x_ref, g_ref, sh_ref, sc_ref, gate_ref, wg_ref, wu_ref, wd_ref, o_ref, h_ref, acc_ref):
    f = pl.program_id(1)

    @pl.when(f == 0)
    def _():
        h_ref[...] = _modulated_norm(x_ref[...], g_ref[...], sh_ref[...], sc_ref[...]).astype(h_ref.dtype)
        acc_ref[...] = jnp.zeros_like(acc_ref)

    h = h_ref[...]
    a = _silu(_dot(h, wg_ref[...])) * _dot(h, wu_ref[...])
    acc_ref[...] += _dot(a.astype(BF16), wd_ref[...])

    @pl.when(f == pl.num_programs(1) - 1)
    def _():
        o_ref[...] = x_ref[...] + gate_ref[...] * acc_ref[...]


def _ffn_dense(x, g, mods, wg, wu, wd, n_tiles, cls_of_tile):
    m, d = x.shape
    ff = wg.shape[1]
    tf = 512
    return pl.pallas_call(
        _ffn_kernel,
        grid=(n_tiles, ff // tf),
        in_specs=[pl.BlockSpec((ROW_TILE, d), lambda i, f: (i, 0)),
                  pl.BlockSpec((1, d), lambda i, f: (0, 0)),
                  _mod_spec(3, d, cls_of_tile), _mod_spec(4, d, cls_of_tile), _mod_spec(5, d, cls_of_tile),
                  pl.BlockSpec((d, tf), lambda i, f: (0, f)),
                  pl.BlockSpec((d, tf), lambda i, f: (0, f)),
                  pl.BlockSpec((tf, d), lambda i, f: (f, 0))],
        out_specs=pl.BlockSpec((ROW_TILE, d), lambda i, f: (i, 0)),
        out_shape=jax.ShapeDtypeStruct((n_tiles * ROW_TILE, d), F32),
        scratch_shapes=[pltpu.VMEM((ROW_TILE, d), BF16), pltpu.VMEM((ROW_TILE, d), F32)],
        compiler_params=_params(("parallel", "arbitrary"), 48),
        name="ffn_dense",
    )(x, g.reshape(1, d), mods, mods, mods, wg, wu, wd)


ROUTE_I1, ROUTE_I2, ROUTE_P1, ROUTE_P2, ROUTE_R1, ROUTE_R2 = 0, 1, 2, 3, 4, 5


def _router_kernel(x_ref, g_ref, sh_ref, sc_ref, wr_ref, br_ref, h_ref, route_ref, counts_ref):
    h = _modulated_norm(x_ref[...], g_ref[...], sh_ref[...], sc_ref[...])
    h_ref[...] = h
    logits = _dot_3pass(h, wr_ref[...]) + br_ref[...]
    lane = lax.broadcasted_iota(jnp.int32, logits.shape, 1).astype(F32)
    neg = -jnp.inf
    lg = jnp.where(lane < N_EXPERTS, logits, neg)
    v1 = jnp.max(lg, axis=-1, keepdims=True)
    i1 = jnp.min(jnp.where(lg == v1, lane, float(LANE)), axis=-1, keepdims=True)
    lg2 = jnp.where(lane == i1, neg, lg)
    v2 = jnp.max(lg2, axis=-1, keepdims=True)
    i2 = jnp.min(jnp.where(lg2 == v2, lane, float(LANE)), axis=-1, keepdims=True)
    e = jnp.exp(v2 - v1)
    p1 = 1.0 / (1.0 + e)
    p2 = e / (1.0 + e)
    rec = jnp.where(lane == ROUTE_I1, i1, 0.0) + jnp.where(lane == ROUTE_I2, i2, 0.0)
    rec = rec + jnp.where(lane == ROUTE_P1, p1, 0.0) + jnp.where(lane == ROUTE_P2, p2, 0.0)

    member = jnp.where(jnp.logical_or(lane == i1, lane == i2), 1.0, 0.0)
    tm = member.shape[0]
    earlier = (lax.broadcasted_iota(jnp.int32, (tm, tm), 1) < lax.broadcasted_iota(jnp.int32, (tm, tm), 0))
    before = _dot(earlier.astype(BF16), member.astype(BF16))
    r1 = jnp.sum(jnp.where(lane == i1, before, 0.0), axis=-1, keepdims=True)
    r2 = jnp.sum(jnp.where(lane == i2, before, 0.0), axis=-1, keepdims=True)
    route_ref[...] = rec + jnp.where(lane == ROUTE_R1, r1, 0.0) + jnp.where(lane == ROUTE_R2, r2, 0.0)
    counts_ref[...] = jnp.broadcast_to(jnp.sum(member, axis=0, keepdims=True), counts_ref.shape)


def _router(x, g, mods, wr, br, n_tiles, cls_of_tile):
    m, d = x.shape
    return pl.pallas_call(
        _router_kernel,
        grid=(n_tiles,),
        in_specs=[pl.BlockSpec((ROW_TILE, d), lambda i: (i, 0)),
                  pl.BlockSpec((1, d), lambda i: (0, 0)),
                  _mod_spec(3, d, cls_of_tile), _mod_spec(4, d, cls_of_tile),
                  pl.BlockSpec((d, LANE), lambda i: (0, 0)),
                  pl.BlockSpec((1, LANE), lambda i: (0, 0))],
        out_specs=[pl.BlockSpec((ROW_TILE, d), lambda i: (i, 0)),
                   pl.BlockSpec((ROW_TILE, LANE), lambda i: (i, 0)),
                   pl.BlockSpec((SUBLANE, LANE), lambda i: (i, 0))],
        out_shape=[jax.ShapeDtypeStruct((n_tiles * ROW_TILE, d), F32),
                   jax.ShapeDtypeStruct((n_tiles * ROW_TILE, LANE), F32),
                   jax.ShapeDtypeStruct((n_tiles * SUBLANE, LANE), F32)],
        compiler_params=_params(("parallel",), 40),
        name="moe_router",
    )(x, g.reshape(1, d), mods, mods, wr, br)


def _routing_tables(route, tile_counts, n_experts, tile):
    n_tok = route.shape[0]
    lanes = jnp.arange(n_experts, dtype=jnp.int32)
    experts = route[:, ROUTE_I1:ROUTE_I2 + 1].astype(jnp.int32)
    local_rank = route[:, ROUTE_R1:ROUTE_R2 + 1].astype(jnp.int32)
    per_tile = tile_counts[::SUBLANE, :n_experts].astype(jnp.int32)
    tile_base = jnp.cumsum(per_tile, axis=0) - per_tile
    counts = jnp.sum(per_tile, axis=0)
    padded = ((counts + tile - 1) // tile) * tile
    ends = jnp.cumsum(padded)
    base = jnp.repeat(tile_base, ROW_TILE, axis=0) + (ends - padded)[None, :]
    chosen = experts[:, :, None] == lanes[None, None, :]
    slot = (local_rank + jnp.sum(jnp.where(chosen, base[:, None, :], 0), axis=-1)).reshape(-1)
    n_rows = 2 * n_tok + n_experts * tile
    n_tiles = n_rows // tile
    token = jnp.repeat(jnp.arange(n_tok, dtype=jnp.int32), 2)
    row_token = jnp.zeros((n_rows,), jnp.int32).at[slot].set(token)
    tile_start = jnp.arange(n_tiles, dtype=jnp.int32) * tile
    tile_expert = jnp.minimum(jnp.sum((tile_start[:, None] >= ends[None, :]).astype(jnp.int32), axis=1),
                              n_experts - 1)
    n_active = (ends[-1] // tile).astype(jnp.int32).reshape(1)
    return tile_expert, row_token, n_active, slot.astype(jnp.int32)


MOE_TILE = 256
MOE_FF_SPLIT = 1
GATHER_UNROLL = 8


def _experts_kernel(te_ref, tok_ref, nact_ref, h_hbm, wg_ref, wu_ref, wd_ref, o_ref, xbuf, sem):
    j = pl.program_id(0)
    n_act = nact_ref[0]
    slot = j % 2
    tile = o_ref.shape[0]

    def row_copy(tok, r, s):
        return pltpu.make_async_copy(h_hbm.at[pl.ds(tok, 1), :], xbuf.at[s, pl.ds(r, 1), :], sem.at[s])

    def start_gather(t, s):
        def body(r, carry):
            row_copy(tok_ref[t * tile + r], r, s).start()
            return carry
        lax.fori_loop(0, tile, body, 0, unroll=GATHER_UNROLL)

    def wait_gather(s):
        pltpu.make_async_copy(h_hbm.at[pl.ds(0, tile), :], xbuf.at[s], sem.at[s]).wait()

    @pl.when(j < n_act)
    def _():
        @pl.when(j == 0)
        def _():
            start_gather(0, 0)

        wait_gather(slot)

        @pl.when(j + 1 < n_act)
        def _():
            start_gather(j + 1, 1 - slot)

        h = xbuf[slot].astype(BF16)
        ff = wg_ref.shape[1]
        w = ff // MOE_FF_SPLIT
        y = None
        for part in range(MOE_FF_SPLIT):
            lo = part * w
            a = _silu(_dot(h, wg_ref[:, lo:lo + w])) * _dot(h, wu_ref[:, lo:lo + w])
            yp = _dot(a.astype(BF16), wd_ref[lo:lo + w, :])
            y = yp if y is None else y + yp
        o_ref[...] = y

    @pl.when(j >= n_act)
    def _():
        o_ref[...] = jnp.zeros_like(o_ref)


def _experts(h, tile_expert, row_token, n_active, wg, wu, wd):
    n_e, d, ff = wg.shape
    tile = MOE_TILE
    n_rows = row_token.shape[0]
    assert ff % (MOE_FF_SPLIT * LANE) == 0
    resident = pl.Buffered(1)
    grid_spec = pltpu.PrefetchScalarGridSpec(
        num_scalar_prefetch=3,
        grid=(n_rows // tile,),
        in_specs=[pl.BlockSpec(memory_space=pl.ANY),
                  pl.BlockSpec((None, d, ff), lambda j, te, tok, na: (te[j], 0, 0), pipeline_mode=resident),
                  pl.BlockSpec((None, d, ff), lambda j, te, tok, na: (te[j], 0, 0), pipeline_mode=resident),
                  pl.BlockSpec((None, ff, d), lambda j, te, tok, na: (te[j], 0, 0), pipeline_mode=resident)],
        out_specs=pl.BlockSpec((tile, d), lambda j, te, tok, na: (j, 0)),
        scratch_shapes=[pltpu.VMEM((2, tile, d), F32), pltpu.SemaphoreType.DMA((2,))],
    )
    return pl.pallas_call(
        _experts_kernel,
        grid_spec=grid_spec,
        out_shape=jax.ShapeDtypeStruct((n_rows, d), F32),
        compiler_params=_params(("arbitrary",), 56),
        name="moe_experts",
    )(tile_expert, row_token, n_active, h, wg, wu, wd)


COMBINE_TILE = 256


def _combine_kernel(slot_ref, y_hbm, x_ref, route_ref, gate_ref, gf_ref, o_ref, ybuf, sem, *, final_norm):
    i = pl.program_id(0)
    n = pl.num_programs(0)
    buf = i % 2
    tile = x_ref.shape[0]

    def row_copy(row, k, r, s):
        return pltpu.make_async_copy(y_hbm.at[pl.ds(row, 1), :], ybuf.at[s, pl.ds(k * tile + r, 1), :], sem.at[s])

    def start_gather(t, s):
        def body(r, carry):
            base = 2 * (t * tile + r)
            row_copy(slot_ref[base], 0, r, s).start()
            row_copy(slot_ref[base + 1], 1, r, s).start()
            return carry
        lax.fori_loop(0, tile, body, 0, unroll=GATHER_UNROLL // 2)

    def wait_gather(s):
        pltpu.make_async_copy(y_hbm.at[pl.ds(0, 2 * tile), :], ybuf.at[s], sem.at[s]).wait()

    @pl.when(i == 0)
    def _():
        start_gather(0, 0)

    wait_gather(buf)

    @pl.when(i + 1 < n)
    def _():
        start_gather(i + 1, 1 - buf)

    route = route_ref[...]
    p1 = route[:, ROUTE_P1:ROUTE_P1 + 1]
    p2 = route[:, ROUTE_P2:ROUTE_P2 + 1]
    y = p1 * ybuf[buf, 0:tile, :] + p2 * ybuf[buf, tile:2 * tile, :]
    out = x_ref[...] + gate_ref[...] * y
    if final_norm:
        out = (out * lax.rsqrt(jnp.mean(out * out, axis=-1, keepdims=True) + NORM_EPS)) * gf_ref[...]
    o_ref[...] = out


def _combine(y_rows, slot, x, route, mods, n_tokens, cls_of_tile, final_g):
    d = x.shape[1]
    tile = COMBINE_TILE
    per = ROW_TILE // tile
    final_norm = final_g is not None
    if not final_norm:
        final_g = jnp.ones((d,), F32)
    grid_spec = pltpu.PrefetchScalarGridSpec(
        num_scalar_prefetch=1,
        grid=(n_tokens // tile,),
        in_specs=[pl.BlockSpec(memory_space=pl.ANY),
                  pl.BlockSpec((tile, d), lambda i, s: (i, 0)),
                  pl.BlockSpec((tile, LANE), lambda i, s: (i, 0)),
                  pl.BlockSpec((None, 1, d), lambda i, s: (cls_of_tile(i // per), 0, 5)),
                  pl.BlockSpec((1, d), lambda i, s: (0, 0))],
        out_specs=pl.BlockSpec((tile, d), lambda i, s: (i, 0)),
        scratch_shapes=[pltpu.VMEM((2, 2 * tile, d), F32), pltpu.SemaphoreType.DMA((2,))],
    )
    return pl.pallas_call(
        functools.partial(_combine_kernel, final_norm=final_norm),
        grid_spec=grid_spec,
        out_shape=jax.ShapeDtypeStruct((n_tokens, d), F32),
        compiler_params=_params(("arbitrary",), 48),
        name="moe_combine",
    )(slot, y_rows, x, route, mods, final_g.reshape(1, d))


def _final_kernel(x_ref, g_ref, o_ref):
    x = x_ref[...]
    o_ref[...] = (x * lax.rsqrt(jnp.mean(x * x, axis=-1, keepdims=True) + NORM_EPS)) * g_ref[...]


def _final_norm(x, g, n_tiles):
    d = x.shape[1]
    return pl.pallas_call(
        _final_kernel,
        grid=(n_tiles,),
        in_specs=[pl.BlockSpec((ROW_TILE, d), lambda i: (i, 0)),
                  pl.BlockSpec((1, d), lambda i: (0, 0))],
        out_specs=pl.BlockSpec((ROW_TILE, d), lambda i: (i, 0)),
        out_shape=jax.ShapeDtypeStruct((n_tiles * ROW_TILE, d), F32),
        compiler_params=_params(("parallel",), 40),
        name="final_norm",
    )(x, g.reshape(1, d))


def _pad_lanes(v, width=LANE):
    return jnp.pad(v, [(0, 0)] * (v.ndim - 1) + [(0, width - v.shape[-1])])


def kernel(x, c, ctx, c_ctx, w_mod, b_mod, g_mix, g_ffn, w_in, conv_w, ssd_conv_w, ssd_conv_b, ssd_a_log, ssd_dt_bias, ssd_d, ssd_norm, da_lambda, da_subln, w_out, ffn_w_gate, ffn_w_up, ffn_w_down, moe_w_router, moe_b_router, moe_w_gate, moe_w_up, moe_w_down, g_final):
    batch, seq, d = x.shape
    ctx_len = ctx.shape[1]
    depth = w_mod.shape[0]
    n_lat = batch * seq
    m = n_lat + batch * ctx_len
    assert seq % ROW_TILE == 0 and (batch * ctx_len) % ROW_TILE == 0 and n_lat % ctx_len == 0
    assert ctx_len % HALO_TILE == 0 and ctx_len % SSD_CHUNK == 0 and batch < MOD_CLASSES
    n_lat_tiles = n_lat // ROW_TILE
    n_all_tiles = m // ROW_TILE
    tiles_per_seq = seq // ROW_TILE
    cls_of_tile = lambda i: jnp.minimum(i // tiles_per_seq, batch)

    x_lat, x_ctx = x.reshape(n_lat, d), ctx.reshape(batch * ctx_len, d)
    cvec = jnp.zeros((MOD_CLASSES, d), F32).at[:batch].set(c).at[batch].set(c_ctx)
    mods_all = _mod_vectors(cvec, w_mod, b_mod)
    cos_t, sin_t = _rope_tables(seq)

    for i in range(depth):
        ctx_out = i < depth - 1
        lam_init = 0.8 - 0.6 * math.exp(-0.3 * i)
        n_tiles = n_all_tiles if ctx_out else n_lat_tiles
        mods = mods_all[i].reshape(MOD_CLASSES, 1, N_MOD * d)
        wi = w_in[i]
        w_a = jnp.concatenate([wi[:, COL_CONV:COL_Q], wi[:, COL_XBC:COL_DT],
                               _pad_lanes(wi[:, COL_DT:COL_K])], axis=1).astype(BF16)
        w_q = _head_lane_order(wi[:, COL_Q:COL_XBC]) * (DA_SCALE * LOG2_E)
        w_k = _head_lane_order(wi[:, COL_K:COL_V])
        w_b = jnp.concatenate([w_q, w_k, wi[:, COL_V:COL_V + DA_WIDTH]], axis=1).astype(BF16)

        h, p_a = _in_proj(x_lat, x_ctx, g_mix[i], mods, w_a, n_all_tiles, n_lat_tiles, cls_of_tile)
        qkv, v_t = _matmul_rope(h, w_b, cos_t, sin_t, n_lat_tiles, tiles_per_seq)

        ya, xbc, dt2 = _prep(p_a, conv_w[i], ssd_conv_w[i], ssd_conv_b[i].reshape(1, -1),
                             _pad_lanes(ssd_dt_bias[i].reshape(1, -1)), n_lat // HALO_TILE,
                             seq // HALO_TILE, ctx_len // HALO_TILE)
        y_fwd, y_bwd = _ssd_scan(xbc, dt2, ssd_a_log[i], batch, seq // SSD_CHUNK, ctx_len // SSD_CHUNK)

        subln = da_subln[i].reshape(-1, 1)
        yc = _attention_lat(qkv, v_t, da_lambda[i], subln, lam_init, batch, seq, ctx_len)
        yc_ctx = _attention_ctx(qkv, v_t, da_lambda[i], subln, lam_init, batch, seq, ctx_len) if ctx_out else None

        x_mid = _out_proj(ya, y_fwd, y_bwd, xbc, p_a, jnp.repeat(ssd_d[i], SSD_HEAD_DIM).reshape(1, -1),
                          ssd_norm[i].reshape(1, -1), yc, yc_ctx, w_out[i].astype(BF16), x_lat, x_ctx, mods,
                          n_tiles, n_lat_tiles, cls_of_tile)

        j = i // 2
        if i % 2 == 0:
            xs_all = _ffn_dense(x_mid, g_ffn[i], mods, ffn_w_gate[j].astype(BF16), ffn_w_up[j].astype(BF16),
                                ffn_w_down[j].astype(BF16), n_tiles, cls_of_tile)
        else:
            hh, route, tile_counts = _router(x_mid, g_ffn[i], mods, _pad_lanes(moe_w_router[j]),
                                             _pad_lanes(moe_b_router[j].reshape(1, -1)), n_tiles, cls_of_tile)
            tile_expert, row_token, n_active, slot = _routing_tables(route, tile_counts, moe_w_gate.shape[1],
                                                                     MOE_TILE)
            y_rows = _experts(hh, tile_expert, row_token, n_active, moe_w_gate[j].astype(BF16),
                              moe_w_up[j].astype(BF16), moe_w_down[j].astype(BF16))
            xs_all = _combine(y_rows, slot, x_mid, route, mods, n_tiles * ROW_TILE, cls_of_tile,
                              None if ctx_out else g_final)
        x_lat = xs_all
        x_ctx = xs_all[n_lat:] if ctx_out else None

    last_is_moe = depth % 2 == 0
    out = x_lat if last_is_moe else _final_norm(x_lat, g_final, n_lat_tiles)
    return out.reshape(batch, seq, d)
```

```python
import functools
import math

import jax
import jax.numpy as jnp
from jax import lax
from jax.experimental import pallas as pl
from jax.experimental.pallas import tpu as pltpu

NORM_EPS = 1e-6
N_MOD = 6
GRID_W = 64

SHORT_CONV = 3
CONV_W = 512

SSD_HEADS = 8
SSD_HEAD_DIM = 64
SSD_INNER = SSD_HEADS * SSD_HEAD_DIM
SSD_STATE = 128
SSD_GROUPS = 2
SSD_CHUNK = 128
SSD_XBC = SSD_INNER + 2 * SSD_GROUPS * SSD_STATE
SSD_GROUP_W = SSD_INNER // SSD_GROUPS

DA_HEADS = 8
DA_HEAD_DIM = 64
DA_V_DIM = 2 * DA_HEAD_DIM
DA_QK = DA_HEADS * 2 * DA_HEAD_DIM
DA_WIDTH = DA_HEADS * DA_V_DIM
DA_SCALE = DA_HEAD_DIM ** -0.5
LOG2_E = math.log2(math.e)
ROPE_THETA = 10000.0

COL_CONV = 0
COL_Z = COL_CONV + 3 * CONV_W
COL_Q = COL_Z + SSD_INNER
COL_XBC = COL_Q + DA_QK
COL_DT = COL_XBC + SSD_XBC
COL_K = COL_DT + 2 * SSD_HEADS
COL_V = COL_K + DA_QK

N_EXPERTS = 8

LANE = 128
SUBLANE = 8
ROW_TILE = 512
HALO_TILE = 256
MOD_CLASSES = 8

F32 = jnp.float32
BF16 = jnp.bfloat16
MIB = 1024 * 1024


def _params(semantics, vmem_mib=None):
    kw = {"dimension_semantics": semantics}
    if vmem_mib is not None:
        kw["vmem_limit_bytes"] = vmem_mib * MIB
    return pltpu.CompilerParams(**kw)


def _silu(v):
    return v * jax.nn.sigmoid(v)


def _dot(a, b):
    return jnp.dot(a, b, preferred_element_type=F32)


def _dot_nt(a, b):
    return lax.dot_general(a, b, (((1,), (1,)), ((), ())), preferred_element_type=F32)


def _dot_tn(a, b):
    return lax.dot_general(a, b, (((0,), (0,)), ((), ())), preferred_element_type=F32)


def _split2(v):
    hi = v.astype(BF16)
    return hi, (v - hi.astype(F32)).astype(BF16)


def _dot_3pass(a, b):
    a_hi, a_lo = _split2(a)
    b_hi, b_lo = _split2(b)
    return _dot(a_hi, b_hi) + (_dot(a_hi, b_lo) + _dot(a_lo, b_hi))


def _modulated_norm(x, g, shift, scale):
    ms = jnp.mean(x * x, axis=-1, keepdims=True)
    y = x * lax.rsqrt(ms + NORM_EPS)
    return (y * g) * (1.0 + scale) + shift


def _mod_spec(k, width, cls_of_tile):
    return pl.BlockSpec((None, 1, width), lambda i, *_: (cls_of_tile(i), 0, k))


def _mod_kernel(c_ref, w_ref, b_ref, o_ref):
    s = _silu(c_ref[...])
    o_ref[...] = _dot_3pass(s, w_ref[...]) + b_ref[...]


def _mod_vectors(cvec, w_mod, b_mod):
    depth, d, n = w_mod.shape
    tn = 1024
    return pl.pallas_call(
        _mod_kernel,
        grid=(depth, n // tn),
        in_specs=[pl.BlockSpec((MOD_CLASSES, d), lambda l, j: (0, 0)),
                  pl.BlockSpec((None, d, tn), lambda l, j: (l, 0, j)),
                  pl.BlockSpec((None, 1, tn), lambda l, j: (l, 0, j))],
        out_specs=pl.BlockSpec((None, MOD_CLASSES, tn), lambda l, j: (l, 0, j)),
        out_shape=jax.ShapeDtypeStruct((depth, MOD_CLASSES, n), F32),
        compiler_params=_params(("parallel", "parallel"), 40),
        name="mod_vectors",
    )(cvec, w_mod, b_mod.reshape(depth, 1, n))


def _lat_ctx_specs(width, n_lat_tiles):
    return [pl.BlockSpec((ROW_TILE, width), lambda i, *_: (jnp.minimum(i, n_lat_tiles - 1), 0)),
            pl.BlockSpec((ROW_TILE, width), lambda i, *_: (jnp.maximum(i - n_lat_tiles, 0), 0))]


def _lat_or_ctx(lat_ref, ctx_ref, n_lat_tiles):
    return jnp.where(pl.program_id(0) < n_lat_tiles, lat_ref[...], ctx_ref[...])


def _in_proj_kernel(xl_ref, xc_ref, g_ref, sh_ref, sc_ref, w_ref, h_ref, p_ref, *, n_lat_tiles):
    x = _lat_or_ctx(xl_ref, xc_ref, n_lat_tiles)
    h = _modulated_norm(x, g_ref[...], sh_ref[...], sc_ref[...]).astype(h_ref.dtype)
    h_ref[...] = h
    p_ref[...] = _dot(h, w_ref[...])


def _in_proj(x_lat, x_ctx, g, mods, w, n_tiles, n_lat_tiles, cls_of_tile):
    d = x_lat.shape[1]
    n = w.shape[1]
    lat_spec, ctx_spec = _lat_ctx_specs(d, n_lat_tiles)
    ctx_spec = pl.BlockSpec(ctx_spec.block_shape, ctx_spec.index_map, pipeline_mode=pl.Buffered(1))
    return pl.pallas_call(
        functools.partial(_in_proj_kernel, n_lat_tiles=n_lat_tiles),
        grid=(n_tiles,),
        in_specs=[lat_spec, ctx_spec,
                  pl.BlockSpec((1, d), lambda i: (0, 0)),
                  _mod_spec(0, d, cls_of_tile),
                  _mod_spec(1, d, cls_of_tile),
                  pl.BlockSpec((d, n), lambda i: (0, 0), pipeline_mode=pl.Buffered(1))],
        out_specs=[pl.BlockSpec((ROW_TILE, d), lambda i: (i, 0)),
                   pl.BlockSpec((ROW_TILE, n), lambda i: (i, 0))],
        out_shape=[jax.ShapeDtypeStruct((n_tiles * ROW_TILE, d), BF16),
                   jax.ShapeDtypeStruct((n_tiles * ROW_TILE, n), F32)],
        compiler_params=_params(("parallel",), 56),
        name="in_proj_norm",
    )(x_lat, x_ctx, g.reshape(1, d), mods, mods, w)


def _mm_rope_kernel(x_ref, w_ref, cos_ref, sin_ref, qk_ref, vt_ref, *, n_lat_tiles):
    x = x_ref[...]
    tm = x.shape[0]
    qk = _dot(x, w_ref[:, 0:2 * DA_QK])
    is_lat = pl.program_id(0) < n_lat_tiles

    @pl.when(is_lat)
    def _():
        c = cos_ref[...]
        s = sin_ref[...]
        for hb in range(2 * DA_QK // LANE):
            blk = qk[:, hb * LANE:(hb + 1) * LANE]
            partner = pltpu.roll(blk, LANE // 2, 1)
            qk_ref[:, hb * LANE:(hb + 1) * LANE] = (blk * c + partner * s).astype(qk_ref.dtype)

    @pl.when(jnp.logical_not(is_lat))
    def _():
        qk_ref[...] = qk.astype(qk_ref.dtype)

    v = _dot(x, w_ref[:, 2 * DA_QK:])
    row = lax.broadcasted_iota(jnp.int32, (VT_ROWS - DA_V_DIM, tm), 0)
    tail = jnp.where(row == 0, 1.0, 0.0).astype(vt_ref.dtype)
    for h in range(DA_HEADS):
        lo = h * VT_ROWS
        vt_ref[lo:lo + DA_V_DIM, :] = v[:, h * DA_V_DIM:(h + 1) * DA_V_DIM].T.astype(vt_ref.dtype)
        vt_ref[lo + DA_V_DIM:lo + VT_ROWS, :] = tail


def _matmul_rope(x, w, cos_t, sin_t, n_lat_tiles, tiles_per_seq):
    m, k = x.shape
    n = w.shape[1]
    assert n == 2 * DA_QK + DA_WIDTH
    kern = functools.partial(_mm_rope_kernel, n_lat_tiles=n_lat_tiles)
    return pl.pallas_call(
        kern,
        grid=(m // ROW_TILE,),
        in_specs=[pl.BlockSpec((ROW_TILE, k), lambda i: (i, 0)),
                  pl.BlockSpec((k, n), lambda i: (0, 0), pipeline_mode=pl.Buffered(1)),
                  pl.BlockSpec((ROW_TILE, LANE), lambda i: (i % tiles_per_seq, 0)),
                  pl.BlockSpec((ROW_TILE, LANE), lambda i: (i % tiles_per_seq, 0))],
        out_specs=[pl.BlockSpec((ROW_TILE, 2 * DA_QK), lambda i: (i, 0)),
                   pl.BlockSpec((DA_HEADS * VT_ROWS, ROW_TILE), lambda i: (0, i))],
        out_shape=[jax.ShapeDtypeStruct((m, 2 * DA_QK), BF16),
                   jax.ShapeDtypeStruct((DA_HEADS * VT_ROWS, m), BF16)],
        compiler_params=_params(("parallel",), 40),
        name="qkv_proj_rope",
    )(x, w, cos_t, sin_t)


def _head_lane_order(w):
    k = w.shape[0]
    q = DA_HEAD_DIM // 4
    w = w.reshape(k, DA_HEADS, 2, 2, 2, q)
    return w.transpose(0, 1, 4, 2, 3, 5).reshape(k, DA_QK)


def _rope_tables(n_tokens):
    rows = n_tokens // GRID_W
    n_freq = DA_HEAD_DIM // 4
    inv = ROPE_THETA ** (-jnp.arange(n_freq, dtype=F32) / n_freq)
    ang_r = jnp.arange(rows, dtype=F32)[:, None] * inv
    ang_c = jnp.arange(GRID_W, dtype=F32)[:, None] * inv
    per_row = lambda t: jnp.repeat(t, GRID_W, axis=0)
    per_col = lambda t: jnp.tile(t, (rows, 1))
    cos_half = jnp.tile(jnp.concatenate([per_row(jnp.cos(ang_r)), per_col(jnp.cos(ang_c))], axis=-1), (1, 2))
    sin_half = jnp.tile(jnp.concatenate([per_row(jnp.sin(ang_r)), per_col(jnp.sin(ang_c))], axis=-1), (1, 2))
    return jnp.concatenate([cos_half, cos_half], axis=-1), jnp.concatenate([-sin_half, sin_half], axis=-1)


def _conv3(u, prev_row, next_row, w_ref, has_prev, has_next):
    tm = u.shape[0]
    row = lax.broadcasted_iota(jnp.int32, u.shape, 0)
    prev_row = jnp.where(has_prev, prev_row, 0.0)
    next_row = jnp.where(has_next, next_row, 0.0)
    before = jnp.where(row == 0, prev_row, pltpu.roll(u, 1, 0))
    after = jnp.where(row == tm - 1, next_row, pltpu.roll(u, tm - 1, 0))
    return before * w_ref[0:1, :] + u * w_ref[1:2, :] + after * w_ref[2:3, :]


def _prep_kernel(cv_ref, cvp_ref, cvn_ref, xb_ref, xbp_ref, xbn_ref, dtr_ref,
                 cw_ref, sw_ref, sb_ref, dtb_ref, ya_ref, xs_ref, dt_ref,
                 *, n_lat_tiles, lat_tiles_per_seq, ctx_tiles_per_seq):
    i = pl.program_id(0)
    is_lat = i < n_lat_tiles
    pos = jnp.where(is_lat, i % lat_tiles_per_seq, (i - n_lat_tiles) % ctx_tiles_per_seq)
    last = jnp.where(is_lat, lat_tiles_per_seq - 1, ctx_tiles_per_seq - 1)
    has_prev = pos != 0
    has_next = pos != last

    w = CONV_W
    cv = cv_ref[...]
    gate_b, u = cv[:, 0:w], cv[:, w:2 * w] * cv[:, 2 * w:3 * w]
    p = cvp_ref[SUBLANE - 1:SUBLANE, :]
    n = cvn_ref[0:1, :]
    u_prev = p[:, w:2 * w] * p[:, 2 * w:3 * w]
    u_next = n[:, w:2 * w] * n[:, 2 * w:3 * w]
    ya_ref[...] = (gate_b * _conv3(u, u_prev, u_next, cw_ref, has_prev, has_next)).astype(ya_ref.dtype)

    xc = _conv3(xb_ref[...], xbp_ref[SUBLANE - 1:SUBLANE, :], xbn_ref[0:1, :], sw_ref, has_prev, has_next)
    xs_ref[...] = _silu(xc + sb_ref[...])

    t = dtr_ref[...] + dtb_ref[...]
    dt = jnp.maximum(t, 0.0) + jnp.log1p(jnp.exp(-jnp.abs(t)))
    dt_ref[0] = dt
    dt_ref[1] = pltpu.roll(dt, LANE - SSD_HEADS, 1)


def _prep(p_a, conv_w, ssd_conv_w, ssd_conv_b, dt_bias, n_lat_tiles, lat_tiles_per_seq,
          ctx_tiles_per_seq):
    m = p_a.shape[0]
    dt_blk = (COL_Q + SSD_XBC) // LANE
    tm = HALO_TILE
    per = tm // SUBLANE
    n8 = m // SUBLANE
    cw = 3 * CONV_W
    xbc_blk = COL_Q // SSD_XBC
    prev = lambda i: jnp.maximum(i * per - 1, 0)
    nxt = lambda i: jnp.minimum((i + 1) * per, n8 - 1)
    kern = functools.partial(_prep_kernel, n_lat_tiles=n_lat_tiles, lat_tiles_per_seq=lat_tiles_per_seq,
                             ctx_tiles_per_seq=ctx_tiles_per_seq)
    return pl.pallas_call(
        kern,
        grid=(m // tm,),
        in_specs=[pl.BlockSpec((tm, cw), lambda i: (i, 0)),
                  pl.BlockSpec((SUBLANE, cw), lambda i: (prev(i), 0)),
                  pl.BlockSpec((SUBLANE, cw), lambda i: (nxt(i), 0)),
                  pl.BlockSpec((tm, SSD_XBC), lambda i: (i, xbc_blk)),
                  pl.BlockSpec((SUBLANE, SSD_XBC), lambda i: (prev(i), xbc_blk)),
                  pl.BlockSpec((SUBLANE, SSD_XBC), lambda i: (nxt(i), xbc_blk)),
                  pl.BlockSpec((tm, LANE), lambda i: (i, dt_blk)),
                  pl.BlockSpec((SHORT_CONV, CONV_W), lambda i: (0, 0)),
                  pl.BlockSpec((SHORT_CONV, SSD_XBC), lambda i: (0, 0)),
                  pl.BlockSpec((1, SSD_XBC), lambda i: (0, 0)),
                  pl.BlockSpec((1, LANE), lambda i: (0, 0))],
        out_specs=[pl.BlockSpec((tm, CONV_W), lambda i: (i, 0)),
                   pl.BlockSpec((tm, SSD_XBC), lambda i: (i, 0)),
                   pl.BlockSpec((2, tm, LANE), lambda i: (0, i, 0))],
        out_shape=[jax.ShapeDtypeStruct((m, CONV_W), BF16),
                   jax.ShapeDtypeStruct((m, SSD_XBC), F32),
                   jax.ShapeDtypeStruct((2, m, LANE), F32)],
        compiler_params=_params(("parallel",), 40),
        name="conv_prep",
    )(p_a, p_a, p_a, p_a, p_a, p_a, p_a, conv_w, ssd_conv_w, ssd_conv_b, dt_bias)


def _split3(v):
    hi = v.astype(BF16)
    rest = v - hi.astype(F32)
    mid = rest.astype(BF16)
    return hi, mid, (rest - mid.astype(F32)).astype(BF16)


def _dot_f32_lhs(a, b01):
    return sum(_dot(piece, b01) for piece in _split3(a))


def _dot_f32_rhs(a01, b):
    return sum(_dot(a01, piece) for piece in _split3(b))


def _ssd_chunk(direction, xs_ref, dt_ref, alog_ref, alog_ch_ref, y_ref, state_ref):
    t = SSD_CHUNK
    r = lax.broadcasted_iota(jnp.int32, (t, t), 0)
    c = lax.broadcasted_iota(jnp.int32, (t, t), 1)
    mask = (r >= c) if direction == 0 else (r <= c)
    tri = mask.astype(BF16)

    er = lax.broadcasted_iota(jnp.int32, (LANE, SSD_INNER), 0)
    ec = lax.broadcasted_iota(jnp.int32, (LANE, SSD_INNER), 1)
    expand = (jnp.right_shift(ec, 6) == er).astype(BF16)

    dt = dt_ref[direction]
    cs = _dot_f32_rhs(tri, dt * (-jnp.exp(alog_ref[direction])))
    cs_t = cs.T
    dt_ch = _dot_f32_lhs(dt, expand)
    da_ch = dt_ch * (-jnp.exp(alog_ch_ref[direction]))
    cs_ch = _dot_f32_rhs(tri, da_ch)
    tot_ch = jnp.sum(da_ch, axis=0, keepdims=True)

    xdt = xs_ref[:, 0:SSD_INNER] * dt_ch
    x_state = xdt * jnp.exp(tot_ch - cs_ch)
    y_scale = jnp.exp(cs_ch)
    carry = jnp.exp(tot_ch)
    gw = SSD_GROUP_W
    heads_per_group = SSD_HEADS // SSD_GROUPS
    lane_head = jnp.right_shift(lax.broadcasted_iota(jnp.int32, (t, gw), 1), 6)

    for g in range(SSD_GROUPS):
        b_lo = SSD_INNER + g * SSD_STATE
        c_lo = SSD_INNER + SSD_GROUPS * SSD_STATE + g * SSD_STATE
        bg = xs_ref[:, b_lo:b_lo + SSD_STATE].astype(BF16)
        cg = xs_ref[:, c_lo:c_lo + SSD_STATE].astype(BF16)
        cb = _dot_nt(cg, bg)
        sg = state_ref[direction, g]
        xdt_g = xdt[:, g * gw:(g + 1) * gw]
        y = _dot(cg, sg.astype(BF16)) * y_scale[:, g * gw:(g + 1) * gw]
        for hh in range(heads_per_group):
            h = g * heads_per_group + hh
            decay = jnp.where(mask, jnp.exp(cs[:, h:h + 1] - cs_t[h:h + 1, :]), 0.0)
            x_h = jnp.where(lane_head == hh, xdt_g, 0.0).astype(BF16)
            y = y + _dot((cb * decay).astype(BF16), x_h)
        y_ref[:, g * gw:(g + 1) * gw] = y
        ds = _dot_tn(bg, x_state[:, g * gw:(g + 1) * gw].astype(BF16))
        state_ref[direction, g] = sg * carry[:, g * gw:(g + 1) * gw] + ds


def _ssd_kernel(xf_ref, dtf_ref, xb_ref, dtb_ref, alog_ref, alog_ch_ref, yf_ref, yb_ref, state_ref):
    @pl.when(pl.program_id(1) == 0)
    def _():
        state_ref[...] = jnp.zeros_like(state_ref)

    _ssd_chunk(0, xf_ref, dtf_ref, alog_ref, alog_ch_ref, yf_ref, state_ref)
    _ssd_chunk(1, xb_ref, dtb_ref, alog_ref, alog_ch_ref, yb_ref, state_ref)


def _ssd_scan(xs, dt2, a_log, batch, n_lat_chunks, n_ctx_chunks):
    m = xs.shape[0]
    t = SSD_CHUNK
    assert SSD_HEAD_DIM == 64 and SSD_HEADS <= LANE
    a_log_ch = jnp.repeat(a_log, SSD_HEAD_DIM, axis=1).reshape(2, 1, SSD_INNER)
    a_log = _pad_lanes(a_log).reshape(2, 1, LANE)

    def chunk(b, d, s):
        j_ctx = s if d == 0 else n_ctx_chunks - 1 - s
        sl = s - n_ctx_chunks
        j_lat = sl if d == 0 else n_lat_chunks - 1 - sl
        return jnp.where(s < n_ctx_chunks, batch * n_lat_chunks + b * n_ctx_chunks + j_ctx,
                         b * n_lat_chunks + j_lat)

    x_spec = lambda d: pl.BlockSpec((t, SSD_XBC), lambda b, s: (chunk(b, d, s), 0))
    dt_spec = lambda d: pl.BlockSpec((2, t, LANE), lambda b, s: (0, chunk(b, d, s), 0))
    y_spec = lambda d: pl.BlockSpec((t, SSD_INNER), lambda b, s: (chunk(b, d, s), 0))
    return pl.pallas_call(
        _ssd_kernel,
        grid=(batch, n_ctx_chunks + n_lat_chunks),
        in_specs=[x_spec(0), dt_spec(0), x_spec(1), dt_spec(1),
                  pl.BlockSpec((2, 1, LANE), lambda b, s: (0, 0, 0)),
                  pl.BlockSpec((2, 1, SSD_INNER), lambda b, s: (0, 0, 0))],
        out_specs=[y_spec(0), y_spec(1)],
        out_shape=[jax.ShapeDtypeStruct((m, SSD_INNER), F32), jax.ShapeDtypeStruct((m, SSD_INNER), F32)],
        scratch_shapes=[pltpu.VMEM((2, SSD_GROUPS, SSD_STATE, SSD_GROUP_W), F32)],
        compiler_params=_params(("parallel", "arbitrary")),
        name="ssd_scan",
    )(xs, dt2, xs, dt2, a_log, a_log_ch)


def _ssd_gate_norm(y_fwd, y_bwd, x, z, d_skip, norm_g):
    u = (d_skip * x + y_fwd + y_bwd) * _silu(z)
    gw = SSD_GROUP_W
    parts = []
    for g in range(SSD_GROUPS):
        ug = u[:, g * gw:(g + 1) * gw]
        ug = ug * lax.rsqrt(jnp.mean(ug * ug, axis=-1, keepdims=True) + NORM_EPS)
        parts.append(ug * norm_g[:, g * gw:(g + 1) * gw])
    return jnp.concatenate(parts, axis=1)


ATT_TQ = 1024
ATT_TKC = 1024


def _attn_kernel(*refs, has_lat, lam_init):
    if has_lat:
        (q_ref, kc_ref, vct_ref, kl_ref, vlt_ref, lam_ref, g_ref, o_ref,
         m_ref, acc_ref, sa_ref, sb_ref) = refs
    else:
        q_ref, kc_ref, vct_ref, lam_ref, g_ref, o_ref, m_ref, acc_ref, sa_ref, sb_ref = refs

    q = q_ref[...]
    lane = lax.broadcasted_iota(jnp.int32, q.shape, 1)
    zero = jnp.zeros_like(q)
    in_map0 = (lane % DA_HEAD_DIM) < (DA_HEAD_DIM // 2)
    q_maps = (jnp.where(in_map0, q, zero), jnp.where(in_map0, zero, q))

    def scores(k, s_ref):
        n = k.shape[0]
        maxima = []
        for mp in range(2):
            s = _dot_nt(k, q_maps[mp])
            s_ref[mp, 0:n, :] = s
            maxima.append(jnp.max(s, axis=0, keepdims=True))
        return tuple(maxima)

    def update(s_ref, maxima, vt):
        n = vt.shape[1]
        for mp in range(2):
            m_old = m_ref[mp]
            m_new = jnp.maximum(m_old, maxima[mp])
            alpha = jnp.exp2(m_old - m_new)
            p = jnp.exp2(s_ref[mp, 0:n, :] - m_new[0:1, :])
            acc_ref[mp] = alpha[0:1, :] * acc_ref[mp] + _dot(vt, p.astype(vt.dtype))
            m_ref[mp] = m_new

    m_ref[...] = jnp.full(m_ref.shape, -jnp.inf, F32)
    acc_ref[...] = jnp.zeros_like(acc_ref)
    mx = scores(kc_ref[...], sa_ref)

    if not has_lat:
        update(sa_ref, mx, vct_ref[...])
    else:
        tkc = min(ATT_TKC, kl_ref.shape[0])
        n_chunks = kl_ref.shape[0] // tkc
        k_at = lambda c: kl_ref[pl.ds(pl.multiple_of(c * tkc, tkc), tkc), :]
        vt_at = lambda c: vlt_ref[:, pl.ds(pl.multiple_of(c * tkc, tkc), tkc)]

        mx_ctx = mx
        mx = scores(k_at(0), sb_ref)
        update(sa_ref, mx_ctx, vct_ref[...])

        def pair(c2, mx_b):
            c = 1 + 2 * c2
            mx_a = scores(k_at(c), sa_ref)
            update(sb_ref, mx_b, vt_at(c - 1))
            mx_b = scores(k_at(c + 1), sb_ref)
            update(sa_ref, mx_a, vt_at(c))
            return mx_b

        mx = lax.fori_loop(0, (n_chunks - 1) // 2, pair, mx)
        if (n_chunks - 1) % 2:
            mx_a = scores(k_at(n_chunks - 1), sa_ref)
            update(sb_ref, mx, vt_at(n_chunks - 2))
            update(sa_ref, mx_a, vt_at(n_chunks - 1))
        else:
            update(sb_ref, mx, vt_at(n_chunks - 1))

    lv = lam_ref[...]
    dotp = lambda a, b: jnp.sum(lv[a:a + 1, :] * lv[b:b + 1, :], axis=-1, keepdims=True)
    lam = jnp.exp(dotp(0, 1)) - jnp.exp(dotp(2, 3)) + lam_init
    vd = DA_V_DIM
    inv_l = 1.0 / acc_ref[:, vd:vd + 1, :]
    o = acc_ref[0, 0:vd, :] * inv_l[0] - lam * (acc_ref[1, 0:vd, :] * inv_l[1])
    y = o * lax.rsqrt(jnp.mean(o * o, axis=0, keepdims=True) + NORM_EPS)
    o_ref[...] = ((y * g_ref[...]) * (1.0 - lam_init)).T.astype(o_ref.dtype)


VT_ROWS = DA_V_DIM + 16


def _attn_scratch(tq, keys_a, keys_b):
    return [pltpu.VMEM((2, SUBLANE, tq), F32), pltpu.VMEM((2, VT_ROWS, tq), F32),
            pltpu.VMEM((2, keys_a, tq), F32), pltpu.VMEM((2, keys_b, tq), F32)]


def _attention_lat(qkv, v_t, da_lambda, subln, lam_init, batch, seq, ctx_len):
    h = DA_HEADS
    tq = ATT_TQ
    nq = seq // tq
    ctx_blk0 = (batch * seq) // ctx_len
    assert seq % tq == 0 and seq % min(ATT_TKC, seq) == 0
    kern = functools.partial(_attn_kernel, has_lat=True, lam_init=lam_init)
    return pl.pallas_call(
        kern,
        grid=(batch, h, nq),
        in_specs=[pl.BlockSpec((tq, LANE), lambda b, hh, i: (b * nq + i, hh)),
                  pl.BlockSpec((ctx_len, LANE), lambda b, hh, i: (ctx_blk0 + b, h + hh)),
                  pl.BlockSpec((VT_ROWS, ctx_len), lambda b, hh, i: (hh, ctx_blk0 + b)),
                  pl.BlockSpec((seq, LANE), lambda b, hh, i: (b, h + hh)),
                  pl.BlockSpec((VT_ROWS, seq), lambda b, hh, i: (hh, b)),
                  pl.BlockSpec(da_lambda.shape, lambda b, hh, i: (0, 0)),
                  pl.BlockSpec((DA_V_DIM, 1), lambda b, hh, i: (0, 0))],
        out_specs=pl.BlockSpec((tq, LANE), lambda b, hh, i: (b * nq + i, hh)),
        out_shape=jax.ShapeDtypeStruct((batch * seq, DA_WIDTH), BF16),
        scratch_shapes=_attn_scratch(tq, max(min(ATT_TKC, seq), ctx_len), min(ATT_TKC, seq)),
        compiler_params=_params(("parallel", "parallel", "parallel"), 48),
        name="diff_attention",
    )(qkv, qkv, v_t, qkv, v_t, da_lambda, subln)


def _attention_ctx(qkv, v_t, da_lambda, subln, lam_init, batch, seq, ctx_len):
    h = DA_HEADS
    ctx_blk0 = (batch * seq) // ctx_len
    kern = functools.partial(_attn_kernel, has_lat=False, lam_init=lam_init)
    return pl.pallas_call(
        kern,
        grid=(batch, h),
        in_specs=[pl.BlockSpec((ctx_len, LANE), lambda b, hh: (ctx_blk0 + b, hh)),
                  pl.BlockSpec((ctx_len, LANE), lambda b, hh: (ctx_blk0 + b, h + hh)),
                  pl.BlockSpec((VT_ROWS, ctx_len), lambda b, hh: (hh, ctx_blk0 + b)),
                  pl.BlockSpec(da_lambda.shape, lambda b, hh: (0, 0)),
                  pl.BlockSpec((DA_V_DIM, 1), lambda b, hh: (0, 0))],
        out_specs=pl.BlockSpec((ctx_len, LANE), lambda b, hh: (b, hh)),
        out_shape=jax.ShapeDtypeStruct((batch * ctx_len, DA_WIDTH), BF16),
        scratch_shapes=_attn_scratch(ctx_len, ctx_len, SUBLANE),
        compiler_params=_params(("parallel", "parallel")),
        name="diff_attention_ctx",
    )(qkv, qkv, v_t, da_lambda, subln)


def _outproj_kernel(ya_ref, yf_ref, yr_ref, xs_ref, z_ref, dsk_ref, gn_ref, yc_ref, ycx_ref,
                    w0_ref, w1_ref, w2_ref, xl_ref, xc_ref, gate_ref, o_ref, *, n_lat_tiles):
    yb = _ssd_gate_norm(yf_ref[...], yr_ref[...], xs_ref[...], z_ref[...], dsk_ref[...], gn_ref[...])
    yc = _lat_or_ctx(yc_ref, ycx_ref, n_lat_tiles)
    acc = _dot(ya_ref[...], w0_ref[...]) + _dot(yb.astype(BF16), w1_ref[...]) + _dot(yc, w2_ref[...])
    o_ref[...] = _lat_or_ctx(xl_ref, xc_ref, n_lat_tiles) + gate_ref[...] * acc


def _out_proj(ya, y_fwd, y_bwd, xbc, p_a, d_skip, norm_g, yc, yc_ctx, w_out, x_lat, x_ctx, mods,
              n_tiles, n_lat_tiles, cls_of_tile):
    d = x_lat.shape[1]
    wa, wb = ya.shape[1], SSD_INNER
    assert wa == wb and yc.shape[1] == wa + wb
    if yc_ctx is None:
        assert n_tiles <= n_lat_tiles
        yc_ctx = yc
    row_blk = lambda col: pl.BlockSpec((ROW_TILE, wb), lambda i: (i, col))
    vec = pl.BlockSpec((1, wb), lambda i: (0, 0))
    w_rows = lambda rows, blk: pl.BlockSpec((rows, d), lambda i: (blk, 0), pipeline_mode=pl.Buffered(1))
    return pl.pallas_call(
        functools.partial(_outproj_kernel, n_lat_tiles=n_lat_tiles),
        grid=(n_tiles,),
        in_specs=[pl.BlockSpec((ROW_TILE, wa), lambda i: (i, 0)),
                  row_blk(0), row_blk(0), row_blk(0), row_blk(COL_Z // wb), vec, vec]
                 + _lat_ctx_specs(wa + wb, n_lat_tiles)
                 + [w_rows(wa, 0), w_rows(wb, 1), w_rows(wa + wb, 1)]
                 + _lat_ctx_specs(d, n_lat_tiles)
                 + [pl.BlockSpec((None, 1, d), lambda i: (cls_of_tile(i), 0, 2))],
        out_specs=pl.BlockSpec((ROW_TILE, d), lambda i: (i, 0)),
        out_shape=jax.ShapeDtypeStruct((n_tiles * ROW_TILE, d), F32),
        compiler_params=_params(("parallel",), 56),
        name="out_proj_residual",
    )(ya, y_fwd, y_bwd, xbc, p_a, d_skip, norm_g, yc, yc_ctx, w_out, w_out, w_out, x_lat, x_ctx, mods)


def _ffn_kernel(x_ref, g_ref, sh_ref, sc_ref, gate_ref, wg_ref, wu_ref, wd_ref, o_ref, h_ref, acc_ref):
    f = pl.program_id(1)

    @pl.when(f == 0)
    def _():
        h_ref[...] = _modulated_norm(x_ref[...], g_ref[...], sh_ref[...], sc_ref[...]).astype(h_ref.dtype)
        acc_ref[...] = jnp.zeros_like(acc_ref)

    h = h_ref[...]
    a = _silu(_dot(h, wg_ref[...])) * _dot(h, wu_ref[...])
    acc_ref[...] += _dot(a.astype(BF16), wd_ref[...])

    @pl.when(f == pl.num_programs(1) - 1)
    def _():
        o_ref[...] = x_ref[...] + gate_ref[...] * acc_ref[...]


def _ffn_dense(x, g, mods, wg, wu, wd, n_tiles, cls_of_tile):
    m, d = x.shape
    ff = wg.shape[1]
    tf = 512
    return pl.pallas_call(
        _ffn_kernel,
        grid=(n_tiles, ff // tf),
        in_specs=[pl.BlockSpec((ROW_TILE, d), lambda i, f: (i, 0)),
                  pl.BlockSpec((1, d), lambda i, f: (0, 0)),
                  _mod_spec(3, d, cls_of_tile), _mod_spec(4, d, cls_of_tile), _mod_spec(5, d, cls_of_tile),
                  pl.BlockSpec((d, tf), lambda i, f: (0, f)),
                  pl.BlockSpec((d, tf), lambda i, f: (0, f)),
                  pl.BlockSpec((tf, d), lambda i, f: (f, 0))],
        out_specs=pl.BlockSpec((ROW_TILE, d), lambda i, f: (i, 0)),
        out_shape=jax.ShapeDtypeStruct((n_tiles * ROW_TILE, d), F32),
        scratch_shapes=[pltpu.VMEM((ROW_TILE, d), BF16), pltpu.VMEM((ROW_TILE, d), F32)],
        compiler_params=_params(("parallel", "arbitrary"), 48),
        name="ffn_dense",
    )(x, g.reshape(1, d), mods, mods, mods, wg, wu, wd)


ROUTE_I1, ROUTE_I2, ROUTE_P1, ROUTE_P2 = 0, 1, 2, 3


def _router_kernel(x_ref, g_ref, sh_ref, sc_ref, wr_ref, br_ref, h_ref, route_ref):
    h = _modulated_norm(x_ref[...], g_ref[...], sh_ref[...], sc_ref[...])
    h_ref[...] = h
    logits = _dot_3pass(h, wr_ref[...]) + br_ref[...]
    lane = lax.broadcasted_iota(jnp.int32, logits.shape, 1).astype(F32)
    neg = -jnp.inf
    lg = jnp.where(lane < N_EXPERTS, logits, neg)
    v1 = jnp.max(lg, axis=-1, keepdims=True)
    i1 = jnp.min(jnp.where(lg == v1, lane, float(LANE)), axis=-1, keepdims=True)
    lg2 = jnp.where(lane == i1, neg, lg)
    v2 = jnp.max(lg2, axis=-1, keepdims=True)
    i2 = jnp.min(jnp.where(lg2 == v2, lane, float(LANE)), axis=-1, keepdims=True)
    e = jnp.exp(v2 - v1)
    p1 = 1.0 / (1.0 + e)
    p2 = e / (1.0 + e)
    rec = jnp.where(lane == ROUTE_I1, i1, 0.0) + jnp.where(lane == ROUTE_I2, i2, 0.0)
    route_ref[...] = rec + jnp.where(lane == ROUTE_P1, p1, 0.0) + jnp.where(lane == ROUTE_P2, p2, 0.0)


def _router(x, g, mods, wr, br, n_tiles, cls_of_tile):
    m, d = x.shape
    return pl.pallas_call(
        _router_kernel,
        grid=(n_tiles,),
        in_specs=[pl.BlockSpec((ROW_TILE, d), lambda i: (i, 0)),
                  pl.BlockSpec((1, d), lambda i: (0, 0)),
                  _mod_spec(3, d, cls_of_tile), _mod_spec(4, d, cls_of_tile),
                  pl.BlockSpec((d, LANE), lambda i: (0, 0)),
                  pl.BlockSpec((1, LANE), lambda i: (0, 0))],
        out_specs=[pl.BlockSpec((ROW_TILE, d), lambda i: (i, 0)),
                   pl.BlockSpec((ROW_TILE, LANE), lambda i: (i, 0))],
        out_shape=[jax.ShapeDtypeStruct((n_tiles * ROW_TILE, d), F32),
                   jax.ShapeDtypeStruct((n_tiles * ROW_TILE, LANE), F32)],
        compiler_params=_params(("parallel",), 40),
        name="moe_router",
    )(x, g.reshape(1, d), mods, mods, wr, br)


def _routing_tables(route, n_experts, tile):
    n_tok = route.shape[0]
    experts = route[:, ROUTE_I1:ROUTE_I2 + 1].astype(jnp.int32).reshape(-1)
    onehot = (experts[:, None] == jnp.arange(n_experts, dtype=jnp.int32)[None, :]).astype(jnp.int32)
    running = jnp.cumsum(onehot, axis=0)
    rank = jnp.sum(running * onehot, axis=1) - 1
    counts = running[-1]
    padded = ((counts + tile - 1) // tile) * tile
    ends = jnp.cumsum(padded)
    slot = (ends - padded)[experts] + rank
    n_rows = 2 * n_tok + n_experts * tile
    n_tiles = n_rows // tile
    token = jnp.repeat(jnp.arange(n_tok, dtype=jnp.int32), 2)
    row_token = jnp.zeros((n_rows,), jnp.int32).at[slot].set(token)
    tile_start = jnp.arange(n_tiles, dtype=jnp.int32) * tile
    tile_expert = jnp.minimum(jnp.sum((tile_start[:, None] >= ends[None, :]).astype(jnp.int32), axis=1),
                              n_experts - 1)
    n_active = (ends[-1] // tile).astype(jnp.int32).reshape(1)
    return tile_expert, row_token, n_active, slot.astype(jnp.int32)


MOE_TILE = 256
MOE_FF_SPLIT = 1
GATHER_UNROLL = 8


def _experts_kernel(te_ref, tok_ref, nact_ref, h_hbm, wg_ref, wu_ref, wd_ref, o_ref, xbuf, sem):
    j = pl.program_id(0)
    n_act = nact_ref[0]
    slot = j % 2
    tile = o_ref.shape[0]

    def row_copy(tok, r, s):
        return pltpu.make_async_copy(h_hbm.at[pl.ds(tok, 1), :], xbuf.at[s, pl.ds(r, 1), :], sem.at[s])

    def start_gather(t, s):
        def body(r, carry):
            row_copy(tok_ref[t * tile + r], r, s).start()
            return carry
        lax.fori_loop(0, tile, body, 0, unroll=GATHER_UNROLL)

    def wait_gather(s):
        pltpu.make_async_copy(h_hbm.at[pl.ds(0, tile), :], xbuf.at[s], sem.at[s]).wait()

    @pl.when(j < n_act)
    def _():
        @pl.when(j == 0)
        def _():
            start_gather(0, 0)

        wait_gather(slot)

        @pl.when(j + 1 < n_act)
        def _():
            start_gather(j + 1, 1 - slot)

        h = xbuf[slot].astype(BF16)
        ff = wg_ref.shape[1]
        w = ff // MOE_FF_SPLIT
        y = None
        for part in range(MOE_FF_SPLIT):
            lo = part * w
            a = _silu(_dot(h, wg_ref[:, lo:lo + w])) * _dot(h, wu_ref[:, lo:lo + w])
            yp = _dot(a.astype(BF16), wd_ref[lo:lo + w, :])
            y = yp if y is None else y + yp
        o_ref[...] = y

    @pl.when(j >= n_act)
    def _():
        o_ref[...] = jnp.zeros_like(o_ref)


def _experts(h, tile_expert, row_token, n_active, wg, wu, wd):
    n_e, d, ff = wg.shape
    tile = MOE_TILE
    n_rows = row_token.shape[0]
    assert ff % (MOE_FF_SPLIT * LANE) == 0
    resident = pl.Buffered(1)
    grid_spec = pltpu.PrefetchScalarGridSpec(
        num_scalar_prefetch=3,
        grid=(n_rows // tile,),
        in_specs=[pl.BlockSpec(memory_space=pl.ANY),
                  pl.BlockSpec((None, d, ff), lambda j, te, tok, na: (te[j], 0, 0), pipeline_mode=resident),
                  pl.BlockSpec((None, d, ff), lambda j, te, tok, na: (te[j], 0, 0), pipeline_mode=resident),
                  pl.BlockSpec((None, ff, d), lambda j, te, tok, na: (te[j], 0, 0), pipeline_mode=resident)],
        out_specs=pl.BlockSpec((tile, d), lambda j, te, tok, na: (j, 0)),
        scratch_shapes=[pltpu.VMEM((2, tile, d), F32), pltpu.SemaphoreType.DMA((2,))],
    )
    return pl.pallas_call(
        _experts_kernel,
        grid_spec=grid_spec,
        out_shape=jax.ShapeDtypeStruct((n_rows, d), F32),
        compiler_params=_params(("arbitrary",), 56),
        name="moe_experts",
    )(tile_expert, row_token, n_active, h, wg, wu, wd)


COMBINE_TILE = 256


def _combine_kernel(slot_ref, y_hbm, x_ref, route_ref, gate_ref, gf_ref, o_ref, ybuf, sem, *, final_norm):
    i = pl.program_id(0)
    n = pl.num_programs(0)
    buf = i % 2
    tile = x_ref.shape[0]

    def row_copy(row, k, r, s):
        return pltpu.make_async_copy(y_hbm.at[pl.ds(row, 1), :], ybuf.at[s, pl.ds(k * tile + r, 1), :], sem.at[s])

    def start_gather(t, s):
        def body(r, carry):
            base = 2 * (t * tile + r)
            row_copy(slot_ref[base], 0, r, s).start()
            row_copy(slot_ref[base + 1], 1, r, s).start()
            return carry
        lax.fori_loop(0, tile, body, 0, unroll=GATHER_UNROLL // 2)

    def wait_gather(s):
        pltpu.make_async_copy(y_hbm.at[pl.ds(0, 2 * tile), :], ybuf.at[s], sem.at[s]).wait()

    @pl.when(i == 0)
    def _():
        start_gather(0, 0)

    wait_gather(buf)

    @pl.when(i + 1 < n)
    def _():
        start_gather(i + 1, 1 - buf)

    route = route_ref[...]
    p1 = route[:, ROUTE_P1:ROUTE_P1 + 1]
    p2 = route[:, ROUTE_P2:ROUTE_P2 + 1]
    y = p1 * ybuf[buf, 0:tile, :] + p2 * ybuf[buf, tile:2 * tile, :]
    out = x_ref[...] + gate_ref[...] * y
    if final_norm:
        out = (out * lax.rsqrt(jnp.mean(out * out, axis=-1, keepdims=True) + NORM_EPS)) * gf_ref[...]
    o_ref[...] = out


def _combine(y_rows, slot, x, route, mods, n_tokens, cls_of_tile, final_g):
    d = x.shape[1]
    tile = COMBINE_TILE
    per = ROW_TILE // tile
    final_norm = final_g is not None
    if not final_norm:
        final_g = jnp.ones((d,), F32)
    grid_spec = pltpu.PrefetchScalarGridSpec(
        num_scalar_prefetch=1,
        grid=(n_tokens // tile,),
        in_specs=[pl.BlockSpec(memory_space=pl.ANY),
                  pl.BlockSpec((tile, d), lambda i, s: (i, 0)),
                  pl.BlockSpec((tile, LANE), lambda i, s: (i, 0)),
                  pl.BlockSpec((None, 1, d), lambda i, s: (cls_of_tile(i // per), 0, 5)),
                  pl.BlockSpec((1, d), lambda i, s: (0, 0))],
        out_specs=pl.BlockSpec((tile, d), lambda i, s: (i, 0)),
        scratch_shapes=[pltpu.VMEM((2, 2 * tile, d), F32), pltpu.SemaphoreType.DMA((2,))],
    )
    return pl.pallas_call(
        functools.partial(_combine_kernel, final_norm=final_norm),
        grid_spec=grid_spec,
        out_shape=jax.ShapeDtypeStruct((n_tokens, d), F32),
        compiler_params=_params(("arbitrary",), 48),
        name="moe_combine",
    )(slot, y_rows, x, route, mods, final_g.reshape(1, d))


def _final_kernel(x_ref, g_ref, o_ref):
    x = x_ref[...]
    o_ref[...] = (x * lax.rsqrt(jnp.mean(x * x, axis=-1, keepdims=True) + NORM_EPS)) * g_ref[...]


def _final_norm(x, g, n_tiles):
    d = x.shape[1]
    return pl.pallas_call(
        _final_kernel,
        grid=(n_tiles,),
        in_specs=[pl.BlockSpec((ROW_TILE, d), lambda i: (i, 0)),
                  pl.BlockSpec((1, d), lambda i: (0, 0))],
        out_specs=pl.BlockSpec((ROW_TILE, d), lambda i: (i, 0)),
        out_shape=jax.ShapeDtypeStruct((n_tiles * ROW_TILE, d), F32),
        compiler_params=_params(("parallel",), 40),
        name="final_norm",
    )(x, g.reshape(1, d))


def _pad_lanes(v, width=LANE):
    return jnp.pad(v, [(0, 0)] * (v.ndim - 1) + [(0, width - v.shape[-1])])


def kernel(x, c, ctx, c_ctx, w_mod, b_mod, g_mix, g_ffn, w_in, conv_w, ssd_conv_w, ssd_conv_b, ssd_a_log, ssd_dt_bias, ssd_d, ssd_norm, da_lambda, da_subln, w_out, ffn_w_gate, ffn_w_up, ffn_w_down, moe_w_router, moe_b_router, moe_w_gate, moe_w_up, moe_w_down, g_final):
    batch, seq, d = x.shape
    ctx_len = ctx.shape[1]
    depth = w_mod.shape[0]
    n_lat = batch * seq
    m = n_lat + batch * ctx_len
    assert seq % ROW_TILE == 0 and (batch * ctx_len) % ROW_TILE == 0 and n_lat % ctx_len == 0
    assert ctx_len % HALO_TILE == 0 and ctx_len % SSD_CHUNK == 0 and batch < MOD_CLASSES
    n_lat_tiles = n_lat // ROW_TILE
    n_all_tiles = m // ROW_TILE
    tiles_per_seq = seq // ROW_TILE
    cls_of_tile = lambda i: jnp.minimum(i // tiles_per_seq, batch)

    x_lat, x_ctx = x.reshape(n_lat, d), ctx.reshape(batch * ctx_len, d)
    cvec = jnp.zeros((MOD_CLASSES, d), F32).at[:batch].set(c).at[batch].set(c_ctx)
    mods_all = _mod_vectors(cvec, w_mod, b_mod)
    cos_t, sin_t = _rope_tables(seq)

    for i in range(depth):
        ctx_out = i < depth - 1
        lam_init = 0.8 - 0.6 * math.exp(-0.3 * i)
        n_tiles = n_all_tiles if ctx_out else n_lat_tiles
        mods = mods_all[i].reshape(MOD_CLASSES, 1, N_MOD * d)
        wi = w_in[i]
        w_a = jnp.concatenate([wi[:, COL_CONV:COL_Q], wi[:, COL_XBC:COL_DT],
                               _pad_lanes(wi[:, COL_DT:COL_K])], axis=1).astype(BF16)
        w_q = _head_lane_order(wi[:, COL_Q:COL_XBC]) * (DA_SCALE * LOG2_E)
        w_k = _head_lane_order(wi[:, COL_K:COL_V])
        w_b = jnp.concatenate([w_q, w_k, wi[:, COL_V:COL_V + DA_WIDTH]], axis=1).astype(BF16)

        h, p_a = _in_proj(x_lat, x_ctx, g_mix[i], mods, w_a, n_all_tiles, n_lat_tiles, cls_of_tile)
        qkv, v_t = _matmul_rope(h, w_b, cos_t, sin_t, n_lat_tiles, tiles_per_seq)

        ya, xbc, dt2 = _prep(p_a, conv_w[i], ssd_conv_w[i], ssd_conv_b[i].reshape(1, -1),
                             _pad_lanes(ssd_dt_bias[i].reshape(1, -1)), n_lat // HALO_TILE,
                             seq // HALO_TILE, ctx_len // HALO_TILE)
        y_fwd, y_bwd = _ssd_scan(xbc, dt2, ssd_a_log[i], batch, seq // SSD_CHUNK, ctx_len // SSD_CHUNK)

        subln = da_subln[i].reshape(-1, 1)
        yc = _attention_lat(qkv, v_t, da_lambda[i], subln, lam_init, batch, seq, ctx_len)
        yc_ctx = _attention_ctx(qkv, v_t, da_lambda[i], subln, lam_init, batch, seq, ctx_len) if ctx_out else None

        x_mid = _out_proj(ya, y_fwd, y_bwd, xbc, p_a, jnp.repeat(ssd_d[i], SSD_HEAD_DIM).reshape(1, -1),
                          ssd_norm[i].reshape(1, -1), yc, yc_ctx, w_out[i].astype(BF16), x_lat, x_ctx, mods,
                          n_tiles, n_lat_tiles, cls_of_tile)

        j = i // 2
        if i % 2 == 0:
            xs_all = _ffn_dense(x_mid, g_ffn[i], mods, ffn_w_gate[j].astype(BF16), ffn_w_up[j].astype(BF16),
                                ffn_w_down[j].astype(BF16), n_tiles, cls_of_tile)
        else:
            hh, route = _router(x_mid, g_ffn[i], mods, _pad_lanes(moe_w_router[j]),
                                _pad_lanes(moe_b_router[j].reshape(1, -1)), n_tiles, cls_of_tile)
            tile_expert, row_token, n_active, slot = _routing_tables(route, moe_w_gate.shape[1], MOE_TILE)
            y_rows = _experts(hh, tile_expert, row_token, n_active, moe_w_gate[j].astype(BF16),
                              moe_w_up[j].astype(BF16), moe_w_down[j].astype(BF16))
            xs_all = _combine(y_rows, slot, x_mid, route, mods, n_tiles * ROW_TILE, cls_of_tile,
                              None if ctx_out else g_final)
        x_lat = xs_all
        x_ctx = xs_all[n_lat:] if ctx_out else None

    last_is_moe = depth % 2 == 0
    out = x_lat if last_is_moe else _final_norm(x_lat, g_final, n_lat_tiles)
    return out.reshape(batch, seq, d)
```

```python
import functools
import math

import jax
import jax.numpy as jnp
from jax import lax
from jax.experimental import pallas as pl
from jax.experimental.pallas import tpu as pltpu

NORM_EPS = 1e-6
N_MOD = 6
GRID_W = 64

SHORT_CONV = 3
CONV_W = 512

SSD_HEADS = 8
SSD_HEAD_DIM = 64
SSD_INNER = SSD_HEADS * SSD_HEAD_DIM
SSD_STATE = 128
SSD_GROUPS = 2
SSD_CHUNK = 128
SSD_XBC = SSD_INNER + 2 * SSD_GROUPS * SSD_STATE
SSD_GROUP_W = SSD_INNER // SSD_GROUPS

DA_HEADS = 8
DA_HEAD_DIM = 64
DA_V_DIM = 2 * DA_HEAD_DIM
DA_QK = DA_HEADS * 2 * DA_HEAD_DIM
DA_WIDTH = DA_HEADS * DA_V_DIM
DA_SCALE = DA_HEAD_DIM ** -0.5
LOG2_E = math.log2(math.e)
ROPE_THETA = 10000.0

COL_CONV = 0
COL_Z = COL_CONV + 3 * CONV_W
COL_Q = COL_Z + SSD_INNER
COL_XBC = COL_Q + DA_QK
COL_DT = COL_XBC + SSD_XBC
COL_K = COL_DT + 2 * SSD_HEADS
COL_V = COL_K + DA_QK

N_EXPERTS = 8

LANE = 128
SUBLANE = 8
ROW_TILE = 512
HALO_TILE = 256
MOD_CLASSES = 8

F32 = jnp.float32
BF16 = jnp.bfloat16
MIB = 1024 * 1024


def _params(semantics, vmem_mib=None):
    kw = {"dimension_semantics": semantics}
    if vmem_mib is not None:
        kw["vmem_limit_bytes"] = vmem_mib * MIB
    return pltpu.CompilerParams(**kw)


def _silu(v):
    return v * jax.nn.sigmoid(v)


def _dot(a, b):
    return jnp.dot(a, b, preferred_element_type=F32)


def _dot_nt(a, b):
    return lax.dot_general(a, b, (((1,), (1,)), ((), ())), preferred_element_type=F32)


def _dot_tn(a, b):
    return lax.dot_general(a, b, (((0,), (0,)), ((), ())), preferred_element_type=F32)


def _split2(v):
    hi = v.astype(BF16)
    return hi, (v - hi.astype(F32)).astype(BF16)


def _dot_3pass(a, b):
    a_hi, a_lo = _split2(a)
    b_hi, b_lo = _split2(b)
    return _dot(a_hi, b_hi) + (_dot(a_hi, b_lo) + _dot(a_lo, b_hi))


def _modulated_norm(x, g, shift, scale):
    ms = jnp.mean(x * x, axis=-1, keepdims=True)
    y = x * lax.rsqrt(ms + NORM_EPS)
    return (y * g) * (1.0 + scale) + shift


def _mod_spec(k, width, cls_of_tile):
    return pl.BlockSpec((None, 1, width), lambda i, *_: (cls_of_tile(i), 0, k))


def _mod_kernel(c_ref, w_ref, b_ref, o_ref):
    s = _silu(c_ref[...])
    o_ref[...] = _dot_3pass(s, w_ref[...]) + b_ref[...]


def _mod_vectors(cvec, w_mod, b_mod):
    depth, d, n = w_mod.shape
    tn = 1024
    return pl.pallas_call(
        _mod_kernel,
        grid=(depth, n // tn),
        in_specs=[pl.BlockSpec((MOD_CLASSES, d), lambda l, j: (0, 0)),
                  pl.BlockSpec((None, d, tn), lambda l, j: (l, 0, j)),
                  pl.BlockSpec((None, 1, tn), lambda l, j: (l, 0, j))],
        out_specs=pl.BlockSpec((None, MOD_CLASSES, tn), lambda l, j: (l, 0, j)),
        out_shape=jax.ShapeDtypeStruct((depth, MOD_CLASSES, n), F32),
        compiler_params=_params(("parallel", "parallel"), 40),
        name="mod_vectors",
    )(cvec, w_mod, b_mod.reshape(depth, 1, n))


def _lat_ctx_specs(width, n_lat_tiles):
    return [pl.BlockSpec((ROW_TILE, width), lambda i, *_: (jnp.minimum(i, n_lat_tiles - 1), 0)),
            pl.BlockSpec((ROW_TILE, width), lambda i, *_: (jnp.maximum(i - n_lat_tiles, 0), 0))]


def _lat_or_ctx(lat_ref, ctx_ref, n_lat_tiles):
    return jnp.where(pl.program_id(0) < n_lat_tiles, lat_ref[...], ctx_ref[...])


def _in_proj_kernel(xl_ref, xc_ref, g_ref, sh_ref, sc_ref, w_ref, h_ref, p_ref, *, n_lat_tiles):
    x = _lat_or_ctx(xl_ref, xc_ref, n_lat_tiles)
    h = _modulated_norm(x, g_ref[...], sh_ref[...], sc_ref[...]).astype(h_ref.dtype)
    h_ref[...] = h
    p_ref[...] = _dot(h, w_ref[...])


def _in_proj(x_lat, x_ctx, g, mods, w, n_tiles, n_lat_tiles, cls_of_tile):
    d = x_lat.shape[1]
    n = w.shape[1]
    lat_spec, ctx_spec = _lat_ctx_specs(d, n_lat_tiles)
    ctx_spec = pl.BlockSpec(ctx_spec.block_shape, ctx_spec.index_map, pipeline_mode=pl.Buffered(1))
    return pl.pallas_call(
        functools.partial(_in_proj_kernel, n_lat_tiles=n_lat_tiles),
        grid=(n_tiles,),
        in_specs=[lat_spec, ctx_spec,
                  pl.BlockSpec((1, d), lambda i: (0, 0)),
                  _mod_spec(0, d, cls_of_tile),
                  _mod_spec(1, d, cls_of_tile),
                  pl.BlockSpec((d, n), lambda i: (0, 0), pipeline_mode=pl.Buffered(1))],
        out_specs=[pl.BlockSpec((ROW_TILE, d), lambda i: (i, 0)),
                   pl.BlockSpec((ROW_TILE, n), lambda i: (i, 0))],
        out_shape=[jax.ShapeDtypeStruct((n_tiles * ROW_TILE, d), BF16),
                   jax.ShapeDtypeStruct((n_tiles * ROW_TILE, n), F32)],
        compiler_params=_params(("parallel",), 56),
        name="in_proj_norm",
    )(x_lat, x_ctx, g.reshape(1, d), mods, mods, w)


def _mm_rope_kernel(x_ref, w_ref, cos_ref, sin_ref, qk_ref, vt_ref, *, n_lat_tiles):
    x = x_ref[...]
    tm = x.shape[0]
    qk = _dot(x, w_ref[:, 0:2 * DA_QK])
    is_lat = pl.program_id(0) < n_lat_tiles

    @pl.when(is_lat)
    def _():
        c = cos_ref[...]
        s = sin_ref[...]
        for hb in range(2 * DA_QK // LANE):
            blk = qk[:, hb * LANE:(hb + 1) * LANE]
            partner = pltpu.roll(blk, LANE // 2, 1)
            qk_ref[:, hb * LANE:(hb + 1) * LANE] = (blk * c + partner * s).astype(qk_ref.dtype)

    @pl.when(jnp.logical_not(is_lat))
    def _():
        qk_ref[...] = qk.astype(qk_ref.dtype)

    v = _dot(x, w_ref[:, 2 * DA_QK:])
    row = lax.broadcasted_iota(jnp.int32, (VT_ROWS - DA_V_DIM, tm), 0)
    tail = jnp.where(row == 0, 1.0, 0.0).astype(vt_ref.dtype)
    for h in range(DA_HEADS):
        lo = h * VT_ROWS
        vt_ref[lo:lo + DA_V_DIM, :] = v[:, h * DA_V_DIM:(h + 1) * DA_V_DIM].T.astype(vt_ref.dtype)
        vt_ref[lo + DA_V_DIM:lo + VT_ROWS, :] = tail


def _matmul_rope(x, w, cos_t, sin_t, n_lat_tiles, tiles_per_seq):
    m, k = x.shape
    n = w.shape[1]
    assert n == 2 * DA_QK + DA_WIDTH
    kern = functools.partial(_mm_rope_kernel, n_lat_tiles=n_lat_tiles)
    return pl.pallas_call(
        kern,
        grid=(m // ROW_TILE,),
        in_specs=[pl.BlockSpec((ROW_TILE, k), lambda i: (i, 0)),
                  pl.BlockSpec((k, n), lambda i: (0, 0), pipeline_mode=pl.Buffered(1)),
                  pl.BlockSpec((ROW_TILE, LANE), lambda i: (i % tiles_per_seq, 0)),
                  pl.BlockSpec((ROW_TILE, LANE), lambda i: (i % tiles_per_seq, 0))],
        out_specs=[pl.BlockSpec((ROW_TILE, 2 * DA_QK), lambda i: (i, 0)),
                   pl.BlockSpec((DA_HEADS * VT_ROWS, ROW_TILE), lambda i: (0, i))],
        out_shape=[jax.ShapeDtypeStruct((m, 2 * DA_QK), BF16),
                   jax.ShapeDtypeStruct((DA_HEADS * VT_ROWS, m), BF16)],
        compiler_params=_params(("parallel",), 40),
        name="qkv_proj_rope",
    )(x, w, cos_t, sin_t)


def _head_lane_order(w):
    k = w.shape[0]
    q = DA_HEAD_DIM // 4
    w = w.reshape(k, DA_HEADS, 2, 2, 2, q)
    return w.transpose(0, 1, 4, 2, 3, 5).reshape(k, DA_QK)


def _rope_tables(n_tokens):
    rows = n_tokens // GRID_W
    n_freq = DA_HEAD_DIM // 4
    inv = ROPE_THETA ** (-jnp.arange(n_freq, dtype=F32) / n_freq)
    ang_r = jnp.arange(rows, dtype=F32)[:, None] * inv
    ang_c = jnp.arange(GRID_W, dtype=F32)[:, None] * inv
    per_row = lambda t: jnp.repeat(t, GRID_W, axis=0)
    per_col = lambda t: jnp.tile(t, (rows, 1))
    cos_half = jnp.tile(jnp.concatenate([per_row(jnp.cos(ang_r)), per_col(jnp.cos(ang_c))], axis=-1), (1, 2))
    sin_half = jnp.tile(jnp.concatenate([per_row(jnp.sin(ang_r)), per_col(jnp.sin(ang_c))], axis=-1), (1, 2))
    return jnp.concatenate([cos_half, cos_half], axis=-1), jnp.concatenate([-sin_half, sin_half], axis=-1)


def _conv3(u, prev_row, next_row, w_ref, has_prev, has_next):
    tm = u.shape[0]
    row = lax.broadcasted_iota(jnp.int32, u.shape, 0)
    prev_row = jnp.where(has_prev, prev_row, 0.0)
    next_row = jnp.where(has_next, next_row, 0.0)
    before = jnp.where(row == 0, prev_row, pltpu.roll(u, 1, 0))
    after = jnp.where(row == tm - 1, next_row, pltpu.roll(u, tm - 1, 0))
    return before * w_ref[0:1, :] + u * w_ref[1:2, :] + after * w_ref[2:3, :]


def _prep_kernel(cv_ref, cvp_ref, cvn_ref, xb_ref, xbp_ref, xbn_ref, dtr_ref,
                 cw_ref, sw_ref, sb_ref, dtb_ref, ya_ref, xs_ref, dt_ref,
                 *, n_lat_tiles, lat_tiles_per_seq, ctx_tiles_per_seq):
    i = pl.program_id(0)
    is_lat = i < n_lat_tiles
    pos = jnp.where(is_lat, i % lat_tiles_per_seq, (i - n_lat_tiles) % ctx_tiles_per_seq)
    last = jnp.where(is_lat, lat_tiles_per_seq - 1, ctx_tiles_per_seq - 1)
    has_prev = pos != 0
    has_next = pos != last

    w = CONV_W
    cv = cv_ref[...]
    gate_b, u = cv[:, 0:w], cv[:, w:2 * w] * cv[:, 2 * w:3 * w]
    p = cvp_ref[SUBLANE - 1:SUBLANE, :]
    n = cvn_ref[0:1, :]
    u_prev = p[:, w:2 * w] * p[:, 2 * w:3 * w]
    u_next = n[:, w:2 * w] * n[:, 2 * w:3 * w]
    ya_ref[...] = (gate_b * _conv3(u, u_prev, u_next, cw_ref, has_prev, has_next)).astype(ya_ref.dtype)

    xc = _conv3(xb_ref[...], xbp_ref[SUBLANE - 1:SUBLANE, :], xbn_ref[0:1, :], sw_ref, has_prev, has_next)
    xs_ref[...] = _silu(xc + sb_ref[...])

    t = dtr_ref[...] + dtb_ref[...]
    dt = jnp.maximum(t, 0.0) + jnp.log1p(jnp.exp(-jnp.abs(t)))
    dt_ref[0] = dt
    dt_ref[1] = pltpu.roll(dt, LANE - SSD_HEADS, 1)


def _prep(p_a, conv_w, ssd_conv_w, ssd_conv_b, dt_bias, n_lat_tiles, lat_tiles_per_seq,
          ctx_tiles_per_seq):
    m = p_a.shape[0]
    dt_blk = (COL_Q + SSD_XBC) // LANE
    tm = HALO_TILE
    per = tm // SUBLANE
    n8 = m // SUBLANE
    cw = 3 * CONV_W
    xbc_blk = COL_Q // SSD_XBC
    prev = lambda i: jnp.maximum(i * per - 1, 0)
    nxt = lambda i: jnp.minimum((i + 1) * per, n8 - 1)
    kern = functools.partial(_prep_kernel, n_lat_tiles=n_lat_tiles, lat_tiles_per_seq=lat_tiles_per_seq,
                             ctx_tiles_per_seq=ctx_tiles_per_seq)
    return pl.pallas_call(
        kern,
        grid=(m // tm,),
        in_specs=[pl.BlockSpec((tm, cw), lambda i: (i, 0)),
                  pl.BlockSpec((SUBLANE, cw), lambda i: (prev(i), 0)),
                  pl.BlockSpec((SUBLANE, cw), lambda i: (nxt(i), 0)),
                  pl.BlockSpec((tm, SSD_XBC), lambda i: (i, xbc_blk)),
                  pl.BlockSpec((SUBLANE, SSD_XBC), lambda i: (prev(i), xbc_blk)),
                  pl.BlockSpec((SUBLANE, SSD_XBC), lambda i: (nxt(i), xbc_blk)),
                  pl.BlockSpec((tm, LANE), lambda i: (i, dt_blk)),
                  pl.BlockSpec((SHORT_CONV, CONV_W), lambda i: (0, 0)),
                  pl.BlockSpec((SHORT_CONV, SSD_XBC), lambda i: (0, 0)),
                  pl.BlockSpec((1, SSD_XBC), lambda i: (0, 0)),
                  pl.BlockSpec((1, LANE), lambda i: (0, 0))],
        out_specs=[pl.BlockSpec((tm, CONV_W), lambda i: (i, 0)),
                   pl.BlockSpec((tm, SSD_XBC), lambda i: (i, 0)),
                   pl.BlockSpec((2, tm, LANE), lambda i: (0, i, 0))],
        out_shape=[jax.ShapeDtypeStruct((m, CONV_W), BF16),
                   jax.ShapeDtypeStruct((m, SSD_XBC), F32),
                   jax.ShapeDtypeStruct((2, m, LANE), F32)],
        compiler_params=_params(("parallel",), 40),
        name="conv_prep",
    )(p_a, p_a, p_a, p_a, p_a, p_a, p_a, conv_w, ssd_conv_w, ssd_conv_b, dt_bias)


def _split3(v):
    hi = v.astype(BF16)
    rest = v - hi.astype(F32)
    mid = rest.astype(BF16)
    return hi, mid, (rest - mid.astype(F32)).astype(BF16)


def _dot_f32_lhs(a, b01):
    return sum(_dot(piece, b01) for piece in _split3(a))


def _dot_f32_rhs(a01, b):
    return sum(_dot(a01, piece) for piece in _split3(b))


def _ssd_chunk(direction, xs_ref, dt_ref, alog_ref, alog_ch_ref, y_ref, state_ref):
    t = SSD_CHUNK
    r = lax.broadcasted_iota(jnp.int32, (t, t), 0)
    c = lax.broadcasted_iota(jnp.int32, (t, t), 1)
    mask = (r >= c) if direction == 0 else (r <= c)
    tri = mask.astype(BF16)

    er = lax.broadcasted_iota(jnp.int32, (LANE, SSD_INNER), 0)
    ec = lax.broadcasted_iota(jnp.int32, (LANE, SSD_INNER), 1)
    expand = (jnp.right_shift(ec, 6) == er).astype(BF16)

    dt = dt_ref[direction]
    cs = _dot_f32_rhs(tri, dt * (-jnp.exp(alog_ref[direction])))
    cs_t = cs.T
    dt_ch = _dot_f32_lhs(dt, expand)
    da_ch = dt_ch * (-jnp.exp(alog_ch_ref[direction]))
    cs_ch = _dot_f32_rhs(tri, da_ch)
    tot_ch = jnp.sum(da_ch, axis=0, keepdims=True)

    xdt = xs_ref[:, 0:SSD_INNER] * dt_ch
    x_state = xdt * jnp.exp(tot_ch - cs_ch)
    y_scale = jnp.exp(cs_ch)
    carry = jnp.exp(tot_ch)
    gw = SSD_GROUP_W
    heads_per_group = SSD_HEADS // SSD_GROUPS
    lane_head = jnp.right_shift(lax.broadcasted_iota(jnp.int32, (t, gw), 1), 6)

    for g in range(SSD_GROUPS):
        b_lo = SSD_INNER + g * SSD_STATE
        c_lo = SSD_INNER + SSD_GROUPS * SSD_STATE + g * SSD_STATE
        bg = xs_ref[:, b_lo:b_lo + SSD_STATE].astype(BF16)
        cg = xs_ref[:, c_lo:c_lo + SSD_STATE].astype(BF16)
        cb = _dot_nt(cg, bg)
        sg = state_ref[direction, g]
        xdt_g = xdt[:, g * gw:(g + 1) * gw]
        y = _dot(cg, sg.astype(BF16)) * y_scale[:, g * gw:(g + 1) * gw]
        for hh in range(heads_per_group):
            h = g * heads_per_group + hh
            decay = jnp.where(mask, jnp.exp(cs[:, h:h + 1] - cs_t[h:h + 1, :]), 0.0)
            x_h = jnp.where(lane_head == hh, xdt_g, 0.0).astype(BF16)
            y = y + _dot((cb * decay).astype(BF16), x_h)
        y_ref[:, g * gw:(g + 1) * gw] = y
        ds = _dot_tn(bg, x_state[:, g * gw:(g + 1) * gw].astype(BF16))
        state_ref[direction, g] = sg * carry[:, g * gw:(g + 1) * gw] + ds


def _ssd_kernel(xf_ref, dtf_ref, xb_ref, dtb_ref, alog_ref, alog_ch_ref, yf_ref, yb_ref, state_ref):
    @pl.when(pl.program_id(1) == 0)
    def _():
        state_ref[...] = jnp.zeros_like(state_ref)

    _ssd_chunk(0, xf_ref, dtf_ref, alog_ref, alog_ch_ref, yf_ref, state_ref)
    _ssd_chunk(1, xb_ref, dtb_ref, alog_ref, alog_ch_ref, yb_ref, state_ref)


def _ssd_scan(xs, dt2, a_log, batch, n_lat_chunks, n_ctx_chunks):
    m = xs.shape[0]
    t = SSD_CHUNK
    assert SSD_HEAD_DIM == 64 and SSD_HEADS <= LANE
    a_log_ch = jnp.repeat(a_log, SSD_HEAD_DIM, axis=1).reshape(2, 1, SSD_INNER)
    a_log = _pad_lanes(a_log).reshape(2, 1, LANE)

    def chunk(b, d, s):
        j_ctx = s if d == 0 else n_ctx_chunks - 1 - s
        sl = s - n_ctx_chunks
        j_lat = sl if d == 0 else n_lat_chunks - 1 - sl
        return jnp.where(s < n_ctx_chunks, batch * n_lat_chunks + b * n_ctx_chunks + j_ctx,
                         b * n_lat_chunks + j_lat)

    x_spec = lambda d: pl.BlockSpec((t, SSD_XBC), lambda b, s: (chunk(b, d, s), 0))
    dt_spec = lambda d: pl.BlockSpec((2, t, LANE), lambda b, s: (0, chunk(b, d, s), 0))
    y_spec = lambda d: pl.BlockSpec((t, SSD_INNER), lambda b, s: (chunk(b, d, s), 0))
    return pl.pallas_call(
        _ssd_kernel,
        grid=(batch, n_ctx_chunks + n_lat_chunks),
        in_specs=[x_spec(0), dt_spec(0), x_spec(1), dt_spec(1),
                  pl.BlockSpec((2, 1, LANE), lambda b, s: (0, 0, 0)),
                  pl.BlockSpec((2, 1, SSD_INNER), lambda b, s: (0, 0, 0))],
        out_specs=[y_spec(0), y_spec(1)],
        out_shape=[jax.ShapeDtypeStruct((m, SSD_INNER), F32), jax.ShapeDtypeStruct((m, SSD_INNER), F32)],
        scratch_shapes=[pltpu.VMEM((2, SSD_GROUPS, SSD_STATE, SSD_GROUP_W), F32)],
        compiler_params=_params(("parallel", "arbitrary")),
        name="ssd_scan",
    )(xs, dt2, xs, dt2, a_log, a_log_ch)


def _ssd_gate_norm(y_fwd, y_bwd, x, z, d_skip, norm_g):
    u = (d_skip * x + y_fwd + y_bwd) * _silu(z)
    gw = SSD_GROUP_W
    parts = []
    for g in range(SSD_GROUPS):
        ug = u[:, g * gw:(g + 1) * gw]
        ug = ug * lax.rsqrt(jnp.mean(ug * ug, axis=-1, keepdims=True) + NORM_EPS)
        parts.append(ug * norm_g[:, g * gw:(g + 1) * gw])
    return jnp.concatenate(parts, axis=1)


ATT_TQ = 1024
ATT_TKC = 1024


def _attn_kernel(*refs, has_lat, lam_init):
    if has_lat:
        (q_ref, kc_ref, vct_ref, kl_ref, vlt_ref, lam_ref, g_ref, o_ref,
         m_ref, acc_ref, sa_ref, sb_ref) = refs
    else:
        q_ref, kc_ref, vct_ref, lam_ref, g_ref, o_ref, m_ref, acc_ref, sa_ref, sb_ref = refs

    q = q_ref[...]
    lane = lax.broadcasted_iota(jnp.int32, q.shape, 1)
    zero = jnp.zeros_like(q)
    in_map0 = (lane % DA_HEAD_DIM) < (DA_HEAD_DIM // 2)
    q_maps = (jnp.where(in_map0, q, zero), jnp.where(in_map0, zero, q))

    def scores(k, s_ref):
        n = k.shape[0]
        maxima = []
        for mp in range(2):
            s = _dot_nt(k, q_maps[mp])
            s_ref[mp, 0:n, :] = s
            maxima.append(jnp.max(s, axis=0, keepdims=True))
        return tuple(maxima)

    def update(s_ref, maxima, vt):
        n = vt.shape[1]
        for mp in range(2):
            m_old = m_ref[mp]
            m_new = jnp.maximum(m_old, maxima[mp])
            alpha = jnp.exp2(m_old - m_new)
            p = jnp.exp2(s_ref[mp, 0:n, :] - m_new[0:1, :])
            acc_ref[mp] = alpha[0:1, :] * acc_ref[mp] + _dot(vt, p.astype(vt.dtype))
            m_ref[mp] = m_new

    m_ref[...] = jnp.full(m_ref.shape, -jnp.inf, F32)
    acc_ref[...] = jnp.zeros_like(acc_ref)
    mx = scores(kc_ref[...], sa_ref)

    if not has_lat:
        update(sa_ref, mx, vct_ref[...])
    else:
        tkc = min(ATT_TKC, kl_ref.shape[0])
        n_chunks = kl_ref.shape[0] // tkc
        k_at = lambda c: kl_ref[pl.ds(pl.multiple_of(c * tkc, tkc), tkc), :]
        vt_at = lambda c: vlt_ref[:, pl.ds(pl.multiple_of(c * tkc, tkc), tkc)]

        mx_ctx = mx
        mx = scores(k_at(0), sb_ref)
        update(sa_ref, mx_ctx, vct_ref[...])

        def pair(c2, mx_b):
            c = 1 + 2 * c2
            mx_a = scores(k_at(c), sa_ref)
            update(sb_ref, mx_b, vt_at(c - 1))
            mx_b = scores(k_at(c + 1), sb_ref)
            update(sa_ref, mx_a, vt_at(c))
            return mx_b

        mx = lax.fori_loop(0, (n_chunks - 1) // 2, pair, mx)
        if (n_chunks - 1) % 2:
            mx_a = scores(k_at(n_chunks - 1), sa_ref)
            update(sb_ref, mx, vt_at(n_chunks - 2))
            update(sa_ref, mx_a, vt_at(n_chunks - 1))
        else:
            update(sb_ref, mx, vt_at(n_chunks - 1))

    lv = lam_ref[...]
    dotp = lambda a, b: jnp.sum(lv[a:a + 1, :] * lv[b:b + 1, :], axis=-1, keepdims=True)
    lam = jnp.exp(dotp(0, 1)) - jnp.exp(dotp(2, 3)) + lam_init
    vd = DA_V_DIM
    inv_l = 1.0 / acc_ref[:, vd:vd + 1, :]
    o = acc_ref[0, 0:vd, :] * inv_l[0] - lam * (acc_ref[1, 0:vd, :] * inv_l[1])
    y = o * lax.rsqrt(jnp.mean(o * o, axis=0, keepdims=True) + NORM_EPS)
    o_ref[...] = ((y * g_ref[...]) * (1.0 - lam_init)).T.astype(o_ref.dtype)


VT_ROWS = DA_V_DIM + 16


def _attn_scratch(tq, keys_a, keys_b):
    return [pltpu.VMEM((2, SUBLANE, tq), F32), pltpu.VMEM((2, VT_ROWS, tq), F32),
            pltpu.VMEM((2, keys_a, tq), F32), pltpu.VMEM((2, keys_b, tq), F32)]


def _attention_lat(qkv, v_t, da_lambda, subln, lam_init, batch, seq, ctx_len):
    h = DA_HEADS
    tq = ATT_TQ
    nq = seq // tq
    ctx_blk0 = (batch * seq) // ctx_len
    assert seq % tq == 0 and seq % min(ATT_TKC, seq) == 0
    kern = functools.partial(_attn_kernel, has_lat=True, lam_init=lam_init)
    return pl.pallas_call(
        kern,
        grid=(batch, h, nq),
        in_specs=[pl.BlockSpec((tq, LANE), lambda b, hh, i: (b * nq + i, hh)),
                  pl.BlockSpec((ctx_len, LANE), lambda b, hh, i: (ctx_blk0 + b, h + hh)),
                  pl.BlockSpec((VT_ROWS, ctx_len), lambda b, hh, i: (hh, ctx_blk0 + b)),
                  pl.BlockSpec((seq, LANE), lambda b, hh, i: (b, h + hh)),
                  pl.BlockSpec((VT_ROWS, seq), lambda b, hh, i: (hh, b)),
                  pl.BlockSpec(da_lambda.shape, lambda b, hh, i: (0, 0)),
                  pl.BlockSpec((DA_V_DIM, 1), lambda b, hh, i: (0, 0))],
        out_specs=pl.BlockSpec((tq, LANE), lambda b, hh, i: (b * nq + i, hh)),
        out_shape=jax.ShapeDtypeStruct((batch * seq, DA_WIDTH), BF16),
        scratch_shapes=_attn_scratch(tq, max(min(ATT_TKC, seq), ctx_len), min(ATT_TKC, seq)),
        compiler_params=_params(("parallel", "parallel", "parallel"), 48),
        name="diff_attention",
    )(qkv, qkv, v_t, qkv, v_t, da_lambda, subln)


def _attention_ctx(qkv, v_t, da_lambda, subln, lam_init, batch, seq, ctx_len):
    h = DA_HEADS
    ctx_blk0 = (batch * seq) // ctx_len
    kern = functools.partial(_attn_kernel, has_lat=False, lam_init=lam_init)
    return pl.pallas_call(
        kern,
        grid=(batch, h),
        in_specs=[pl.BlockSpec((ctx_len, LANE), lambda b, hh: (ctx_blk0 + b, hh)),
                  pl.BlockSpec((ctx_len, LANE), lambda b, hh: (ctx_blk0 + b, h + hh)),
                  pl.BlockSpec((VT_ROWS, ctx_len), lambda b, hh: (hh, ctx_blk0 + b)),
                  pl.BlockSpec(da_lambda.shape, lambda b, hh: (0, 0)),
                  pl.BlockSpec((DA_V_DIM, 1), lambda b, hh: (0, 0))],
        out_specs=pl.BlockSpec((ctx_len, LANE), lambda b, hh: (b, hh)),
        out_shape=jax.ShapeDtypeStruct((batch * ctx_len, DA_WIDTH), BF16),
        scratch_shapes=_attn_scratch(ctx_len, ctx_len, SUBLANE),
        compiler_params=_params(("parallel", "parallel")),
        name="diff_attention_ctx",
    )(qkv, qkv, v_t, da_lambda, subln)


def _outproj_kernel(ya_ref, yf_ref, yr_ref, xs_ref, z_ref, dsk_ref, gn_ref, yc_ref, ycx_ref,
                    w0_ref, w1_ref, w2_ref, xl_ref, xc_ref, gate_ref, o_ref, *, n_lat_tiles):
    yb = _ssd_gate_norm(yf_ref[...], yr_ref[...], xs_ref[...], z_ref[...], dsk_ref[...], gn_ref[...])
    yc = _lat_or_ctx(yc_ref, ycx_ref, n_lat_tiles)
    acc = _dot(ya_ref[...], w0_ref[...]) + _dot(yb.astype(BF16), w1_ref[...]) + _dot(yc, w2_ref[...])
    o_ref[...] = _lat_or_ctx(xl_ref, xc_ref, n_lat_tiles) + gate_ref[...] * acc


def _out_proj(ya, y_fwd, y_bwd, xbc, p_a, d_skip, norm_g, yc, yc_ctx, w_out, x_lat, x_ctx, mods,
              n_tiles, n_lat_tiles, cls_of_tile):
    d = x_lat.shape[1]
    wa, wb = ya.shape[1], SSD_INNER
    assert wa == wb and yc.shape[1] == wa + wb
    if yc_ctx is None:
        assert n_tiles <= n_lat_tiles
        yc_ctx = yc
    row_blk = lambda col: pl.BlockSpec((ROW_TILE, wb), lambda i: (i, col))
    vec = pl.BlockSpec((1, wb), lambda i: (0, 0))
    w_rows = lambda rows, blk: pl.BlockSpec((rows, d), lambda i: (blk, 0), pipeline_mode=pl.Buffered(1))
    return pl.pallas_call(
        functools.partial(_outproj_kernel, n_lat_tiles=n_lat_tiles),
        grid=(n_tiles,),
        in_specs=[pl.BlockSpec((ROW_TILE, wa), lambda i: (i, 0)),
                  row_blk(0), row_blk(0), row_blk(0), row_blk(COL_Z // wb), vec, vec]
                 + _lat_ctx_specs(wa + wb, n_lat_tiles)
                 + [w_rows(wa, 0), w_rows(wb, 1), w_rows(wa + wb, 1)]
                 + _lat_ctx_specs(d, n_lat_tiles)
                 + [pl.BlockSpec((None, 1, d), lambda i: (cls_of_tile(i), 0, 2))],
        out_specs=pl.BlockSpec((ROW_TILE, d), lambda i: (i, 0)),
        out_shape=jax.ShapeDtypeStruct((n_tiles * ROW_TILE, d), F32),
        compiler_params=_params(("parallel",), 56),
        name="out_proj_residual",
    )(ya, y_fwd, y_bwd, xbc, p_a, d_skip, norm_g, yc, yc_ctx, w_out, w_out, w_out, x_lat, x_ctx, mods)


def _ffn_kernel(x_ref, g_ref, sh_ref, sc_ref, gate_ref, wg_ref, wu_ref, wd_ref, o_ref, h_ref, acc_ref):
    f = pl.program_id(1)

    @pl.when(f == 0)
    def _():
        h_ref[...] = _modulated_norm(x_ref[...], g_ref[...], sh_ref[...], sc_ref[...]).astype(h_ref.dtype)
        acc_ref[...] = jnp.zeros_like(acc_ref)

    h = h_ref[...]
    a = _silu(_dot(h, wg_ref[...])) * _dot(h, wu_ref[...])
    acc_ref[...] += _dot(a.astype(BF16), wd_ref[...])

    @pl.when(f == pl.num_programs(1) - 1)
    def _():
        o_ref[...] = x_ref[...] + gate_ref[...] * acc_ref[...]


def _ffn_dense(x, g, mods, wg, wu, wd, n_tiles, cls_of_tile):
    m, d = x.shape
    ff = wg.shape[1]
    tf = 512
    return pl.pallas_call(
        _ffn_kernel,
        grid=(n_tiles, ff // tf),
        in_specs=[pl.BlockSpec((ROW_TILE, d), lambda i, f: (i, 0)),
                  pl.BlockSpec((1, d), lambda i, f: (0, 0)),
                  _mod_spec(3, d, cls_of_tile), _mod_spec(4, d, cls_of_tile), _mod_spec(5, d, cls_of_tile),
                  pl.BlockSpec((d, tf), lambda i, f: (0, f)),
                  pl.BlockSpec((d, tf), lambda i, f: (0, f)),
                  pl.BlockSpec((tf, d), lambda i, f: (f, 0))],
        out_specs=pl.BlockSpec((ROW_TILE, d), lambda i, f: (i, 0)),
        out_shape=jax.ShapeDtypeStruct((n_tiles * ROW_TILE, d), F32),
        scratch_shapes=[pltpu.VMEM((ROW_TILE, d), BF16), pltpu.VMEM((ROW_TILE, d), F32)],
        compiler_params=_params(("parallel", "arbitrary"), 48),
        name="ffn_dense",
    )(x, g.reshape(1, d), mods, mods, mods, wg, wu, wd)


ROUTE_I1, ROUTE_I2, ROUTE_P1, ROUTE_P2 = 0, 1, 2, 3


def _router_kernel(x_ref, g_ref, sh_ref, sc_ref, wr_ref, br_ref, h_ref, route_ref):
    h = _modulated_norm(x_ref[...], g_ref[...], sh_ref[...], sc_ref[...])
    h_ref[...] = h
    logits = _dot_3pass(h, wr_ref[...]) + br_ref[...]
    lane = lax.broadcasted_iota(jnp.int32, logits.shape, 1).astype(F32)
    neg = -jnp.inf
    lg = jnp.where(lane < N_EXPERTS, logits, neg)
    v1 = jnp.max(lg, axis=-1, keepdims=True)
    i1 = jnp.min(jnp.where(lg == v1, lane, float(LANE)), axis=-1, keepdims=True)
    lg2 = jnp.where(lane == i1, neg, lg)
    v2 = jnp.max(lg2, axis=-1, keepdims=True)
    i2 = jnp.min(jnp.where(lg2 == v2, lane, float(LANE)), axis=-1, keepdims=True)
    e = jnp.exp(v2 - v1)
    p1 = 1.0 / (1.0 + e)
    p2 = e / (1.0 + e)
    rec = jnp.where(lane == ROUTE_I1, i1, 0.0) + jnp.where(lane == ROUTE_I2, i2, 0.0)
    route_ref[...] = rec + jnp.where(lane == ROUTE_P1, p1, 0.0) + jnp.where(lane == ROUTE_P2, p2, 0.0)


def _router(x, g, mods, wr, br, n_tiles, cls_of_tile):
    m, d = x.shape
    return pl.pallas_call(
        _router_kernel,
        grid=(n_tiles,),
        in_specs=[pl.BlockSpec((ROW_TILE, d), lambda i: (i, 0)),
                  pl.BlockSpec((1, d), lambda i: (0, 0)),
                  _mod_spec(3, d, cls_of_tile), _mod_spec(4, d, cls_of_tile),
                  pl.BlockSpec((d, LANE), lambda i: (0, 0)),
                  pl.BlockSpec((1, LANE), lambda i: (0, 0))],
        out_specs=[pl.BlockSpec((ROW_TILE, d), lambda i: (i, 0)),
                   pl.BlockSpec((ROW_TILE, LANE), lambda i: (i, 0))],
        out_shape=[jax.ShapeDtypeStruct((n_tiles * ROW_TILE, d), F32),
                   jax.ShapeDtypeStruct((n_tiles * ROW_TILE, LANE), F32)],
        compiler_params=_params(("parallel",), 40),
        name="moe_router",
    )(x, g.reshape(1, d), mods, mods, wr, br)


def _routing_tables(route, n_experts, tile):
    n_tok = route.shape[0]
    experts = route[:, ROUTE_I1:ROUTE_I2 + 1].astype(jnp.int32).reshape(-1)
    onehot = (experts[:, None] == jnp.arange(n_experts, dtype=jnp.int32)[None, :]).astype(jnp.int32)
    running = jnp.cumsum(onehot, axis=0)
    rank = jnp.sum(running * onehot, axis=1) - 1
    counts = running[-1]
    padded = ((counts + tile - 1) // tile) * tile
    ends = jnp.cumsum(padded)
    slot = (ends - padded)[experts] + rank
    n_rows = 2 * n_tok + n_experts * tile
    n_tiles = n_rows // tile
    token = jnp.repeat(jnp.arange(n_tok, dtype=jnp.int32), 2)
    row_token = jnp.zeros((n_rows,), jnp.int32).at[slot].set(token)
    tile_start = jnp.arange(n_tiles, dtype=jnp.int32) * tile
    tile_expert = jnp.minimum(jnp.sum((tile_start[:, None] >= ends[None, :]).astype(jnp.int32), axis=1),
                              n_experts - 1)
    n_active = (ends[-1] // tile).astype(jnp.int32).reshape(1)
    return tile_expert, row_token, n_active, slot.astype(jnp.int32)


MOE_TILE = 256
MOE_FF_SPLIT = 1
GATHER_UNROLL = 8


def _experts_kernel(te_ref, tok_ref, nact_ref, h_hbm, wg_ref, wu_ref, wd_ref, o_ref, xbuf, sem):
    j = pl.program_id(0)
    n_act = nact_ref[0]
    slot = j % 2
    tile = o_ref.shape[0]

    def row_copy(tok, r, s):
        return pltpu.make_async_copy(h_hbm.at[pl.ds(tok, 1), :], xbuf.at[s, pl.ds(r, 1), :], sem.at[s])

    def start_gather(t, s):
        def body(r, carry):
            row_copy(tok_ref[t * tile + r], r, s).start()
            return carry
        lax.fori_loop(0, tile, body, 0, unroll=GATHER_UNROLL)

    def wait_gather(s):
        pltpu.make_async_copy(h_hbm.at[pl.ds(0, tile), :], xbuf.at[s], sem.at[s]).wait()

    @pl.when(j < n_act)
    def _():
        @pl.when(j == 0)
        def _():
            start_gather(0, 0)

        wait_gather(slot)

        @pl.when(j + 1 < n_act)
        def _():
            start_gather(j + 1, 1 - slot)

        h = xbuf[slot].astype(BF16)
        ff = wg_ref.shape[1]
        w = ff // MOE_FF_SPLIT
        y = None
        for part in range(MOE_FF_SPLIT):
            lo = part * w
            a = _silu(_dot(h, wg_ref[:, lo:lo + w])) * _dot(h, wu_ref[:, lo:lo + w])
            yp = _dot(a.astype(BF16), wd_ref[lo:lo + w, :])
            y = yp if y is None else y + yp
        o_ref[...] = y

    @pl.when(j >= n_act)
    def _():
        o_ref[...] = jnp.zeros_like(o_ref)


def _experts(h, tile_expert, row_token, n_active, wg, wu, wd):
    n_e, d, ff = wg.shape
    tile = MOE_TILE
    n_rows = row_token.shape[0]
    assert ff % (MOE_FF_SPLIT * LANE) == 0
    resident = pl.Buffered(1)
    grid_spec = pltpu.PrefetchScalarGridSpec(
        num_scalar_prefetch=3,
        grid=(n_rows // tile,),
        in_specs=[pl.BlockSpec(memory_space=pl.ANY),
                  pl.BlockSpec((None, d, ff), lambda j, te, tok, na: (te[j], 0, 0), pipeline_mode=resident),
                  pl.BlockSpec((None, d, ff), lambda j, te, tok, na: (te[j], 0, 0), pipeline_mode=resident),
                  pl.BlockSpec((None, ff, d), lambda j, te, tok, na: (te[j], 0, 0), pipeline_mode=resident)],
        out_specs=pl.BlockSpec((tile, d), lambda j, te, tok, na: (j, 0)),
        scratch_shapes=[pltpu.VMEM((2, tile, d), F32), pltpu.SemaphoreType.DMA((2,))],
    )
    return pl.pallas_call(
        _experts_kernel,
        grid_spec=grid_spec,
        out_shape=jax.ShapeDtypeStruct((n_rows, d), F32),
        compiler_params=_params(("arbitrary",), 56),
        name="moe_experts",
    )(tile_expert, row_token, n_active, h, wg, wu, wd)


COMBINE_TILE = 256


def _combine_kernel(slot_ref, y_hbm, x_ref, route_ref, gate_ref, gf_ref, o_ref, ybuf, sem, *, final_norm):
    i = pl.program_id(0)
    n = pl.num_programs(0)
    buf = i % 2
    tile = x_ref.shape[0]

    def row_copy(row, k, r, s):
        return pltpu.make_async_copy(y_hbm.at[pl.ds(row, 1), :], ybuf.at[s, pl.ds(k * tile + r, 1), :], sem.at[s])

    def start_gather(t, s):
        def body(r, carry):
            base = 2 * (t * tile + r)
            row_copy(slot_ref[base], 0, r, s).start()
            row_copy(slot_ref[base + 1], 1, r, s).start()
            return carry
        lax.fori_loop(0, tile, body, 0, unroll=GATHER_UNROLL // 2)

    def wait_gather(s):
        pltpu.make_async_copy(y_hbm.at[pl.ds(0, 2 * tile), :], ybuf.at[s], sem.at[s]).wait()

    @pl.when(i == 0)
    def _():
        start_gather(0, 0)

    wait_gather(buf)

    @pl.when(i + 1 < n)
    def _():
        start_gather(i + 1, 1 - buf)

    route = route_ref[...]
    p1 = route[:, ROUTE_P1:ROUTE_P1 + 1]
    p2 = route[:, ROUTE_P2:ROUTE_P2 + 1]
    y = p1 * ybuf[buf, 0:tile, :] + p2 * ybuf[buf, tile:2 * tile, :]
    out = x_ref[...] + gate_ref[...] * y
    if final_norm:
        out = (out * lax.rsqrt(jnp.mean(out * out, axis=-1, keepdims=True) + NORM_EPS)) * gf_ref[...]
    o_ref[...] = out


def _combine(y_rows, slot, x, route, mods, n_tokens, cls_of_tile, final_g):
    d = x.shape[1]
    tile = COMBINE_TILE
    per = ROW_TILE // tile
    final_norm = final_g is not None
    if not final_norm:
        final_g = jnp.ones((d,), F32)
    grid_spec = pltpu.PrefetchScalarGridSpec(
        num_scalar_prefetch=1,
        grid=(n_tokens // tile,),
        in_specs=[pl.BlockSpec(memory_space=pl.ANY),
                  pl.BlockSpec((tile, d), lambda i, s: (i, 0)),
                  pl.BlockSpec((tile, LANE), lambda i, s: (i, 0)),
                  pl.BlockSpec((None, 1, d), lambda i, s: (cls_of_tile(i // per), 0, 5)),
                  pl.BlockSpec((1, d), lambda i, s: (0, 0))],
        out_specs=pl.BlockSpec((tile, d), lambda i, s: (i, 0)),
        scratch_shapes=[pltpu.VMEM((2, 2 * tile, d), F32), pltpu.SemaphoreType.DMA((2,))],
    )
    return pl.pallas_call(
        functools.partial(_combine_kernel, final_norm=final_norm),
        grid_spec=grid_spec,
        out_shape=jax.ShapeDtypeStruct((n_tokens, d), F32),
        compiler_params=_params(("arbitrary",), 48),
        name="moe_combine",
    )(slot, y_rows, x, route, mods, final_g.reshape(1, d))


def _final_kernel(x_ref, g_ref, o_ref):
    x = x_ref[...]
    o_ref[...] = (x * lax.rsqrt(jnp.mean(x * x, axis=-1, keepdims=True) + NORM_EPS)) * g_ref[...]


def _final_norm(x, g, n_tiles):
    d = x.shape[1]
    return pl.pallas_call(
        _final_kernel,
        grid=(n_tiles,),
        in_specs=[pl.BlockSpec((ROW_TILE, d), lambda i: (i, 0)),
                  pl.BlockSpec((1, d), lambda i: (0, 0))],
        out_specs=pl.BlockSpec((ROW_TILE, d), lambda i: (i, 0)),
        out_shape=jax.ShapeDtypeStruct((n_tiles * ROW_TILE, d), F32),
        compiler_params=_params(("parallel",), 40),
        name="final_norm",
    )(x, g.reshape(1, d))


def _pad_lanes(v, width=LANE):
    return jnp.pad(v, [(0, 0)] * (v.ndim - 1) + [(0, width - v.shape[-1])])


def kernel(x, c, ctx, c_ctx, w_mod, b_mod, g_mix, g_ffn, w_in, conv_w, ssd_conv_w, ssd_conv_b, ssd_a_log, ssd_dt_bias, ssd_d, ssd_norm, da_lambda, da_subln, w_out, ffn_w_gate, ffn_w_up, ffn_w_down, moe_w_router, moe_b_router, moe_w_gate, moe_w_up, moe_w_down, g_final):
    batch, seq, d = x.shape
    ctx_len = ctx.shape[1]
    depth = w_mod.shape[0]
    n_lat = batch * seq
    m = n_lat + batch * ctx_len
    assert seq % ROW_TILE == 0 and (batch * ctx_len) % ROW_TILE == 0 and n_lat % ctx_len == 0
    assert ctx_len % HALO_TILE == 0 and ctx_len % SSD_CHUNK == 0 and batch < MOD_CLASSES
    n_lat_tiles = n_lat // ROW_TILE
    n_all_tiles = m // ROW_TILE
    tiles_per_seq = seq // ROW_TILE
    cls_of_tile = lambda i: jnp.minimum(i // tiles_per_seq, batch)

    x_lat, x_ctx = x.reshape(n_lat, d), ctx.reshape(batch * ctx_len, d)
    cvec = jnp.zeros((MOD_CLASSES, d), F32).at[:batch].set(c).at[batch].set(c_ctx)
    mods_all = _mod_vectors(cvec, w_mod, b_mod)
    cos_t, sin_t = _rope_tables(seq)

    for i in range(depth):
        ctx_out = i < depth - 1
        lam_init = 0.8 - 0.6 * math.exp(-0.3 * i)
        n_tiles = n_all_tiles if ctx_out else n_lat_tiles
        mods = mods_all[i].reshape(MOD_CLASSES, 1, N_MOD * d)
        cols = lambda lo, hi: w_in[i, :, lo:hi]
        w_a = jnp.concatenate([cols(COL_CONV, COL_Q), cols(COL_XBC, COL_DT),
                               _pad_lanes(cols(COL_DT, COL_K))], axis=1).astype(BF16)
        w_q = _head_lane_order(cols(COL_Q, COL_XBC)) * (DA_SCALE * LOG2_E)
        w_k = _head_lane_order(cols(COL_K, COL_V))
        w_b = jnp.concatenate([w_q, w_k, cols(COL_V, COL_V + DA_WIDTH)], axis=1).astype(BF16)

        h, p_a = _in_proj(x_lat, x_ctx, g_mix[i], mods, w_a, n_all_tiles, n_lat_tiles, cls_of_tile)
        qkv, v_t = _matmul_rope(h, w_b, cos_t, sin_t, n_lat_tiles, tiles_per_seq)

        ya, xbc, dt2 = _prep(p_a, conv_w[i], ssd_conv_w[i], ssd_conv_b[i].reshape(1, -1),
                             _pad_lanes(ssd_dt_bias[i].reshape(1, -1)), n_lat // HALO_TILE,
                             seq // HALO_TILE, ctx_len // HALO_TILE)
        y_fwd, y_bwd = _ssd_scan(xbc, dt2, ssd_a_log[i], batch, seq // SSD_CHUNK, ctx_len // SSD_CHUNK)

        subln = da_subln[i].reshape(-1, 1)
        yc = _attention_lat(qkv, v_t, da_lambda[i], subln, lam_init, batch, seq, ctx_len)
        yc_ctx = _attention_ctx(qkv, v_t, da_lambda[i], subln, lam_init, batch, seq, ctx_len) if ctx_out else None

        x_mid = _out_proj(ya, y_fwd, y_bwd, xbc, p_a, jnp.repeat(ssd_d[i], SSD_HEAD_DIM).reshape(1, -1),
                          ssd_norm[i].reshape(1, -1), yc, yc_ctx, w_out[i].astype(BF16), x_lat, x_ctx, mods,
                          n_tiles, n_lat_tiles, cls_of_tile)

        j = i // 2
        if i % 2 == 0:
            xs_all = _ffn_dense(x_mid, g_ffn[i], mods, ffn_w_gate[j].astype(BF16), ffn_w_up[j].astype(BF16),
                                ffn_w_down[j].astype(BF16), n_tiles, cls_of_tile)
        else:
            hh, route = _router(x_mid, g_ffn[i], mods, _pad_lanes(moe_w_router[j]),
                                _pad_lanes(moe_b_router[j].reshape(1, -1)), n_tiles, cls_of_tile)
            tile_expert, row_token, n_active, slot = _routing_tables(route, moe_w_gate.shape[1], MOE_TILE)
            y_rows = _experts(hh, tile_expert, row_token, n_active, moe_w_gate[j].astype(BF16),
                              moe_w_up[j].astype(BF16), moe_w_down[j].astype(BF16))
            xs_all = _combine(y_rows, slot, x_mid, route, mods, n_tiles * ROW_TILE, cls_of_tile,
                              None if ctx_out else g_final)
        x_lat = xs_all
        x_ctx = xs_all[n_lat:] if ctx_out else None

    last_is_moe = depth % 2 == 0
    out = x_lat if last_is_moe else _final_norm(x_lat, g_final, n_lat_tiles)
    return out.reshape(batch, seq, d)
```

```python
import functools
import math

import jax
import jax.numpy as jnp
from jax import lax
from jax.experimental import pallas as pl
from jax.experimental.pallas import tpu as pltpu

NORM_EPS = 1e-6
N_MOD = 6
GRID_W = 64

SHORT_CONV = 3
CONV_W = 512

SSD_HEADS = 8
SSD_HEAD_DIM = 64
SSD_INNER = SSD_HEADS * SSD_HEAD_DIM
SSD_STATE = 128
SSD_GROUPS = 2
SSD_CHUNK = 128
SSD_XBC = SSD_INNER + 2 * SSD_GROUPS * SSD_STATE
SSD_GROUP_W = SSD_INNER // SSD_GROUPS

DA_HEADS = 8
DA_HEAD_DIM = 64
DA_V_DIM = 2 * DA_HEAD_DIM
DA_QK = DA_HEADS * 2 * DA_HEAD_DIM
DA_WIDTH = DA_HEADS * DA_V_DIM
DA_SCALE = DA_HEAD_DIM ** -0.5
LOG2_E = math.log2(math.e)
ROPE_THETA = 10000.0

COL_CONV = 0
COL_Z = COL_CONV + 3 * CONV_W
COL_Q = COL_Z + SSD_INNER
COL_XBC = COL_Q + DA_QK
COL_DT = COL_XBC + SSD_XBC
COL_K = COL_DT + 2 * SSD_HEADS
COL_V = COL_K + DA_QK

N_EXPERTS = 8

LANE = 128
SUBLANE = 8
ROW_TILE = 512
HALO_TILE = 256
MOD_CLASSES = 8

F32 = jnp.float32
BF16 = jnp.bfloat16
MIB = 1024 * 1024


def _params(semantics, vmem_mib=None):
    kw = {"dimension_semantics": semantics}
    if vmem_mib is not None:
        kw["vmem_limit_bytes"] = vmem_mib * MIB
    return pltpu.CompilerParams(**kw)


def _silu(v):
    return v * jax.nn.sigmoid(v)


def _dot(a, b):
    return jnp.dot(a, b, preferred_element_type=F32)


def _dot_nt(a, b):
    return lax.dot_general(a, b, (((1,), (1,)), ((), ())), preferred_element_type=F32)


def _dot_tn(a, b):
    return lax.dot_general(a, b, (((0,), (0,)), ((), ())), preferred_element_type=F32)


def _split2(v):
    hi = v.astype(BF16)
    return hi, (v - hi.astype(F32)).astype(BF16)


def _dot_3pass(a, b):
    a_hi, a_lo = _split2(a)
    b_hi, b_lo = _split2(b)
    return _dot(a_hi, b_hi) + (_dot(a_hi, b_lo) + _dot(a_lo, b_hi))


def _modulated_norm(x, g, shift, scale):
    ms = jnp.mean(x * x, axis=-1, keepdims=True)
    y = x * lax.rsqrt(ms + NORM_EPS)
    return (y * g) * (1.0 + scale) + shift


def _mod_spec(k, width, cls_of_tile):
    return pl.BlockSpec((None, 1, width), lambda i, *_: (cls_of_tile(i), 0, k))


def _mod_kernel(c_ref, w_ref, b_ref, o_ref):
    s = _silu(c_ref[...])
    o_ref[...] = _dot_3pass(s, w_ref[...]) + b_ref[...]


def _mod_vectors(cvec, w_mod, b_mod):
    depth, d, n = w_mod.shape
    tn = 1024
    return pl.pallas_call(
        _mod_kernel,
        grid=(depth, n // tn),
        in_specs=[pl.BlockSpec((MOD_CLASSES, d), lambda l, j: (0, 0)),
                  pl.BlockSpec((None, d, tn), lambda l, j: (l, 0, j)),
                  pl.BlockSpec((None, 1, tn), lambda l, j: (l, 0, j))],
        out_specs=pl.BlockSpec((None, MOD_CLASSES, tn), lambda l, j: (l, 0, j)),
        out_shape=jax.ShapeDtypeStruct((depth, MOD_CLASSES, n), F32),
        compiler_params=_params(("parallel", "parallel"), 40),
        name="mod_vectors",
    )(cvec, w_mod, b_mod.reshape(depth, 1, n))


def _lat_ctx_specs(width, n_lat_tiles):
    return [pl.BlockSpec((ROW_TILE, width), lambda i, *_: (jnp.minimum(i, n_lat_tiles - 1), 0)),
            pl.BlockSpec((ROW_TILE, width), lambda i, *_: (jnp.maximum(i - n_lat_tiles, 0), 0))]


def _lat_or_ctx(lat_ref, ctx_ref, n_lat_tiles):
    return jnp.where(pl.program_id(0) < n_lat_tiles, lat_ref[...], ctx_ref[...])


def _in_proj_kernel(xl_ref, xc_ref, g_ref, sh_ref, sc_ref, w_ref, h_ref, p_ref, *, n_lat_tiles):
    x = _lat_or_ctx(xl_ref, xc_ref, n_lat_tiles)
    h = _modulated_norm(x, g_ref[...], sh_ref[...], sc_ref[...]).astype(h_ref.dtype)
    h_ref[...] = h
    p_ref[...] = _dot(h, w_ref[...])


def _in_proj(x_lat, x_ctx, g, mods, w, n_tiles, n_lat_tiles, cls_of_tile):
    d = x_lat.shape[1]
    n = w.shape[1]
    lat_spec, ctx_spec = _lat_ctx_specs(d, n_lat_tiles)
    ctx_spec = pl.BlockSpec(ctx_spec.block_shape, ctx_spec.index_map, pipeline_mode=pl.Buffered(1))
    return pl.pallas_call(
        functools.partial(_in_proj_kernel, n_lat_tiles=n_lat_tiles),
        grid=(n_tiles,),
        in_specs=[lat_spec, ctx_spec,
                  pl.BlockSpec((1, d), lambda i: (0, 0)),
                  _mod_spec(0, d, cls_of_tile),
                  _mod_spec(1, d, cls_of_tile),
                  pl.BlockSpec((d, n), lambda i: (0, 0), pipeline_mode=pl.Buffered(1))],
        out_specs=[pl.BlockSpec((ROW_TILE, d), lambda i: (i, 0)),
                   pl.BlockSpec((ROW_TILE, n), lambda i: (i, 0))],
        out_shape=[jax.ShapeDtypeStruct((n_tiles * ROW_TILE, d), BF16),
                   jax.ShapeDtypeStruct((n_tiles * ROW_TILE, n), F32)],
        compiler_params=_params(("parallel",), 56),
        name="in_proj_norm",
    )(x_lat, x_ctx, g.reshape(1, d), mods, mods, w)


def _mm_rope_kernel(x_ref, w_ref, cos_ref, sin_ref, qk_ref, vt_ref, *, n_lat_tiles):
    x = x_ref[...]
    tm = x.shape[0]
    qk = _dot(x, w_ref[:, 0:2 * DA_QK])
    is_lat = pl.program_id(0) < n_lat_tiles

    @pl.when(is_lat)
    def _():
        c = cos_ref[...]
        s = sin_ref[...]
        for hb in range(2 * DA_QK // LANE):
            blk = qk[:, hb * LANE:(hb + 1) * LANE]
            partner = pltpu.roll(blk, LANE // 2, 1)
            qk_ref[:, hb * LANE:(hb + 1) * LANE] = (blk * c + partner * s).astype(qk_ref.dtype)

    @pl.when(jnp.logical_not(is_lat))
    def _():
        qk_ref[...] = qk.astype(qk_ref.dtype)

    v = _dot(x, w_ref[:, 2 * DA_QK:])
    row = lax.broadcasted_iota(jnp.int32, (VT_ROWS - DA_V_DIM, tm), 0)
    tail = jnp.where(row == 0, 1.0, 0.0).astype(vt_ref.dtype)
    for h in range(DA_HEADS):
        lo = h * VT_ROWS
        vt_ref[lo:lo + DA_V_DIM, :] = v[:, h * DA_V_DIM:(h + 1) * DA_V_DIM].T.astype(vt_ref.dtype)
        vt_ref[lo + DA_V_DIM:lo + VT_ROWS, :] = tail


def _matmul_rope(x, w, cos_t, sin_t, n_lat_tiles, tiles_per_seq):
    m, k = x.shape
    n = w.shape[1]
    assert n == 2 * DA_QK + DA_WIDTH
    kern = functools.partial(_mm_rope_kernel, n_lat_tiles=n_lat_tiles)
    return pl.pallas_call(
        kern,
        grid=(m // ROW_TILE,),
        in_specs=[pl.BlockSpec((ROW_TILE, k), lambda i: (i, 0)),
                  pl.BlockSpec((k, n), lambda i: (0, 0), pipeline_mode=pl.Buffered(1)),
                  pl.BlockSpec((ROW_TILE, LANE), lambda i: (i % tiles_per_seq, 0)),
                  pl.BlockSpec((ROW_TILE, LANE), lambda i: (i % tiles_per_seq, 0))],
        out_specs=[pl.BlockSpec((ROW_TILE, 2 * DA_QK), lambda i: (i, 0)),
                   pl.BlockSpec((DA_HEADS * VT_ROWS, ROW_TILE), lambda i: (0, i))],
        out_shape=[jax.ShapeDtypeStruct((m, 2 * DA_QK), BF16),
                   jax.ShapeDtypeStruct((DA_HEADS * VT_ROWS, m), BF16)],
        compiler_params=_params(("parallel",), 40),
        name="qkv_proj_rope",
    )(x, w, cos_t, sin_t)


def _head_lane_order(w):
    k = w.shape[0]
    q = DA_HEAD_DIM // 4
    w = w.reshape(k, DA_HEADS, 2, 2, 2, q)
    return w.transpose(0, 1, 4, 2, 3, 5).reshape(k, DA_QK)


def _rope_tables(n_tokens):
    rows = n_tokens // GRID_W
    n_freq = DA_HEAD_DIM // 4
    inv = ROPE_THETA ** (-jnp.arange(n_freq, dtype=F32) / n_freq)
    ang_r = jnp.arange(rows, dtype=F32)[:, None] * inv
    ang_c = jnp.arange(GRID_W, dtype=F32)[:, None] * inv
    per_row = lambda t: jnp.repeat(t, GRID_W, axis=0)
    per_col = lambda t: jnp.tile(t, (rows, 1))
    cos_half = jnp.tile(jnp.concatenate([per_row(jnp.cos(ang_r)), per_col(jnp.cos(ang_c))], axis=-1), (1, 2))
    sin_half = jnp.tile(jnp.concatenate([per_row(jnp.sin(ang_r)), per_col(jnp.sin(ang_c))], axis=-1), (1, 2))
    return jnp.concatenate([cos_half, cos_half], axis=-1), jnp.concatenate([-sin_half, sin_half], axis=-1)


def _conv3(u, prev_row, next_row, w_ref, has_prev, has_next):
    tm = u.shape[0]
    row = lax.broadcasted_iota(jnp.int32, u.shape, 0)
    prev_row = jnp.where(has_prev, prev_row, 0.0)
    next_row = jnp.where(has_next, next_row, 0.0)
    before = jnp.where(row == 0, prev_row, pltpu.roll(u, 1, 0))
    after = jnp.where(row == tm - 1, next_row, pltpu.roll(u, tm - 1, 0))
    return before * w_ref[0:1, :] + u * w_ref[1:2, :] + after * w_ref[2:3, :]


def _prep_kernel(cv_ref, cvp_ref, cvn_ref, xb_ref, xbp_ref, xbn_ref, dtr_ref,
                 cw_ref, sw_ref, sb_ref, dtb_ref, ya_ref, xs_ref, dt_ref,
                 *, n_lat_tiles, lat_tiles_per_seq, ctx_tiles_per_seq):
    i = pl.program_id(0)
    is_lat = i < n_lat_tiles
    pos = jnp.where(is_lat, i % lat_tiles_per_seq, (i - n_lat_tiles) % ctx_tiles_per_seq)
    last = jnp.where(is_lat, lat_tiles_per_seq - 1, ctx_tiles_per_seq - 1)
    has_prev = pos != 0
    has_next = pos != last

    w = CONV_W
    cv = cv_ref[...]
    gate_b, u = cv[:, 0:w], cv[:, w:2 * w] * cv[:, 2 * w:3 * w]
    p = cvp_ref[SUBLANE - 1:SUBLANE, :]
    n = cvn_ref[0:1, :]
    u_prev = p[:, w:2 * w] * p[:, 2 * w:3 * w]
    u_next = n[:, w:2 * w] * n[:, 2 * w:3 * w]
    ya_ref[...] = (gate_b * _conv3(u, u_prev, u_next, cw_ref, has_prev, has_next)).astype(ya_ref.dtype)

    xc = _conv3(xb_ref[...], xbp_ref[SUBLANE - 1:SUBLANE, :], xbn_ref[0:1, :], sw_ref, has_prev, has_next)
    xs_ref[...] = _silu(xc + sb_ref[...])

    t = dtr_ref[...] + dtb_ref[...]
    dt = jnp.maximum(t, 0.0) + jnp.log1p(jnp.exp(-jnp.abs(t)))
    dt_ref[0] = dt
    dt_ref[1] = pltpu.roll(dt, LANE - SSD_HEADS, 1)


def _prep(p_a, conv_w, ssd_conv_w, ssd_conv_b, dt_bias, n_lat_tiles, lat_tiles_per_seq,
          ctx_tiles_per_seq):
    m = p_a.shape[0]
    dt_blk = (COL_Q + SSD_XBC) // LANE
    tm = HALO_TILE
    per = tm // SUBLANE
    n8 = m // SUBLANE
    cw = 3 * CONV_W
    xbc_blk = COL_Q // SSD_XBC
    prev = lambda i: jnp.maximum(i * per - 1, 0)
    nxt = lambda i: jnp.minimum((i + 1) * per, n8 - 1)
    kern = functools.partial(_prep_kernel, n_lat_tiles=n_lat_tiles, lat_tiles_per_seq=lat_tiles_per_seq,
                             ctx_tiles_per_seq=ctx_tiles_per_seq)
    return pl.pallas_call(
        kern,
        grid=(m // tm,),
        in_specs=[pl.BlockSpec((tm, cw), lambda i: (i, 0)),
                  pl.BlockSpec((SUBLANE, cw), lambda i: (prev(i), 0)),
                  pl.BlockSpec((SUBLANE, cw), lambda i: (nxt(i), 0)),
                  pl.BlockSpec((tm, SSD_XBC), lambda i: (i, xbc_blk)),
                  pl.BlockSpec((SUBLANE, SSD_XBC), lambda i: (prev(i), xbc_blk)),
                  pl.BlockSpec((SUBLANE, SSD_XBC), lambda i: (nxt(i), xbc_blk)),
                  pl.BlockSpec((tm, LANE), lambda i: (i, dt_blk)),
                  pl.BlockSpec((SHORT_CONV, CONV_W), lambda i: (0, 0)),
                  pl.BlockSpec((SHORT_CONV, SSD_XBC), lambda i: (0, 0)),
                  pl.BlockSpec((1, SSD_XBC), lambda i: (0, 0)),
                  pl.BlockSpec((1, LANE), lambda i: (0, 0))],
        out_specs=[pl.BlockSpec((tm, CONV_W), lambda i: (i, 0)),
                   pl.BlockSpec((tm, SSD_XBC), lambda i: (i, 0)),
                   pl.BlockSpec((2, tm, LANE), lambda i: (0, i, 0))],
        out_shape=[jax.ShapeDtypeStruct((m, CONV_W), BF16),
                   jax.ShapeDtypeStruct((m, SSD_XBC), F32),
                   jax.ShapeDtypeStruct((2, m, LANE), F32)],
        compiler_params=_params(("parallel",), 40),
        name="conv_prep",
    )(p_a, p_a, p_a, p_a, p_a, p_a, p_a, conv_w, ssd_conv_w, ssd_conv_b, dt_bias)


def _split3(v):
    hi = v.astype(BF16)
    rest = v - hi.astype(F32)
    mid = rest.astype(BF16)
    return hi, mid, (rest - mid.astype(F32)).astype(BF16)


def _dot_f32_lhs(a, b01):
    return sum(_dot(piece, b01) for piece in _split3(a))


def _dot_f32_rhs(a01, b):
    return sum(_dot(a01, piece) for piece in _split3(b))


def _ssd_chunk(direction, xs_ref, dt_ref, alog_ref, alog_ch_ref, y_ref, state_ref):
    t = SSD_CHUNK
    r = lax.broadcasted_iota(jnp.int32, (t, t), 0)
    c = lax.broadcasted_iota(jnp.int32, (t, t), 1)
    mask = (r >= c) if direction == 0 else (r <= c)
    tri = mask.astype(BF16)

    er = lax.broadcasted_iota(jnp.int32, (LANE, SSD_INNER), 0)
    ec = lax.broadcasted_iota(jnp.int32, (LANE, SSD_INNER), 1)
    expand = (jnp.right_shift(ec, 6) == er).astype(BF16)

    dt = dt_ref[direction]
    cs = _dot_f32_rhs(tri, dt * (-jnp.exp(alog_ref[direction])))
    cs_t = cs.T
    dt_ch = _dot_f32_lhs(dt, expand)
    da_ch = dt_ch * (-jnp.exp(alog_ch_ref[direction]))
    cs_ch = _dot_f32_rhs(tri, da_ch)
    tot_ch = jnp.sum(da_ch, axis=0, keepdims=True)

    xdt = xs_ref[:, 0:SSD_INNER] * dt_ch
    x_state = xdt * jnp.exp(tot_ch - cs_ch)
    y_scale = jnp.exp(cs_ch)
    carry = jnp.exp(tot_ch)
    gw = SSD_GROUP_W
    heads_per_group = SSD_HEADS // SSD_GROUPS
    lane_head = jnp.right_shift(lax.broadcasted_iota(jnp.int32, (t, gw), 1), 6)

    for g in range(SSD_GROUPS):
        b_lo = SSD_INNER + g * SSD_STATE
        c_lo = SSD_INNER + SSD_GROUPS * SSD_STATE + g * SSD_STATE
        bg = xs_ref[:, b_lo:b_lo + SSD_STATE].astype(BF16)
        cg = xs_ref[:, c_lo:c_lo + SSD_STATE].astype(BF16)
        cb = _dot_nt(cg, bg)
        sg = state_ref[direction, g]
        xdt_g = xdt[:, g * gw:(g + 1) * gw]
        y = _dot(cg, sg.astype(BF16)) * y_scale[:, g * gw:(g + 1) * gw]
        for hh in range(heads_per_group):
            h = g * heads_per_group + hh
            decay = jnp.where(mask, jnp.exp(cs[:, h:h + 1] - cs_t[h:h + 1, :]), 0.0)
            x_h = jnp.where(lane_head == hh, xdt_g, 0.0).astype(BF16)
            y = y + _dot((cb * decay).astype(BF16), x_h)
        y_ref[:, g * gw:(g + 1) * gw] = y
        ds = _dot_tn(bg, x_state[:, g * gw:(g + 1) * gw].astype(BF16))
        state_ref[direction, g] = sg * carry[:, g * gw:(g + 1) * gw] + ds


def _ssd_kernel(xf_ref, dtf_ref, xb_ref, dtb_ref, alog_ref, alog_ch_ref, yf_ref, yb_ref, state_ref):
    @pl.when(pl.program_id(1) == 0)
    def _():
        state_ref[...] = jnp.zeros_like(state_ref)

    _ssd_chunk(0, xf_ref, dtf_ref, alog_ref, alog_ch_ref, yf_ref, state_ref)
    _ssd_chunk(1, xb_ref, dtb_ref, alog_ref, alog_ch_ref, yb_ref, state_ref)


def _ssd_scan(xs, dt2, a_log, batch, n_lat_chunks, n_ctx_chunks):
    m = xs.shape[0]
    t = SSD_CHUNK
    assert SSD_HEAD_DIM == 64 and SSD_HEADS <= LANE
    a_log_ch = jnp.repeat(a_log, SSD_HEAD_DIM, axis=1).reshape(2, 1, SSD_INNER)
    a_log = _pad_lanes(a_log).reshape(2, 1, LANE)

    def chunk(b, d, s):
        j_ctx = s if d == 0 else n_ctx_chunks - 1 - s
        sl = s - n_ctx_chunks
        j_lat = sl if d == 0 else n_lat_chunks - 1 - sl
        return jnp.where(s < n_ctx_chunks, batch * n_lat_chunks + b * n_ctx_chunks + j_ctx,
                         b * n_lat_chunks + j_lat)

    x_spec = lambda d: pl.BlockSpec((t, SSD_XBC), lambda b, s: (chunk(b, d, s), 0))
    dt_spec = lambda d: pl.BlockSpec((2, t, LANE), lambda b, s: (0, chunk(b, d, s), 0))
    y_spec = lambda d: pl.BlockSpec((t, SSD_INNER), lambda b, s: (chunk(b, d, s), 0))
    return pl.pallas_call(
        _ssd_kernel,
        grid=(batch, n_ctx_chunks + n_lat_chunks),
        in_specs=[x_spec(0), dt_spec(0), x_spec(1), dt_spec(1),
                  pl.BlockSpec((2, 1, LANE), lambda b, s: (0, 0, 0)),
                  pl.BlockSpec((2, 1, SSD_INNER), lambda b, s: (0, 0, 0))],
        out_specs=[y_spec(0), y_spec(1)],
        out_shape=[jax.ShapeDtypeStruct((m, SSD_INNER), F32), jax.ShapeDtypeStruct((m, SSD_INNER), F32)],
        scratch_shapes=[pltpu.VMEM((2, SSD_GROUPS, SSD_STATE, SSD_GROUP_W), F32)],
        compiler_params=_params(("parallel", "arbitrary")),
        name="ssd_scan",
    )(xs, dt2, xs, dt2, a_log, a_log_ch)


def _ssd_gate_norm(y_fwd, y_bwd, x, z, d_skip, norm_g):
    u = (d_skip * x + y_fwd + y_bwd) * _silu(z)
    gw = SSD_GROUP_W
    parts = []
    for g in range(SSD_GROUPS):
        ug = u[:, g * gw:(g + 1) * gw]
        ug = ug * lax.rsqrt(jnp.mean(ug * ug, axis=-1, keepdims=True) + NORM_EPS)
        parts.append(ug * norm_g[:, g * gw:(g + 1) * gw])
    return jnp.concatenate(parts, axis=1)


ATT_TQ = 1024
ATT_TKC = 1024


def _attn_kernel(*refs, has_lat, lam_init):
    if has_lat:
        (q_ref, kc_ref, vct_ref, kl_ref, vlt_ref, lam_ref, g_ref, o_ref,
         m_ref, acc_ref, sa_ref, sb_ref) = refs
    else:
        q_ref, kc_ref, vct_ref, lam_ref, g_ref, o_ref, m_ref, acc_ref, sa_ref, sb_ref = refs

    q = q_ref[...]
    lane = lax.broadcasted_iota(jnp.int32, q.shape, 1)
    zero = jnp.zeros_like(q)
    in_map0 = (lane % DA_HEAD_DIM) < (DA_HEAD_DIM // 2)
    q_maps = (jnp.where(in_map0, q, zero), jnp.where(in_map0, zero, q))

    def scores(k, s_ref):
        n = k.shape[0]
        maxima = []
        for mp in range(2):
            s = _dot_nt(k, q_maps[mp])
            s_ref[mp, 0:n, :] = s
            maxima.append(jnp.max(s, axis=0, keepdims=True))
        return tuple(maxima)

    def update(s_ref, maxima, vt):
        n = vt.shape[1]
        for mp in range(2):
            m_old = m_ref[mp]
            m_new = jnp.maximum(m_old, maxima[mp])
            alpha = jnp.exp2(m_old - m_new)
            p = jnp.exp2(s_ref[mp, 0:n, :] - m_new[0:1, :])
            acc_ref[mp] = alpha[0:1, :] * acc_ref[mp] + _dot(vt, p.astype(vt.dtype))
            m_ref[mp] = m_new

    m_ref[...] = jnp.full(m_ref.shape, -jnp.inf, F32)
    acc_ref[...] = jnp.zeros_like(acc_ref)
    mx = scores(kc_ref[...], sa_ref)

    if not has_lat:
        update(sa_ref, mx, vct_ref[...])
    else:
        tkc = min(ATT_TKC, kl_ref.shape[0])
        n_chunks = kl_ref.shape[0] // tkc
        k_at = lambda c: kl_ref[pl.ds(pl.multiple_of(c * tkc, tkc), tkc), :]
        vt_at = lambda c: vlt_ref[:, pl.ds(pl.multiple_of(c * tkc, tkc), tkc)]

        mx_ctx = mx
        mx = scores(k_at(0), sb_ref)
        update(sa_ref, mx_ctx, vct_ref[...])

        def pair(c2, mx_b):
            c = 1 + 2 * c2
            mx_a = scores(k_at(c), sa_ref)
            update(sb_ref, mx_b, vt_at(c - 1))
            mx_b = scores(k_at(c + 1), sb_ref)
            update(sa_ref, mx_a, vt_at(c))
            return mx_b

        mx = lax.fori_loop(0, (n_chunks - 1) // 2, pair, mx)
        if (n_chunks - 1) % 2:
            mx_a = scores(k_at(n_chunks - 1), sa_ref)
            update(sb_ref, mx, vt_at(n_chunks - 2))
            update(sa_ref, mx_a, vt_at(n_chunks - 1))
        else:
            update(sb_ref, mx, vt_at(n_chunks - 1))

    lv = lam_ref[...]
    dotp = lambda a, b: jnp.sum(lv[a:a + 1, :] * lv[b:b + 1, :], axis=-1, keepdims=True)
    lam = jnp.exp(dotp(0, 1)) - jnp.exp(dotp(2, 3)) + lam_init
    vd = DA_V_DIM
    inv_l = 1.0 / acc_ref[:, vd:vd + 1, :]
    o = acc_ref[0, 0:vd, :] * inv_l[0] - lam * (acc_ref[1, 0:vd, :] * inv_l[1])
    y = o * lax.rsqrt(jnp.mean(o * o, axis=0, keepdims=True) + NORM_EPS)
    o_ref[...] = ((y * g_ref[...]) * (1.0 - lam_init)).T.astype(o_ref.dtype)


VT_ROWS = DA_V_DIM + 16


def _attn_scratch(tq, keys_a, keys_b):
    return [pltpu.VMEM((2, SUBLANE, tq), F32), pltpu.VMEM((2, VT_ROWS, tq), F32),
            pltpu.VMEM((2, keys_a, tq), F32), pltpu.VMEM((2, keys_b, tq), F32)]


def _attention_lat(qkv, v_t, da_lambda, subln, lam_init, batch, seq, ctx_len):
    h = DA_HEADS
    tq = ATT_TQ
    nq = seq // tq
    ctx_blk0 = (batch * seq) // ctx_len
    assert seq % tq == 0 and seq % min(ATT_TKC, seq) == 0
    kern = functools.partial(_attn_kernel, has_lat=True, lam_init=lam_init)
    return pl.pallas_call(
        kern,
        grid=(batch, h, nq),
        in_specs=[pl.BlockSpec((tq, LANE), lambda b, hh, i: (b * nq + i, hh)),
                  pl.BlockSpec((ctx_len, LANE), lambda b, hh, i: (ctx_blk0 + b, h + hh)),
                  pl.BlockSpec((VT_ROWS, ctx_len), lambda b, hh, i: (hh, ctx_blk0 + b)),
                  pl.BlockSpec((seq, LANE), lambda b, hh, i: (b, h + hh)),
                  pl.BlockSpec((VT_ROWS, seq), lambda b, hh, i: (hh, b)),
                  pl.BlockSpec(da_lambda.shape, lambda b, hh, i: (0, 0)),
                  pl.BlockSpec((DA_V_DIM, 1), lambda b, hh, i: (0, 0))],
        out_specs=pl.BlockSpec((tq, LANE), lambda b, hh, i: (b * nq + i, hh)),
        out_shape=jax.ShapeDtypeStruct((batch * seq, DA_WIDTH), BF16),
        scratch_shapes=_attn_scratch(tq, max(min(ATT_TKC, seq), ctx_len), min(ATT_TKC, seq)),
        compiler_params=_params(("parallel", "parallel", "parallel"), 48),
        name="diff_attention",
    )(qkv, qkv, v_t, qkv, v_t, da_lambda, subln)


def _attention_ctx(qkv, v_t, da_lambda, subln, lam_init, batch, seq, ctx_len):
    h = DA_HEADS
    ctx_blk0 = (batch * seq) // ctx_len
    kern = functools.partial(_attn_kernel, has_lat=False, lam_init=lam_init)
    return pl.pallas_call(
        kern,
        grid=(batch, h),
        in_specs=[pl.BlockSpec((ctx_len, LANE), lambda b, hh: (ctx_blk0 + b, hh)),
                  pl.BlockSpec((ctx_len, LANE), lambda b, hh: (ctx_blk0 + b, h + hh)),
                  pl.BlockSpec((VT_ROWS, ctx_len), lambda b, hh: (hh, ctx_blk0 + b)),
                  pl.BlockSpec(da_lambda.shape, lambda b, hh: (0, 0)),
                  pl.BlockSpec((DA_V_DIM, 1), lambda b, hh: (0, 0))],
        out_specs=pl.BlockSpec((ctx_len, LANE), lambda b, hh: (b, hh)),
        out_shape=jax.ShapeDtypeStruct((batch * ctx_len, DA_WIDTH), BF16),
        scratch_shapes=_attn_scratch(ctx_len, ctx_len, SUBLANE),
        compiler_params=_params(("parallel", "parallel")),
        name="diff_attention_ctx",
    )(qkv, qkv, v_t, da_lambda, subln)


def _outproj_kernel(ya_ref, yf_ref, yr_ref, xs_ref, z_ref, dsk_ref, gn_ref, yc_ref, ycx_ref,
                    w0_ref, w1_ref, w2_ref, xl_ref, xc_ref, gate_ref, o_ref, *, n_lat_tiles):
    yb = _ssd_gate_norm(yf_ref[...], yr_ref[...], xs_ref[...], z_ref[...], dsk_ref[...], gn_ref[...])
    yc = _lat_or_ctx(yc_ref, ycx_ref, n_lat_tiles)
    acc = _dot(ya_ref[...], w0_ref[...]) + _dot(yb.astype(BF16), w1_ref[...]) + _dot(yc, w2_ref[...])
    o_ref[...] = _lat_or_ctx(xl_ref, xc_ref, n_lat_tiles) + gate_ref[...] * acc


def _out_proj(ya, y_fwd, y_bwd, xbc, p_a, d_skip, norm_g, yc, yc_ctx, w_out, x_lat, x_ctx, mods,
              n_tiles, n_lat_tiles, cls_of_tile):
    d = x_lat.shape[1]
    wa, wb = ya.shape[1], SSD_INNER
    assert wa == wb and yc.shape[1] == wa + wb
    if yc_ctx is None:
        assert n_tiles <= n_lat_tiles
        yc_ctx = yc
    row_blk = lambda col: pl.BlockSpec((ROW_TILE, wb), lambda i: (i, col))
    vec = pl.BlockSpec((1, wb), lambda i: (0, 0))
    w_rows = lambda rows, blk: pl.BlockSpec((rows, d), lambda i: (blk, 0), pipeline_mode=pl.Buffered(1))
    return pl.pallas_call(
        functools.partial(_outproj_kernel, n_lat_tiles=n_lat_tiles),
        grid=(n_tiles,),
        in_specs=[pl.BlockSpec((ROW_TILE, wa), lambda i: (i, 0)),
                  row_blk(0), row_blk(0), row_blk(0), row_blk(COL_Z // wb), vec, vec]
                 + _lat_ctx_specs(wa + wb, n_lat_tiles)
                 + [w_rows(wa, 0), w_rows(wb, 1), w_rows(wa + wb, 1)]
                 + _lat_ctx_specs(d, n_lat_tiles)
                 + [pl.BlockSpec((None, 1, d), lambda i: (cls_of_tile(i), 0, 2))],
        out_specs=pl.BlockSpec((ROW_TILE, d), lambda i: (i, 0)),
        out_shape=jax.ShapeDtypeStruct((n_tiles * ROW_TILE, d), F32),
        compiler_params=_params(("parallel",), 56),
        name="out_proj_residual",
    )(ya, y_fwd, y_bwd, xbc, p_a, d_skip, norm_g, yc, yc_ctx, w_out, w_out, w_out, x_lat, x_ctx, mods)


def _ffn_kernel(x_ref, g_ref, sh_ref, sc_ref, gate_ref, wg_ref, wu_ref, wd_ref, o_ref, h_ref, acc_ref):
    f = pl.program_id(1)

    @pl.when(f == 0)
    def _():
        h_ref[...] = _modulated_norm(x_ref[...], g_ref[...], sh_ref[...], sc_ref[...]).astype(h_ref.dtype)
        acc_ref[...] = jnp.zeros_like(acc_ref)

    h = h_ref[...]
    a = _silu(_dot(h, wg_ref[...])) * _dot(h, wu_ref[...])
    acc_ref[...] += _dot(a.astype(BF16), wd_ref[...])

    @pl.when(f == pl.num_programs(1) - 1)
    def _():
        o_ref[...] = x_ref[...] + gate_ref[...] * acc_ref[...]


def _ffn_dense(x, g, mods, wg, wu, wd, n_tiles, cls_of_tile):
    m, d = x.shape
    ff = wg.shape[1]
    tf = 512
    return pl.pallas_call(
        _ffn_kernel,
        grid=(n_tiles, ff // tf),
        in_specs=[pl.BlockSpec((ROW_TILE, d), lambda i, f: (i, 0)),
                  pl.BlockSpec((1, d), lambda i, f: (0, 0)),
                  _mod_spec(3, d, cls_of_tile), _mod_spec(4, d, cls_of_tile), _mod_spec(5, d, cls_of_tile),
                  pl.BlockSpec((d, tf), lambda i, f: (0, f)),
                  pl.BlockSpec((d, tf), lambda i, f: (0, f)),
                  pl.BlockSpec((tf, d), lambda i, f: (f, 0))],
        out_specs=pl.BlockSpec((ROW_TILE, d), lambda i, f: (i, 0)),
        out_shape=jax.ShapeDtypeStruct((n_tiles * ROW_TILE, d), F32),
        scratch_shapes=[pltpu.VMEM((ROW_TILE, d), BF16), pltpu.VMEM((ROW_TILE, d), F32)],
        compiler_params=_params(("parallel", "arbitrary"), 48),
        name="ffn_dense",
    )(x, g.reshape(1, d), mods, mods, mods, wg, wu, wd)


ROUTE_I1, ROUTE_I2, ROUTE_P1, ROUTE_P2 = 0, 1, 2, 3


def _router_kernel(x_ref, g_ref, sh_ref, sc_ref, wr_ref, br_ref, h_ref, route_ref):
    h = _modulated_norm(x_ref[...], g_ref[...], sh_ref[...], sc_ref[...])
    h_ref[...] = h
    logits = _dot_3pass(h, wr_ref[...]) + br_ref[...]
    lane = lax.broadcasted_iota(jnp.int32, logits.shape, 1).astype(F32)
    neg = -jnp.inf
    lg = jnp.where(lane < N_EXPERTS, logits, neg)
    v1 = jnp.max(lg, axis=-1, keepdims=True)
    i1 = jnp.min(jnp.where(lg == v1, lane, float(LANE)), axis=-1, keepdims=True)
    lg2 = jnp.where(lane == i1, neg, lg)
    v2 = jnp.max(lg2, axis=-1, keepdims=True)
    i2 = jnp.min(jnp.where(lg2 == v2, lane, float(LANE)), axis=-1, keepdims=True)
    e = jnp.exp(v2 - v1)
    p1 = 1.0 / (1.0 + e)
    p2 = e / (1.0 + e)
    rec = jnp.where(lane == ROUTE_I1, i1, 0.0) + jnp.where(lane == ROUTE_I2, i2, 0.0)
    route_ref[...] = rec + jnp.where(lane == ROUTE_P1, p1, 0.0) + jnp.where(lane == ROUTE_P2, p2, 0.0)


def _router(x, g, mods, wr, br, n_tiles, cls_of_tile):
    m, d = x.shape
    return pl.pallas_call(
        _router_kernel,
        grid=(n_tiles,),
        in_specs=[pl.BlockSpec((ROW_TILE, d), lambda i: (i, 0)),
                  pl.BlockSpec((1, d), lambda i: (0, 0)),
                  _mod_spec(3, d, cls_of_tile), _mod_spec(4, d, cls_of_tile),
                  pl.BlockSpec((d, LANE), lambda i: (0, 0)),
                  pl.BlockSpec((1, LANE), lambda i: (0, 0))],
        out_specs=[pl.BlockSpec((ROW_TILE, d), lambda i: (i, 0)),
                   pl.BlockSpec((ROW_TILE, LANE), lambda i: (i, 0))],
        out_shape=[jax.ShapeDtypeStruct((n_tiles * ROW_TILE, d), F32),
                   jax.ShapeDtypeStruct((n_tiles * ROW_TILE, LANE), F32)],
        compiler_params=_params(("parallel",), 40),
        name="moe_router",
    )(x, g.reshape(1, d), mods, mods, wr, br)


def _routing_tables(route, n_experts, tile):
    n_tok = route.shape[0]
    experts = route[:, ROUTE_I1:ROUTE_I2 + 1].astype(jnp.int32).reshape(-1)
    onehot = (experts[:, None] == jnp.arange(n_experts, dtype=jnp.int32)[None, :]).astype(jnp.int32)
    running = jnp.cumsum(onehot, axis=0)
    rank = jnp.sum(running * onehot, axis=1) - 1
    counts = running[-1]
    padded = ((counts + tile - 1) // tile) * tile
    ends = jnp.cumsum(padded)
    slot = (ends - padded)[experts] + rank
    n_rows = 2 * n_tok + n_experts * tile
    n_tiles = n_rows // tile
    token = jnp.repeat(jnp.arange(n_tok, dtype=jnp.int32), 2)
    row_token = jnp.zeros((n_rows,), jnp.int32).at[slot].set(token, unique_indices=True)
    tile_start = jnp.arange(n_tiles, dtype=jnp.int32) * tile
    tile_expert = jnp.minimum(jnp.sum((tile_start[:, None] >= ends[None, :]).astype(jnp.int32), axis=1),
                              n_experts - 1)
    n_active = (ends[-1] // tile).astype(jnp.int32).reshape(1)
    return tile_expert, row_token, n_active, slot.astype(jnp.int32)


MOE_TILE = 256
MOE_FF_SPLIT = 1
GATHER_UNROLL = 8


def _experts_kernel(te_ref, tok_ref, nact_ref, h_hbm, wg_ref, wu_ref, wd_ref, o_ref, xbuf, sem):
    j = pl.program_id(0)
    n_act = nact_ref[0]
    slot = j % 2
    tile = o_ref.shape[0]

    def row_copy(tok, r, s):
        return pltpu.make_async_copy(h_hbm.at[pl.ds(tok, 1), :], xbuf.at[s, pl.ds(r, 1), :], sem.at[s])

    def start_gather(t, s):
        def body(r, carry):
            row_copy(tok_ref[t * tile + r], r, s).start()
            return carry
        lax.fori_loop(0, tile, body, 0, unroll=GATHER_UNROLL)

    def wait_gather(s):
        pltpu.make_async_copy(h_hbm.at[pl.ds(0, tile), :], xbuf.at[s], sem.at[s]).wait()

    @pl.when(j < n_act)
    def _():
        @pl.when(j == 0)
        def _():
            start_gather(0, 0)

        wait_gather(slot)

        @pl.when(j + 1 < n_act)
        def _():
            start_gather(j + 1, 1 - slot)

        h = xbuf[slot].astype(BF16)
        ff = wg_ref.shape[1]
        w = ff // MOE_FF_SPLIT
        y = None
        for part in range(MOE_FF_SPLIT):
            lo = part * w
            a = _silu(_dot(h, wg_ref[:, lo:lo + w])) * _dot(h, wu_ref[:, lo:lo + w])
            yp = _dot(a.astype(BF16), wd_ref[lo:lo + w, :])
            y = yp if y is None else y + yp
        o_ref[...] = y

    @pl.when(j >= n_act)
    def _():
        o_ref[...] = jnp.zeros_like(o_ref)


def _experts(h, tile_expert, row_token, n_active, wg, wu, wd):
    n_e, d, ff = wg.shape
    tile = MOE_TILE
    n_rows = row_token.shape[0]
    assert ff % (MOE_FF_SPLIT * LANE) == 0
    resident = pl.Buffered(1)
    grid_spec = pltpu.PrefetchScalarGridSpec(
        num_scalar_prefetch=3,
        grid=(n_rows // tile,),
        in_specs=[pl.BlockSpec(memory_space=pl.ANY),
                  pl.BlockSpec((None, d, ff), lambda j, te, tok, na: (te[j], 0, 0), pipeline_mode=resident),
                  pl.BlockSpec((None, d, ff), lambda j, te, tok, na: (te[j], 0, 0), pipeline_mode=resident),
                  pl.BlockSpec((None, ff, d), lambda j, te, tok, na: (te[j], 0, 0), pipeline_mode=resident)],
        out_specs=pl.BlockSpec((tile, d), lambda j, te, tok, na: (j, 0)),
        scratch_shapes=[pltpu.VMEM((2, tile, d), F32), pltpu.SemaphoreType.DMA((2,))],
    )
    return pl.pallas_call(
        _experts_kernel,
        grid_spec=grid_spec,
        out_shape=jax.ShapeDtypeStruct((n_rows, d), F32),
        compiler_params=_params(("arbitrary",), 56),
        name="moe_experts",
    )(tile_expert, row_token, n_active, h, wg, wu, wd)


COMBINE_TILE = 256


def _combine_kernel(slot_ref, y_hbm, x_ref, route_ref, gate_ref, gf_ref, o_ref, ybuf, sem, *, final_norm):
    i = pl.program_id(0)
    n = pl.num_programs(0)
    buf = i % 2
    tile = x_ref.shape[0]

    def row_copy(row, k, r, s):
        return pltpu.make_async_copy(y_hbm.at[pl.ds(row, 1), :], ybuf.at[s, pl.ds(k * tile + r, 1), :], sem.at[s])

    def start_gather(t, s):
        def body(r, carry):
            base = 2 * (t * tile + r)
            row_copy(slot_ref[base], 0, r, s).start()
            row_copy(slot_ref[base + 1], 1, r, s).start()
            return carry
        lax.fori_loop(0, tile, body, 0, unroll=GATHER_UNROLL // 2)

    def wait_gather(s):
        pltpu.make_async_copy(y_hbm.at[pl.ds(0, 2 * tile), :], ybuf.at[s], sem.at[s]).wait()

    @pl.when(i == 0)
    def _():
        start_gather(0, 0)

    wait_gather(buf)

    @pl.when(i + 1 < n)
    def _():
        start_gather(i + 1, 1 - buf)

    route = route_ref[...]
    p1 = route[:, ROUTE_P1:ROUTE_P1 + 1]
    p2 = route[:, ROUTE_P2:ROUTE_P2 + 1]
    y = p1 * ybuf[buf, 0:tile, :] + p2 * ybuf[buf, tile:2 * tile, :]
    out = x_ref[...] + gate_ref[...] * y
    if final_norm:
        out = (out * lax.rsqrt(jnp.mean(out * out, axis=-1, keepdims=True) + NORM_EPS)) * gf_ref[...]
    o_ref[...] = out


def _combine(y_rows, slot, x, route, mods, n_tokens, cls_of_tile, final_g):
    d = x.shape[1]
    tile = COMBINE_TILE
    per = ROW_TILE // tile
    final_norm = final_g is not None
    if not final_norm:
        final_g = jnp.ones((d,), F32)
    grid_spec = pltpu.PrefetchScalarGridSpec(
        num_scalar_prefetch=1,
        grid=(n_tokens // tile,),
        in_specs=[pl.BlockSpec(memory_space=pl.ANY),
                  pl.BlockSpec((tile, d), lambda i, s: (i, 0)),
                  pl.BlockSpec((tile, LANE), lambda i, s: (i, 0)),
                  pl.BlockSpec((None, 1, d), lambda i, s: (cls_of_tile(i // per), 0, 5)),
                  pl.BlockSpec((1, d), lambda i, s: (0, 0))],
        out_specs=pl.BlockSpec((tile, d), lambda i, s: (i, 0)),
        scratch_shapes=[pltpu.VMEM((2, 2 * tile, d), F32), pltpu.SemaphoreType.DMA((2,))],
    )
    return pl.pallas_call(
        functools.partial(_combine_kernel, final_norm=final_norm),
        grid_spec=grid_spec,
        out_shape=jax.ShapeDtypeStruct((n_tokens, d), F32),
        compiler_params=_params(("arbitrary",), 48),
        name="moe_combine",
    )(slot, y_rows, x, route, mods, final_g.reshape(1, d))


def _final_kernel(x_ref, g_ref, o_ref):
    x = x_ref[...]
    o_ref[...] = (x * lax.rsqrt(jnp.mean(x * x, axis=-1, keepdims=True) + NORM_EPS)) * g_ref[...]


def _final_norm(x, g, n_tiles):
    d = x.shape[1]
    return pl.pallas_call(
        _final_kernel,
        grid=(n_tiles,),
        in_specs=[pl.BlockSpec((ROW_TILE, d), lambda i: (i, 0)),
                  pl.BlockSpec((1, d), lambda i: (0, 0))],
        out_specs=pl.BlockSpec((ROW_TILE, d), lambda i: (i, 0)),
        out_shape=jax.ShapeDtypeStruct((n_tiles * ROW_TILE, d), F32),
        compiler_params=_params(("parallel",), 40),
        name="final_norm",
    )(x, g.reshape(1, d))


def _pad_lanes(v, width=LANE):
    return jnp.pad(v, [(0, 0)] * (v.ndim - 1) + [(0, width - v.shape[-1])])


def kernel(x, c, ctx, c_ctx, w_mod, b_mod, g_mix, g_ffn, w_in, conv_w, ssd_conv_w, ssd_conv_b, ssd_a_log, ssd_dt_bias, ssd_d, ssd_norm, da_lambda, da_subln, w_out, ffn_w_gate, ffn_w_up, ffn_w_down, moe_w_router, moe_b_router, moe_w_gate, moe_w_up, moe_w_down, g_final):
    batch, seq, d = x.shape
    ctx_len = ctx.shape[1]
    depth = w_mod.shape[0]
    n_lat = batch * seq
    m = n_lat + batch * ctx_len
    assert seq % ROW_TILE == 0 and (batch * ctx_len) % ROW_TILE == 0 and n_lat % ctx_len == 0
    assert ctx_len % HALO_TILE == 0 and ctx_len % SSD_CHUNK == 0 and batch < MOD_CLASSES
    n_lat_tiles = n_lat // ROW_TILE
    n_all_tiles = m // ROW_TILE
    tiles_per_seq = seq // ROW_TILE
    cls_of_tile = lambda i: jnp.minimum(i // tiles_per_seq, batch)

    x_lat, x_ctx = x.reshape(n_lat, d), ctx.reshape(batch * ctx_len, d)
    cvec = jnp.zeros((MOD_CLASSES, d), F32).at[:batch].set(c).at[batch].set(c_ctx)
    mods_all = _mod_vectors(cvec, w_mod, b_mod)
    cos_t, sin_t = _rope_tables(seq)

    for i in range(depth):
        ctx_out = i < depth - 1
        lam_init = 0.8 - 0.6 * math.exp(-0.3 * i)
        n_tiles = n_all_tiles if ctx_out else n_lat_tiles
        mods = mods_all[i].reshape(MOD_CLASSES, 1, N_MOD * d)
        cols = lambda lo, hi: w_in[i, :, lo:hi]
        w_a = jnp.concatenate([cols(COL_CONV, COL_Q), cols(COL_XBC, COL_DT),
                               _pad_lanes(cols(COL_DT, COL_K))], axis=1).astype(BF16)
        w_q = _head_lane_order(cols(COL_Q, COL_XBC)) * (DA_SCALE * LOG2_E)
        w_k = _head_lane_order(cols(COL_K, COL_V))
        w_b = jnp.concatenate([w_q, w_k, cols(COL_V, COL_V + DA_WIDTH)], axis=1).astype(BF16)

        h, p_a = _in_proj(x_lat, x_ctx, g_mix[i], mods, w_a, n_all_tiles, n_lat_tiles, cls_of_tile)
        qkv, v_t = _matmul_rope(h, w_b, cos_t, sin_t, n_lat_tiles, tiles_per_seq)

        ya, xbc, dt2 = _prep(p_a, conv_w[i], ssd_conv_w[i], ssd_conv_b[i].reshape(1, -1),
                             _pad_lanes(ssd_dt_bias[i].reshape(1, -1)), n_lat // HALO_TILE,
                             seq // HALO_TILE, ctx_len // HALO_TILE)
        y_fwd, y_bwd = _ssd_scan(xbc, dt2, ssd_a_log[i], batch, seq // SSD_CHUNK, ctx_len // SSD_CHUNK)

        subln = da_subln[i].reshape(-1, 1)
        yc = _attention_lat(qkv, v_t, da_lambda[i], subln, lam_init, batch, seq, ctx_len)
        yc_ctx = _attention_ctx(qkv, v_t, da_lambda[i], subln, lam_init, batch, seq, ctx_len) if ctx_out else None

        x_mid = _out_proj(ya, y_fwd, y_bwd, xbc, p_a, jnp.repeat(ssd_d[i], SSD_HEAD_DIM).reshape(1, -1),
                          ssd_norm[i].reshape(1, -1), yc, yc_ctx, w_out[i].astype(BF16), x_lat, x_ctx, mods,
                          n_tiles, n_lat_tiles, cls_of_tile)

        j = i // 2
        if i % 2 == 0:
            xs_all = _ffn_dense(x_mid, g_ffn[i], mods, ffn_w_gate[j].astype(BF16), ffn_w_up[j].astype(BF16),
                                ffn_w_down[j].astype(BF16), n_tiles, cls_of_tile)
        else:
            hh, route = _router(x_mid, g_ffn[i], mods, _pad_lanes(moe_w_router[j]),
                                _pad_lanes(moe_b_router[j].reshape(1, -1)), n_tiles, cls_of_tile)
            tile_expert, row_token, n_active, slot = _routing_tables(route, moe_w_gate.shape[1], MOE_TILE)
            y_rows = _experts(hh, tile_expert, row_token, n_active, moe_w_gate[j].astype(BF16),
                              moe_w_up[j].astype(BF16), moe_w_down[j].astype(BF16))
            xs_all = _combine(y_rows, slot, x_mid, route, mods, n_tiles * ROW_TILE, cls_of_tile,
                              None if ctx_out else g_final)
        x_lat = xs_all
        x_ctx = xs_all[n_lat:] if ctx_out else None

    last_is_moe = depth % 2 == 0
    out = x_lat if last_is_moe else _final_norm(x_lat, g_final, n_lat_tiles)
    return out.reshape(batch, seq, d)
```
